```python
import math
import jax, jax.numpy as jnp
from jax import lax
import numpy as np

D_MODEL = 1024
BATCH = 8
SEQ = 2048
DEPTH = 1

ROPE_THETA = 500000.0
EPS = 1e-6
Q_BLOCK = 128
NEG = -1e30

MLA_HEADS = 8
MLA_NOPE = 64
MLA_ROPE = 32
MLA_QK = MLA_NOPE + MLA_ROPE
MLA_V = 64
MLA_Q_LORA = 768
MLA_KV_LORA = 256

NSA_HEADS = 8
NSA_KV_GROUPS = 2
NSA_REP = NSA_HEADS // NSA_KV_GROUPS
NSA_HEAD = 64
NSA_ROT = NSA_HEAD // 4
CMP_LEN = 32
CMP_STRIDE = 16
CMP_HIDDEN = 256
SEL_LEN = 64
SEL_TOP = 8
SEL_Q_BLOCK = 64
WINDOW = 256
N_NSA_BRANCH = 3
FORCE_BONUS = 1e4
KV_W = NSA_KV_GROUPS * NSA_HEAD

N_BRANCH = 2
D_FF = -(-8 * D_MODEL // (3 * 256)) * 256
N_MOD = 6

IN_SIZES = (MLA_Q_LORA, MLA_KV_LORA, MLA_ROPE, NSA_HEADS * NSA_HEAD,
            KV_W, KV_W, KV_W, KV_W, KV_W, KV_W,
            NSA_HEADS * N_NSA_BRANCH, D_MODEL, D_MODEL)
D_IN = sum(IN_SIZES)

kernel_name = "hybrid_mla_nsa_adaln_block"


def rms_norm(x, gain):
    xf = x.astype(jnp.float32)
    y = xf * lax.rsqrt(jnp.mean(xf * xf, axis=-1, keepdims=True) + EPS)
    return (y * gain.astype(jnp.float32)).astype(x.dtype)


def rope_angles(pos, dim):
    inv = ROPE_THETA ** (-jnp.arange(0, dim, 2, dtype=jnp.float32) / dim)
    ang = pos.astype(jnp.float32)[..., None] * inv
    return jnp.cos(ang), jnp.sin(ang)


def apply_rope(x, cos, sin):
    x1, x2 = jnp.split(x, 2, axis=-1)
    cos = cos.astype(x.dtype)
    sin = sin.astype(x.dtype)
    return jnp.concatenate([x1 * cos - x2 * sin, x2 * cos + x1 * sin], axis=-1)


def partial_rope(x, cos, sin):
    return jnp.concatenate([apply_rope(x[..., :NSA_ROT], cos, sin), x[..., NSA_ROT:]], axis=-1)


def causal_block_attention(q, k, v, scale):
    B, S, H, Dk = q.shape
    nb = S // Q_BLOCK
    qb = q.reshape(B, nb, Q_BLOCK, H, Dk).transpose(1, 0, 2, 3, 4)
    kpos = jnp.arange(S)

    def one(args):
        qi, i = args
        s = jnp.einsum('bqhd,bkhd->bhqk', qi, k).astype(jnp.float32) * scale
        qpos = i * Q_BLOCK + jnp.arange(Q_BLOCK)
        s = jnp.where(kpos[None, :] <= qpos[:, None], s, NEG)
        p = jax.nn.softmax(s, axis=-1).astype(v.dtype)
        return jnp.einsum('bhqk,bkhd->bqhd', p, v)

    o = lax.map(one, (qb, jnp.arange(nb)))
    return o.transpose(1, 0, 2, 3, 4).reshape(B, S, H, v.shape[-1])


def mla_attention(c_q, c_kv, k_pe, cos, sin, q_a_gain, w_q_b, kv_a_gain, w_kv_b, q_gain, k_gain):
    B, S, _ = c_q.shape
    q = (rms_norm(c_q, q_a_gain) @ w_q_b).reshape(B, S, MLA_HEADS, MLA_QK)
    kv = (rms_norm(c_kv, kv_a_gain) @ w_kv_b).reshape(B, S, MLA_HEADS, MLA_NOPE + MLA_V)
    k_nope, v = kv[..., :MLA_NOPE], kv[..., MLA_NOPE:]
    k = jnp.concatenate([k_nope, jnp.broadcast_to(k_pe[:, :, None, :], (B, S, MLA_HEADS, MLA_ROPE))], axis=-1)
    q = rms_norm(q, q_gain)
    k = rms_norm(k, k_gain)
    cos_h, sin_h = cos[:, :, None], sin[:, :, None]
    q = jnp.concatenate([q[..., :MLA_NOPE], apply_rope(q[..., MLA_NOPE:], cos_h, sin_h)], axis=-1)
    k = jnp.concatenate([k[..., :MLA_NOPE], apply_rope(k[..., MLA_NOPE:], cos_h, sin_h)], axis=-1)
    o = causal_block_attention(q, k, v, MLA_QK ** -0.5)
    return o.reshape(B, S, MLA_HEADS * MLA_V)


def nsa_attention(q, k_c, v_c, k_s, v_s, k_w, v_w, g_logit, cos, sin,
                  q_gain, kc_gain, ks_gain, kw_gain,
                  cmp_pos_k, cmp_w1_k, cmp_w2_k, cmp_pos_v, cmp_w1_v, cmp_w2_v):
    B, S, H, Dh = q.shape
    G, R = NSA_KV_GROUPS, NSA_REP
    f32 = jnp.float32
    scale = Dh ** -0.5
    cos_t, sin_t = cos[:, :, None], sin[:, :, None]
    q = partial_rope(rms_norm(q, q_gain), cos_t, sin_t)
    qg = q.reshape(B, S, G, R, Dh)
    t_idx = jnp.arange(S)

    n_cmp = (S - CMP_LEN) // CMP_STRIDE + 1
    starts = jnp.arange(n_cmp) * CMP_STRIDE
    blk_idx = starts[:, None] + jnp.arange(CMP_LEN)[None, :]

    def compress(t, pos_emb, w1, w2):
        blocks = t[:, blk_idx] + pos_emb[None, None, :, None, :]
        flat = blocks.transpose(0, 1, 3, 2, 4).reshape(B, n_cmp, G, CMP_LEN * Dh)
        return jax.nn.silu(flat @ w1) @ w2

    cmp_end = starts + CMP_LEN - 1
    kc = compress(k_c, cmp_pos_k, cmp_w1_k, cmp_w2_k)
    vc = compress(v_c, cmp_pos_v, cmp_w1_v, cmp_w2_v)
    kc = partial_rope(rms_norm(kc, kc_gain), cos[:, cmp_end][:, :, None], sin[:, cmp_end][:, :, None])
    s_c = jnp.einsum('bsgrd,bngd->bgrsn', qg, kc).astype(f32) * scale
    valid_c = cmp_end[None, :] <= t_idx[:, None]
    p_c = jax.nn.softmax(jnp.where(valid_c, s_c, NEG), axis=-1) * valid_c
    o_c = jnp.einsum('bgrsn,bngd->bsgrd', p_c.astype(vc.dtype), vc)

    n_sel = S // SEL_LEN
    sel_start = jnp.arange(n_sel) * SEL_LEN
    overlap = ((starts[:, None] < sel_start[None, :] + SEL_LEN) &
               (starts[:, None] + CMP_LEN > sel_start[None, :])).astype(f32)
    imp = jnp.einsum('bgrsn,nj->bgsj', p_c, overlap)
    cur = t_idx // SEL_LEN
    j = jnp.arange(n_sel)
    forced = (j[None, :] == 0) | (j[None, :] == cur[:, None]) | (j[None, :] == cur[:, None] - 1)
    imp = jnp.where(forced, imp + FORCE_BONUS, imp)
    imp = jnp.where(j[None, :] <= cur[:, None], imp, NEG)
    top = min(SEL_TOP, n_sel)
    _, sel_idx = lax.top_k(imp, top)

    ks = partial_rope(rms_norm(k_s, ks_gain), cos_t, sin_t)
    k_blocks = ks.reshape(B, n_sel, SEL_LEN, G, Dh).transpose(0, 3, 1, 2, 4)
    v_blocks = v_s.reshape(B, n_sel, SEL_LEN, G, Dh).transpose(0, 3, 1, 2, 4)
    nqb = S // SEL_Q_BLOCK
    q_chunks = qg.reshape(B, nqb, SEL_Q_BLOCK, G, R, Dh).transpose(1, 0, 2, 3, 4, 5)
    idx_chunks = sel_idx.reshape(B, G, nqb, SEL_Q_BLOCK, top).transpose(2, 0, 1, 3, 4)
    gather = jax.vmap(jax.vmap(lambda blk, ix: blk[ix]))

    def sel_chunk(args):
        qi, ix, ci = args
        kg = gather(k_blocks, ix)
        vg = gather(v_blocks, ix)
        s = jnp.einsum('bqgrd,bgqnld->bgrqnl', qi, kg).astype(f32) * scale
        tok = ix[..., None] * SEL_LEN + jnp.arange(SEL_LEN)
        qt = ci * SEL_Q_BLOCK + jnp.arange(SEL_Q_BLOCK)
        mask = tok <= qt[None, None, :, None, None]
        s = jnp.where(mask[:, :, None], s, NEG)
        sh = s.shape
        p = jax.nn.softmax(s.reshape(sh[0], sh[1], sh[2], sh[3], -1), axis=-1).reshape(sh)
        return jnp.einsum('bgrqnl,bgqnld->bqgrd', p.astype(vg.dtype), vg)

    o_s = lax.map(sel_chunk, (q_chunks, idx_chunks, jnp.arange(nqb)))
    o_s = o_s.transpose(1, 0, 2, 3, 4, 5).reshape(B, S, G, R, Dh)

    kw = partial_rope(rms_norm(k_w, kw_gain), cos_t, sin_t)
    nqw = S // Q_BLOCK
    span = WINDOW + Q_BLOCK
    kw_pad = jnp.pad(kw, ((0, 0), (WINDOW, 0), (0, 0), (0, 0)))
    vw_pad = jnp.pad(v_w, ((0, 0), (WINDOW, 0), (0, 0), (0, 0)))
    band = jnp.arange(nqw)[:, None] * Q_BLOCK + jnp.arange(span)[None, :]
    kb = kw_pad[:, band]
    vb = vw_pad[:, band]
    qb = qg.reshape(B, nqw, Q_BLOCK, G, R, Dh)
    s_w = jnp.einsum('bcqgrd,bckgd->bcgrqk', qb, kb).astype(f32) * scale
    key_t = band - WINDOW
    q_t = jnp.arange(nqw)[:, None] * Q_BLOCK + jnp.arange(Q_BLOCK)[None, :]
    diff = q_t[:, :, None] - key_t[:, None, :]
    mask_w = (diff >= 0) & (diff < WINDOW) & (key_t[:, None, :] >= 0)
    s_w = jnp.where(mask_w[None, :, None, None], s_w, NEG)
    p_w = jax.nn.softmax(s_w, axis=-1).astype(vb.dtype)
    o_w = jnp.einsum('bcgrqk,bckgd->bcqgrd', p_w, vb).reshape(B, S, G, R, Dh)

    g = jax.nn.sigmoid(g_logit).reshape(B, S, G, R, N_NSA_BRANCH, 1)
    o = g[..., 0, :] * o_c + g[..., 1, :] * o_s + g[..., 2, :] * o_w
    return o.reshape(B, S, H * Dh)


def setup_inputs(seed: int = 0) -> dict:
    key = jax.random.key(seed)

    def nrm(i, shape, scale):
        return jax.random.normal(jax.random.fold_in(key, i), shape, jnp.float32) * scale

    def gain(i, n):
        return 1.0 + nrm(i, (DEPTH, n), 0.02)

    L = DEPTH
    offset = jax.random.randint(jax.random.fold_in(key, 99), (BATCH, 1), 0, 4096)
    positions = (offset + jnp.arange(SEQ)[None, :]).astype(jnp.int32)
    return {
        "x": nrm(0, (BATCH, SEQ, D_MODEL), 1.0),
        "c": nrm(1, (BATCH, D_MODEL), 1.0),
        "positions": positions,
        "ada_w": nrm(2, (L, D_MODEL, N_MOD * D_MODEL), 0.5 * D_MODEL ** -0.5),
        "ada_b": nrm(3, (L, N_MOD * D_MODEL), 0.02),
        "norm1_gain": gain(4, D_MODEL),
        "w_in": nrm(5, (L, D_MODEL, D_IN), D_MODEL ** -0.5),
        "mla_q_a_gain": gain(6, MLA_Q_LORA),
        "mla_w_q_b": nrm(7, (L, MLA_Q_LORA, MLA_HEADS * MLA_QK), MLA_Q_LORA ** -0.5),
        "mla_kv_a_gain": gain(8, MLA_KV_LORA),
        "mla_w_kv_b": nrm(9, (L, MLA_KV_LORA, MLA_HEADS * (MLA_NOPE + MLA_V)), MLA_KV_LORA ** -0.5),
        "mla_q_gain": gain(10, MLA_QK),
        "mla_k_gain": gain(11, MLA_QK),
        "nsa_q_gain": gain(12, NSA_HEAD),
        "nsa_kc_gain": gain(13, NSA_HEAD),
        "nsa_ks_gain": gain(14, NSA_HEAD),
        "nsa_kw_gain": gain(15, NSA_HEAD),
        "cmp_pos_k": nrm(16, (L, CMP_LEN, NSA_HEAD), 0.1),
        "cmp_w1_k": nrm(17, (L, CMP_LEN * NSA_HEAD, CMP_HIDDEN), (CMP_LEN * NSA_HEAD) ** -0.5),
        "cmp_w2_k": nrm(18, (L, CMP_HIDDEN, NSA_HEAD), CMP_HIDDEN ** -0.5),
        "cmp_pos_v": nrm(19, (L, CMP_LEN, NSA_HEAD), 0.1),
        "cmp_w1_v": nrm(20, (L, CMP_LEN * NSA_HEAD, CMP_HIDDEN), (CMP_LEN * NSA_HEAD) ** -0.5),
        "cmp_w2_v": nrm(21, (L, CMP_HIDDEN, NSA_HEAD), CMP_HIDDEN ** -0.5),
        "w_o_mla": nrm(22, (L, MLA_HEADS * MLA_V, D_MODEL), (MLA_HEADS * MLA_V) ** -0.5),
        "w_o_nsa": nrm(23, (L, NSA_HEADS * NSA_HEAD, D_MODEL), (NSA_HEADS * NSA_HEAD) ** -0.5),
        "w_out": nrm(24, (L, D_MODEL, D_MODEL), D_MODEL ** -0.5),
        "norm2_gain": gain(25, D_MODEL),
        "ffn_w_gate": nrm(26, (L, D_MODEL, D_FF), D_MODEL ** -0.5),
        "ffn_w_up": nrm(27, (L, D_MODEL, D_FF), D_MODEL ** -0.5),
        "ffn_w_down": nrm(28, (L, D_FF, D_MODEL), D_FF ** -0.5),
    }


def reference(x, c, positions, ada_w, ada_b, norm1_gain, w_in,
              mla_q_a_gain, mla_w_q_b, mla_kv_a_gain, mla_w_kv_b, mla_q_gain, mla_k_gain,
              nsa_q_gain, nsa_kc_gain, nsa_ks_gain, nsa_kw_gain,
              cmp_pos_k, cmp_w1_k, cmp_w2_k, cmp_pos_v, cmp_w1_v, cmp_w2_v,
              w_o_mla, w_o_nsa, w_out, norm2_gain, ffn_w_gate, ffn_w_up, ffn_w_down):
    B, S, _ = x.shape
    mla_cos, mla_sin = rope_angles(positions, MLA_ROPE)
    nsa_cos, nsa_sin = rope_angles(positions, NSA_ROT)
    split_at = np.cumsum(IN_SIZES)[:-1]
    for l in range(DEPTH):
        mod = jax.nn.silu(c) @ ada_w[l] + ada_b[l]
        sh1, sc1, gt1, sh2, sc2, gt2 = [m[:, None, :] for m in jnp.split(mod, N_MOD, axis=-1)]

        h = rms_norm(x, norm1_gain[l]) * (1.0 + sc1) + sh1
        z = h @ w_in[l]
        (c_q, c_kv, k_pe, q_n, k_c, v_c, k_s, v_s, k_w, v_w,
         g_nsa, g_mla_merge, g_nsa_merge) = jnp.split(z, split_at, axis=-1)
        y_mla = mla_attention(c_q, c_kv, k_pe, mla_cos, mla_sin,
                              mla_q_a_gain[l], mla_w_q_b[l], mla_kv_a_gain[l], mla_w_kv_b[l],
                              mla_q_gain[l], mla_k_gain[l]) @ w_o_mla[l]
        kvs = lambda t: t.reshape(B, S, NSA_KV_GROUPS, NSA_HEAD)
        y_nsa = nsa_attention(q_n.reshape(B, S, NSA_HEADS, NSA_HEAD),
                              kvs(k_c), kvs(v_c), kvs(k_s), kvs(v_s), kvs(k_w), kvs(v_w),
                              g_nsa.reshape(B, S, NSA_HEADS, N_NSA_BRANCH), nsa_cos, nsa_sin,
                              nsa_q_gain[l], nsa_kc_gain[l], nsa_ks_gain[l], nsa_kw_gain[l],
                              cmp_pos_k[l], cmp_w1_k[l], cmp_w2_k[l],
                              cmp_pos_v[l], cmp_w1_v[l], cmp_w2_v[l]) @ w_o_nsa[l]
        merged = jax.nn.sigmoid(g_mla_merge) * y_mla + jax.nn.sigmoid(g_nsa_merge) * y_nsa
        x = x + gt1 * (merged @ w_out[l])

        h2 = rms_norm(x, norm2_gain[l]) * (1.0 + sc2) + sh2
        x = x + gt2 * ((jax.nn.silu(h2 @ ffn_w_gate[l]) * (h2 @ ffn_w_up[l])) @ ffn_w_down[l])
    return x
```

```python
import functools

import jax
import jax.numpy as jnp
from jax import lax
from jax.experimental import pallas as pl
from jax.experimental.pallas import tpu as pltpu

F32 = jnp.float32
BF16 = jnp.bfloat16

D_MODEL = 1024
ROPE_THETA = 500000.0
EPS = 1e-6
NEG = -1e30

MLA_HEADS = 8
MLA_NOPE = 64
MLA_ROPE = 32
MLA_QK = MLA_NOPE + MLA_ROPE
MLA_V = 64
MLA_Q_LORA = 768
MLA_KV_LORA = 256
MLA_PAD = 128

NSA_HEADS = 8
NSA_KV_GROUPS = 2
NSA_REP = NSA_HEADS // NSA_KV_GROUPS
NSA_HEAD = 64
NSA_ROT = NSA_HEAD // 4
CMP_LEN = 32
CMP_STRIDE = 16
CMP_HIDDEN = 256
SEL_LEN = 64
SEL_TOP = 8
WINDOW = 256
N_NSA_BRANCH = 3
FORCE_BONUS = 1e4
KV_W = NSA_KV_GROUPS * NSA_HEAD

D_FF = -(-8 * D_MODEL // (3 * 256)) * 256
N_MOD = 6
LANES = 128

TM_IN = 256
TQ_MLA = 256
TQ_NSA = 128
TK_SEL = 256
TM_OUT = 512
FF_CHUNK = D_FF // 2
VMEM_LIMIT = 56 * 1024 * 1024


def _dot(a, b):
    return jnp.dot(a, b, preferred_element_type=F32)


def _dot_nt(a, b):
    return lax.dot_general(a, b, (((1,), (1,)), ((), ())), preferred_element_type=F32)


def _dot_hilo(a, m):
    hi = a.astype(BF16)
    lo = (a - hi.astype(F32)).astype(BF16)
    return _dot(hi, m) + _dot(lo, m)


def _sigmoid(v):
    return 1.0 / (1.0 + jnp.exp(-v))


def _rms(v, n):
    return v * lax.rsqrt(jnp.sum(v * v, axis=-1, keepdims=True) * (1.0 / n) + EPS)


def _const_spec(shape):
    nd = len(shape)
    return pl.BlockSpec(shape, lambda *_: (0,) * nd)


def _rope_kernel(pos_ref, inv_ref, cos_ref, sin_ref):
    ang = pos_ref[0].astype(F32) * inv_ref[...]
    cos_ref[0] = jnp.cos(ang)
    sin_ref[0] = jnp.sin(ang)


def _rope_tables(positions, inv):
    B, S = positions.shape
    nf = inv.shape[0]
    return pl.pallas_call(
        _rope_kernel,
        grid=(B,),
        in_specs=[pl.BlockSpec((1, 1, S), lambda b: (b, 0, 0)),
                  _const_spec((nf, 1))],
        out_specs=[pl.BlockSpec((1, nf, S), lambda b: (b, 0, 0))] * 2,
        out_shape=[jax.ShapeDtypeStruct((B, nf, S), F32)] * 2,
        name="rope_tables",
    )(positions.reshape(B, 1, S), inv.reshape(nf, 1))


def _ada_kernel(c_ref, w_ref, b_ref, o_ref):
    c = c_ref[...]
    sc = c * _sigmoid(c)
    o_ref[...] = jnp.dot(sc, w_ref[...], preferred_element_type=F32,
                         precision=lax.Precision.HIGHEST) + b_ref[...]


def _ada(c, w, b):
    B, D = c.shape
    N = w.shape[1]
    tn = D_MODEL
    return pl.pallas_call(
        _ada_kernel,
        grid=(N // tn,),
        in_specs=[_const_spec((B, D)),
                  pl.BlockSpec((D, tn), lambda j: (0, j)),
                  pl.BlockSpec((1, tn), lambda j: (0, j))],
        out_specs=pl.BlockSpec((B, tn), lambda j: (0, j)),
        out_shape=jax.ShapeDtypeStruct((B, N), F32),
        name="ada_mod",
    )(c, w, b.reshape(1, N))


def _inproj_kernel(x_ref, mod_ref, tab_ref, g1_ref,
                   wcq_ref, wckv_ref, wsm_ref, wqn_ref, wkv6_ref, wgm_ref,
                   qag_ref, wqb_ref, kvag_ref, wkvb_ref,
                   mqg_ref, mkg_ref, nqg_ref, nksg_ref, nkwg_ref,
                   qm_ref, km_ref, vm_ref, qn_ref, ks_ref, kw_ref, vs_ref, vw_ref,
                   kc_ref, vc_ref, gn_ref, gm_ref):
    x = x_ref[0]
    tm = x.shape[0]
    mod = mod_ref[0]
    sh1, sc1 = mod[0:1], mod[1:2]
    h = _rms(x, D_MODEL) * g1_ref[...] * (1.0 + sc1) + sh1
    hb = h.astype(BF16)

    tab = tab_ref[0]
    cos_m, sin_m = tab[:, 0:MLA_QK], tab[:, LANES:LANES + MLA_QK]
    cos_n, sin_n = tab[:, 2 * LANES:2 * LANES + NSA_HEAD], tab[:, 2 * LANES + NSA_HEAD:3 * LANES]
    zpad = jnp.zeros((tm, MLA_PAD - MLA_QK), F32)
    hr = MLA_ROPE // 2

    cq = _dot(hb, wcq_ref[...])
    cqn = (_rms(cq, MLA_Q_LORA) * qag_ref[...]).astype(BF16)
    q = _dot(cqn, wqb_ref[...])
    mqg = mqg_ref[...]
    m_scale = MLA_QK ** -0.5
    for hd in range(MLA_HEADS):
        qh = _rms(q[:, MLA_QK * hd:MLA_QK * (hd + 1)], MLA_QK) * mqg
        rot = jnp.concatenate([qh[:, :MLA_NOPE], qh[:, MLA_NOPE + hr:], qh[:, MLA_NOPE:MLA_NOPE + hr]], axis=-1)
        qh = (qh * cos_m + rot * sin_m) * m_scale
        qm_ref[0, :, MLA_PAD * hd:MLA_PAD * (hd + 1)] = jnp.concatenate([qh, zpad], axis=-1).astype(BF16)

    zs = _dot(hb, wsm_ref[...])
    ckv = _dot(hb, wckv_ref[...])
    ckvn = (_rms(ckv, MLA_KV_LORA) * kvag_ref[...]).astype(BF16)
    kv = _dot(ckvn, wkvb_ref[...])
    mkg = mkg_ref[...]
    kpe = zs[:, 0:MLA_ROPE]
    kpe_ss = jnp.sum(kpe * kpe, axis=-1, keepdims=True)
    kr = kpe * mkg[:, MLA_NOPE:]
    rotk = jnp.concatenate([kr[:, hr:], kr[:, :hr]], axis=-1)
    kr = kr * cos_m[:, MLA_NOPE:] + rotk * sin_m[:, MLA_NOPE:]
    for hd in range(MLA_HEADS):
        kn = kv[:, MLA_NOPE * hd:MLA_NOPE * (hd + 1)]
        inv = lax.rsqrt((jnp.sum(kn * kn, axis=-1, keepdims=True) + kpe_ss) * (1.0 / MLA_QK) + EPS)
        kh = jnp.concatenate([kn * mkg[:, :MLA_NOPE], kr], axis=-1) * inv
        km_ref[0, :, MLA_PAD * hd:MLA_PAD * (hd + 1)] = jnp.concatenate([kh, zpad], axis=-1).astype(BF16)
    vm_ref[0] = kv[:, MLA_HEADS * MLA_NOPE:].astype(BF16)

    hn = NSA_ROT // 2

    def nsa_norm_rope(v, gain):
        v = _rms(v, NSA_HEAD) * gain
        rot = jnp.concatenate([v[:, hn:NSA_ROT], v[:, :hn], v[:, NSA_ROT:]], axis=-1)
        return v * cos_n + rot * sin_n

    qn = _dot(hb, wqn_ref[...])
    nqg = nqg_ref[...]
    n_scale = NSA_HEAD ** -0.5
    for hd in range(NSA_HEADS):
        qh = nsa_norm_rope(qn[:, NSA_HEAD * hd:NSA_HEAD * (hd + 1)], nqg) * n_scale
        qn_ref[0, hd] = qh.astype(BF16)

    kv6 = _dot(hb, wkv6_ref[...])
    nksg, nkwg = nksg_ref[...], nkwg_ref[...]
    for g in range(NSA_KV_GROUPS):
        def piece(i):
            o = i * KV_W + g * NSA_HEAD
            return kv6[:, o:o + NSA_HEAD]
        kc_ref[0, g] = piece(0)
        vc_ref[0, g] = piece(1)
        ks_ref[0, g] = nsa_norm_rope(piece(2), nksg).astype(BF16)
        vs_ref[0, g] = piece(3).astype(BF16)
        kw_ref[0, g] = nsa_norm_rope(piece(4), nkwg).astype(BF16)
        vw_ref[0, g] = piece(5).astype(BF16)
        go = MLA_ROPE + g * NSA_REP * N_NSA_BRANCH
        gn_ref[0, g] = _sigmoid(zs[:, go:go + NSA_REP * N_NSA_BRANCH])

    gm_ref[0] = _sigmoid(_dot(hb, wgm_ref[...]))


def _inproj(x, mod, tab, g1, weights, gains):
    B, S, D = x.shape
    tm = TM_IN
    wcq, wckv, wsm, wqn, wkv6, wgm, wqb, wkvb = weights
    qag, kvag, mqg, mkg, nqg, nksg, nkwg = gains
    tok = lambda w: pl.BlockSpec((1, tm, w), lambda b, i: (b, i, 0))
    head = lambda n, w: pl.BlockSpec((1, n, tm, w), lambda b, i: (b, 0, i, 0))
    in_specs = [tok(D),
                pl.BlockSpec((1, N_MOD, D), lambda b, i: (b, 0, 0)),
                tok(3 * LANES),
                _const_spec(g1.shape)]
    in_specs += [_const_spec(w.shape) for w in (wcq, wckv, wsm, wqn, wkv6, wgm)]
    in_specs += [_const_spec(qag.shape), _const_spec(wqb.shape), _const_spec(kvag.shape), _const_spec(wkvb.shape)]
    in_specs += [_const_spec(g.shape) for g in (mqg, mkg, nqg, nksg, nkwg)]
    G = NSA_KV_GROUPS
    out_specs = [tok(MLA_HEADS * MLA_PAD), tok(MLA_HEADS * MLA_PAD), tok(MLA_HEADS * MLA_V),
                 head(NSA_HEADS, NSA_HEAD)] + [head(G, NSA_HEAD)] * 6 + \
                [head(G, NSA_REP * N_NSA_BRANCH), tok(2 * D)]
    sds = jax.ShapeDtypeStruct
    out_shape = [sds((B, S, MLA_HEADS * MLA_PAD), BF16), sds((B, S, MLA_HEADS * MLA_PAD), BF16),
                 sds((B, S, MLA_HEADS * MLA_V), BF16),
                 sds((B, NSA_HEADS, S, NSA_HEAD), BF16)] + \
                [sds((B, G, S, NSA_HEAD), BF16)] * 4 + [sds((B, G, S, NSA_HEAD), F32)] * 2 + \
                [sds((B, G, S, NSA_REP * N_NSA_BRANCH), F32), sds((B, S, 2 * D), F32)]
    return pl.pallas_call(
        _inproj_kernel,
        grid=(B, S // tm),
        in_specs=in_specs,
        out_specs=out_specs,
        out_shape=out_shape,
        compiler_params=pltpu.CompilerParams(dimension_semantics=("arbitrary", "arbitrary"),
                                             vmem_limit_bytes=VMEM_LIMIT),
        name="inproj_prep",
    )(x, mod, tab, g1, wcq, wckv, wsm, wqn, wkv6, wgm, qag, wqb, kvag, wkvb, mqg, mkg, nqg, nksg, nkwg)


def _compress_kernel(ck_ref, cv_ref, pk_ref, pv_ref, w1k_ref, w2k_ref, w1v_ref, w2v_ref,
                     kcg_ref, tab_ref, kc_ref, vc_ref):
    half = CMP_STRIDE * NSA_HEAD
    n = ck_ref.shape[2]

    def compress(chunk, pos, w1_ref, w2_ref):
        a = _dot((chunk + pos[0:1]).astype(BF16), w1_ref[0:half, :])
        b = _dot((chunk + pos[1:2]).astype(BF16), w1_ref[half:2 * half, :])
        hid = a + pltpu.roll(b, n - 1, 0)
        hid = hid * _sigmoid(hid)
        return _dot(hid.astype(BF16), w2_ref[...])

    kc = compress(ck_ref[0, 0], pk_ref[...], w1k_ref, w2k_ref)
    kc = _rms(kc, NSA_HEAD) * kcg_ref[...]
    tab = tab_ref[0]
    hn = NSA_ROT // 2
    rot = jnp.concatenate([kc[:, hn:NSA_ROT], kc[:, :hn], kc[:, NSA_ROT:]], axis=-1)
    kc = kc * tab[:, :NSA_HEAD] + rot * tab[:, NSA_HEAD:]
    kc_ref[0, 0] = kc.astype(BF16)
    vc_ref[0, 0] = compress(cv_ref[0, 0], pv_ref[...], w1v_ref, w2v_ref).astype(BF16)


def _compress(ck, cv, pk, pv, w1k, w2k, w1v, w2v, kcg, tabc):
    B, G, n, w = ck.shape
    blk = pl.BlockSpec((1, 1, n, w), lambda b, g: (b, g, 0, 0))
    oblk = pl.BlockSpec((1, 1, n, NSA_HEAD), lambda b, g: (b, g, 0, 0))
    return pl.pallas_call(
        _compress_kernel,
        grid=(B, G),
        in_specs=[blk, blk, _const_spec(pk.shape), _const_spec(pv.shape),
                  _const_spec(w1k.shape), _const_spec(w2k.shape),
                  _const_spec(w1v.shape), _const_spec(w2v.shape),
                  _const_spec(kcg.shape),
                  pl.BlockSpec((1, n, LANES), lambda b, g: (b, 0, 0))],
        out_specs=[oblk, oblk],
        out_shape=[jax.ShapeDtypeStruct((B, G, n, NSA_HEAD), BF16)] * 2,
        name="nsa_compress",
    )(ck, cv, pk, pv, w1k, w2k, w1v, w2v, kcg, tabc)


def _nsa_kernel(q_ref, kc_ref, vc_ref, ks_ref, vs_ref, kw_ref, vw_ref, gn_ref, ov_ref, ex_ref, o_ref):
    tq = q_ref.shape[3]
    R = NSA_REP
    M = R * tq
    q0 = pl.program_id(2) * tq
    q = q_ref[0, 0].reshape(M, NSA_HEAD)
    row = lax.broadcasted_iota(jnp.int32, (M, 1), 0)
    t = q0 + jnp.bitwise_and(row, tq - 1)

    ncp = kc_ref.shape[2]
    s = _dot_nt(q, kc_ref[0, 0])
    n_idx = lax.broadcasted_iota(jnp.int32, (M, ncp), 1)
    valid = (n_idx * CMP_STRIDE + (CMP_LEN - 1)) <= t
    sm = jnp.where(valid, s, NEG)
    mx = jnp.max(sm, axis=-1, keepdims=True)
    e = jnp.where(valid, jnp.exp(sm - mx), 0.0)
    den = jnp.sum(e, axis=-1, keepdims=True)
    p_c = e / jnp.where(den > 0.0, den, 1.0)
    o_c = _dot(p_c.astype(BF16), vc_ref[0, 0])

    psum = p_c[0:tq]
    for r in range(1, R):
        psum = psum + p_c[r * tq:(r + 1) * tq]
    imp = _dot_hilo(psum, ov_ref[...])
    j = lax.broadcasted_iota(jnp.int32, imp.shape, 1)
    cur = lax.shift_right_logical(t[0:tq], SEL_LEN.bit_length() - 1)
    forced = (j == 0) | (j == cur) | (j == cur - 1)
    imp = jnp.where(forced, imp + FORCE_BONUS, imp)
    imp = jnp.where(j <= cur, imp, NEG)
    n_sel = ex_ref.shape[1]
    cnt = jnp.zeros(imp.shape, F32)
    for jj in range(n_sel):
        col = imp[:, jj:jj + 1]
        beats = (col > imp) | ((col == imp) & (j > jj))
        cnt = cnt + jnp.where(beats, 1.0, 0.0)
    sel = jnp.where(cnt < float(SEL_TOP), 1.0, 0.0).astype(BF16)

    tk = ex_ref.shape[2]
    kcol = lax.broadcasted_iota(jnp.int32, (M, tk), 1)

    def sel_body(c, carry):
        m_i, l_i, acc = carry
        k0 = pl.multiple_of(c * tk, tk)
        k = ks_ref[0, 0, pl.ds(k0, tk), :]
        v = vs_ref[0, 0, pl.ds(k0, tk), :]
        sc = _dot_nt(q, k)
        selm = _dot(sel, ex_ref[c])
        selm = jnp.concatenate([selm] * R, axis=0)
        mask = (selm > 0.5) & ((kcol + k0) <= t)
        scm = jnp.where(mask, sc, NEG)
        m_new = jnp.maximum(m_i, jnp.max(scm, axis=-1, keepdims=True))
        alpha = jnp.exp(m_i - m_new)
        p = jnp.where(mask, jnp.exp(scm - m_new), 0.0)
        l_new = alpha * l_i + jnp.sum(p, axis=-1, keepdims=True)
        acc = alpha * acc + _dot(p.astype(BF16), v)
        return m_new, l_new, acc

    n_chunks = q0 // tk + 1
    init = (jnp.full((M, 1), NEG, F32), jnp.zeros((M, 1), F32), jnp.zeros((M, NSA_HEAD), F32))
    _, l_s, acc_s = lax.fori_loop(0, n_chunks, sel_body, init)
    o_s = acc_s / l_s

    span = WINDOW + tq
    w0 = pl.multiple_of(jnp.maximum(q0 - WINDOW, 0), tq)
    kw = kw_ref[0, 0, pl.ds(w0, span), :]
    vw = vw_ref[0, 0, pl.ds(w0, span), :]
    sw = _dot_nt(q, kw)
    d = t - (lax.broadcasted_iota(jnp.int32, (M, span), 1) + w0)
    mask_w = (d >= 0) & (d < WINDOW)
    swm = jnp.where(mask_w, sw, NEG)
    mw = jnp.max(swm, axis=-1, keepdims=True)
    pw = jnp.where(mask_w, jnp.exp(swm - mw), 0.0)
    o_w = _dot(pw.astype(BF16), vw) / jnp.sum(pw, axis=-1, keepdims=True)

    gn = gn_ref[0, 0]
    outs = []
    for r in range(R):
        sl = slice(r * tq, (r + 1) * tq)
        b0 = r * N_NSA_BRANCH
        outs.append(gn[:, b0:b0 + 1] * o_c[sl] + gn[:, b0 + 1:b0 + 2] * o_s[sl] + gn[:, b0 + 2:b0 + 3] * o_w[sl])
    o_ref[0] = jnp.concatenate(outs, axis=-1).astype(BF16)


def _nsa_attention(qn, kc, vc, ks, vs, kw, vw, gn, overlap, expand):
    B, H, S, Dh = qn.shape
    G, R = NSA_KV_GROUPS, NSA_REP
    tq = TQ_NSA
    qn = qn.reshape(B, G, R, S, Dh)
    ncp = kc.shape[2]
    full = pl.BlockSpec((1, 1, S, Dh), lambda b, g, i: (b, g, 0, 0))
    cmp_spec = pl.BlockSpec((1, 1, ncp, Dh), lambda b, g, i: (b, g, 0, 0))
    return pl.pallas_call(
        _nsa_kernel,
        grid=(B, G, S // tq),
        in_specs=[pl.BlockSpec((1, 1, R, tq, Dh), lambda b, g, i: (b, g, 0, i, 0)),
                  cmp_spec, cmp_spec, full, full, full, full,
                  pl.BlockSpec((1, 1, tq, R * N_NSA_BRANCH), lambda b, g, i: (b, g, i, 0)),
                  _const_spec(overlap.shape), _const_spec(expand.shape)],
        out_specs=pl.BlockSpec((1, tq, R * Dh), lambda b, g, i: (b, i, g)),
        out_shape=jax.ShapeDtypeStruct((B, S, H * Dh), BF16),
        compiler_params=pltpu.CompilerParams(dimension_semantics=("arbitrary",) * 3,
                                             vmem_limit_bytes=VMEM_LIMIT),
        name="nsa_attention",
    )(qn, kc, vc, ks, vs, kw, vw, gn, overlap, expand)


def _mla_kernel(q_ref, k_ref, v_ref, o_ref):
    tq = q_ref.shape[1]
    tk = tq
    qi = pl.program_id(2)
    rowi = lax.broadcasted_iota(jnp.int32, (tq, tk), 0)
    coli = lax.broadcasted_iota(jnp.int32, (tq, tk), 1)
    outs = []
    for hh in range(2):
        q = q_ref[0, :, MLA_PAD * hh:MLA_PAD * (hh + 1)]

        def step(c, carry, diag):
            m_i, l_i, acc = carry
            k0 = pl.multiple_of(c * tk, tk)
            k = k_ref[0, pl.ds(k0, tk), MLA_PAD * hh:MLA_PAD * (hh + 1)]
            v = v_ref[0, pl.ds(k0, tk), MLA_V * hh:MLA_V * (hh + 1)]
            s = _dot_nt(q, k)
            if diag:
                s = jnp.where(coli <= rowi, s, NEG)
            m_new = jnp.maximum(m_i, jnp.max(s, axis=-1, keepdims=True))
            alpha = jnp.exp(m_i - m_new)
            p = jnp.exp(s - m_new)
            l_new = alpha * l_i + jnp.sum(p, axis=-1, keepdims=True)
            acc = alpha * acc + _dot(p.astype(BF16), v)
            return m_new, l_new, acc

        init = (jnp.full((tq, 1), NEG, F32), jnp.zeros((tq, 1), F32), jnp.zeros((tq, MLA_V), F32))
        carry = lax.fori_loop(0, qi, functools.partial(step, diag=False), init)
        _, l_f, acc_f = step(qi, carry, True)
        outs.append(acc_f / l_f)
    o_ref[0] = jnp.concatenate(outs, axis=-1).astype(BF16)


def _mla_attention(qm, km, vm):
    B, S, _ = qm.shape
    tq = TQ_MLA
    return pl.pallas_call(
        _mla_kernel,
        grid=(B, MLA_HEADS // 2, S // tq),
        in_specs=[pl.BlockSpec((1, tq, 2 * MLA_PAD), lambda b, h, i: (b, i, h)),
                  pl.BlockSpec((1, S, 2 * MLA_PAD), lambda b, h, i: (b, 0, h)),
                  pl.BlockSpec((1, S, 2 * MLA_V), lambda b, h, i: (b, 0, h))],
        out_specs=pl.BlockSpec((1, tq, 2 * MLA_V), lambda b, h, i: (b, i, h)),
        out_shape=jax.ShapeDtypeStruct((B, S, MLA_HEADS * MLA_V), BF16),
        compiler_params=pltpu.CompilerParams(dimension_semantics=("arbitrary",) * 3,
                                             vmem_limit_bytes=VMEM_LIMIT),
        name="mla_attention",
    )(qm, km, vm)


def _out_ffn_kernel(x_ref, om_ref, on_ref, gm_ref, mod_ref, g2_ref,
                    wom_ref, won_ref, wout_ref, wg_ref, wu_ref, wd_ref, o_ref):
    x = x_ref[0]
    mod = mod_ref[0]
    gt1, sh2, sc2, gt2 = mod[2:3], mod[3:4], mod[4:5], mod[5:6]
    ym = _dot(om_ref[0], wom_ref[...])
    yn = _dot(on_ref[0], won_ref[...])
    merged = gm_ref[0, :, :D_MODEL] * ym + gm_ref[0, :, D_MODEL:] * yn
    x1 = x + gt1 * _dot(merged.astype(BF16), wout_ref[...])
    h2 = (_rms(x1, D_MODEL) * g2_ref[...] * (1.0 + sc2) + sh2).astype(BF16)
    acc = jnp.zeros(x.shape, F32)
    for c in range(D_FF // FF_CHUNK):
        sl = slice(c * FF_CHUNK, (c + 1) * FF_CHUNK)
        g = _dot(h2, wg_ref[:, sl])
        u = _dot(h2, wu_ref[:, sl])
        a = (g * _sigmoid(g) * u).astype(BF16)
        acc = acc + _dot(a, wd_ref[sl, :])
    o_ref[0] = x1 + gt2 * acc


def _out_ffn(x, om, on, gm, mod, g2, wom, won, wout, wg, wu, wd):
    B, S, D = x.shape
    tm = TM_OUT
    tok = lambda w: pl.BlockSpec((1, tm, w), lambda b, i: (b, i, 0))
    wspec = lambda w: pl.BlockSpec(w.shape, lambda b, i: (0, 0), pipeline_mode=pl.Buffered(1))
    return pl.pallas_call(
        _out_ffn_kernel,
        grid=(B, S // tm),
        in_specs=[tok(D), tok(om.shape[2]), tok(on.shape[2]), tok(2 * D),
                  pl.BlockSpec((1, N_MOD, D), lambda b, i: (b, 0, 0)),
                  _const_spec(g2.shape)] + [wspec(w) for w in (wom, won, wout, wg, wu, wd)],
        out_specs=tok(D),
        out_shape=jax.ShapeDtypeStruct((B, S, D), F32),
        compiler_params=pltpu.CompilerParams(dimension_semantics=("arbitrary", "arbitrary"),
                                             vmem_limit_bytes=VMEM_LIMIT),
        name="out_ffn",
    )(x, om, on, gm, mod, g2, wom, won, wout, wg, wu, wd)


def _layer(x, mod, tab, tabc, p):
    B, S, D = x.shape
    w_in = p["w_in"]
    o = 0
    cols = {}
    for name, wdt in (("cq", MLA_Q_LORA), ("ckv", MLA_KV_LORA), ("kpe", MLA_ROPE),
                      ("qn", NSA_HEADS * NSA_HEAD), ("kv6", 6 * KV_W),
                      ("gn", NSA_HEADS * N_NSA_BRANCH), ("gm", 2 * D)):
        cols[name] = w_in[:, o:o + wdt]
        o += wdt
    n_small = MLA_ROPE + NSA_HEADS * N_NSA_BRANCH
    wsm = jnp.concatenate([cols["kpe"], cols["gn"], jnp.zeros((D, LANES - n_small), F32)], axis=1)
    wkvb = p["mla_w_kv_b"].reshape(MLA_KV_LORA, MLA_HEADS, MLA_NOPE + MLA_V)
    wkvb = jnp.concatenate([wkvb[:, :, :MLA_NOPE].reshape(MLA_KV_LORA, -1),
                            wkvb[:, :, MLA_NOPE:].reshape(MLA_KV_LORA, -1)], axis=1)
    bf = lambda w: w.astype(BF16)
    row = lambda g: g.reshape(1, -1)
    weights = tuple(bf(w) for w in (cols["cq"], cols["ckv"], wsm, cols["qn"], cols["kv6"], cols["gm"],
                                    p["mla_w_q_b"], wkvb))
    gains = tuple(row(p[k]) for k in ("mla_q_a_gain", "mla_kv_a_gain", "mla_q_gain", "mla_k_gain",
                                      "nsa_q_gain", "nsa_ks_gain", "nsa_kw_gain"))
    (qm, km, vm, qn, ks, kw, vs, vw, kc_raw, vc_raw, gn, gm) = _inproj(
        x, mod, tab, row(p["norm1_gain"]), weights, gains)

    G = NSA_KV_GROUPS
    n_chunk = S // CMP_STRIDE
    half = CMP_STRIDE * NSA_HEAD
    ck = kc_raw.reshape(B, G, n_chunk, half)
    cv = vc_raw.reshape(B, G, n_chunk, half)
    kc, vc = _compress(ck, cv, p["cmp_pos_k"].reshape(2, half), p["cmp_pos_v"].reshape(2, half),
                       bf(p["cmp_w1_k"]), bf(p["cmp_w2_k"]), bf(p["cmp_w1_v"]), bf(p["cmp_w2_v"]),
                       row(p["nsa_kc_gain"]), tabc)

    n_sel = S // SEL_LEN
    starts = jnp.arange(n_chunk) * CMP_STRIDE
    sel_start = jnp.arange(LANES) * SEL_LEN
    overlap = ((starts[:, None] < sel_start[None, :] + SEL_LEN) &
               (starts[:, None] + CMP_LEN > sel_start[None, :]) &
               (jnp.arange(n_chunk)[:, None] < n_chunk - 1) &
               (jnp.arange(LANES)[None, :] < n_sel)).astype(BF16)
    key_blk = jnp.arange(S) // SEL_LEN
    expand = (jnp.arange(LANES)[:, None] == key_blk[None, :]).astype(BF16)
    expand = expand.reshape(LANES, S // TK_SEL, TK_SEL).transpose(1, 0, 2)

    o_nsa = _nsa_attention(qn, kc, vc, ks, vs, kw, vw, gn, overlap, expand)
    o_mla = _mla_attention(qm, km, vm)

    return _out_ffn(x, o_mla, o_nsa, gm, mod, row(p["norm2_gain"]),
                    bf(p["w_o_mla"]), bf(p["w_o_nsa"]), bf(p["w_out"]),
                    bf(p["ffn_w_gate"]), bf(p["ffn_w_up"]), bf(p["ffn_w_down"]))


def _build_tables(positions):
    B, S = positions.shape
    inv_m = ROPE_THETA ** (-jnp.arange(0, MLA_ROPE, 2, dtype=F32) / MLA_ROPE)
    inv_n = ROPE_THETA ** (-jnp.arange(0, NSA_ROT, 2, dtype=F32) / NSA_ROT)
    cos_t, sin_t = _rope_tables(positions, jnp.concatenate([inv_m, inv_n]))
    cos_t = cos_t.transpose(0, 2, 1)
    sin_t = sin_t.transpose(0, 2, 1)
    nm = MLA_ROPE // 2
    cm, sm = cos_t[..., :nm], sin_t[..., :nm]
    cn, sn = cos_t[..., nm:], sin_t[..., nm:]
    one = lambda n: jnp.ones((B, S, n), F32)
    zero = lambda n: jnp.zeros((B, S, n), F32)
    tab = jnp.concatenate([
        one(MLA_NOPE), cm, cm, zero(LANES - MLA_QK),
        zero(MLA_NOPE), -sm, sm, zero(LANES - MLA_QK),
        cn, cn, one(NSA_HEAD - NSA_ROT),
        -sn, sn, zero(NSA_HEAD - NSA_ROT)], axis=-1)
    n_cmp = (S - CMP_LEN) // CMP_STRIDE + 1
    tabc = tab[:, CMP_LEN - 1::CMP_STRIDE, 2 * LANES:][:, :n_cmp]
    tabc = jnp.pad(tabc, ((0, 0), (0, S // CMP_STRIDE - n_cmp), (0, 0)))
    return tab, tabc


def kernel(x, c, positions, ada_w, ada_b, norm1_gain, w_in, mla_q_a_gain, mla_w_q_b, mla_kv_a_gain, mla_w_kv_b, mla_q_gain, mla_k_gain, nsa_q_gain, nsa_kc_gain, nsa_ks_gain, nsa_kw_gain, cmp_pos_k, cmp_w1_k, cmp_w2_k, cmp_pos_v, cmp_w1_v, cmp_w2_v, w_o_mla, w_o_nsa, w_out, norm2_gain, ffn_w_gate, ffn_w_up, ffn_w_down):
    params = dict(norm1_gain=norm1_gain, w_in=w_in, mla_q_a_gain=mla_q_a_gain, mla_w_q_b=mla_w_q_b,
                  mla_kv_a_gain=mla_kv_a_gain, mla_w_kv_b=mla_w_kv_b, mla_q_gain=mla_q_gain,
                  mla_k_gain=mla_k_gain, nsa_q_gain=nsa_q_gain, nsa_kc_gain=nsa_kc_gain,
                  nsa_ks_gain=nsa_ks_gain, nsa_kw_gain=nsa_kw_gain, cmp_pos_k=cmp_pos_k,
                  cmp_w1_k=cmp_w1_k, cmp_w2_k=cmp_w2_k, cmp_pos_v=cmp_pos_v, cmp_w1_v=cmp_w1_v,
                  cmp_w2_v=cmp_w2_v, w_o_mla=w_o_mla, w_o_nsa=w_o_nsa, w_out=w_out,
                  norm2_gain=norm2_gain, ffn_w_gate=ffn_w_gate, ffn_w_up=ffn_w_up, ffn_w_down=ffn_w_down)
    B = x.shape[0]
    tab, tabc = _build_tables(positions)
    depth = ada_w.shape[0]
    for l in range(depth):
        mod = _ada(c, ada_w[l], ada_b[l]).reshape(B, N_MOD, D_MODEL)
        x = _layer(x, mod, tab, tabc, {k: v[l] for k, v in params.items()})
    return x
```

```python
import jax
import jax.numpy as jnp
from jax import lax
from jax.experimental import pallas as pl
from jax.experimental.pallas import tpu as pltpu

F32 = jnp.float32
BF16 = jnp.bfloat16

D_MODEL = 1024
ROPE_THETA = 500000.0
EPS = 1e-6
NEG = -1e30

MLA_HEADS = 8
MLA_NOPE = 64
MLA_ROPE = 32
MLA_QK = MLA_NOPE + MLA_ROPE
MLA_V = 64
MLA_Q_LORA = 768
MLA_KV_LORA = 256

NSA_HEADS = 8
NSA_KV_GROUPS = 2
NSA_REP = NSA_HEADS // NSA_KV_GROUPS
NSA_HEAD = 64
NSA_ROT = NSA_HEAD // 4
CMP_LEN = 32
CMP_STRIDE = 16
CMP_HIDDEN = 256
SEL_LEN = 64
SEL_TOP = 8
WINDOW = 256
N_NSA_BRANCH = 3
FORCE_BONUS = 1e4
KV_W = NSA_KV_GROUPS * NSA_HEAD

D_FF = -(-8 * D_MODEL // (3 * 256)) * 256
N_MOD = 6
LANES = 128
HEAD_PAD = LANES
SEL_PAD = 32

TM_IN = 256
TQ_ATT = 256
TM_OUT = 512
FF_CHUNK = D_FF // 2
VMEM_LIMIT = 56 * 1024 * 1024


def _dot(a, b):
    return jnp.dot(a, b, preferred_element_type=F32)


def _dot_nt(a, b):
    return lax.dot_general(a, b, (((1,), (1,)), ((), ())), preferred_element_type=F32)


def _split_hilo(a):
    hi = a.astype(BF16)
    return hi, (a - hi.astype(F32)).astype(BF16)


def _dot_hilo(a, m):
    hi, lo = _split_hilo(a)
    return _dot(hi, m) + _dot(lo, m)


def _sigmoid(v):
    return 1.0 / (1.0 + jnp.exp(-v))


def _rms(v, n):
    return v * lax.rsqrt(jnp.sum(v * v, axis=-1, keepdims=True) * (1.0 / n) + EPS)


def _const_spec(shape):
    nd = len(shape)
    return pl.BlockSpec(shape, lambda *_: (0,) * nd)


def _rowmax(s):
    return jnp.max(s, axis=-1, keepdims=True)


def _causal_attention(q, k_ref, v_ref, kmax, dbias):
    tq = q.shape[0]
    k0 = kmax - tq
    s_d = _dot_nt(q, k_ref(k0, kmax)) + dbias
    m = _rowmax(s_d)
    if k0 > 0:
        s_m = _dot_nt(q, k_ref(0, k0))
        m = jnp.maximum(m, _rowmax(s_m))
        acc = _dot(jnp.exp(s_m - m).astype(BF16), v_ref(0, k0))
        acc = acc + _dot(jnp.exp(s_d - m).astype(BF16), v_ref(k0, kmax))
    else:
        acc = _dot(jnp.exp(s_d - m).astype(BF16), v_ref(k0, kmax))
    return acc[:, :NSA_HEAD] / acc[:, NSA_HEAD:NSA_HEAD + 1]


def _rope_kernel(pos_ref, inv_ref, cos_ref, sin_ref):
    ang = pos_ref[0].astype(F32) * inv_ref[...]
    cos_ref[0] = jnp.cos(ang)
    sin_ref[0] = jnp.sin(ang)


def _rope_tables(positions, inv):
    B, S = positions.shape
    nf = inv.shape[0]
    return pl.pallas_call(
        _rope_kernel,
        grid=(B,),
        in_specs=[pl.BlockSpec((1, 1, S), lambda b: (b, 0, 0)),
                  _const_spec((nf, 1))],
        out_specs=[pl.BlockSpec((1, nf, S), lambda b: (b, 0, 0))] * 2,
        out_shape=[jax.ShapeDtypeStruct((B, nf, S), F32)] * 2,
        name="rope_tables",
    )(positions.reshape(B, 1, S), inv.reshape(nf, 1))


def _ada_kernel(c_ref, w_ref, b_ref, o_ref):
    c = c_ref[...]
    sc = c * _sigmoid(c)
    o_ref[...] = jnp.dot(sc, w_ref[...], preferred_element_type=F32,
                         precision=lax.Precision.HIGHEST) + b_ref[...]


def _ada(c, w, b):
    B, D = c.shape
    N = w.shape[1]
    tn = D_MODEL
    return pl.pallas_call(
        _ada_kernel,
        grid=(N // tn,),
        in_specs=[_const_spec((B, D)),
                  pl.BlockSpec((D, tn), lambda j: (0, j)),
                  pl.BlockSpec((1, tn), lambda j: (0, j))],
        out_specs=pl.BlockSpec((B, tn), lambda j: (0, j)),
        out_shape=jax.ShapeDtypeStruct((B, N), F32),
        name="ada_mod",
    )(c, w, b.reshape(1, N))


def _inproj_kernel(x_ref, mod_ref, tab_ref, g1_ref,
                   wcq_ref, wckv_ref, wsm_ref, wqn_ref, wkv6_ref, wgm_ref,
                   qag_ref, wqb_ref, kvag_ref, wkvb_ref,
                   mqg_ref, mkg_ref, nqg_ref, nksg_ref, nkwg_ref,
                   qm_ref, km_ref, vm_ref, qn_ref, ks_ref, kw_ref, vs_ref, vw_ref,
                   kc_ref, vc_ref, gn_ref, gm_ref):
    x = x_ref[0]
    tm = x.shape[0]
    mod = mod_ref[0]
    sh1, sc1 = mod[0:1], mod[1:2]
    h = _rms(x, D_MODEL) * g1_ref[...] * (1.0 + sc1) + sh1
    hb = h.astype(BF16)

    tab = tab_ref[0]
    cos_m, sin_m = tab[:, 0:MLA_QK], tab[:, LANES:LANES + MLA_QK]
    cos_n, sin_n = tab[:, 2 * LANES:2 * LANES + NSA_HEAD], tab[:, 2 * LANES + NSA_HEAD:3 * LANES]
    zpad = jnp.zeros((tm, HEAD_PAD - MLA_QK), F32)
    vpad = jnp.where(lax.broadcasted_iota(jnp.int32, (tm, HEAD_PAD - MLA_V), 1) == 0, 1.0, 0.0)
    hr = MLA_ROPE // 2

    cq = _dot(hb, wcq_ref[...])
    cqn = (_rms(cq, MLA_Q_LORA) * qag_ref[...]).astype(BF16)
    q = _dot(cqn, wqb_ref[...])
    mqg = mqg_ref[...]
    m_scale = MLA_QK ** -0.5
    for hd in range(MLA_HEADS):
        qh = _rms(q[:, MLA_QK * hd:MLA_QK * (hd + 1)], MLA_QK) * mqg
        rot = jnp.concatenate([qh[:, :MLA_NOPE], qh[:, MLA_NOPE + hr:], qh[:, MLA_NOPE:MLA_NOPE + hr]], axis=-1)
        qh = (qh * cos_m + rot * sin_m) * m_scale
        qm_ref[0, :, HEAD_PAD * hd:HEAD_PAD * (hd + 1)] = jnp.concatenate([qh, zpad], axis=-1).astype(BF16)

    zs = _dot(hb, wsm_ref[...])
    ckv = _dot(hb, wckv_ref[...])
    ckvn = (_rms(ckv, MLA_KV_LORA) * kvag_ref[...]).astype(BF16)
    kv = _dot(ckvn, wkvb_ref[...])
    mkg = mkg_ref[...]
    kpe = zs[:, 0:MLA_ROPE]
    kpe_ss = jnp.sum(kpe * kpe, axis=-1, keepdims=True)
    kr = kpe * mkg[:, MLA_NOPE:]
    rotk = jnp.concatenate([kr[:, hr:], kr[:, :hr]], axis=-1)
    kr = kr * cos_m[:, MLA_NOPE:] + rotk * sin_m[:, MLA_NOPE:]
    v_off = MLA_HEADS * MLA_NOPE
    for hd in range(MLA_HEADS):
        kn = kv[:, MLA_NOPE * hd:MLA_NOPE * (hd + 1)]
        inv = lax.rsqrt((jnp.sum(kn * kn, axis=-1, keepdims=True) + kpe_ss) * (1.0 / MLA_QK) + EPS)
        kh = jnp.concatenate([kn * mkg[:, :MLA_NOPE], kr], axis=-1) * inv
        km_ref[0, :, HEAD_PAD * hd:HEAD_PAD * (hd + 1)] = jnp.concatenate([kh, zpad], axis=-1).astype(BF16)
        vh = kv[:, v_off + MLA_V * hd:v_off + MLA_V * (hd + 1)]
        vm_ref[0, :, HEAD_PAD * hd:HEAD_PAD * (hd + 1)] = jnp.concatenate([vh, vpad], axis=-1).astype(BF16)

    hn = NSA_ROT // 2

    def nsa_norm_rope(v, gain):
        v = _rms(v, NSA_HEAD) * gain
        rot = jnp.concatenate([v[:, hn:NSA_ROT], v[:, :hn], v[:, NSA_ROT:]], axis=-1)
        return v * cos_n + rot * sin_n

    qn = _dot(hb, wqn_ref[...])
    nqg = nqg_ref[...]
    n_scale = NSA_HEAD ** -0.5
    for hd in range(NSA_HEADS):
        qh = nsa_norm_rope(qn[:, NSA_HEAD * hd:NSA_HEAD * (hd + 1)], nqg) * n_scale
        qn_ref[0, hd] = qh.astype(BF16)

    tok = pl.program_id(1) * tm + lax.broadcasted_iota(jnp.int32, (tm, SEL_PAD), 0)
    blk = lax.shift_right_logical(tok, SEL_LEN.bit_length() - 1)
    ind = jnp.where(lax.broadcasted_iota(jnp.int32, (tm, SEL_PAD), 1) == blk, NEG, 0.0)
    kzero = jnp.zeros((tm, HEAD_PAD - NSA_HEAD - SEL_PAD), F32)

    kv6 = _dot(hb, wkv6_ref[...])
    nksg, nkwg = nksg_ref[...], nkwg_ref[...]
    for g in range(NSA_KV_GROUPS):
        def piece(i):
            o = i * KV_W + g * NSA_HEAD
            return kv6[:, o:o + NSA_HEAD]
        kc_ref[0, g] = piece(0)
        vc_ref[0, g] = piece(1)
        ks_ref[0, g] = jnp.concatenate([nsa_norm_rope(piece(2), nksg), ind, kzero], axis=-1).astype(BF16)
        vs_ref[0, g] = jnp.concatenate([piece(3), vpad], axis=-1).astype(BF16)
        kw_ref[0, g] = nsa_norm_rope(piece(4), nkwg).astype(BF16)
        vw_ref[0, g] = jnp.concatenate([piece(5), vpad], axis=-1).astype(BF16)
        go = MLA_ROPE + g * NSA_REP * N_NSA_BRANCH
        gn_ref[0, g] = _sigmoid(zs[:, go:go + NSA_REP * N_NSA_BRANCH])

    gm_ref[0] = _sigmoid(_dot(hb, wgm_ref[...]))


def _inproj(x, mod, tab, g1, weights, gains):
    B, S, D = x.shape
    tm = TM_IN
    wcq, wckv, wsm, wqn, wkv6, wgm, wqb, wkvb = weights
    qag, kvag, mqg, mkg, nqg, nksg, nkwg = gains
    tok = lambda w: pl.BlockSpec((1, tm, w), lambda b, i: (b, i, 0))
    head = lambda n, w: pl.BlockSpec((1, n, tm, w), lambda b, i: (b, 0, i, 0))
    in_specs = [tok(D),
                pl.BlockSpec((1, N_MOD, D), lambda b, i: (b, 0, 0)),
                tok(3 * LANES),
                _const_spec(g1.shape)]
    in_specs += [_const_spec(w.shape) for w in (wcq, wckv, wsm, wqn, wkv6, wgm)]
    in_specs += [_const_spec(qag.shape), _const_spec(wqb.shape), _const_spec(kvag.shape), _const_spec(wkvb.shape)]
    in_specs += [_const_spec(g.shape) for g in (mqg, mkg, nqg, nksg, nkwg)]
    G = NSA_KV_GROUPS
    sds = jax.ShapeDtypeStruct
    mla_w = MLA_HEADS * HEAD_PAD
    outs = [
        (tok(mla_w), sds((B, S, mla_w), BF16)),
        (tok(mla_w), sds((B, S, mla_w), BF16)),
        (tok(mla_w), sds((B, S, mla_w), BF16)),
        (head(NSA_HEADS, NSA_HEAD), sds((B, NSA_HEADS, S, NSA_HEAD), BF16)),
        (head(G, HEAD_PAD), sds((B, G, S, HEAD_PAD), BF16)),
        (head(G, NSA_HEAD), sds((B, G, S, NSA_HEAD), BF16)),
        (head(G, HEAD_PAD), sds((B, G, S, HEAD_PAD), BF16)),
        (head(G, HEAD_PAD), sds((B, G, S, HEAD_PAD), BF16)),
        (head(G, NSA_HEAD), sds((B, G, S, NSA_HEAD), F32)),
        (head(G, NSA_HEAD), sds((B, G, S, NSA_HEAD), F32)),
        (head(G, NSA_REP * N_NSA_BRANCH), sds((B, G, S, NSA_REP * N_NSA_BRANCH), F32)),
        (tok(2 * D), sds((B, S, 2 * D), F32)),
    ]
    return pl.pallas_call(
        _inproj_kernel,
        grid=(B, S // tm),
        in_specs=in_specs,
        out_specs=[o[0] for o in outs],
        out_shape=[o[1] for o in outs],
        compiler_params=pltpu.CompilerParams(dimension_semantics=("arbitrary", "arbitrary"),
                                             vmem_limit_bytes=VMEM_LIMIT),
        name="inproj_prep",
    )(x, mod, tab, g1, wcq, wckv, wsm, wqn, wkv6, wgm, qag, wqb, kvag, wkvb, mqg, mkg, nqg, nksg, nkwg)


def _compress_kernel(ck_ref, cv_ref, pk_ref, pv_ref, w1k_ref, w2k_ref, w1v_ref, w2v_ref,
                     kcg_ref, tab_ref, kc_ref, vc_ref):
    half = CMP_STRIDE * NSA_HEAD
    n = ck_ref.shape[2]

    def compress(chunk, pos, w1_ref, w2_ref):
        a = _dot((chunk + pos[0:1]).astype(BF16), w1_ref[0:half, :])
        b = _dot((chunk + pos[1:2]).astype(BF16), w1_ref[half:2 * half, :])
        hid = a + pltpu.roll(b, n - 1, 0)
        hid = hid * _sigmoid(hid)
        return _dot(hid.astype(BF16), w2_ref[...])

    kc = compress(ck_ref[0, 0], pk_ref[...], w1k_ref, w2k_ref)
    kc = _rms(kc, NSA_HEAD) * kcg_ref[...]
    tab = tab_ref[0]
    hn = NSA_ROT // 2
    rot = jnp.concatenate([kc[:, hn:NSA_ROT], kc[:, :hn], kc[:, NSA_ROT:]], axis=-1)
    kc = kc * tab[:, :NSA_HEAD] + rot * tab[:, NSA_HEAD:]
    kc_ref[0, 0] = kc.astype(BF16)
    vc_ref[0, 0] = compress(cv_ref[0, 0], pv_ref[...], w1v_ref, w2v_ref).astype(BF16)


def _compress(ck, cv, pk, pv, w1k, w2k, w1v, w2v, kcg, tabc):
    B, G, n, w = ck.shape
    blk = pl.BlockSpec((1, 1, n, w), lambda b, g: (b, g, 0, 0))
    oblk = pl.BlockSpec((1, 1, n, NSA_HEAD), lambda b, g: (b, g, 0, 0))
    return pl.pallas_call(
        _compress_kernel,
        grid=(B, G),
        in_specs=[blk, blk, _const_spec(pk.shape), _const_spec(pv.shape),
                  _const_spec(w1k.shape), _const_spec(w2k.shape),
                  _const_spec(w1v.shape), _const_spec(w2v.shape),
                  _const_spec(kcg.shape),
                  pl.BlockSpec((1, n, LANES), lambda b, g: (b, 0, 0))],
        out_specs=[oblk, oblk],
        out_shape=[jax.ShapeDtypeStruct((B, G, n, NSA_HEAD), BF16)] * 2,
        name="nsa_compress",
    )(ck, cv, pk, pv, w1k, w2k, w1v, w2v, kcg, tabc)


def _nsa_kernel(q_ref, kc_ref, vc_ref, ks_ref, vs_ref, kw_ref, vw_ref, gn_ref,
                ovt_ref, gexp_ref, dbias_ref, wbias_ref, o_ref, imp_ref, os_ref):
    tq = q_ref.shape[3]
    R = NSA_REP
    M = R * tq
    qi = pl.program_id(2)
    q0 = qi * tq
    q = q_ref[0, 0].reshape(M, NSA_HEAD)
    row = lax.broadcasted_iota(jnp.int32, (M, 1), 0)
    t = q0 + jnp.bitwise_and(row, tq - 1)

    ncp = kc_ref.shape[2]
    s = _dot_nt(q, kc_ref[0, 0])
    n_idx = lax.broadcasted_iota(jnp.int32, (M, ncp), 1)
    valid = (n_idx * CMP_STRIDE + (CMP_LEN - 1)) <= t
    sm = jnp.where(valid, s, NEG)
    e = jnp.where(valid, jnp.exp(sm - _rowmax(sm)), 0.0)
    den = jnp.sum(e, axis=-1, keepdims=True)
    p_c = e / jnp.where(den > 0.0, den, 1.0)
    o_c = _dot(p_c.astype(BF16), vc_ref[0, 0])

    psum = p_c[0:tq]
    for r in range(1, R):
        psum = psum + p_c[r * tq:(r + 1) * tq]
    hi, lo = _split_hilo(psum.T)
    n_sel = imp_ref.shape[0]
    imp = (_dot(ovt_ref[...], hi) + _dot(ovt_ref[...], lo))[0:n_sel]
    j = lax.broadcasted_iota(jnp.int32, (n_sel, tq), 0)
    cur = lax.shift_right_logical(q0 + lax.broadcasted_iota(jnp.int32, (1, tq), 1), SEL_LEN.bit_length() - 1)
    forced = (j == 0) | (j == cur) | (j == cur - 1)
    imp = jnp.where(forced, imp + FORCE_BONUS, imp)
    imp = jnp.where(j <= cur, imp, NEG)
    imp_ref[...] = imp
    cnt = jnp.zeros((n_sel, tq), F32)
    for jj in range(n_sel):
        other = imp_ref[jj:jj + 1, :]
        beats = (other > imp) | ((other == imp) & (j > jj))
        cnt = cnt + jnp.where(beats, 1.0, 0.0)
    nsel = jnp.where((cnt < float(SEL_TOP)) & (j <= cur), 0.0, 1.0)
    nsel = jnp.concatenate([nsel, jnp.zeros((LANES - n_sel, tq), F32)], axis=0).T
    nsel = nsel[:, :HEAD_PAD - NSA_HEAD].astype(BF16)

    dbias = dbias_ref[...]
    for n in range(ks_ref.shape[2] // tq):
        @pl.when(qi == n)
        def _(n=n):
            kmax = (n + 1) * tq
            kf = lambda a, b: ks_ref[0, 0, a:b, :]
            vf = lambda a, b: vs_ref[0, 0, a:b, :]
            for r in range(R):
                qa = jnp.concatenate([q[r * tq:(r + 1) * tq], nsel], axis=-1)
                os_ref[r * tq:(r + 1) * tq, :] = _causal_attention(qa, kf, vf, kmax, dbias)
    o_s = os_ref[...]

    span = WINDOW + tq
    w0 = pl.multiple_of(jnp.maximum(q0 - WINDOW, 0), tq)
    kw = kw_ref[0, 0, pl.ds(w0, span), :]
    vw = vw_ref[0, 0, pl.ds(w0, span), :]
    wb = wbias_ref[jnp.minimum(qi, 1)]
    sw = _dot_nt(q, kw)
    sw = jnp.concatenate([sw[r * tq:(r + 1) * tq] + wb for r in range(R)], axis=0)
    pw = jnp.exp(sw - _rowmax(sw)).astype(BF16)
    acc_w = _dot(pw, vw)
    o_w = acc_w[:, :NSA_HEAD] / acc_w[:, NSA_HEAD:NSA_HEAD + 1]

    gn = gn_ref[0, 0]
    gn = jnp.concatenate([gn, jnp.zeros((tq, LANES - gn.shape[1]), F32)], axis=-1)
    g_hi, g_lo = _split_hilo(gn)
    out = None
    for br, o_b in enumerate((o_c, o_s, o_w)):
        gate = _dot(g_hi, gexp_ref[br]) + _dot(g_lo, gexp_ref[br])
        wide = jnp.concatenate([o_b[r * tq:(r + 1) * tq] for r in range(R)], axis=-1)
        out = gate * wide if out is None else out + gate * wide
    o_ref[0] = out.astype(BF16)


def _nsa_attention(qn, kc, vc, ks, vs, kw, vw, gn, ovt, gexp, dbias, wbias):
    B, H, S, Dh = qn.shape
    G, R = NSA_KV_GROUPS, NSA_REP
    tq = TQ_ATT
    qn = qn.reshape(B, G, R, S, Dh)
    ncp = kc.shape[2]
    full = lambda w: pl.BlockSpec((1, 1, S, w), lambda b, g, i: (b, g, 0, 0))
    cmp_spec = pl.BlockSpec((1, 1, ncp, Dh), lambda b, g, i: (b, g, 0, 0))
    return pl.pallas_call(
        _nsa_kernel,
        grid=(B, G, S // tq),
        in_specs=[pl.BlockSpec((1, 1, R, tq, Dh), lambda b, g, i: (b, g, 0, i, 0)),
                  cmp_spec, cmp_spec, full(HEAD_PAD), full(HEAD_PAD), full(Dh), full(HEAD_PAD),
                  pl.BlockSpec((1, 1, tq, R * N_NSA_BRANCH), lambda b, g, i: (b, g, i, 0)),
                  _const_spec(ovt.shape), _const_spec(gexp.shape),
                  _const_spec(dbias.shape), _const_spec(wbias.shape)],
        out_specs=pl.BlockSpec((1, tq, R * Dh), lambda b, g, i: (b, i, g)),
        out_shape=jax.ShapeDtypeStruct((B, S, H * Dh), BF16),
        scratch_shapes=[pltpu.VMEM((S // SEL_LEN, tq), F32),
                        pltpu.VMEM((R * tq, Dh), F32)],
        compiler_params=pltpu.CompilerParams(dimension_semantics=("arbitrary",) * 3,
                                             vmem_limit_bytes=VMEM_LIMIT),
        name="nsa_attention",
    )(qn, kc, vc, ks, vs, kw, vw, gn, ovt, gexp, dbias, wbias)


def _mla_kernel(q_ref, k_ref, v_ref, dbias_ref, o_ref):
    tq = dbias_ref.shape[0]
    S = q_ref.shape[1]
    dbias = dbias_ref[...]
    for i in range(S // tq):
        outs = []
        for hh in range(2):
            cols = slice(HEAD_PAD * hh, HEAD_PAD * (hh + 1))
            q = q_ref[0, i * tq:(i + 1) * tq, cols]
            kf = lambda a, b, cols=cols: k_ref[0, a:b, cols]
            vf = lambda a, b, cols=cols: v_ref[0, a:b, cols]
            outs.append(_causal_attention(q, kf, vf, (i + 1) * tq, dbias))
        o_ref[0, i * tq:(i + 1) * tq, :] = jnp.concatenate(outs, axis=-1).astype(BF16)


def _mla_attention(qm, km, vm, dbias):
    B, S, _ = qm.shape
    pair = pl.BlockSpec((1, S, 2 * HEAD_PAD), lambda b, h: (b, 0, h))
    return pl.pallas_call(
        _mla_kernel,
        grid=(B, MLA_HEADS // 2),
        in_specs=[pair, pair, pair, _const_spec(dbias.shape)],
        out_specs=pl.BlockSpec((1, S, 2 * MLA_V), lambda b, h: (b, 0, h)),
        out_shape=jax.ShapeDtypeStruct((B, S, MLA_HEADS * MLA_V), BF16),
        compiler_params=pltpu.CompilerParams(dimension_semantics=("arbitrary",) * 2,
                                             vmem_limit_bytes=VMEM_LIMIT),
        name="mla_attention",
    )(qm, km, vm, dbias)


def _out_ffn_kernel(x_ref, om_ref, on_ref, gm_ref, mod_ref, g2_ref,
                    wom_ref, won_ref, wout_ref, wg_ref, wu_ref, wd_ref, o_ref):
    x = x_ref[0]
    mod = mod_ref[0]
    gt1, sh2, sc2, gt2 = mod[2:3], mod[3:4], mod[4:5], mod[5:6]
    ym = _dot(om_ref[0], wom_ref[...])
    yn = _dot(on_ref[0], won_ref[...])
    merged = gm_ref[0, :, :D_MODEL] * ym + gm_ref[0, :, D_MODEL:] * yn
    x1 = x + gt1 * _dot(merged.astype(BF16), wout_ref[...])
    h2 = (_rms(x1, D_MODEL) * g2_ref[...] * (1.0 + sc2) + sh2).astype(BF16)
    acc = jnp.zeros(x.shape, F32)
    for c in range(D_FF // FF_CHUNK):
        sl = slice(c * FF_CHUNK, (c + 1) * FF_CHUNK)
        g = _dot(h2, wg_ref[:, sl])
        u = _dot(h2, wu_ref[:, sl])
        a = (g * _sigmoid(g) * u).astype(BF16)
        acc = acc + _dot(a, wd_ref[sl, :])
    o_ref[0] = x1 + gt2 * acc


def _out_ffn(x, om, on, gm, mod, g2, wom, won, wout, wg, wu, wd):
    B, S, D = x.shape
    tm = TM_OUT
    tok = lambda w: pl.BlockSpec((1, tm, w), lambda b, i: (b, i, 0))
    wspec = lambda w: pl.BlockSpec(w.shape, lambda b, i: (0, 0), pipeline_mode=pl.Buffered(1))
    return pl.pallas_call(
        _out_ffn_kernel,
        grid=(B, S // tm),
        in_specs=[tok(D), tok(om.shape[2]), tok(on.shape[2]), tok(2 * D),
                  pl.BlockSpec((1, N_MOD, D), lambda b, i: (b, 0, 0)),
                  _const_spec(g2.shape)] + [wspec(w) for w in (wom, won, wout, wg, wu, wd)],
        out_specs=tok(D),
        out_shape=jax.ShapeDtypeStruct((B, S, D), F32),
        compiler_params=pltpu.CompilerParams(dimension_semantics=("arbitrary", "arbitrary"),
                                             vmem_limit_bytes=VMEM_LIMIT),
        name="out_ffn",
    )(x, om, on, gm, mod, g2, wom, won, wout, wg, wu, wd)


def _mask_tables(S):
    tq = TQ_ATT
    n_chunk = S // CMP_STRIDE
    n_sel = S // SEL_LEN
    starts = jnp.arange(n_chunk) * CMP_STRIDE
    sel_start = jnp.arange(LANES) * SEL_LEN
    ovt = ((starts[None, :] < sel_start[:, None] + SEL_LEN) &
           (starts[None, :] + CMP_LEN > sel_start[:, None]) &
           (jnp.arange(n_chunk)[None, :] < n_chunk - 1) &
           (jnp.arange(LANES)[:, None] < n_sel)).astype(BF16)
    gcol = jnp.arange(LANES)[:, None]
    head = jnp.arange(NSA_REP * NSA_HEAD)[None, :] // NSA_HEAD
    gexp = jnp.stack([(gcol == head * N_NSA_BRANCH + br) for br in range(N_NSA_BRANCH)]).astype(BF16)
    qi = jnp.arange(tq)[:, None]
    dbias = jnp.where(jnp.arange(tq)[None, :] <= qi, 0.0, NEG).astype(F32)
    kk = jnp.arange(WINDOW + tq)[None, :]
    d0 = qi - kk
    d1 = qi + WINDOW - kk
    band = lambda d: jnp.where((d >= 0) & (d < WINDOW), 0.0, NEG).astype(F32)
    wbias = jnp.stack([band(d0), band(d1)])
    return ovt, gexp, dbias, wbias


def _layer(x, mod, tab, tabc, p):
    B, S, D = x.shape
    w_in = p["w_in"]
    o = 0
    cols = {}
    for name, wdt in (("cq", MLA_Q_LORA), ("ckv", MLA_KV_LORA), ("kpe", MLA_ROPE),
                      ("qn", NSA_HEADS * NSA_HEAD), ("kv6", 6 * KV_W),
                      ("gn", NSA_HEADS * N_NSA_BRANCH), ("gm", 2 * D)):
        cols[name] = w_in[:, o:o + wdt]
        o += wdt
    n_small = MLA_ROPE + NSA_HEADS * N_NSA_BRANCH
    wsm = jnp.concatenate([cols["kpe"], cols["gn"], jnp.zeros((D, LANES - n_small), F32)], axis=1)
    wkvb = p["mla_w_kv_b"].reshape(MLA_KV_LORA, MLA_HEADS, MLA_NOPE + MLA_V)
    wkvb = jnp.concatenate([wkvb[:, :, :MLA_NOPE].reshape(MLA_KV_LORA, -1),
                            wkvb[:, :, MLA_NOPE:].reshape(MLA_KV_LORA, -1)], axis=1)
    bf = lambda w: w.astype(BF16)
    row = lambda g: g.reshape(1, -1)
    weights = tuple(bf(w) for w in (cols["cq"], cols["ckv"], wsm, cols["qn"], cols["kv6"], cols["gm"],
                                    p["mla_w_q_b"], wkvb))
    gains = tuple(row(p[k]) for k in ("mla_q_a_gain", "mla_kv_a_gain", "mla_q_gain", "mla_k_gain",
                                      "nsa_q_gain", "nsa_ks_gain", "nsa_kw_gain"))
    (qm, km, vm, qn, ks, kw, vs, vw, kc_raw, vc_raw, gn, gm) = _inproj(
        x, mod, tab, row(p["norm1_gain"]), weights, gains)

    G = NSA_KV_GROUPS
    n_chunk = S // CMP_STRIDE
    half = CMP_STRIDE * NSA_HEAD
    ck = kc_raw.reshape(B, G, n_chunk, half)
    cv = vc_raw.reshape(B, G, n_chunk, half)
    kc, vc = _compress(ck, cv, p["cmp_pos_k"].reshape(2, half), p["cmp_pos_v"].reshape(2, half),
                       bf(p["cmp_w1_k"]), bf(p["cmp_w2_k"]), bf(p["cmp_w1_v"]), bf(p["cmp_w2_v"]),
                       row(p["nsa_kc_gain"]), tabc)

    ovt, gexp, dbias, wbias = _mask_tables(S)
    o_nsa = _nsa_attention(qn, kc, vc, ks, vs, kw, vw, gn, ovt, gexp, dbias, wbias)
    o_mla = _mla_attention(qm, km, vm, dbias)

    return _out_ffn(x, o_mla, o_nsa, gm, mod, row(p["norm2_gain"]),
                    bf(p["w_o_mla"]), bf(p["w_o_nsa"]), bf(p["w_out"]),
                    bf(p["ffn_w_gate"]), bf(p["ffn_w_up"]), bf(p["ffn_w_down"]))


def _build_tables(positions):
    B, S = positions.shape
    inv_m = ROPE_THETA ** (-jnp.arange(0, MLA_ROPE, 2, dtype=F32) / MLA_ROPE)
    inv_n = ROPE_THETA ** (-jnp.arange(0, NSA_ROT, 2, dtype=F32) / NSA_ROT)
    cos_t, sin_t = _rope_tables(positions, jnp.concatenate([inv_m, inv_n]))
    cos_t = cos_t.transpose(0, 2, 1)
    sin_t = sin_t.transpose(0, 2, 1)
    nm = MLA_ROPE // 2
    cm, sm = cos_t[..., :nm], sin_t[..., :nm]
    cn, sn = cos_t[..., nm:], sin_t[..., nm:]
    one = lambda n: jnp.ones((B, S, n), F32)
    zero = lambda n: jnp.zeros((B, S, n), F32)
    tab = jnp.concatenate([
        one(MLA_NOPE), cm, cm, zero(LANES - MLA_QK),
        zero(MLA_NOPE), -sm, sm, zero(LANES - MLA_QK),
        cn, cn, one(NSA_HEAD - NSA_ROT),
        -sn, sn, zero(NSA_HEAD - NSA_ROT)], axis=-1)
    n_cmp = (S - CMP_LEN) // CMP_STRIDE + 1
    tabc = tab[:, CMP_LEN - 1::CMP_STRIDE, 2 * LANES:][:, :n_cmp]
    tabc = jnp.pad(tabc, ((0, 0), (0, S // CMP_STRIDE - n_cmp), (0, 0)))
    return tab, tabc


def kernel(x, c, positions, ada_w, ada_b, norm1_gain, w_in, mla_q_a_gain, mla_w_q_b, mla_kv_a_gain, mla_w_kv_b, mla_q_gain, mla_k_gain, nsa_q_gain, nsa_kc_gain, nsa_ks_gain, nsa_kw_gain, cmp_pos_k, cmp_w1_k, cmp_w2_k, cmp_pos_v, cmp_w1_v, cmp_w2_v, w_o_mla, w_o_nsa, w_out, norm2_gain, ffn_w_gate, ffn_w_up, ffn_w_down):
    params = dict(norm1_gain=norm1_gain, w_in=w_in, mla_q_a_gain=mla_q_a_gain, mla_w_q_b=mla_w_q_b,
                  mla_kv_a_gain=mla_kv_a_gain, mla_w_kv_b=mla_w_kv_b, mla_q_gain=mla_q_gain,
                  mla_k_gain=mla_k_gain, nsa_q_gain=nsa_q_gain, nsa_kc_gain=nsa_kc_gain,
                  nsa_ks_gain=nsa_ks_gain, nsa_kw_gain=nsa_kw_gain, cmp_pos_k=cmp_pos_k,
                  cmp_w1_k=cmp_w1_k, cmp_w2_k=cmp_w2_k, cmp_pos_v=cmp_pos_v, cmp_w1_v=cmp_w1_v,
                  cmp_w2_v=cmp_w2_v, w_o_mla=w_o_mla, w_o_nsa=w_o_nsa, w_out=w_out,
                  norm2_gain=norm2_gain, ffn_w_gate=ffn_w_gate, ffn_w_up=ffn_w_up, ffn_w_down=ffn_w_down)
    B = x.shape[0]
    tab, tabc = _build_tables(positions)
    depth = ada_w.shape[0]
    for l in range(depth):
        mod = _ada(c, ada_w[l], ada_b[l]).reshape(B, N_MOD, D_MODEL)
        x = _layer(x, mod, tab, tabc, {k: v[l] for k, v in params.items()})
    return x
```

```python
import numpy as np
import jax
import jax.numpy as jnp
from jax import lax
from jax.experimental import pallas as pl
from jax.experimental.pallas import tpu as pltpu

F32 = jnp.float32
BF16 = jnp.bfloat16

D_MODEL = 1024
ROPE_THETA = 500000.0
EPS = 1e-6
NEG = -1e30

MLA_HEADS = 8
MLA_NOPE = 64
MLA_ROPE = 32
MLA_QK = MLA_NOPE + MLA_ROPE
MLA_V = 64
MLA_Q_LORA = 768
MLA_KV_LORA = 256

NSA_HEADS = 8
NSA_KV_GROUPS = 2
NSA_REP = NSA_HEADS // NSA_KV_GROUPS
NSA_HEAD = 64
NSA_ROT = NSA_HEAD // 4
CMP_LEN = 32
CMP_STRIDE = 16
CMP_HIDDEN = 256
SEL_LEN = 64
SEL_TOP = 8
WINDOW = 256
N_NSA_BRANCH = 3
FORCE_BONUS = 1e4
KV_W = NSA_KV_GROUPS * NSA_HEAD

D_FF = -(-8 * D_MODEL // (3 * 256)) * 256
N_MOD = 6
LANES = 128
HEAD_PAD = LANES
N_FREQ = 32

TM_IN = 256
TQ_ATT = 256
TM_OUT = 512
FF_CHUNK = D_FF // 2
VMEM_LIMIT = 56 * 1024 * 1024


def _dot(a, b):
    return jnp.dot(a, b, preferred_element_type=F32)


def _dot_nt(a, b):
    return lax.dot_general(a, b, (((1,), (1,)), ((), ())), preferred_element_type=F32)


def _split_hilo(a):
    hi = a.astype(BF16)
    return hi, (a - hi.astype(F32)).astype(BF16)


def _dot_hilo(a, m):
    hi, lo = _split_hilo(a)
    return _dot(hi, m) + _dot(lo, m)


def _sigmoid(v):
    return 1.0 / (1.0 + jnp.exp(-v))


def _rms(v, n):
    return v * lax.rsqrt(jnp.sum(v * v, axis=-1, keepdims=True) * (1.0 / n) + EPS)


def _rope(v, cos_v, sin_v, lo, half):
    lane = lax.broadcasted_iota(jnp.int32, v.shape, 1)
    is_x1 = (lane >= lo) & (lane < lo + half)
    rot = jnp.where(is_x1, pltpu.roll(v, LANES - half, 1), pltpu.roll(v, half, 1))
    return v * cos_v + rot * sin_v


def _rope_multipliers(cs, texp_ref, trow_ref):
    tabs = _dot_hilo(cs, texp_ref[...]) + trow_ref[...]
    return tuple(tabs[:, LANES * i:LANES * (i + 1)] for i in range(4))


def _const_spec(shape):
    nd = len(shape)
    return pl.BlockSpec(shape, lambda *_: (0,) * nd)


def _rowmax(s):
    return jnp.max(s, axis=-1, keepdims=True)


def _causal_attention(q, k_ref, v_ref, kmax, dbias):
    tq = q.shape[0]
    k0 = kmax - tq
    s_d = _dot_nt(q, k_ref(k0, kmax)) + dbias
    m = _rowmax(s_d)
    if k0 > 0:
        s_m = _dot_nt(q, k_ref(0, k0))
        m = jnp.maximum(m, _rowmax(s_m))
        acc = _dot(jnp.exp(s_m - m).astype(BF16), v_ref(0, k0))
        acc = acc + _dot(jnp.exp(s_d - m).astype(BF16), v_ref(k0, kmax))
    else:
        acc = _dot(jnp.exp(s_d - m).astype(BF16), v_ref(k0, kmax))
    return acc[:, :NSA_HEAD] / acc[:, NSA_HEAD:NSA_HEAD + 1]


def _rope_kernel(pos_ref, inv_ref, cs_ref):
    ang = pos_ref[0].astype(F32) * inv_ref[...]
    nf, S = ang.shape
    rows = jnp.concatenate([jnp.cos(ang), jnp.sin(ang), jnp.zeros((LANES - 2 * nf, S), F32)], axis=0)
    cs_ref[0] = rows.T


def _rope_tables(positions, inv):
    B, S = positions.shape
    nf = inv.shape[0]
    return pl.pallas_call(
        _rope_kernel,
        grid=(B,),
        in_specs=[pl.BlockSpec((1, 1, S), lambda b: (b, 0, 0)),
                  _const_spec((nf, 1))],
        out_specs=pl.BlockSpec((1, S, LANES), lambda b: (b, 0, 0)),
        out_shape=jax.ShapeDtypeStruct((B, S, LANES), F32),
        name="rope_tables",
    )(positions.reshape(B, 1, S), inv.reshape(nf, 1))


def _ada_kernel(c_ref, w_ref, b_ref, o_ref):
    c = c_ref[...]
    sc = c * _sigmoid(c)
    o_ref[...] = jnp.dot(sc, w_ref[...], preferred_element_type=F32,
                         precision=lax.Precision.HIGHEST) + b_ref[...]


def _ada(c, w, b):
    B, D = c.shape
    N = w.shape[1]
    tn = D_MODEL
    return pl.pallas_call(
        _ada_kernel,
        grid=(N // tn,),
        in_specs=[_const_spec((B, D)),
                  pl.BlockSpec((D, tn), lambda j: (0, j)),
                  pl.BlockSpec((1, tn), lambda j: (0, j))],
        out_specs=pl.BlockSpec((B, tn), lambda j: (0, j)),
        out_shape=jax.ShapeDtypeStruct((B, N), F32),
        name="ada_mod",
    )(c, w, b.reshape(1, N))


def _inproj_kernel(x_ref, mod_ref, cs_ref, g1_ref, texp_ref, trow_ref,
                   wcq_ref, wckv_ref, wsm_ref, wqn_ref, wkv6_ref, wgm_ref,
                   qag_ref, wqb_ref, kvag_ref, wkvb_ref,
                   mqg_ref, mkn_ref, mkr_ref, nqg_ref, nksg_ref, nkwg_ref, vone_ref,
                   qm_ref, km_ref, vm_ref, qn_ref, ks_ref, kw_ref, vs_ref, vw_ref,
                   kc_ref, vc_ref, gn_ref, gm_ref):
    x = x_ref[0]
    tm = x.shape[0]
    mod = mod_ref[0]
    sh1, sc1 = mod[0:1], mod[1:2]
    h = _rms(x, D_MODEL) * g1_ref[...] * (1.0 + sc1) + sh1
    hb = h.astype(BF16)

    cos_m, sin_m, cos_n, sin_n = _rope_multipliers(cs_ref[0], texp_ref, trow_ref)
    lane = lax.broadcasted_iota(jnp.int32, (tm, LANES), 1)
    blk = lambda a, i: a[:, HEAD_PAD * i:HEAD_PAD * (i + 1)]
    hm = MLA_ROPE // 2
    hn = NSA_ROT // 2

    cq = _dot(hb, wcq_ref[...])
    cqn = (_rms(cq, MLA_Q_LORA) * qag_ref[...]).astype(BF16)
    q = _dot(cqn, wqb_ref[...])
    mqg = mqg_ref[...]
    m_scale = MLA_QK ** -0.5
    for hd in range(MLA_HEADS):
        qh = _rope(_rms(blk(q, hd), MLA_QK) * mqg, cos_m, sin_m, MLA_NOPE, hm) * m_scale
        qm_ref[0, :, HEAD_PAD * hd:HEAD_PAD * (hd + 1)] = qh.astype(BF16)

    zs = _dot(hb, wsm_ref[...])
    kpe = jnp.where((lane >= MLA_NOPE) & (lane < MLA_QK), zs, 0.0)
    kpe_ss = jnp.sum(kpe * kpe, axis=-1, keepdims=True)
    kr = _rope(kpe * mkr_ref[...], cos_m, sin_m, MLA_NOPE, hm)
    ckv = _dot(hb, wckv_ref[...])
    ckvn = (_rms(ckv, MLA_KV_LORA) * kvag_ref[...]).astype(BF16)
    kv = _dot(ckvn, wkvb_ref[...])
    mkn = mkn_ref[...]
    for hd in range(MLA_HEADS):
        kn = blk(kv, hd)
        inv = lax.rsqrt((jnp.sum(kn * kn, axis=-1, keepdims=True) + kpe_ss) * (1.0 / MLA_QK) + EPS)
        km_ref[0, :, HEAD_PAD * hd:HEAD_PAD * (hd + 1)] = ((kn * mkn + kr) * inv).astype(BF16)
    vm_ref[0] = (kv[:, MLA_HEADS * HEAD_PAD:] + vone_ref[...]).astype(BF16)

    qn = _dot(hb, wqn_ref[...])
    nqg = nqg_ref[...]
    n_scale = NSA_HEAD ** -0.5
    for hd in range(NSA_HEADS):
        qh = _rope(_rms(blk(qn, hd), NSA_HEAD) * nqg, cos_n, sin_n, 0, hn) * n_scale
        qn_ref[0, :, HEAD_PAD * hd:HEAD_PAD * (hd + 1)] = qh.astype(BF16)

    kv6 = _dot(hb, wkv6_ref[...])
    tok = pl.program_id(1) * tm + lax.broadcasted_iota(jnp.int32, (tm, 1), 0)
    sblk = lax.shift_right_logical(tok, SEL_LEN.bit_length() - 1)
    ind = jnp.where(lane - NSA_HEAD == sblk, NEG, 0.0)
    vone = vone_ref[:, 0:HEAD_PAD]
    nksg, nkwg = nksg_ref[...], nkwg_ref[...]
    kcb, vcb = blk(kv6, 0), blk(kv6, 1)
    for g in range(NSA_KV_GROUPS):
        kc_ref[0, g] = kcb[:, NSA_HEAD * g:NSA_HEAD * (g + 1)]
        vc_ref[0, g] = vcb[:, NSA_HEAD * g:NSA_HEAD * (g + 1)]
        ks = _rope(_rms(blk(kv6, 2 + g), NSA_HEAD) * nksg, cos_n, sin_n, 0, hn)
        ks_ref[0, g] = (ks + ind).astype(BF16)
        vs_ref[0, g] = (blk(kv6, 4 + g) + vone).astype(BF16)
        kw = _rope(_rms(blk(kv6, 6 + g), NSA_HEAD) * nkwg, cos_n, sin_n, 0, hn)
        kw_ref[0, g] = kw.astype(BF16)
        vw_ref[0, g] = (blk(kv6, 8 + g) + vone).astype(BF16)

    gn_ref[0] = _sigmoid(zs)
    gm_ref[0] = _sigmoid(_dot(hb, wgm_ref[...])).astype(BF16)


def _inproj(x, mod, cs, consts, weights, rows):
    B, S, D = x.shape
    tm = TM_IN
    tok = lambda w: pl.BlockSpec((1, tm, w), lambda b, i: (b, i, 0))
    head = lambda n, w: pl.BlockSpec((1, n, tm, w), lambda b, i: (b, 0, i, 0))
    operands = list(consts) + list(weights[:6]) + [rows[0], weights[6], rows[1], weights[7]] + list(rows[2:])
    in_specs = [tok(D), pl.BlockSpec((1, N_MOD, D), lambda b, i: (b, 0, 0)), tok(LANES)]
    in_specs += [_const_spec(a.shape) for a in operands]
    G = NSA_KV_GROUPS
    sds = jax.ShapeDtypeStruct
    wide = MLA_HEADS * HEAD_PAD
    outs = [
        (tok(wide), sds((B, S, wide), BF16)),
        (tok(wide), sds((B, S, wide), BF16)),
        (tok(wide), sds((B, S, wide), BF16)),
        (tok(wide), sds((B, S, wide), BF16)),
        (head(G, HEAD_PAD), sds((B, G, S, HEAD_PAD), BF16)),
        (head(G, HEAD_PAD), sds((B, G, S, HEAD_PAD), BF16)),
        (head(G, HEAD_PAD), sds((B, G, S, HEAD_PAD), BF16)),
        (head(G, HEAD_PAD), sds((B, G, S, HEAD_PAD), BF16)),
        (head(G, NSA_HEAD), sds((B, G, S, NSA_HEAD), F32)),
        (head(G, NSA_HEAD), sds((B, G, S, NSA_HEAD), F32)),
        (tok(LANES), sds((B, S, LANES), F32)),
        (tok(2 * D), sds((B, S, 2 * D), BF16)),
    ]
    return pl.pallas_call(
        _inproj_kernel,
        grid=(B, S // tm),
        in_specs=in_specs,
        out_specs=[o[0] for o in outs],
        out_shape=[o[1] for o in outs],
        compiler_params=pltpu.CompilerParams(dimension_semantics=("arbitrary", "arbitrary"),
                                             vmem_limit_bytes=VMEM_LIMIT),
        name="inproj_prep",
    )(x, mod, cs, *operands)


def _compress_kernel(ck_ref, cv_ref, pk_ref, pv_ref, w1k_ref, w2k_ref, w1v_ref, w2v_ref,
                     kcg_ref, cs_ref, texp_ref, trow_ref, kc_ref, vc_ref):
    half = CMP_STRIDE * NSA_HEAD
    n = ck_ref.shape[2]

    def compress(chunk, pos, w1_ref, w2_ref):
        a = _dot((chunk + pos[0:1]).astype(BF16), w1_ref[0:half, :])
        b = _dot((chunk + pos[1:2]).astype(BF16), w1_ref[half:2 * half, :])
        hid = a + pltpu.roll(b, n - 1, 0)
        hid = hid * _sigmoid(hid)
        return _dot(hid.astype(BF16), w2_ref[...])

    kc = compress(ck_ref[0, 0], pk_ref[...], w1k_ref, w2k_ref)
    _, _, cos_n, sin_n = _rope_multipliers(cs_ref[0], texp_ref, trow_ref)
    kc = _rope(_rms(kc, NSA_HEAD) * kcg_ref[...], cos_n, sin_n, 0, NSA_ROT // 2)
    kc_ref[0, 0] = kc.astype(BF16)
    vc_ref[0, 0] = compress(cv_ref[0, 0], pv_ref[...], w1v_ref, w2v_ref).astype(BF16)


def _compress(ck, cv, pk, pv, w1k, w2k, w1v, w2v, kcg, cs_end, texp, trow):
    B, G, n, w = ck.shape
    blk = pl.BlockSpec((1, 1, n, w), lambda b, g: (b, g, 0, 0))
    oblk = lambda wd: pl.BlockSpec((1, 1, n, wd), lambda b, g: (b, g, 0, 0))
    consts = (pk, pv, w1k, w2k, w1v, w2v, kcg)
    return pl.pallas_call(
        _compress_kernel,
        grid=(B, G),
        in_specs=[blk, blk] + [_const_spec(a.shape) for a in consts] +
                 [pl.BlockSpec((1, n, LANES), lambda b, g: (b, 0, 0)),
                  _const_spec(texp.shape), _const_spec(trow.shape)],
        out_specs=[oblk(HEAD_PAD), oblk(NSA_HEAD)],
        out_shape=[jax.ShapeDtypeStruct((B, G, n, HEAD_PAD), BF16),
                   jax.ShapeDtypeStruct((B, G, n, NSA_HEAD), BF16)],
        name="nsa_compress",
    )(ck, cv, *consts, cs_end, texp, trow)


def _nsa_kernel(q_ref, kc_ref, vc_ref, ks_ref, vs_ref, kw_ref, vw_ref, gn_ref,
                ovt_ref, gexp_ref, dbias_ref, wbias_ref, o_ref, imp_ref, os_ref):
    tq = q_ref.shape[1]
    R = NSA_REP
    M = R * tq
    qi = pl.program_id(2)
    q0 = qi * tq
    heads = [q_ref[0, :, HEAD_PAD * r:HEAD_PAD * (r + 1)] for r in range(R)]
    q = jnp.concatenate(heads, axis=0)
    row = lax.broadcasted_iota(jnp.int32, (M, 1), 0)
    t = q0 + jnp.bitwise_and(row, tq - 1)

    ncp = kc_ref.shape[2]
    s = _dot_nt(q, kc_ref[0, 0])
    n_idx = lax.broadcasted_iota(jnp.int32, (M, ncp), 1)
    valid = (n_idx * CMP_STRIDE + (CMP_LEN - 1)) <= t
    sm = jnp.where(valid, s, NEG)
    e = jnp.where(valid, jnp.exp(sm - _rowmax(sm)), 0.0)
    den = jnp.sum(e, axis=-1, keepdims=True)
    p_c = e / jnp.where(den > 0.0, den, 1.0)
    o_c = _dot(p_c.astype(BF16), vc_ref[0, 0])

    psum = p_c[0:tq]
    for r in range(1, R):
        psum = psum + p_c[r * tq:(r + 1) * tq]
    hi, lo = _split_hilo(psum.T)
    n_sel = imp_ref.shape[0]
    imp = (_dot(ovt_ref[...], hi) + _dot(ovt_ref[...], lo))[0:n_sel]
    j = lax.broadcasted_iota(jnp.int32, (n_sel, tq), 0)
    cur = lax.shift_right_logical(q0 + lax.broadcasted_iota(jnp.int32, (1, tq), 1), SEL_LEN.bit_length() - 1)
    forced = (j == 0) | (j == cur) | (j == cur - 1)
    imp = jnp.where(forced, imp + FORCE_BONUS, imp)
    imp = jnp.where(j <= cur, imp, NEG)
    imp_ref[...] = imp
    cnt = jnp.zeros((n_sel, tq), F32)
    for jj in range(n_sel):
        other = imp_ref[jj:jj + 1, :]
        beats = (other > imp) | ((other == imp) & (j > jj))
        cnt = cnt + jnp.where(beats, 1.0, 0.0)
    nsel = jnp.where((cnt < float(SEL_TOP)) & (j <= cur), 0.0, 1.0)
    nsel = jnp.concatenate([jnp.zeros((NSA_HEAD, tq), F32), nsel,
                            jnp.zeros((LANES - NSA_HEAD - n_sel, tq), F32)], axis=0).T.astype(BF16)

    dbias = dbias_ref[...]
    for n in range(ks_ref.shape[2] // tq):
        @pl.when(qi == n)
        def _(n=n):
            kmax = (n + 1) * tq
            kf = lambda a, b: ks_ref[0, 0, a:b, :]
            vf = lambda a, b: vs_ref[0, 0, a:b, :]
            for r in range(R):
                os_ref[r * tq:(r + 1) * tq, :] = _causal_attention(heads[r] + nsel, kf, vf, kmax, dbias)
    o_s = os_ref[...]

    span = WINDOW + tq
    w0 = pl.multiple_of(jnp.maximum(q0 - WINDOW, 0), tq)
    kw = kw_ref[0, 0, pl.ds(w0, span), :]
    vw = vw_ref[0, 0, pl.ds(w0, span), :]
    wb = wbias_ref[jnp.minimum(qi, 1)]
    sw = _dot_nt(q, kw)
    sw = jnp.concatenate([sw[r * tq:(r + 1) * tq] + wb for r in range(R)], axis=0)
    pw = jnp.exp(sw - _rowmax(sw)).astype(BF16)
    acc_w = _dot(pw, vw)
    o_w = acc_w[:, :NSA_HEAD] / acc_w[:, NSA_HEAD:NSA_HEAD + 1]

    g_hi, g_lo = _split_hilo(gn_ref[0])
    grp = pl.program_id(1)
    out = None
    for br, o_b in enumerate((o_c, o_s, o_w)):
        gate = _dot(g_hi, gexp_ref[grp, br]) + _dot(g_lo, gexp_ref[grp, br])
        wide = jnp.concatenate([o_b[r * tq:(r + 1) * tq] for r in range(R)], axis=-1)
        out = gate * wide if out is None else out + gate * wide
    o_ref[0] = out.astype(BF16)


def _nsa_attention(qn, kc, vc, ks, vs, kw, vw, gn, ovt, gexp, dbias, wbias):
    B, S, _ = qn.shape
    G, R, Dh = NSA_KV_GROUPS, NSA_REP, NSA_HEAD
    tq = TQ_ATT
    ncp = kc.shape[2]
    full = pl.BlockSpec((1, 1, S, HEAD_PAD), lambda b, g, i: (b, g, 0, 0))
    cmp_spec = lambda w: pl.BlockSpec((1, 1, ncp, w), lambda b, g, i: (b, g, 0, 0))
    return pl.pallas_call(
        _nsa_kernel,
        grid=(B, G, S // tq),
        in_specs=[pl.BlockSpec((1, tq, R * HEAD_PAD), lambda b, g, i: (b, i, g)),
                  cmp_spec(HEAD_PAD), cmp_spec(Dh), full, full, full, full,
                  pl.BlockSpec((1, tq, LANES), lambda b, g, i: (b, i, 0)),
                  _const_spec(ovt.shape), _const_spec(gexp.shape),
                  _const_spec(dbias.shape), _const_spec(wbias.shape)],
        out_specs=pl.BlockSpec((1, tq, R * Dh), lambda b, g, i: (b, i, g)),
        out_shape=jax.ShapeDtypeStruct((B, S, G * R * Dh), BF16),
        scratch_shapes=[pltpu.VMEM((S // SEL_LEN, tq), F32),
                        pltpu.VMEM((R * tq, Dh), F32)],
        compiler_params=pltpu.CompilerParams(dimension_semantics=("arbitrary",) * 3,
                                             vmem_limit_bytes=VMEM_LIMIT),
        name="nsa_attention",
    )(qn, kc, vc, ks, vs, kw, vw, gn, ovt, gexp, dbias, wbias)


def _mla_kernel(q_ref, k_ref, v_ref, dbias_ref, o_ref):
    tq = dbias_ref.shape[0]
    S = q_ref.shape[1]
    dbias = dbias_ref[...]
    for i in range(S // tq):
        outs = []
        for hh in range(2):
            cols = slice(HEAD_PAD * hh, HEAD_PAD * (hh + 1))
            q = q_ref[0, i * tq:(i + 1) * tq, cols]
            kf = lambda a, b, cols=cols: k_ref[0, a:b, cols]
            vf = lambda a, b, cols=cols: v_ref[0, a:b, cols]
            outs.append(_causal_attention(q, kf, vf, (i + 1) * tq, dbias))
        o_ref[0, i * tq:(i + 1) * tq, :] = jnp.concatenate(outs, axis=-1).astype(BF16)


def _mla_attention(qm, km, vm, dbias):
    B, S, _ = qm.shape
    pair = pl.BlockSpec((1, S, 2 * HEAD_PAD), lambda b, h: (b, 0, h))
    return pl.pallas_call(
        _mla_kernel,
        grid=(B, MLA_HEADS // 2),
        in_specs=[pair, pair, pair, _const_spec(dbias.shape)],
        out_specs=pl.BlockSpec((1, S, 2 * MLA_V), lambda b, h: (b, 0, h)),
        out_shape=jax.ShapeDtypeStruct((B, S, MLA_HEADS * MLA_V), BF16),
        compiler_params=pltpu.CompilerParams(dimension_semantics=("arbitrary",) * 2,
                                             vmem_limit_bytes=VMEM_LIMIT),
        name="mla_attention",
    )(qm, km, vm, dbias)


def _out_ffn_kernel(x_ref, om_ref, on_ref, gm_ref, mod_ref, g2_ref,
                    wom_ref, won_ref, wout_ref, wg_ref, wu_ref, wd_ref, o_ref):
    x = x_ref[0]
    mod = mod_ref[0]
    gt1, sh2, sc2, gt2 = mod[2:3], mod[3:4], mod[4:5], mod[5:6]
    ym = _dot(om_ref[0], wom_ref[...])
    yn = _dot(on_ref[0], won_ref[...])
    merged = gm_ref[0, :, :D_MODEL] * ym + gm_ref[0, :, D_MODEL:] * yn
    x1 = x + gt1 * _dot(merged.astype(BF16), wout_ref[...])
    h2 = (_rms(x1, D_MODEL) * g2_ref[...] * (1.0 + sc2) + sh2).astype(BF16)
    acc = jnp.zeros(x.shape, F32)
    for c in range(D_FF // FF_CHUNK):
        sl = slice(c * FF_CHUNK, (c + 1) * FF_CHUNK)
        g = _dot(h2, wg_ref[:, sl])
        u = _dot(h2, wu_ref[:, sl])
        a = (g * _sigmoid(g) * u).astype(BF16)
        acc = acc + _dot(a, wd_ref[sl, :])
    o_ref[0] = x1 + gt2 * acc


def _out_ffn(x, om, on, gm, mod, g2, wom, won, wout, wg, wu, wd):
    B, S, D = x.shape
    tm = TM_OUT
    tok = lambda w: pl.BlockSpec((1, tm, w), lambda b, i: (b, i, 0))
    wspec = lambda w: pl.BlockSpec(w.shape, lambda b, i: (0, 0), pipeline_mode=pl.Buffered(1))
    return pl.pallas_call(
        _out_ffn_kernel,
        grid=(B, S // tm),
        in_specs=[tok(D), tok(om.shape[2]), tok(on.shape[2]), tok(2 * D),
                  pl.BlockSpec((1, N_MOD, D), lambda b, i: (b, 0, 0)),
                  _const_spec(g2.shape)] + [wspec(w) for w in (wom, won, wout, wg, wu, wd)],
        out_specs=tok(D),
        out_shape=jax.ShapeDtypeStruct((B, S, D), F32),
        compiler_params=pltpu.CompilerParams(dimension_semantics=("arbitrary", "arbitrary"),
                                             vmem_limit_bytes=VMEM_LIMIT),
        name="out_ffn",
    )(x, om, on, gm, mod, g2, wom, won, wout, wg, wu, wd)


def _rope_expansion():
    texp = np.zeros((LANES, 4 * LANES), np.float32)
    trow = np.zeros((1, 4 * LANES), np.float32)
    hm, hn = MLA_ROPE // 2, NSA_ROT // 2
    trow[0, 0:LANES] = 1.0
    trow[0, 2 * LANES:3 * LANES] = 1.0
    for i in range(hm):
        for off, sgn in ((MLA_NOPE + i, -1.0), (MLA_NOPE + hm + i, 1.0)):
            texp[i, off] = 1.0
            trow[0, off] = 0.0
            texp[N_FREQ + i, LANES + off] = sgn
    for i in range(hn):
        for off, sgn in ((i, -1.0), (hn + i, 1.0)):
            texp[hm + i, 2 * LANES + off] = 1.0
            trow[0, 2 * LANES + off] = 0.0
            texp[N_FREQ + hm + i, 3 * LANES + off] = sgn
    return jnp.asarray(texp, BF16), jnp.asarray(trow, F32)


def _mask_tables(S):
    tq = TQ_ATT
    n_chunk = S // CMP_STRIDE
    n_sel = S // SEL_LEN
    starts = np.arange(n_chunk) * CMP_STRIDE
    sel_start = np.arange(LANES) * SEL_LEN
    ovt = ((starts[None, :] < sel_start[:, None] + SEL_LEN) &
           (starts[None, :] + CMP_LEN > sel_start[:, None]) &
           (np.arange(n_chunk)[None, :] < n_chunk - 1) &
           (np.arange(LANES)[:, None] < n_sel))
    gcol = np.arange(LANES)[:, None]
    head = np.arange(NSA_REP * NSA_HEAD)[None, :] // NSA_HEAD
    gexp = np.stack([np.stack([gcol == (g * NSA_REP + head) * N_NSA_BRANCH + br
                               for br in range(N_NSA_BRANCH)]) for g in range(NSA_KV_GROUPS)])
    qi = np.arange(tq)[:, None]
    dbias = np.where(np.arange(tq)[None, :] <= qi, 0.0, NEG)
    kk = np.arange(WINDOW + tq)[None, :]
    band = lambda d: np.where((d >= 0) & (d < WINDOW), 0.0, NEG)
    wbias = np.stack([band(qi - kk), band(qi + WINDOW - kk)])
    return (jnp.asarray(ovt, BF16), jnp.asarray(gexp, BF16),
            jnp.asarray(dbias, F32), jnp.asarray(wbias, F32))


def _pad_heads(w, n_heads, width):
    k = w.shape[0]
    w = w.reshape(k, n_heads, width)
    return jnp.pad(w, ((0, 0), (0, 0), (0, HEAD_PAD - width))).reshape(k, n_heads * HEAD_PAD)


def _pad_row(g, lo=0):
    return jnp.pad(g, (lo, HEAD_PAD - lo - g.shape[0])).reshape(1, HEAD_PAD)


def _layer(x, mod, cs, p):
    B, S, D = x.shape
    w_in = p["w_in"]
    o = 0
    cols = {}
    for name, wdt in (("cq", MLA_Q_LORA), ("ckv", MLA_KV_LORA), ("kpe", MLA_ROPE),
                      ("qn", NSA_HEADS * NSA_HEAD), ("kc", KV_W), ("vc", KV_W), ("ks", KV_W),
                      ("vs", KV_W), ("kw", KV_W), ("vw", KV_W),
                      ("gn", NSA_HEADS * N_NSA_BRANCH), ("gm", 2 * D)):
        cols[name] = w_in[:, o:o + wdt]
        o += wdt
    G = NSA_KV_GROUPS
    n_gate = NSA_HEADS * N_NSA_BRANCH
    zc = lambda n: jnp.zeros((D, n), F32)
    wsm = jnp.concatenate([cols["gn"], zc(MLA_NOPE - n_gate), cols["kpe"], zc(LANES - MLA_QK)], axis=1)
    wkv6 = jnp.concatenate([cols["kc"], cols["vc"]] +
                           [_pad_heads(cols[k], G, NSA_HEAD) for k in ("ks", "vs", "kw", "vw")], axis=1)
    wkvb = p["mla_w_kv_b"].reshape(MLA_KV_LORA, MLA_HEADS, MLA_NOPE + MLA_V)
    wkvb = jnp.concatenate([_pad_heads(wkvb[:, :, :MLA_NOPE].reshape(MLA_KV_LORA, -1), MLA_HEADS, MLA_NOPE),
                            _pad_heads(wkvb[:, :, MLA_NOPE:].reshape(MLA_KV_LORA, -1), MLA_HEADS, MLA_V)], axis=1)
    bf = lambda w: w.astype(BF16)
    row = lambda g: g.reshape(1, -1)
    weights = tuple(bf(w) for w in (cols["cq"], cols["ckv"], wsm, _pad_heads(cols["qn"], NSA_HEADS, NSA_HEAD),
                                    wkv6, cols["gm"],
                                    _pad_heads(p["mla_w_q_b"], MLA_HEADS, MLA_QK), wkvb))
    vone = jnp.tile(jnp.zeros((1, HEAD_PAD), F32).at[0, MLA_V].set(1.0), (1, MLA_HEADS))
    rows = (row(p["mla_q_a_gain"]), row(p["mla_kv_a_gain"]),
            _pad_row(p["mla_q_gain"]), _pad_row(p["mla_k_gain"][:MLA_NOPE]),
            _pad_row(p["mla_k_gain"][MLA_NOPE:], MLA_NOPE),
            _pad_row(p["nsa_q_gain"]), _pad_row(p["nsa_ks_gain"]), _pad_row(p["nsa_kw_gain"]), vone)
    texp, trow = _rope_expansion()
    (qm, km, vm, qn, ks, kw, vs, vw, kc_raw, vc_raw, gn, gm) = _inproj(
        x, mod, cs, (row(p["norm1_gain"]), texp, trow), weights, rows)

    n_chunk = S // CMP_STRIDE
    half = CMP_STRIDE * NSA_HEAD
    ck = kc_raw.reshape(B, G, n_chunk, half)
    cv = vc_raw.reshape(B, G, n_chunk, half)
    cs_end = cs[:, CMP_LEN - 1::CMP_STRIDE]
    cs_end = jnp.pad(cs_end, ((0, 0), (0, n_chunk - cs_end.shape[1]), (0, 0)))
    w2k = jnp.pad(p["cmp_w2_k"], ((0, 0), (0, HEAD_PAD - NSA_HEAD)))
    kc, vc = _compress(ck, cv, p["cmp_pos_k"].reshape(2, half), p["cmp_pos_v"].reshape(2, half),
                       bf(p["cmp_w1_k"]), bf(w2k), bf(p["cmp_w1_v"]), bf(p["cmp_w2_v"]),
                       _pad_row(p["nsa_kc_gain"]), cs_end, texp, trow)

    ovt, gexp, dbias, wbias = _mask_tables(S)
    o_nsa = _nsa_attention(qn, kc, vc, ks, vs, kw, vw, gn, ovt, gexp, dbias, wbias)
    o_mla = _mla_attention(qm, km, vm, dbias)

    return _out_ffn(x, o_mla, o_nsa, gm, mod, row(p["norm2_gain"]),
                    bf(p["w_o_mla"]), bf(p["w_o_nsa"]), bf(p["w_out"]),
                    bf(p["ffn_w_gate"]), bf(p["ffn_w_up"]), bf(p["ffn_w_down"]))


def kernel(x, c, positions, ada_w, ada_b, norm1_gain, w_in, mla_q_a_gain, mla_w_q_b, mla_kv_a_gain, mla_w_kv_b, mla_q_gain, mla_k_gain, nsa_q_gain, nsa_kc_gain, nsa_ks_gain, nsa_kw_gain, cmp_pos_k, cmp_w1_k, cmp_w2_k, cmp_pos_v, cmp_w1_v, cmp_w2_v, w_o_mla, w_o_nsa, w_out, norm2_gain, ffn_w_gate, ffn_w_up, ffn_w_down):
    params = dict(norm1_gain=norm1_gain, w_in=w_in, mla_q_a_gain=mla_q_a_gain, mla_w_q_b=mla_w_q_b,
                  mla_kv_a_gain=mla_kv_a_gain, mla_w_kv_b=mla_w_kv_b, mla_q_gain=mla_q_gain,
                  mla_k_gain=mla_k_gain, nsa_q_gain=nsa_q_gain, nsa_kc_gain=nsa_kc_gain,
                  nsa_ks_gain=nsa_ks_gain, nsa_kw_gain=nsa_kw_gain, cmp_pos_k=cmp_pos_k,
                  cmp_w1_k=cmp_w1_k, cmp_w2_k=cmp_w2_k, cmp_pos_v=cmp_pos_v, cmp_w1_v=cmp_w1_v,
                  cmp_w2_v=cmp_w2_v, w_o_mla=w_o_mla, w_o_nsa=w_o_nsa, w_out=w_out,
                  norm2_gain=norm2_gain, ffn_w_gate=ffn_w_gate, ffn_w_up=ffn_w_up, ffn_w_down=ffn_w_down)
    B = x.shape[0]
    inv_m = ROPE_THETA ** (-jnp.arange(0, MLA_ROPE, 2, dtype=F32) / MLA_ROPE)
    inv_n = ROPE_THETA ** (-jnp.arange(0, NSA_ROT, 2, dtype=F32) / NSA_ROT)
    n_unused = N_FREQ - inv_m.shape[0] - inv_n.shape[0]
    cs = _rope_tables(positions, jnp.concatenate([inv_m, inv_n, jnp.zeros((n_unused,), F32)]))
    depth = ada_w.shape[0]
    for l in range(depth):
        mod = _ada(c, ada_w[l], ada_b[l]).reshape(B, N_MOD, D_MODEL)
        x = _layer(x, mod, cs, {k: v[l] for k, v in params.items()})
    return x
```

```python
import numpy as np
import jax
import jax.numpy as jnp
from jax import lax
from jax.experimental import pallas as pl
from jax.experimental.pallas import tpu as pltpu

F32 = jnp.float32
BF16 = jnp.bfloat16

D_MODEL = 1024
ROPE_THETA = 500000.0
EPS = 1e-6
NEG = -1e30

MLA_HEADS = 8
MLA_NOPE = 64
MLA_ROPE = 32
MLA_QK = MLA_NOPE + MLA_ROPE
MLA_V = 64
MLA_Q_LORA = 768
MLA_KV_LORA = 256

NSA_HEADS = 8
NSA_KV_GROUPS = 2
NSA_REP = NSA_HEADS // NSA_KV_GROUPS
NSA_HEAD = 64
NSA_ROT = NSA_HEAD // 4
CMP_LEN = 32
CMP_STRIDE = 16
CMP_HIDDEN = 256
SEL_LEN = 64
SEL_TOP = 8
WINDOW = 256
N_NSA_BRANCH = 3
FORCE_BONUS = 1e4
KV_W = NSA_KV_GROUPS * NSA_HEAD

D_FF = -(-8 * D_MODEL // (3 * 256)) * 256
N_MOD = 6
LANES = 128
HEAD_PAD = LANES
N_FREQ = 32

TM_IN = 256
TQ_ATT = 256
TM_OUT = 512
FF_CHUNK = D_FF // 2
ATT_LOOKAHEAD = 2
VMEM_LIMIT = 56 * 1024 * 1024


def _dot(a, b):
    return jnp.dot(a, b, preferred_element_type=F32)


def _dot_nt(a, b):
    return lax.dot_general(a, b, (((1,), (1,)), ((), ())), preferred_element_type=F32)


def _split_hilo(a):
    hi = a.astype(BF16)
    return hi, (a - hi.astype(F32)).astype(BF16)


def _dot_hilo(a, m):
    hi, lo = _split_hilo(a)
    return _dot(hi, m) + _dot(lo, m)


def _sigmoid(v):
    return 1.0 / (1.0 + jnp.exp(-v))


def _rms(v, n):
    return v * lax.rsqrt(jnp.sum(v * v, axis=-1, keepdims=True) * (1.0 / n) + EPS)


def _rope(v, cos_v, sin_v, lo, half):
    lane = lax.broadcasted_iota(jnp.int32, v.shape, 1)
    is_x1 = (lane >= lo) & (lane < lo + half)
    rot = jnp.where(is_x1, pltpu.roll(v, LANES - half, 1), pltpu.roll(v, half, 1))
    return v * cos_v + rot * sin_v


def _rope_multipliers(cs, texp_ref, trow_ref):
    tabs = _dot_hilo(cs, texp_ref[...]) + trow_ref[...]
    return tuple(tabs[:, LANES * i:LANES * (i + 1)] for i in range(4))


def _const_spec(shape):
    nd = len(shape)
    return pl.BlockSpec(shape, lambda *_: (0,) * nd)


def _rowmax(s):
    return jnp.max(s, axis=-1, keepdims=True)


def _attention_scores(q, k_ref, kmax, dbias):
    tq = q.shape[0]
    k0 = kmax - tq
    s_d = _dot_nt(q, k_ref(k0, kmax)) + dbias
    m = _rowmax(s_d)
    s_m = None
    if k0 > 0:
        s_m = _dot_nt(q, k_ref(0, k0))
        m = jnp.maximum(m, _rowmax(s_m))
    return s_m, s_d, m


def _attention_values(scores, v_ref, kmax):
    s_m, s_d, m = scores
    k0 = kmax - s_d.shape[0]
    acc = _dot(jnp.exp(s_d - m).astype(BF16), v_ref(k0, kmax))
    if s_m is not None:
        acc = acc + _dot(jnp.exp(s_m - m).astype(BF16), v_ref(0, k0))
    return acc[:, :NSA_HEAD] / acc[:, NSA_HEAD:NSA_HEAD + 1]


def _rope_kernel(pos_ref, inv_ref, cs_ref):
    ang = pos_ref[0].astype(F32) * inv_ref[...]
    nf, S = ang.shape
    rows = jnp.concatenate([jnp.cos(ang), jnp.sin(ang), jnp.zeros((LANES - 2 * nf, S), F32)], axis=0)
    cs_ref[0] = rows.T


def _rope_tables(positions, inv):
    B, S = positions.shape
    nf = inv.shape[0]
    return pl.pallas_call(
        _rope_kernel,
        grid=(B,),
        in_specs=[pl.BlockSpec((1, 1, S), lambda b: (b, 0, 0)),
                  _const_spec((nf, 1))],
        out_specs=pl.BlockSpec((1, S, LANES), lambda b: (b, 0, 0)),
        out_shape=jax.ShapeDtypeStruct((B, S, LANES), F32),
        name="rope_tables",
    )(positions.reshape(B, 1, S), inv.reshape(nf, 1))


def _ada_kernel(c_ref, w_ref, b_ref, o_ref):
    c = c_ref[...]
    sc = c * _sigmoid(c)
    o_ref[...] = jnp.dot(sc, w_ref[...], preferred_element_type=F32,
                         precision=lax.Precision.HIGHEST) + b_ref[...]


def _ada(c, w, b):
    B, D = c.shape
    N = w.shape[1]
    tn = D_MODEL
    return pl.pallas_call(
        _ada_kernel,
        grid=(N // tn,),
        in_specs=[_const_spec((B, D)),
                  pl.BlockSpec((D, tn), lambda j: (0, j)),
                  pl.BlockSpec((1, tn), lambda j: (0, j))],
        out_specs=pl.BlockSpec((B, tn), lambda j: (0, j)),
        out_shape=jax.ShapeDtypeStruct((B, N), F32),
        name="ada_mod",
    )(c, w, b.reshape(1, N))


def _inproj_kernel(x_ref, mod_ref, cs_ref, g1_ref, texp_ref, trow_ref,
                   wcq_ref, wckv_ref, wsm_ref, wqn_ref, wkv6_ref, wgm_ref,
                   qag_ref, wqb_ref, kvag_ref, wkvb_ref,
                   mqg_ref, mkn_ref, mkr_ref, nqg_ref, nksg_ref, nkwg_ref, vone_ref,
                   qm_ref, km_ref, vm_ref, qn_ref, ks_ref, kw_ref, vs_ref, vw_ref,
                   kc_ref, vc_ref, gn_ref, gm_ref):
    x = x_ref[0]
    tm = x.shape[0]
    mod = mod_ref[0]
    sh1, sc1 = mod[0:1], mod[1:2]
    h = _rms(x, D_MODEL) * g1_ref[...] * (1.0 + sc1) + sh1
    hb = h.astype(BF16)

    cos_m, sin_m, cos_n, sin_n = _rope_multipliers(cs_ref[0], texp_ref, trow_ref)
    lane = lax.broadcasted_iota(jnp.int32, (tm, LANES), 1)
    blk = lambda a, i: a[:, HEAD_PAD * i:HEAD_PAD * (i + 1)]
    hm = MLA_ROPE // 2
    hn = NSA_ROT // 2

    cq = _dot(hb, wcq_ref[...])
    cqn = (_rms(cq, MLA_Q_LORA) * qag_ref[...]).astype(BF16)
    q = _dot(cqn, wqb_ref[...])
    mqg = mqg_ref[...]
    m_scale = MLA_QK ** -0.5
    for hd in range(MLA_HEADS):
        qh = _rope(_rms(blk(q, hd), MLA_QK) * mqg, cos_m, sin_m, MLA_NOPE, hm) * m_scale
        qm_ref[0, :, HEAD_PAD * hd:HEAD_PAD * (hd + 1)] = qh.astype(BF16)

    zs = _dot(hb, wsm_ref[...])
    kpe = jnp.where((lane >= MLA_NOPE) & (lane < MLA_QK), zs, 0.0)
    kpe_ss = jnp.sum(kpe * kpe, axis=-1, keepdims=True)
    kr = _rope(kpe * mkr_ref[...], cos_m, sin_m, MLA_NOPE, hm)
    ckv = _dot(hb, wckv_ref[...])
    ckvn = (_rms(ckv, MLA_KV_LORA) * kvag_ref[...]).astype(BF16)
    kv = _dot(ckvn, wkvb_ref[...])
    mkn = mkn_ref[...]
    for hd in range(MLA_HEADS):
        kn = blk(kv, hd)
        inv = lax.rsqrt((jnp.sum(kn * kn, axis=-1, keepdims=True) + kpe_ss) * (1.0 / MLA_QK) + EPS)
        km_ref[0, :, HEAD_PAD * hd:HEAD_PAD * (hd + 1)] = ((kn * mkn + kr) * inv).astype(BF16)
    vm_ref[0] = (kv[:, MLA_HEADS * HEAD_PAD:] + vone_ref[...]).astype(BF16)

    qn = _dot(hb, wqn_ref[...])
    nqg = nqg_ref[...]
    n_scale = NSA_HEAD ** -0.5
    for hd in range(NSA_HEADS):
        qh = _rope(_rms(blk(qn, hd), NSA_HEAD) * nqg, cos_n, sin_n, 0, hn) * n_scale
        qn_ref[0, :, HEAD_PAD * hd:HEAD_PAD * (hd + 1)] = qh.astype(BF16)

    kv6 = _dot(hb, wkv6_ref[...])
    tok = pl.program_id(1) * tm + lax.broadcasted_iota(jnp.int32, (tm, 1), 0)
    sblk = lax.shift_right_logical(tok, SEL_LEN.bit_length() - 1)
    ind = jnp.where(lane - NSA_HEAD == sblk, NEG, 0.0)
    vone = vone_ref[:, 0:HEAD_PAD]
    nksg, nkwg = nksg_ref[...], nkwg_ref[...]
    kcb, vcb = blk(kv6, 0), blk(kv6, 1)
    for g in range(NSA_KV_GROUPS):
        kc_ref[0, g] = kcb[:, NSA_HEAD * g:NSA_HEAD * (g + 1)]
        vc_ref[0, g] = vcb[:, NSA_HEAD * g:NSA_HEAD * (g + 1)]
        ks = _rope(_rms(blk(kv6, 2 + g), NSA_HEAD) * nksg, cos_n, sin_n, 0, hn)
        ks_ref[0, g] = (ks + ind).astype(BF16)
        vs_ref[0, g] = (blk(kv6, 4 + g) + vone).astype(BF16)
        kw = _rope(_rms(blk(kv6, 6 + g), NSA_HEAD) * nkwg, cos_n, sin_n, 0, hn)
        kw_ref[0, g] = kw.astype(BF16)
        vw_ref[0, g] = (blk(kv6, 8 + g) + vone).astype(BF16)

    gn_ref[0] = _sigmoid(zs)
    gm_ref[0] = _sigmoid(_dot(hb, wgm_ref[...])).astype(BF16)


def _inproj(x, mod, cs, consts, weights, rows):
    B, S, D = x.shape
    tm = TM_IN
    tok = lambda w: pl.BlockSpec((1, tm, w), lambda b, i: (b, i, 0))
    head = lambda n, w: pl.BlockSpec((1, n, tm, w), lambda b, i: (b, 0, i, 0))
    operands = list(consts) + list(weights[:6]) + [rows[0], weights[6], rows[1], weights[7]] + list(rows[2:])
    in_specs = [tok(D), pl.BlockSpec((1, N_MOD, D), lambda b, i: (b, 0, 0)), tok(LANES)]
    in_specs += [_const_spec(a.shape) for a in operands]
    G = NSA_KV_GROUPS
    sds = jax.ShapeDtypeStruct
    wide = MLA_HEADS * HEAD_PAD
    outs = [
        (tok(wide), sds((B, S, wide), BF16)),
        (tok(wide), sds((B, S, wide), BF16)),
        (tok(wide), sds((B, S, wide), BF16)),
        (tok(wide), sds((B, S, wide), BF16)),
        (head(G, HEAD_PAD), sds((B, G, S, HEAD_PAD), BF16)),
        (head(G, HEAD_PAD), sds((B, G, S, HEAD_PAD), BF16)),
        (head(G, HEAD_PAD), sds((B, G, S, HEAD_PAD), BF16)),
        (head(G, HEAD_PAD), sds((B, G, S, HEAD_PAD), BF16)),
        (head(G, NSA_HEAD), sds((B, G, S, NSA_HEAD), F32)),
        (head(G, NSA_HEAD), sds((B, G, S, NSA_HEAD), F32)),
        (tok(LANES), sds((B, S, LANES), F32)),
        (tok(2 * D), sds((B, S, 2 * D), BF16)),
    ]
    return pl.pallas_call(
        _inproj_kernel,
        grid=(B, S // tm),
        in_specs=in_specs,
        out_specs=[o[0] for o in outs],
        out_shape=[o[1] for o in outs],
        compiler_params=pltpu.CompilerParams(dimension_semantics=("arbitrary", "arbitrary"),
                                             vmem_limit_bytes=VMEM_LIMIT),
        name="inproj_prep",
    )(x, mod, cs, *operands)


def _compress_kernel(ck_ref, cv_ref, pk_ref, pv_ref, w1k_ref, w2k_ref, w1v_ref, w2v_ref,
                     kcg_ref, cs_ref, texp_ref, trow_ref, kc_ref, vc_ref):
    half = CMP_STRIDE * NSA_HEAD
    n = ck_ref.shape[2]

    def compress(chunk, pos, w1_ref, w2_ref):
        a = _dot((chunk + pos[0:1]).astype(BF16), w1_ref[0:half, :])
        b = _dot((chunk + pos[1:2]).astype(BF16), w1_ref[half:2 * half, :])
        hid = a + pltpu.roll(b, n - 1, 0)
        hid = hid * _sigmoid(hid)
        return _dot(hid.astype(BF16), w2_ref[...])

    kc = compress(ck_ref[0, 0], pk_ref[...], w1k_ref, w2k_ref)
    _, _, cos_n, sin_n = _rope_multipliers(cs_ref[0], texp_ref, trow_ref)
    kc = _rope(_rms(kc, NSA_HEAD) * kcg_ref[...], cos_n, sin_n, 0, NSA_ROT // 2)
    kc_ref[0, 0] = kc.astype(BF16)
    vc_ref[0, 0] = compress(cv_ref[0, 0], pv_ref[...], w1v_ref, w2v_ref).astype(BF16)


def _compress(ck, cv, pk, pv, w1k, w2k, w1v, w2v, kcg, cs_end, texp, trow):
    B, G, n, w = ck.shape
    blk = pl.BlockSpec((1, 1, n, w), lambda b, g: (b, g, 0, 0))
    oblk = lambda wd: pl.BlockSpec((1, 1, n, wd), lambda b, g: (b, g, 0, 0))
    consts = (pk, pv, w1k, w2k, w1v, w2v, kcg)
    return pl.pallas_call(
        _compress_kernel,
        grid=(B, G),
        in_specs=[blk, blk] + [_const_spec(a.shape) for a in consts] +
                 [pl.BlockSpec((1, n, LANES), lambda b, g: (b, 0, 0)),
                  _const_spec(texp.shape), _const_spec(trow.shape)],
        out_specs=[oblk(HEAD_PAD), oblk(NSA_HEAD)],
        out_shape=[jax.ShapeDtypeStruct((B, G, n, HEAD_PAD), BF16),
                   jax.ShapeDtypeStruct((B, G, n, NSA_HEAD), BF16)],
        name="nsa_compress",
    )(ck, cv, *consts, cs_end, texp, trow)


def _nsa_kernel(q_ref, kc_ref, vc_ref, ks_ref, vs_ref, kw_ref, vw_ref, gn_ref,
                ovt_ref, gexp_ref, dbias_ref, wbias_ref, o_ref, imp_ref):
    tq = dbias_ref.shape[0]
    S = q_ref.shape[1]
    R = NSA_REP
    M = R * tq
    n_sel = imp_ref.shape[0]
    ncp = kc_ref.shape[2]
    span = WINDOW + tq
    grp = pl.program_id(1)
    dbias = dbias_ref[...]
    kf = lambda a, b: ks_ref[0, 0, a:b, :]
    vf = lambda a, b: vs_ref[0, 0, a:b, :]
    row = lax.broadcasted_iota(jnp.int32, (M, 1), 0)
    n_idx = lax.broadcasted_iota(jnp.int32, (M, ncp), 1)
    j = lax.broadcasted_iota(jnp.int32, (n_sel, tq), 0)
    head_q = lambda i, r: q_ref[0, i * tq:(i + 1) * tq, HEAD_PAD * r:HEAD_PAD * (r + 1)]
    tile_q = lambda i: jnp.concatenate([head_q(i, r) for r in range(R)], axis=0)
    tiles = {}

    def compressed_and_select(i):
        q0 = i * tq
        t = q0 + jnp.bitwise_and(row, tq - 1)
        s = _dot_nt(tile_q(i), kc_ref[0, 0])
        valid = (n_idx * CMP_STRIDE + (CMP_LEN - 1)) <= t
        sm = jnp.where(valid, s, NEG)
        e = jnp.where(valid, jnp.exp(sm - _rowmax(sm)), 0.0)
        den = jnp.sum(e, axis=-1, keepdims=True)
        p_c = e / jnp.where(den > 0.0, den, 1.0)
        o_c = _dot(p_c.astype(BF16), vc_ref[0, 0])
        psum = p_c[0:tq]
        for r in range(1, R):
            psum = psum + p_c[r * tq:(r + 1) * tq]
        hi, lo = _split_hilo(psum.T)
        imp = (_dot(ovt_ref[...], hi) + _dot(ovt_ref[...], lo))[0:n_sel]
        cur = lax.shift_right_logical(q0 + lax.broadcasted_iota(jnp.int32, (1, tq), 1),
                                      SEL_LEN.bit_length() - 1)
        forced = (j == 0) | (j == cur) | (j == cur - 1)
        imp = jnp.where(forced, imp + FORCE_BONUS, imp)
        imp = jnp.where(j <= cur, imp, NEG)
        imp_ref[...] = imp
        cnt = jnp.zeros((n_sel, tq), F32)
        for jj in range(n_sel):
            other = imp_ref[jj:jj + 1, :]
            beats = (other > imp) | ((other == imp) & (j > jj))
            cnt = cnt + jnp.where(beats, 1.0, 0.0)
        nsel = jnp.where((cnt < float(SEL_TOP)) & (j <= cur), 0.0, 1.0)
        nsel = jnp.concatenate([jnp.zeros((NSA_HEAD, tq), F32), nsel,
                                jnp.zeros((LANES - NSA_HEAD - n_sel, tq), F32)], axis=0).T.astype(BF16)
        tiles[i] = dict(o_c=o_c, nsel=nsel, o_s=[])

    def scores(u):
        kind, i, r = u
        if kind == "cmp":
            compressed_and_select(i)
            return None
        if kind == "sel":
            return _attention_scores(head_q(i, r) + tiles[i]["nsel"], kf, (i + 1) * tq, dbias)
        w0 = max(i * tq - WINDOW, 0)
        wb = wbias_ref[min(i, 1)]
        sw = _dot_nt(tile_q(i), kw_ref[0, 0, w0:w0 + span, :])
        sw = jnp.concatenate([sw[r * tq:(r + 1) * tq] + wb for r in range(R)], axis=0)
        return sw, _rowmax(sw)

    def values(u, sc):
        kind, i, r = u
        if kind == "sel":
            tiles[i]["o_s"].append(_attention_values(sc, vf, (i + 1) * tq))
        elif kind == "win":
            sw, mw = sc
            w0 = max(i * tq - WINDOW, 0)
            acc_w = _dot(jnp.exp(sw - mw).astype(BF16), vw_ref[0, 0, w0:w0 + span, :])
            o_w = acc_w[:, :NSA_HEAD] / acc_w[:, NSA_HEAD:NSA_HEAD + 1]
            tile = tiles.pop(i)
            g_hi, g_lo = _split_hilo(gn_ref[0, i * tq:(i + 1) * tq, :])
            wide = lambda o_b: jnp.concatenate([o_b[r * tq:(r + 1) * tq] for r in range(R)], axis=-1)
            branches = (wide(tile["o_c"]), jnp.concatenate(tile["o_s"], axis=-1), wide(o_w))
            out = None
            for br, o_b in enumerate(branches):
                gate = _dot(g_hi, gexp_ref[grp, br]) + _dot(g_lo, gexp_ref[grp, br])
                out = gate * o_b if out is None else out + gate * o_b
            o_ref[0, i * tq:(i + 1) * tq, :] = out.astype(BF16)

    units = []
    for i in range(S // tq):
        units += [("cmp", i, 0)] + [("sel", i, r) for r in range(R)] + [("win", i, 0)]
    pending = [scores(u) for u in units[:ATT_LOOKAHEAD]]
    for n, u in enumerate(units):
        if n + ATT_LOOKAHEAD < len(units):
            pending.append(scores(units[n + ATT_LOOKAHEAD]))
        values(u, pending.pop(0))


def _nsa_attention(qn, kc, vc, ks, vs, kw, vw, gn, ovt, gexp, dbias, wbias):
    B, S, _ = qn.shape
    G, R, Dh = NSA_KV_GROUPS, NSA_REP, NSA_HEAD
    ncp = kc.shape[2]
    full = pl.BlockSpec((1, 1, S, HEAD_PAD), lambda b, g: (b, g, 0, 0))
    cmp_spec = lambda w: pl.BlockSpec((1, 1, ncp, w), lambda b, g: (b, g, 0, 0))
    return pl.pallas_call(
        _nsa_kernel,
        grid=(B, G),
        in_specs=[pl.BlockSpec((1, S, R * HEAD_PAD), lambda b, g: (b, 0, g)),
                  cmp_spec(HEAD_PAD), cmp_spec(Dh), full, full, full, full,
                  pl.BlockSpec((1, S, LANES), lambda b, g: (b, 0, 0)),
                  _const_spec(ovt.shape), _const_spec(gexp.shape),
                  _const_spec(dbias.shape), _const_spec(wbias.shape)],
        out_specs=pl.BlockSpec((1, S, R * Dh), lambda b, g: (b, 0, g)),
        out_shape=jax.ShapeDtypeStruct((B, S, G * R * Dh), BF16),
        scratch_shapes=[pltpu.VMEM((S // SEL_LEN, TQ_ATT), F32)],
        compiler_params=pltpu.CompilerParams(dimension_semantics=("arbitrary",) * 2,
                                             vmem_limit_bytes=VMEM_LIMIT),
        name="nsa_attention",
    )(qn, kc, vc, ks, vs, kw, vw, gn, ovt, gexp, dbias, wbias)


def _mla_kernel(q_ref, k_ref, v_ref, dbias_ref, o_ref):
    tq = dbias_ref.shape[0]
    S = q_ref.shape[1]
    dbias = dbias_ref[...]
    units = [(i, hh) for i in range(S // tq) for hh in range(2)]

    def scores(u):
        i, hh = u
        cols = slice(HEAD_PAD * hh, HEAD_PAD * (hh + 1))
        q = q_ref[0, i * tq:(i + 1) * tq, cols]
        return _attention_scores(q, lambda a, b: k_ref[0, a:b, cols], (i + 1) * tq, dbias)

    def values(u, sc):
        i, hh = u
        cols = slice(HEAD_PAD * hh, HEAD_PAD * (hh + 1))
        o = _attention_values(sc, lambda a, b: v_ref[0, a:b, cols], (i + 1) * tq)
        o_ref[0, i * tq:(i + 1) * tq, MLA_V * hh:MLA_V * (hh + 1)] = o.astype(BF16)

    pending = [scores(u) for u in units[:ATT_LOOKAHEAD]]
    for n, u in enumerate(units):
        if n + ATT_LOOKAHEAD < len(units):
            pending.append(scores(units[n + ATT_LOOKAHEAD]))
        values(u, pending.pop(0))


def _mla_attention(qm, km, vm, dbias):
    B, S, _ = qm.shape
    pair = pl.BlockSpec((1, S, 2 * HEAD_PAD), lambda b, h: (b, 0, h))
    return pl.pallas_call(
        _mla_kernel,
        grid=(B, MLA_HEADS // 2),
        in_specs=[pair, pair, pair, _const_spec(dbias.shape)],
        out_specs=pl.BlockSpec((1, S, 2 * MLA_V), lambda b, h: (b, 0, h)),
        out_shape=jax.ShapeDtypeStruct((B, S, MLA_HEADS * MLA_V), BF16),
        compiler_params=pltpu.CompilerParams(dimension_semantics=("arbitrary",) * 2,
                                             vmem_limit_bytes=VMEM_LIMIT),
        name="mla_attention",
    )(qm, km, vm, dbias)


def _out_ffn_kernel(x_ref, om_ref, on_ref, gm_ref, mod_ref, g2_ref,
                    wom_ref, won_ref, wout_ref, wg_ref, wu_ref, wd_ref, o_ref):
    x = x_ref[0]
    mod = mod_ref[0]
    gt1, sh2, sc2, gt2 = mod[2:3], mod[3:4], mod[4:5], mod[5:6]
    ym = _dot(om_ref[0], wom_ref[...])
    yn = _dot(on_ref[0], won_ref[...])
    merged = gm_ref[0, :, :D_MODEL] * ym + gm_ref[0, :, D_MODEL:] * yn
    x1 = x + gt1 * _dot(merged.astype(BF16), wout_ref[...])
    h2 = (_rms(x1, D_MODEL) * g2_ref[...] * (1.0 + sc2) + sh2).astype(BF16)
    acc = jnp.zeros(x.shape, F32)
    for c in range(D_FF // FF_CHUNK):
        sl = slice(c * FF_CHUNK, (c + 1) * FF_CHUNK)
        g = _dot(h2, wg_ref[:, sl])
        u = _dot(h2, wu_ref[:, sl])
        a = (g * _sigmoid(g) * u).astype(BF16)
        acc = acc + _dot(a, wd_ref[sl, :])
    o_ref[0] = x1 + gt2 * acc


def _out_ffn(x, om, on, gm, mod, g2, wom, won, wout, wg, wu, wd):
    B, S, D = x.shape
    tm = TM_OUT
    tok = lambda w: pl.BlockSpec((1, tm, w), lambda b, i: (b, i, 0))
    wspec = lambda w: pl.BlockSpec(w.shape, lambda b, i: (0, 0), pipeline_mode=pl.Buffered(1))
    return pl.pallas_call(
        _out_ffn_kernel,
        grid=(B, S // tm),
        in_specs=[tok(D), tok(om.shape[2]), tok(on.shape[2]), tok(2 * D),
                  pl.BlockSpec((1, N_MOD, D), lambda b, i: (b, 0, 0)),
                  _const_spec(g2.shape)] + [wspec(w) for w in (wom, won, wout, wg, wu, wd)],
        out_specs=tok(D),
        out_shape=jax.ShapeDtypeStruct((B, S, D), F32),
        compiler_params=pltpu.CompilerParams(dimension_semantics=("arbitrary", "arbitrary"),
                                             vmem_limit_bytes=VMEM_LIMIT),
        name="out_ffn",
    )(x, om, on, gm, mod, g2, wom, won, wout, wg, wu, wd)


def _rope_expansion():
    texp = np.zeros((LANES, 4 * LANES), np.float32)
    trow = np.zeros((1, 4 * LANES), np.float32)
    hm, hn = MLA_ROPE // 2, NSA_ROT // 2
    trow[0, 0:LANES] = 1.0
    trow[0, 2 * LANES:3 * LANES] = 1.0
    for i in range(hm):
        for off, sgn in ((MLA_NOPE + i, -1.0), (MLA_NOPE + hm + i, 1.0)):
            texp[i, off] = 1.0
            trow[0, off] = 0.0
            texp[N_FREQ + i, LANES + off] = sgn
    for i in range(hn):
        for off, sgn in ((i, -1.0), (hn + i, 1.0)):
            texp[hm + i, 2 * LANES + off] = 1.0
            trow[0, 2 * LANES + off] = 0.0
            texp[N_FREQ + hm + i, 3 * LANES + off] = sgn
    return jnp.asarray(texp, BF16), jnp.asarray(trow, F32)


def _mask_tables(S):
    tq = TQ_ATT
    n_chunk = S // CMP_STRIDE
    n_sel = S // SEL_LEN
    starts = np.arange(n_chunk) * CMP_STRIDE
    sel_start = np.arange(LANES) * SEL_LEN
    ovt = ((starts[None, :] < sel_start[:, None] + SEL_LEN) &
           (starts[None, :] + CMP_LEN > sel_start[:, None]) &
           (np.arange(n_chunk)[None, :] < n_chunk - 1) &
           (np.arange(LANES)[:, None] < n_sel))
    gcol = np.arange(LANES)[:, None]
    head = np.arange(NSA_REP * NSA_HEAD)[None, :] // NSA_HEAD
    gexp = np.stack([np.stack([gcol == (g * NSA_REP + head) * N_NSA_BRANCH + br
                               for br in range(N_NSA_BRANCH)]) for g in range(NSA_KV_GROUPS)])
    qi = np.arange(tq)[:, None]
    dbias = np.where(np.arange(tq)[None, :] <= qi, 0.0, NEG)
    kk = np.arange(WINDOW + tq)[None, :]
    band = lambda d: np.where((d >= 0) & (d < WINDOW), 0.0, NEG)
    wbias = np.stack([band(qi - kk), band(qi + WINDOW - kk)])
    return (jnp.asarray(ovt, BF16), jnp.asarray(gexp, BF16),
            jnp.asarray(dbias, F32), jnp.asarray(wbias, F32))


def _pad_heads(w, n_heads, width):
    k = w.shape[0]
    w = w.reshape(k, n_heads, width)
    return jnp.pad(w, ((0, 0), (0, 0), (0, HEAD_PAD - width))).reshape(k, n_heads * HEAD_PAD)


def _pad_row(g, lo=0):
    return jnp.pad(g, (lo, HEAD_PAD - lo - g.shape[0])).reshape(1, HEAD_PAD)


def _layer(x, mod, cs, p):
    B, S, D = x.shape
    w_in = p["w_in"]
    o = 0
    cols = {}
    for name, wdt in (("cq", MLA_Q_LORA), ("ckv", MLA_KV_LORA), ("kpe", MLA_ROPE),
                      ("qn", NSA_HEADS * NSA_HEAD), ("kc", KV_W), ("vc", KV_W), ("ks", KV_W),
                      ("vs", KV_W), ("kw", KV_W), ("vw", KV_W),
                      ("gn", NSA_HEADS * N_NSA_BRANCH), ("gm", 2 * D)):
        cols[name] = w_in[:, o:o + wdt]
        o += wdt
    G = NSA_KV_GROUPS
    n_gate = NSA_HEADS * N_NSA_BRANCH
    zc = lambda n: jnp.zeros((D, n), F32)
    wsm = jnp.concatenate([cols["gn"], zc(MLA_NOPE - n_gate), cols["kpe"], zc(LANES - MLA_QK)], axis=1)
    wkv6 = jnp.concatenate([cols["kc"], cols["vc"]] +
                           [_pad_heads(cols[k], G, NSA_HEAD) for k in ("ks", "vs", "kw", "vw")], axis=1)
    wkvb = p["mla_w_kv_b"].reshape(MLA_KV_LORA, MLA_HEADS, MLA_NOPE + MLA_V)
    wkvb = jnp.concatenate([_pad_heads(wkvb[:, :, :MLA_NOPE].reshape(MLA_KV_LORA, -1), MLA_HEADS, MLA_NOPE),
                            _pad_heads(wkvb[:, :, MLA_NOPE:].reshape(MLA_KV_LORA, -1), MLA_HEADS, MLA_V)], axis=1)
    bf = lambda w: w.astype(BF16)
    row = lambda g: g.reshape(1, -1)
    weights = tuple(bf(w) for w in (cols["cq"], cols["ckv"], wsm, _pad_heads(cols["qn"], NSA_HEADS, NSA_HEAD),
                                    wkv6, cols["gm"],
                                    _pad_heads(p["mla_w_q_b"], MLA_HEADS, MLA_QK), wkvb))
    vone = jnp.tile(jnp.zeros((1, HEAD_PAD), F32).at[0, MLA_V].set(1.0), (1, MLA_HEADS))
    rows = (row(p["mla_q_a_gain"]), row(p["mla_kv_a_gain"]),
            _pad_row(p["mla_q_gain"]), _pad_row(p["mla_k_gain"][:MLA_NOPE]),
            _pad_row(p["mla_k_gain"][MLA_NOPE:], MLA_NOPE),
            _pad_row(p["nsa_q_gain"]), _pad_row(p["nsa_ks_gain"]), _pad_row(p["nsa_kw_gain"]), vone)
    texp, trow = _rope_expansion()
    (qm, km, vm, qn, ks, kw, vs, vw, kc_raw, vc_raw, gn, gm) = _inproj(
        x, mod, cs, (row(p["norm1_gain"]), texp, trow), weights, rows)

    n_chunk = S // CMP_STRIDE
    half = CMP_STRIDE * NSA_HEAD
    ck = kc_raw.reshape(B, G, n_chunk, half)
    cv = vc_raw.reshape(B, G, n_chunk, half)
    cs_end = cs[:, CMP_LEN - 1::CMP_STRIDE]
    cs_end = jnp.pad(cs_end, ((0, 0), (0, n_chunk - cs_end.shape[1]), (0, 0)))
    w2k = jnp.pad(p["cmp_w2_k"], ((0, 0), (0, HEAD_PAD - NSA_HEAD)))
    kc, vc = _compress(ck, cv, p["cmp_pos_k"].reshape(2, half), p["cmp_pos_v"].reshape(2, half),
                       bf(p["cmp_w1_k"]), bf(w2k), bf(p["cmp_w1_v"]), bf(p["cmp_w2_v"]),
                       _pad_row(p["nsa_kc_gain"]), cs_end, texp, trow)

    ovt, gexp, dbias, wbias = _mask_tables(S)
    o_nsa = _nsa_attention(qn, kc, vc, ks, vs, kw, vw, gn, ovt, gexp, dbias, wbias)
    o_mla = _mla_attention(qm, km, vm, dbias)

    return _out_ffn(x, o_mla, o_nsa, gm, mod, row(p["norm2_gain"]),
                    bf(p["w_o_mla"]), bf(p["w_o_nsa"]), bf(p["w_out"]),
                    bf(p["ffn_w_gate"]), bf(p["ffn_w_up"]), bf(p["ffn_w_down"]))


def kernel(x, c, positions, ada_w, ada_b, norm1_gain, w_in, mla_q_a_gain, mla_w_q_b, mla_kv_a_gain, mla_w_kv_b, mla_q_gain, mla_k_gain, nsa_q_gain, nsa_kc_gain, nsa_ks_gain, nsa_kw_gain, cmp_pos_k, cmp_w1_k, cmp_w2_k, cmp_pos_v, cmp_w1_v, cmp_w2_v, w_o_mla, w_o_nsa, w_out, norm2_gain, ffn_w_gate, ffn_w_up, ffn_w_down):
    params = dict(norm1_gain=norm1_gain, w_in=w_in, mla_q_a_gain=mla_q_a_gain, mla_w_q_b=mla_w_q_b,
                  mla_kv_a_gain=mla_kv_a_gain, mla_w_kv_b=mla_w_kv_b, mla_q_gain=mla_q_gain,
                  mla_k_gain=mla_k_gain, nsa_q_gain=nsa_q_gain, nsa_kc_gain=nsa_kc_gain,
                  nsa_ks_gain=nsa_ks_gain, nsa_kw_gain=nsa_kw_gain, cmp_pos_k=cmp_pos_k,
                  cmp_w1_k=cmp_w1_k, cmp_w2_k=cmp_w2_k, cmp_pos_v=cmp_pos_v, cmp_w1_v=cmp_w1_v,
                  cmp_w2_v=cmp_w2_v, w_o_mla=w_o_mla, w_o_nsa=w_o_nsa, w_out=w_out,
                  norm2_gain=norm2_gain, ffn_w_gate=ffn_w_gate, ffn_w_up=ffn_w_up, ffn_w_down=ffn_w_down)
    B = x.shape[0]
    inv_m = ROPE_THETA ** (-jnp.arange(0, MLA_ROPE, 2, dtype=F32) / MLA_ROPE)
    inv_n = ROPE_THETA ** (-jnp.arange(0, NSA_ROT, 2, dtype=F32) / NSA_ROT)
    n_unused = N_FREQ - inv_m.shape[0] - inv_n.shape[0]
    cs = _rope_tables(positions, jnp.concatenate([inv_m, inv_n, jnp.zeros((n_unused,), F32)]))
    depth = ada_w.shape[0]
    for l in range(depth):
        mod = _ada(c, ada_w[l], ada_b[l]).reshape(B, N_MOD, D_MODEL)
        x = _layer(x, mod, cs, {k: v[l] for k, v in params.items()})
    return x
```

```python
import numpy as np
import jax
import jax.numpy as jnp
from jax import lax
from jax.experimental import pallas as pl
from jax.experimental.pallas import tpu as pltpu

F32 = jnp.float32
BF16 = jnp.bfloat16

D_MODEL = 1024
ROPE_THETA = 500000.0
EPS = 1e-6
NEG = -1e30

MLA_HEADS = 8
MLA_NOPE = 64
MLA_ROPE = 32
MLA_QK = MLA_NOPE + MLA_ROPE
MLA_V = 64
MLA_Q_LORA = 768
MLA_KV_LORA = 256

NSA_HEADS = 8
NSA_KV_GROUPS = 2
NSA_REP = NSA_HEADS // NSA_KV_GROUPS
NSA_HEAD = 64
NSA_ROT = NSA_HEAD // 4
CMP_LEN = 32
CMP_STRIDE = 16
CMP_HIDDEN = 256
SEL_LEN = 64
SEL_TOP = 8
WINDOW = 256
N_NSA_BRANCH = 3
FORCE_BONUS = 1e4
KV_W = NSA_KV_GROUPS * NSA_HEAD

D_FF = -(-8 * D_MODEL // (3 * 256)) * 256
N_MOD = 6
LANES = 128
HEAD_PAD = LANES
N_FREQ = 32

TM_IN = 256
TQ_ATT = 256
TM_OUT = 512
FF_CHUNK = D_FF // 2
ATT_LOOKAHEAD = 2
VMEM_LIMIT = 56 * 1024 * 1024


def _dot(a, b):
    return jnp.dot(a, b, preferred_element_type=F32)


def _dot_nt(a, b):
    return lax.dot_general(a, b, (((1,), (1,)), ((), ())), preferred_element_type=F32)


def _split_hilo(a):
    hi = a.astype(BF16)
    return hi, (a - hi.astype(F32)).astype(BF16)


def _dot_hilo(a, m):
    hi, lo = _split_hilo(a)
    return _dot(hi, m) + _dot(lo, m)


def _sigmoid(v):
    return 1.0 / (1.0 + jnp.exp(-v))


def _rms(v, n):
    return v * lax.rsqrt(jnp.sum(v * v, axis=-1, keepdims=True) * (1.0 / n) + EPS)


def _rope(v, cos_v, sin_v, lo, half):
    lane = lax.broadcasted_iota(jnp.int32, v.shape, 1)
    is_x1 = (lane >= lo) & (lane < lo + half)
    rot = jnp.where(is_x1, pltpu.roll(v, LANES - half, 1), pltpu.roll(v, half, 1))
    return v * cos_v + rot * sin_v


def _rope_multipliers(cs, texp_ref, trow_ref):
    tabs = _dot_hilo(cs, texp_ref[...]) + trow_ref[...]
    return tuple(tabs[:, LANES * i:LANES * (i + 1)] for i in range(4))


def _const_spec(shape):
    nd = len(shape)
    return pl.BlockSpec(shape, lambda *_: (0,) * nd)


def _rowmax(s):
    return jnp.max(s, axis=-1, keepdims=True)


def _attention_scores(q, k_ref, kmax, dbias):
    k0 = kmax - dbias.shape[1]
    s_d = _dot_nt(q, k_ref(k0, kmax)) + dbias
    m = _rowmax(s_d)
    s_m = None
    if k0 > 0:
        s_m = _dot_nt(q, k_ref(0, k0))
        m = jnp.maximum(m, _rowmax(s_m))
    return s_m, s_d, m


def _attention_values(scores, v_ref, kmax):
    s_m, s_d, m = scores
    k0 = kmax - s_d.shape[1]
    acc = _dot(jnp.exp(s_d - m).astype(BF16), v_ref(k0, kmax))
    if s_m is not None:
        acc = acc + _dot(jnp.exp(s_m - m).astype(BF16), v_ref(0, k0))
    return acc[:, :NSA_HEAD] / acc[:, NSA_HEAD:NSA_HEAD + 1]


def _rope_kernel(pos_ref, inv_ref, cs_ref):
    ang = pos_ref[0].astype(F32) * inv_ref[...]
    nf, S = ang.shape
    rows = jnp.concatenate([jnp.cos(ang), jnp.sin(ang), jnp.zeros((LANES - 2 * nf, S), F32)], axis=0)
    cs_ref[0] = rows.T


def _rope_tables(positions, inv):
    B, S = positions.shape
    nf = inv.shape[0]
    return pl.pallas_call(
        _rope_kernel,
        grid=(B,),
        in_specs=[pl.BlockSpec((1, 1, S), lambda b: (b, 0, 0)),
                  _const_spec((nf, 1))],
        out_specs=pl.BlockSpec((1, S, LANES), lambda b: (b, 0, 0)),
        out_shape=jax.ShapeDtypeStruct((B, S, LANES), F32),
        name="rope_tables",
    )(positions.reshape(B, 1, S), inv.reshape(nf, 1))


def _ada_kernel(c_ref, w_ref, b_ref, o_ref):
    c = c_ref[...]
    sc = c * _sigmoid(c)
    o_ref[...] = jnp.dot(sc, w_ref[...], preferred_element_type=F32,
                         precision=lax.Precision.HIGHEST) + b_ref[...]


def _ada(c, w, b):
    B, D = c.shape
    N = w.shape[1]
    tn = D_MODEL
    return pl.pallas_call(
        _ada_kernel,
        grid=(N // tn,),
        in_specs=[_const_spec((B, D)),
                  pl.BlockSpec((D, tn), lambda j: (0, j)),
                  pl.BlockSpec((1, tn), lambda j: (0, j))],
        out_specs=pl.BlockSpec((B, tn), lambda j: (0, j)),
        out_shape=jax.ShapeDtypeStruct((B, N), F32),
        name="ada_mod",
    )(c, w, b.reshape(1, N))


def _inproj_kernel(x_ref, mod_ref, cs_ref, g1_ref, texp_ref, trow_ref,
                   wcq_ref, wckv_ref, wsm_ref, wqn_ref, wkv6_ref, wgm_ref,
                   qag_ref, wqb_ref, kvag_ref, wkvb_ref,
                   mqg_ref, mkn_ref, mkr_ref, nqg_ref, nksg_ref, nkwg_ref, vone_ref,
                   qm_ref, km_ref, vm_ref, qn_ref, ks_ref, kw_ref, vs_ref, vw_ref,
                   kc_ref, vc_ref, gn_ref, gm_ref):
    x = x_ref[0]
    tm = x.shape[0]
    mod = mod_ref[0]
    sh1, sc1 = mod[0:1], mod[1:2]
    h = _rms(x, D_MODEL) * g1_ref[...] * (1.0 + sc1) + sh1
    hb = h.astype(BF16)

    cos_m, sin_m, cos_n, sin_n = _rope_multipliers(cs_ref[0], texp_ref, trow_ref)
    lane = lax.broadcasted_iota(jnp.int32, (tm, LANES), 1)
    blk = lambda a, i: a[:, HEAD_PAD * i:HEAD_PAD * (i + 1)]
    hm = MLA_ROPE // 2
    hn = NSA_ROT // 2

    cq = _dot(hb, wcq_ref[...])
    cqn = (_rms(cq, MLA_Q_LORA) * qag_ref[...]).astype(BF16)
    q = _dot(cqn, wqb_ref[...])
    mqg = mqg_ref[...]
    m_scale = MLA_QK ** -0.5
    for hd in range(MLA_HEADS):
        qh = _rope(_rms(blk(q, hd), MLA_QK) * mqg, cos_m, sin_m, MLA_NOPE, hm) * m_scale
        qm_ref[0, :, HEAD_PAD * hd:HEAD_PAD * (hd + 1)] = qh.astype(BF16)

    zs = _dot(hb, wsm_ref[...])
    kpe = jnp.where((lane >= MLA_NOPE) & (lane < MLA_QK), zs, 0.0)
    kpe_ss = jnp.sum(kpe * kpe, axis=-1, keepdims=True)
    kr = _rope(kpe * mkr_ref[...], cos_m, sin_m, MLA_NOPE, hm)
    ckv = _dot(hb, wckv_ref[...])
    ckvn = (_rms(ckv, MLA_KV_LORA) * kvag_ref[...]).astype(BF16)
    kv = _dot(ckvn, wkvb_ref[...])
    mkn = mkn_ref[...]
    for hd in range(MLA_HEADS):
        kn = blk(kv, hd)
        inv = lax.rsqrt((jnp.sum(kn * kn, axis=-1, keepdims=True) + kpe_ss) * (1.0 / MLA_QK) + EPS)
        km_ref[0, :, HEAD_PAD * hd:HEAD_PAD * (hd + 1)] = ((kn * mkn + kr) * inv).astype(BF16)
    vm_ref[0] = (kv[:, MLA_HEADS * HEAD_PAD:] + vone_ref[...]).astype(BF16)

    qn = _dot(hb, wqn_ref[...])
    nqg = nqg_ref[...]
    n_scale = NSA_HEAD ** -0.5
    for hd in range(NSA_HEADS):
        qh = _rope(_rms(blk(qn, hd), NSA_HEAD) * nqg, cos_n, sin_n, 0, hn) * n_scale
        qn_ref[0, :, HEAD_PAD * hd:HEAD_PAD * (hd + 1)] = qh.astype(BF16)

    kv6 = _dot(hb, wkv6_ref[...])
    tok = pl.program_id(1) * tm + lax.broadcasted_iota(jnp.int32, (tm, 1), 0)
    sblk = lax.shift_right_logical(tok, SEL_LEN.bit_length() - 1)
    ind = jnp.where(lane - NSA_HEAD == sblk, NEG, 0.0)
    vone = vone_ref[:, 0:HEAD_PAD]
    nksg, nkwg = nksg_ref[...], nkwg_ref[...]
    kcb, vcb = blk(kv6, 0), blk(kv6, 1)
    for g in range(NSA_KV_GROUPS):
        kc_ref[0, g] = kcb[:, NSA_HEAD * g:NSA_HEAD * (g + 1)]
        vc_ref[0, g] = vcb[:, NSA_HEAD * g:NSA_HEAD * (g + 1)]
        ks = _rope(_rms(blk(kv6, 2 + g), NSA_HEAD) * nksg, cos_n, sin_n, 0, hn)
        ks_ref[0, g] = (ks + ind).astype(BF16)
        vs_ref[0, g] = (blk(kv6, 4 + g) + vone).astype(BF16)
        kw = _rope(_rms(blk(kv6, 6 + g), NSA_HEAD) * nkwg, cos_n, sin_n, 0, hn)
        kw_ref[0, g] = kw.astype(BF16)
        vw_ref[0, g] = (blk(kv6, 8 + g) + vone).astype(BF16)

    gn_ref[0] = _sigmoid(zs)
    gm_ref[0] = _sigmoid(_dot(hb, wgm_ref[...])).astype(BF16)


def _inproj(x, mod, cs, consts, weights, rows):
    B, S, D = x.shape
    tm = TM_IN
    tok = lambda w: pl.BlockSpec((1, tm, w), lambda b, i: (b, i, 0))
    head = lambda n, w: pl.BlockSpec((1, n, tm, w), lambda b, i: (b, 0, i, 0))
    operands = list(consts) + list(weights[:6]) + [rows[0], weights[6], rows[1], weights[7]] + list(rows[2:])
    in_specs = [tok(D), pl.BlockSpec((1, N_MOD, D), lambda b, i: (b, 0, 0)), tok(LANES)]
    in_specs += [_const_spec(a.shape) for a in operands]
    G = NSA_KV_GROUPS
    sds = jax.ShapeDtypeStruct
    wide = MLA_HEADS * HEAD_PAD
    outs = [
        (tok(wide), sds((B, S, wide), BF16)),
        (tok(wide), sds((B, S, wide), BF16)),
        (tok(wide), sds((B, S, wide), BF16)),
        (tok(wide), sds((B, S, wide), BF16)),
        (head(G, HEAD_PAD), sds((B, G, S, HEAD_PAD), BF16)),
        (head(G, HEAD_PAD), sds((B, G, S, HEAD_PAD), BF16)),
        (head(G, HEAD_PAD), sds((B, G, S, HEAD_PAD), BF16)),
        (head(G, HEAD_PAD), sds((B, G, S, HEAD_PAD), BF16)),
        (head(G, NSA_HEAD), sds((B, G, S, NSA_HEAD), F32)),
        (head(G, NSA_HEAD), sds((B, G, S, NSA_HEAD), F32)),
        (tok(LANES), sds((B, S, LANES), F32)),
        (tok(2 * D), sds((B, S, 2 * D), BF16)),
    ]
    return pl.pallas_call(
        _inproj_kernel,
        grid=(B, S // tm),
        in_specs=in_specs,
        out_specs=[o[0] for o in outs],
        out_shape=[o[1] for o in outs],
        compiler_params=pltpu.CompilerParams(dimension_semantics=("arbitrary", "arbitrary"),
                                             vmem_limit_bytes=VMEM_LIMIT),
        name="inproj_prep",
    )(x, mod, cs, *operands)


def _compress_kernel(ck_ref, cv_ref, pk_ref, pv_ref, w1k_ref, w2k_ref, w1v_ref, w2v_ref,
                     kcg_ref, cs_ref, texp_ref, trow_ref, kc_ref, vc_ref):
    half = CMP_STRIDE * NSA_HEAD
    n = ck_ref.shape[2]

    def compress(chunk, pos, w1_ref, w2_ref):
        a = _dot((chunk + pos[0:1]).astype(BF16), w1_ref[0:half, :])
        b = _dot((chunk + pos[1:2]).astype(BF16), w1_ref[half:2 * half, :])
        hid = a + pltpu.roll(b, n - 1, 0)
        hid = hid * _sigmoid(hid)
        return _dot(hid.astype(BF16), w2_ref[...])

    kc = compress(ck_ref[0, 0], pk_ref[...], w1k_ref, w2k_ref)
    _, _, cos_n, sin_n = _rope_multipliers(cs_ref[0], texp_ref, trow_ref)
    kc = _rope(_rms(kc, NSA_HEAD) * kcg_ref[...], cos_n, sin_n, 0, NSA_ROT // 2)
    kc_ref[0, 0] = kc.astype(BF16)
    vc_ref[0, 0] = compress(cv_ref[0, 0], pv_ref[...], w1v_ref, w2v_ref).astype(BF16)


def _compress(ck, cv, pk, pv, w1k, w2k, w1v, w2v, kcg, cs_end, texp, trow):
    B, G, n, w = ck.shape
    blk = pl.BlockSpec((1, 1, n, w), lambda b, g: (b, g, 0, 0))
    oblk = lambda wd: pl.BlockSpec((1, 1, n, wd), lambda b, g: (b, g, 0, 0))
    consts = (pk, pv, w1k, w2k, w1v, w2v, kcg)
    return pl.pallas_call(
        _compress_kernel,
        grid=(B, G),
        in_specs=[blk, blk] + [_const_spec(a.shape) for a in consts] +
                 [pl.BlockSpec((1, n, LANES), lambda b, g: (b, 0, 0)),
                  _const_spec(texp.shape), _const_spec(trow.shape)],
        out_specs=[oblk(HEAD_PAD), oblk(NSA_HEAD)],
        out_shape=[jax.ShapeDtypeStruct((B, G, n, HEAD_PAD), BF16),
                   jax.ShapeDtypeStruct((B, G, n, NSA_HEAD), BF16)],
        name="nsa_compress",
    )(ck, cv, *consts, cs_end, texp, trow)


def _nsa_kernel(q_ref, kc_ref, vc_ref, ks_ref, vs_ref, kw_ref, vw_ref, gn_ref,
                ovt_ref, gexp_ref, dbias_ref, wbias_ref, o_ref, imp_ref):
    tq = dbias_ref.shape[0]
    S = q_ref.shape[1]
    R = NSA_REP
    M = R * tq
    n_sel = imp_ref.shape[0]
    ncp = kc_ref.shape[2]
    span = WINDOW + tq
    grp = pl.program_id(1)
    dbias = dbias_ref[...]
    kf = lambda a, b: ks_ref[0, 0, a:b, :]
    vf = lambda a, b: vs_ref[0, 0, a:b, :]
    row = lax.broadcasted_iota(jnp.int32, (M, 1), 0)
    n_idx = lax.broadcasted_iota(jnp.int32, (M, ncp), 1)
    j = lax.broadcasted_iota(jnp.int32, (n_sel, tq), 0)
    head_q = lambda i, r: q_ref[0, i * tq:(i + 1) * tq, HEAD_PAD * r:HEAD_PAD * (r + 1)]
    tile_q = lambda i: jnp.concatenate([head_q(i, r) for r in range(R)], axis=0)
    tiles = {}

    def compressed_and_select(i):
        q0 = i * tq
        t = q0 + jnp.bitwise_and(row, tq - 1)
        s = _dot_nt(tile_q(i), kc_ref[0, 0])
        valid = (n_idx * CMP_STRIDE + (CMP_LEN - 1)) <= t
        sm = jnp.where(valid, s, NEG)
        e = jnp.where(valid, jnp.exp(sm - _rowmax(sm)), 0.0)
        den = jnp.sum(e, axis=-1, keepdims=True)
        p_c = e / jnp.where(den > 0.0, den, 1.0)
        o_c = _dot(p_c.astype(BF16), vc_ref[0, 0])
        psum = p_c[0:tq]
        for r in range(1, R):
            psum = psum + p_c[r * tq:(r + 1) * tq]
        hi, lo = _split_hilo(psum.T)
        imp = (_dot(ovt_ref[...], hi) + _dot(ovt_ref[...], lo))[0:n_sel]
        cur = lax.shift_right_logical(q0 + lax.broadcasted_iota(jnp.int32, (1, tq), 1),
                                      SEL_LEN.bit_length() - 1)
        forced = (j == 0) | (j == cur) | (j == cur - 1)
        imp = jnp.where(forced, imp + FORCE_BONUS, imp)
        imp = jnp.where(j <= cur, imp, NEG)
        imp_ref[...] = imp
        cnt = jnp.zeros((n_sel, tq), F32)
        for jj in range(n_sel):
            other = imp_ref[jj:jj + 1, :]
            beats = (other > imp) | ((other == imp) & (j > jj))
            cnt = cnt + jnp.where(beats, 1.0, 0.0)
        nsel = jnp.where((cnt < float(SEL_TOP)) & (j <= cur), 0.0, 1.0)
        nsel = jnp.concatenate([jnp.zeros((NSA_HEAD, tq), F32), nsel,
                                jnp.zeros((LANES - NSA_HEAD - n_sel, tq), F32)], axis=0).T.astype(BF16)
        tiles[i] = dict(o_c=o_c, nsel=nsel, o_s=[], o_w=[])

    def scores(u):
        kind, i, r = u
        if kind == "cmp":
            compressed_and_select(i)
            return None
        if kind == "sel":
            return _attention_scores(head_q(i, r) + tiles[i]["nsel"], kf, (i + 1) * tq, dbias)
        w0 = max(i * tq - WINDOW, 0)
        wb = wbias_ref[min(i, 1)]
        sw = _dot_nt(head_q(i, r), kw_ref[0, 0, w0:w0 + span, :]) + wb
        return sw, _rowmax(sw)

    def values(u, sc):
        kind, i, r = u
        if kind == "sel":
            tiles[i]["o_s"].append(_attention_values(sc, vf, (i + 1) * tq))
        elif kind == "win":
            sw, mw = sc
            w0 = max(i * tq - WINDOW, 0)
            acc_w = _dot(jnp.exp(sw - mw).astype(BF16), vw_ref[0, 0, w0:w0 + span, :])
            tiles[i]["o_w"].append(acc_w[:, :NSA_HEAD] / acc_w[:, NSA_HEAD:NSA_HEAD + 1])
        if kind == "win" and r == R - 1:
            tile = tiles.pop(i)
            g_hi, g_lo = _split_hilo(gn_ref[0, i * tq:(i + 1) * tq, :])
            o_c = jnp.concatenate([tile["o_c"][r * tq:(r + 1) * tq] for r in range(R)], axis=-1)
            branches = (o_c, jnp.concatenate(tile["o_s"], axis=-1), jnp.concatenate(tile["o_w"], axis=-1))
            out = None
            for br, o_b in enumerate(branches):
                gate = _dot(g_hi, gexp_ref[grp, br]) + _dot(g_lo, gexp_ref[grp, br])
                out = gate * o_b if out is None else out + gate * o_b
            o_ref[0, i * tq:(i + 1) * tq, :] = out.astype(BF16)

    nq = S // tq
    units = [("cmp", 0, 0)]
    for i in range(nq):
        units += [("cmp", i + 1, 0)] if i + 1 < nq else []
        for r in range(R):
            units += [("sel", i, r), ("win", i, r)]
    pending = [scores(u) for u in units[:ATT_LOOKAHEAD]]
    for n, u in enumerate(units):
        if n + ATT_LOOKAHEAD < len(units):
            pending.append(scores(units[n + ATT_LOOKAHEAD]))
        values(u, pending.pop(0))


def _nsa_attention(qn, kc, vc, ks, vs, kw, vw, gn, ovt, gexp, dbias, wbias):
    B, S, _ = qn.shape
    G, R, Dh = NSA_KV_GROUPS, NSA_REP, NSA_HEAD
    ncp = kc.shape[2]
    full = pl.BlockSpec((1, 1, S, HEAD_PAD), lambda b, g: (b, g, 0, 0))
    cmp_spec = lambda w: pl.BlockSpec((1, 1, ncp, w), lambda b, g: (b, g, 0, 0))
    return pl.pallas_call(
        _nsa_kernel,
        grid=(B, G),
        in_specs=[pl.BlockSpec((1, S, R * HEAD_PAD), lambda b, g: (b, 0, g)),
                  cmp_spec(HEAD_PAD), cmp_spec(Dh), full, full, full, full,
                  pl.BlockSpec((1, S, LANES), lambda b, g: (b, 0, 0)),
                  _const_spec(ovt.shape), _const_spec(gexp.shape),
                  _const_spec(dbias.shape), _const_spec(wbias.shape)],
        out_specs=pl.BlockSpec((1, S, R * Dh), lambda b, g: (b, 0, g)),
        out_shape=jax.ShapeDtypeStruct((B, S, G * R * Dh), BF16),
        scratch_shapes=[pltpu.VMEM((S // SEL_LEN, TQ_ATT), F32)],
        compiler_params=pltpu.CompilerParams(dimension_semantics=("arbitrary",) * 2,
                                             vmem_limit_bytes=VMEM_LIMIT),
        name="nsa_attention",
    )(qn, kc, vc, ks, vs, kw, vw, gn, ovt, gexp, dbias, wbias)


def _mla_kernel(q_ref, k_ref, v_ref, dbias_ref, o_ref):
    tq = dbias_ref.shape[0]
    S = q_ref.shape[1]
    dbias = dbias_ref[...]
    units = [(i, hh) for i in range(S // tq) for hh in range(2)]

    def scores(u):
        i, hh = u
        cols = slice(HEAD_PAD * hh, HEAD_PAD * (hh + 1))
        q = q_ref[0, i * tq:(i + 1) * tq, cols]
        return _attention_scores(q, lambda a, b: k_ref[0, a:b, cols], (i + 1) * tq, dbias)

    def values(u, sc):
        i, hh = u
        cols = slice(HEAD_PAD * hh, HEAD_PAD * (hh + 1))
        o = _attention_values(sc, lambda a, b: v_ref[0, a:b, cols], (i + 1) * tq)
        o_ref[0, i * tq:(i + 1) * tq, MLA_V * hh:MLA_V * (hh + 1)] = o.astype(BF16)

    pending = [scores(u) for u in units[:ATT_LOOKAHEAD]]
    for n, u in enumerate(units):
        if n + ATT_LOOKAHEAD < len(units):
            pending.append(scores(units[n + ATT_LOOKAHEAD]))
        values(u, pending.pop(0))


def _mla_attention(qm, km, vm, dbias):
    B, S, _ = qm.shape
    pair = pl.BlockSpec((1, S, 2 * HEAD_PAD), lambda b, h: (b, 0, h))
    return pl.pallas_call(
        _mla_kernel,
        grid=(B, MLA_HEADS // 2),
        in_specs=[pair, pair, pair, _const_spec(dbias.shape)],
        out_specs=pl.BlockSpec((1, S, 2 * MLA_V), lambda b, h: (b, 0, h)),
        out_shape=jax.ShapeDtypeStruct((B, S, MLA_HEADS * MLA_V), BF16),
        compiler_params=pltpu.CompilerParams(dimension_semantics=("arbitrary",) * 2,
                                             vmem_limit_bytes=VMEM_LIMIT),
        name="mla_attention",
    )(qm, km, vm, dbias)


def _out_ffn_kernel(x_ref, om_ref, on_ref, gm_ref, mod_ref, g2_ref,
                    wom_ref, won_ref, wout_ref, wg_ref, wu_ref, wd_ref, o_ref):
    x = x_ref[0]
    mod = mod_ref[0]
    gt1, sh2, sc2, gt2 = mod[2:3], mod[3:4], mod[4:5], mod[5:6]
    ym = _dot(om_ref[0], wom_ref[...])
    yn = _dot(on_ref[0], won_ref[...])
    merged = gm_ref[0, :, :D_MODEL] * ym + gm_ref[0, :, D_MODEL:] * yn
    x1 = x + gt1 * _dot(merged.astype(BF16), wout_ref[...])
    h2 = (_rms(x1, D_MODEL) * g2_ref[...] * (1.0 + sc2) + sh2).astype(BF16)
    acc = jnp.zeros(x.shape, F32)
    for c in range(D_FF // FF_CHUNK):
        sl = slice(c * FF_CHUNK, (c + 1) * FF_CHUNK)
        g = _dot(h2, wg_ref[:, sl])
        u = _dot(h2, wu_ref[:, sl])
        a = (g * _sigmoid(g) * u).astype(BF16)
        acc = acc + _dot(a, wd_ref[sl, :])
    o_ref[0] = x1 + gt2 * acc


def _out_ffn(x, om, on, gm, mod, g2, wom, won, wout, wg, wu, wd):
    B, S, D = x.shape
    tm = TM_OUT
    tok = lambda w: pl.BlockSpec((1, tm, w), lambda b, i: (b, i, 0))
    wspec = lambda w: pl.BlockSpec(w.shape, lambda b, i: (0, 0), pipeline_mode=pl.Buffered(1))
    return pl.pallas_call(
        _out_ffn_kernel,
        grid=(B, S // tm),
        in_specs=[tok(D), tok(om.shape[2]), tok(on.shape[2]), tok(2 * D),
                  pl.BlockSpec((1, N_MOD, D), lambda b, i: (b, 0, 0)),
                  _const_spec(g2.shape)] + [wspec(w) for w in (wom, won, wout, wg, wu, wd)],
        out_specs=tok(D),
        out_shape=jax.ShapeDtypeStruct((B, S, D), F32),
        compiler_params=pltpu.CompilerParams(dimension_semantics=("arbitrary", "arbitrary"),
                                             vmem_limit_bytes=VMEM_LIMIT),
        name="out_ffn",
    )(x, om, on, gm, mod, g2, wom, won, wout, wg, wu, wd)


def _rope_expansion():
    texp = np.zeros((LANES, 4 * LANES), np.float32)
    trow = np.zeros((1, 4 * LANES), np.float32)
    hm, hn = MLA_ROPE // 2, NSA_ROT // 2
    trow[0, 0:LANES] = 1.0
    trow[0, 2 * LANES:3 * LANES] = 1.0
    for i in range(hm):
        for off, sgn in ((MLA_NOPE + i, -1.0), (MLA_NOPE + hm + i, 1.0)):
            texp[i, off] = 1.0
            trow[0, off] = 0.0
            texp[N_FREQ + i, LANES + off] = sgn
    for i in range(hn):
        for off, sgn in ((i, -1.0), (hn + i, 1.0)):
            texp[hm + i, 2 * LANES + off] = 1.0
            trow[0, 2 * LANES + off] = 0.0
            texp[N_FREQ + hm + i, 3 * LANES + off] = sgn
    return jnp.asarray(texp, BF16), jnp.asarray(trow, F32)


def _mask_tables(S):
    tq = TQ_ATT
    n_chunk = S // CMP_STRIDE
    n_sel = S // SEL_LEN
    starts = np.arange(n_chunk) * CMP_STRIDE
    sel_start = np.arange(LANES) * SEL_LEN
    ovt = ((starts[None, :] < sel_start[:, None] + SEL_LEN) &
           (starts[None, :] + CMP_LEN > sel_start[:, None]) &
           (np.arange(n_chunk)[None, :] < n_chunk - 1) &
           (np.arange(LANES)[:, None] < n_sel))
    gcol = np.arange(LANES)[:, None]
    head = np.arange(NSA_REP * NSA_HEAD)[None, :] // NSA_HEAD
    gexp = np.stack([np.stack([gcol == (g * NSA_REP + head) * N_NSA_BRANCH + br
                               for br in range(N_NSA_BRANCH)]) for g in range(NSA_KV_GROUPS)])
    qi = np.arange(tq)[:, None]
    dbias = np.where(np.arange(tq)[None, :] <= qi, 0.0, NEG)
    kk = np.arange(WINDOW + tq)[None, :]
    band = lambda d: np.where((d >= 0) & (d < WINDOW), 0.0, NEG)
    wbias = np.stack([band(qi - kk), band(qi + WINDOW - kk)])
    return (jnp.asarray(ovt, BF16), jnp.asarray(gexp, BF16),
            jnp.asarray(dbias, F32), jnp.asarray(wbias, F32))


def _pad_heads(w, n_heads, width):
    k = w.shape[0]
    w = w.reshape(k, n_heads, width)
    return jnp.pad(w, ((0, 0), (0, 0), (0, HEAD_PAD - width))).reshape(k, n_heads * HEAD_PAD)


def _pad_row(g, lo=0):
    return jnp.pad(g, (lo, HEAD_PAD - lo - g.shape[0])).reshape(1, HEAD_PAD)


def _layer(x, mod, cs, p):
    B, S, D = x.shape
    w_in = p["w_in"]
    o = 0
    cols = {}
    for name, wdt in (("cq", MLA_Q_LORA), ("ckv", MLA_KV_LORA), ("kpe", MLA_ROPE),
                      ("qn", NSA_HEADS * NSA_HEAD), ("kc", KV_W), ("vc", KV_W), ("ks", KV_W),
                      ("vs", KV_W), ("kw", KV_W), ("vw", KV_W),
                      ("gn", NSA_HEADS * N_NSA_BRANCH), ("gm", 2 * D)):
        cols[name] = w_in[:, o:o + wdt]
        o += wdt
    G = NSA_KV_GROUPS
    n_gate = NSA_HEADS * N_NSA_BRANCH
    zc = lambda n: jnp.zeros((D, n), F32)
    wsm = jnp.concatenate([cols["gn"], zc(MLA_NOPE - n_gate), cols["kpe"], zc(LANES - MLA_QK)], axis=1)
    wkv6 = jnp.concatenate([cols["kc"], cols["vc"]] +
                           [_pad_heads(cols[k], G, NSA_HEAD) for k in ("ks", "vs", "kw", "vw")], axis=1)
    wkvb = p["mla_w_kv_b"].reshape(MLA_KV_LORA, MLA_HEADS, MLA_NOPE + MLA_V)
    wkvb = jnp.concatenate([_pad_heads(wkvb[:, :, :MLA_NOPE].reshape(MLA_KV_LORA, -1), MLA_HEADS, MLA_NOPE),
                            _pad_heads(wkvb[:, :, MLA_NOPE:].reshape(MLA_KV_LORA, -1), MLA_HEADS, MLA_V)], axis=1)
    bf = lambda w: w.astype(BF16)
    row = lambda g: g.reshape(1, -1)
    weights = tuple(bf(w) for w in (cols["cq"], cols["ckv"], wsm, _pad_heads(cols["qn"], NSA_HEADS, NSA_HEAD),
                                    wkv6, cols["gm"],
                                    _pad_heads(p["mla_w_q_b"], MLA_HEADS, MLA_QK), wkvb))
    vone = jnp.tile(jnp.zeros((1, HEAD_PAD), F32).at[0, MLA_V].set(1.0), (1, MLA_HEADS))
    rows = (row(p["mla_q_a_gain"]), row(p["mla_kv_a_gain"]),
            _pad_row(p["mla_q_gain"]), _pad_row(p["mla_k_gain"][:MLA_NOPE]),
            _pad_row(p["mla_k_gain"][MLA_NOPE:], MLA_NOPE),
            _pad_row(p["nsa_q_gain"]), _pad_row(p["nsa_ks_gain"]), _pad_row(p["nsa_kw_gain"]), vone)
    texp, trow = _rope_expansion()
    (qm, km, vm, qn, ks, kw, vs, vw, kc_raw, vc_raw, gn, gm) = _inproj(
        x, mod, cs, (row(p["norm1_gain"]), texp, trow), weights, rows)

    n_chunk = S // CMP_STRIDE
    half = CMP_STRIDE * NSA_HEAD
    ck = kc_raw.reshape(B, G, n_chunk, half)
    cv = vc_raw.reshape(B, G, n_chunk, half)
    cs_end = cs[:, CMP_LEN - 1::CMP_STRIDE]
    cs_end = jnp.pad(cs_end, ((0, 0), (0, n_chunk - cs_end.shape[1]), (0, 0)))
    w2k = jnp.pad(p["cmp_w2_k"], ((0, 0), (0, HEAD_PAD - NSA_HEAD)))
    kc, vc = _compress(ck, cv, p["cmp_pos_k"].reshape(2, half), p["cmp_pos_v"].reshape(2, half),
                       bf(p["cmp_w1_k"]), bf(w2k), bf(p["cmp_w1_v"]), bf(p["cmp_w2_v"]),
                       _pad_row(p["nsa_kc_gain"]), cs_end, texp, trow)

    ovt, gexp, dbias, wbias = _mask_tables(S)
    o_nsa = _nsa_attention(qn, kc, vc, ks, vs, kw, vw, gn, ovt, gexp, dbias, wbias)
    o_mla = _mla_attention(qm, km, vm, dbias)

    return _out_ffn(x, o_mla, o_nsa, gm, mod, row(p["norm2_gain"]),
                    bf(p["w_o_mla"]), bf(p["w_o_nsa"]), bf(p["w_out"]),
                    bf(p["ffn_w_gate"]), bf(p["ffn_w_up"]), bf(p["ffn_w_down"]))


def kernel(x, c, positions, ada_w, ada_b, norm1_gain, w_in, mla_q_a_gain, mla_w_q_b, mla_kv_a_gain, mla_w_kv_b, mla_q_gain, mla_k_gain, nsa_q_gain, nsa_kc_gain, nsa_ks_gain, nsa_kw_gain, cmp_pos_k, cmp_w1_k, cmp_w2_k, cmp_pos_v, cmp_w1_v, cmp_w2_v, w_o_mla, w_o_nsa, w_out, norm2_gain, ffn_w_gate, ffn_w_up, ffn_w_down):
    params = dict(norm1_gain=norm1_gain, w_in=w_in, mla_q_a_gain=mla_q_a_gain, mla_w_q_b=mla_w_q_b,
                  mla_kv_a_gain=mla_kv_a_gain, mla_w_kv_b=mla_w_kv_b, mla_q_gain=mla_q_gain,
                  mla_k_gain=mla_k_gain, nsa_q_gain=nsa_q_gain, nsa_kc_gain=nsa_kc_gain,
                  nsa_ks_gain=nsa_ks_gain, nsa_kw_gain=nsa_kw_gain, cmp_pos_k=cmp_pos_k,
                  cmp_w1_k=cmp_w1_k, cmp_w2_k=cmp_w2_k, cmp_pos_v=cmp_pos_v, cmp_w1_v=cmp_w1_v,
                  cmp_w2_v=cmp_w2_v, w_o_mla=w_o_mla, w_o_nsa=w_o_nsa, w_out=w_out,
                  norm2_gain=norm2_gain, ffn_w_gate=ffn_w_gate, ffn_w_up=ffn_w_up, ffn_w_down=ffn_w_down)
    B = x.shape[0]
    inv_m = ROPE_THETA ** (-jnp.arange(0, MLA_ROPE, 2, dtype=F32) / MLA_ROPE)
    inv_n = ROPE_THETA ** (-jnp.arange(0, NSA_ROT, 2, dtype=F32) / NSA_ROT)
    n_unused = N_FREQ - inv_m.shape[0] - inv_n.shape[0]
    cs = _rope_tables(positions, jnp.concatenate([inv_m, inv_n, jnp.zeros((n_unused,), F32)]))
    depth = ada_w.shape[0]
    for l in range(depth):
        mod = _ada(c, ada_w[l], ada_b[l]).reshape(B, N_MOD, D_MODEL)
        x = _layer(x, mod, cs, {k: v[l] for k, v in params.items()})
    return x
```

```python
import numpy as np
import jax
import jax.numpy as jnp
from jax import lax
from jax.experimental import pallas as pl
from jax.experimental.pallas import tpu as pltpu

F32 = jnp.float32
BF16 = jnp.bfloat16

D_MODEL = 1024
ROPE_THETA = 500000.0
EPS = 1e-6
NEG = -1e30
LOG2E = 1.4426950408889634

MLA_HEADS = 8
MLA_NOPE = 64
MLA_ROPE = 32
MLA_QK = MLA_NOPE + MLA_ROPE
MLA_V = 64
MLA_Q_LORA = 768
MLA_KV_LORA = 256

NSA_HEADS = 8
NSA_KV_GROUPS = 2
NSA_REP = NSA_HEADS // NSA_KV_GROUPS
NSA_HEAD = 64
NSA_ROT = NSA_HEAD // 4
CMP_LEN = 32
CMP_STRIDE = 16
CMP_HIDDEN = 256
SEL_LEN = 64
SEL_TOP = 8
WINDOW = 256
N_NSA_BRANCH = 3
FORCE_BONUS = 1e4
KV_W = NSA_KV_GROUPS * NSA_HEAD

D_FF = -(-8 * D_MODEL // (3 * 256)) * 256
N_MOD = 6
LANES = 128
HEAD_PAD = LANES
N_FREQ = 32

TM_IN = 512
IN_ROWS = 256
N_INPROJ_OUT = 12
TQ_ATT = 256
TM_OUT = 512
FF_CHUNK = D_FF // 2
ATT_LOOKAHEAD = 2
VMEM_LIMIT = 56 * 1024 * 1024


def _dot(a, b):
    return jnp.dot(a, b, preferred_element_type=F32)


def _dot_nt(a, b):
    return lax.dot_general(a, b, (((1,), (1,)), ((), ())), preferred_element_type=F32)


def _split_hilo(a):
    hi = a.astype(BF16)
    return hi, (a - hi.astype(F32)).astype(BF16)


def _dot_hilo(a, m):
    hi, lo = _split_hilo(a)
    return _dot(hi, m) + _dot(lo, m)


def _sigmoid(v):
    return 1.0 / (1.0 + jnp.exp(-v))


def _rms(v, n):
    return v * lax.rsqrt(jnp.sum(v * v, axis=-1, keepdims=True) * (1.0 / n) + EPS)


def _rope(v, cos_v, sin_v, lo, half):
    lane = lax.broadcasted_iota(jnp.int32, v.shape, 1)
    is_x1 = (lane >= lo) & (lane < lo + half)
    rot = jnp.where(is_x1, pltpu.roll(v, LANES - half, 1), pltpu.roll(v, half, 1))
    return v * cos_v + rot * sin_v


def _rope_multipliers(cs, texp_ref, trow_ref):
    tabs = _dot_hilo(cs, texp_ref[...]) + trow_ref[...]
    return tuple(tabs[:, LANES * i:LANES * (i + 1)] for i in range(4))


def _const_spec(shape):
    nd = len(shape)
    return pl.BlockSpec(shape, lambda *_: (0,) * nd)


def _rowmax(s):
    return jnp.max(s, axis=-1, keepdims=True)


def _attention_scores(q, k_ref, kmax, dbias):
    k0 = kmax - dbias.shape[1]
    s_d = _dot_nt(q, k_ref(k0, kmax)) + dbias
    m = _rowmax(s_d)
    s_m = None
    if k0 > 0:
        s_m = _dot_nt(q, k_ref(0, k0))
        m = jnp.maximum(m, _rowmax(s_m))
    return s_m, s_d, m


def _attention_values(scores, v_ref, kmax):
    s_m, s_d, m = scores
    k0 = kmax - s_d.shape[1]
    acc = _dot(jnp.exp2(s_d - m).astype(BF16), v_ref(k0, kmax))
    if s_m is not None:
        acc = acc + _dot(jnp.exp2(s_m - m).astype(BF16), v_ref(0, k0))
    return acc[:, :NSA_HEAD] / acc[:, NSA_HEAD:NSA_HEAD + 1]


def _rope_kernel(pos_ref, inv_ref, cs_ref):
    ang = pos_ref[0].astype(F32) * inv_ref[...]
    nf, S = ang.shape
    rows = jnp.concatenate([jnp.cos(ang), jnp.sin(ang), jnp.zeros((LANES - 2 * nf, S), F32)], axis=0)
    cs_ref[0] = rows.T


def _rope_tables(positions, inv):
    B, S = positions.shape
    nf = inv.shape[0]
    return pl.pallas_call(
        _rope_kernel,
        grid=(B,),
        in_specs=[pl.BlockSpec((1, 1, S), lambda b: (b, 0, 0)),
                  _const_spec((nf, 1))],
        out_specs=pl.BlockSpec((1, S, LANES), lambda b: (b, 0, 0)),
        out_shape=jax.ShapeDtypeStruct((B, S, LANES), F32),
        name="rope_tables",
    )(positions.reshape(B, 1, S), inv.reshape(nf, 1))


def _ada_kernel(c_ref, w_ref, b_ref, o_ref):
    c = c_ref[...]
    sc = c * _sigmoid(c)
    o_ref[...] = jnp.dot(sc, w_ref[...], preferred_element_type=F32,
                         precision=lax.Precision.HIGHEST) + b_ref[...]


def _ada(c, w, b):
    B, D = c.shape
    N = w.shape[1]
    tn = D_MODEL
    return pl.pallas_call(
        _ada_kernel,
        grid=(N // tn,),
        in_specs=[_const_spec((B, D)),
                  pl.BlockSpec((D, tn), lambda j: (0, j)),
                  pl.BlockSpec((1, tn), lambda j: (0, j))],
        out_specs=pl.BlockSpec((B, tn), lambda j: (0, j)),
        out_shape=jax.ShapeDtypeStruct((B, N), F32),
        name="ada_mod",
    )(c, w, b.reshape(1, N))


def _inproj_kernel(x_ref, mod_ref, cs_ref, *refs):
    n_const = len(refs) - N_INPROJ_OUT
    consts, outs = refs[:n_const], refs[n_const:]
    tm = x_ref.shape[1]
    for s in range(tm // IN_ROWS):
        rows = slice(s * IN_ROWS, (s + 1) * IN_ROWS)
        views = [r.at[:, rows] if len(r.shape) == 3 else r.at[:, :, rows] for r in outs]
        _inproj_rows(pl.program_id(1) * tm + s * IN_ROWS, x_ref.at[:, rows], mod_ref, cs_ref.at[:, rows],
                     *consts, *views)


def _inproj_rows(tok0, x_ref, mod_ref, cs_ref, g1_ref, texp_ref, trow_ref,
                 wcq_ref, wckv_ref, wsm_ref, wqn_ref, wkv6_ref, wgm_ref,
                 qag_ref, wqb_ref, kvag_ref, wkvb_ref,
                 mqg_ref, mkn_ref, mkr_ref, nqg_ref, nksg_ref, nkwg_ref, vone_ref,
                 qm_ref, km_ref, vm_ref, qn_ref, ks_ref, kw_ref, vs_ref, vw_ref,
                 kc_ref, vc_ref, gn_ref, gm_ref):
    x = x_ref[0]
    tm = x.shape[0]
    mod = mod_ref[0]
    sh1, sc1 = mod[0:1], mod[1:2]
    h = _rms(x, D_MODEL) * g1_ref[...] * (1.0 + sc1) + sh1
    hb = h.astype(BF16)

    cos_m, sin_m, cos_n, sin_n = _rope_multipliers(cs_ref[0], texp_ref, trow_ref)
    lane = lax.broadcasted_iota(jnp.int32, (tm, LANES), 1)
    blk = lambda a, i: a[:, HEAD_PAD * i:HEAD_PAD * (i + 1)]
    hm = MLA_ROPE // 2
    hn = NSA_ROT // 2

    cq = _dot(hb, wcq_ref[...])
    cqn = (_rms(cq, MLA_Q_LORA) * qag_ref[...]).astype(BF16)
    q = _dot(cqn, wqb_ref[...])
    mqg = mqg_ref[...]
    m_scale = MLA_QK ** -0.5 * LOG2E
    for hd in range(MLA_HEADS):
        qh = _rope(_rms(blk(q, hd), MLA_QK) * mqg, cos_m, sin_m, MLA_NOPE, hm) * m_scale
        qm_ref[0, :, HEAD_PAD * hd:HEAD_PAD * (hd + 1)] = qh.astype(BF16)

    zs = _dot(hb, wsm_ref[...])
    kpe = jnp.where((lane >= MLA_NOPE) & (lane < MLA_QK), zs, 0.0)
    kpe_ss = jnp.sum(kpe * kpe, axis=-1, keepdims=True)
    kr = _rope(kpe * mkr_ref[...], cos_m, sin_m, MLA_NOPE, hm)
    ckv = _dot(hb, wckv_ref[...])
    ckvn = (_rms(ckv, MLA_KV_LORA) * kvag_ref[...]).astype(BF16)
    kv = _dot(ckvn, wkvb_ref[...])
    mkn = mkn_ref[...]
    for hd in range(MLA_HEADS):
        kn = blk(kv, hd)
        inv = lax.rsqrt((jnp.sum(kn * kn, axis=-1, keepdims=True) + kpe_ss) * (1.0 / MLA_QK) + EPS)
        km_ref[0, :, HEAD_PAD * hd:HEAD_PAD * (hd + 1)] = ((kn * mkn + kr) * inv).astype(BF16)
    vm_ref[0] = (kv[:, MLA_HEADS * HEAD_PAD:] + vone_ref[...]).astype(BF16)

    qn = _dot(hb, wqn_ref[...])
    nqg = nqg_ref[...]
    n_scale = NSA_HEAD ** -0.5 * LOG2E
    for hd in range(NSA_HEADS):
        qh = _rope(_rms(blk(qn, hd), NSA_HEAD) * nqg, cos_n, sin_n, 0, hn) * n_scale
        qn_ref[0, :, HEAD_PAD * hd:HEAD_PAD * (hd + 1)] = qh.astype(BF16)

    kv6 = _dot(hb, wkv6_ref[...])
    tok = tok0 + lax.broadcasted_iota(jnp.int32, (tm, 1), 0)
    sblk = lax.shift_right_logical(tok, SEL_LEN.bit_length() - 1)
    ind = jnp.where(lane - NSA_HEAD == sblk, NEG, 0.0)
    vone = vone_ref[:, 0:HEAD_PAD]
    nksg, nkwg = nksg_ref[...], nkwg_ref[...]
    kcb, vcb = blk(kv6, 0), blk(kv6, 1)
    for g in range(NSA_KV_GROUPS):
        kc_ref[0, g] = kcb[:, NSA_HEAD * g:NSA_HEAD * (g + 1)]
        vc_ref[0, g] = vcb[:, NSA_HEAD * g:NSA_HEAD * (g + 1)]
        ks = _rope(_rms(blk(kv6, 2 + g), NSA_HEAD) * nksg, cos_n, sin_n, 0, hn)
        ks_ref[0, g] = (ks + ind).astype(BF16)
        vs_ref[0, g] = (blk(kv6, 4 + g) + vone).astype(BF16)
        kw = _rope(_rms(blk(kv6, 6 + g), NSA_HEAD) * nkwg, cos_n, sin_n, 0, hn)
        kw_ref[0, g] = kw.astype(BF16)
        vw_ref[0, g] = (blk(kv6, 8 + g) + vone).astype(BF16)

    gn_ref[0] = _sigmoid(zs)
    gm_ref[0] = _sigmoid(_dot(hb, wgm_ref[...])).astype(BF16)


def _inproj(x, mod, cs, consts, weights, rows):
    B, S, D = x.shape
    tm = TM_IN
    tok = lambda w: pl.BlockSpec((1, tm, w), lambda b, i: (b, i, 0))
    head = lambda n, w: pl.BlockSpec((1, n, tm, w), lambda b, i: (b, 0, i, 0))
    operands = list(consts) + list(weights[:6]) + [rows[0], weights[6], rows[1], weights[7]] + list(rows[2:])
    in_specs = [tok(D), pl.BlockSpec((1, N_MOD, D), lambda b, i: (b, 0, 0)), tok(LANES)]
    in_specs += [_const_spec(a.shape) for a in operands]
    G = NSA_KV_GROUPS
    sds = jax.ShapeDtypeStruct
    wide = MLA_HEADS * HEAD_PAD
    outs = [
        (tok(wide), sds((B, S, wide), BF16)),
        (tok(wide), sds((B, S, wide), BF16)),
        (tok(wide), sds((B, S, wide), BF16)),
        (tok(wide), sds((B, S, wide), BF16)),
        (head(G, HEAD_PAD), sds((B, G, S, HEAD_PAD), BF16)),
        (head(G, HEAD_PAD), sds((B, G, S, HEAD_PAD), BF16)),
        (head(G, HEAD_PAD), sds((B, G, S, HEAD_PAD), BF16)),
        (head(G, HEAD_PAD), sds((B, G, S, HEAD_PAD), BF16)),
        (head(G, NSA_HEAD), sds((B, G, S, NSA_HEAD), F32)),
        (head(G, NSA_HEAD), sds((B, G, S, NSA_HEAD), F32)),
        (tok(LANES), sds((B, S, LANES), F32)),
        (tok(2 * D), sds((B, S, 2 * D), BF16)),
    ]
    return pl.pallas_call(
        _inproj_kernel,
        grid=(B, S // tm),
        in_specs=in_specs,
        out_specs=[o[0] for o in outs],
        out_shape=[o[1] for o in outs],
        compiler_params=pltpu.CompilerParams(dimension_semantics=("arbitrary", "arbitrary"),
                                             vmem_limit_bytes=VMEM_LIMIT),
        name="inproj_prep",
    )(x, mod, cs, *operands)


def _compress_kernel(ck_ref, cv_ref, pk_ref, pv_ref, w1k_ref, w2k_ref, w1v_ref, w2v_ref,
                     kcg_ref, cs_ref, texp_ref, trow_ref, kc_ref, vc_ref):
    half = CMP_STRIDE * NSA_HEAD
    n = ck_ref.shape[2]

    def compress(chunk, pos, w1_ref, w2_ref):
        a = _dot((chunk + pos[0:1]).astype(BF16), w1_ref[0:half, :])
        b = _dot((chunk + pos[1:2]).astype(BF16), w1_ref[half:2 * half, :])
        hid = a + pltpu.roll(b, n - 1, 0)
        hid = hid * _sigmoid(hid)
        return _dot(hid.astype(BF16), w2_ref[...])

    kc = compress(ck_ref[0, 0], pk_ref[...], w1k_ref, w2k_ref)
    _, _, cos_n, sin_n = _rope_multipliers(cs_ref[0], texp_ref, trow_ref)
    kc = _rope(_rms(kc, NSA_HEAD) * kcg_ref[...], cos_n, sin_n, 0, NSA_ROT // 2)
    kc_ref[0, 0] = kc.astype(BF16)
    vc_ref[0, 0] = compress(cv_ref[0, 0], pv_ref[...], w1v_ref, w2v_ref).astype(BF16)


def _compress(ck, cv, pk, pv, w1k, w2k, w1v, w2v, kcg, cs_end, texp, trow):
    B, G, n, w = ck.shape
    blk = pl.BlockSpec((1, 1, n, w), lambda b, g: (b, g, 0, 0))
    oblk = lambda wd: pl.BlockSpec((1, 1, n, wd), lambda b, g: (b, g, 0, 0))
    consts = (pk, pv, w1k, w2k, w1v, w2v, kcg)
    return pl.pallas_call(
        _compress_kernel,
        grid=(B, G),
        in_specs=[blk, blk] + [_const_spec(a.shape) for a in consts] +
                 [pl.BlockSpec((1, n, LANES), lambda b, g: (b, 0, 0)),
                  _const_spec(texp.shape), _const_spec(trow.shape)],
        out_specs=[oblk(HEAD_PAD), oblk(NSA_HEAD)],
        out_shape=[jax.ShapeDtypeStruct((B, G, n, HEAD_PAD), BF16),
                   jax.ShapeDtypeStruct((B, G, n, NSA_HEAD), BF16)],
        name="nsa_compress",
    )(ck, cv, *consts, cs_end, texp, trow)


def _nsa_kernel(q_ref, kc_ref, vc_ref, ks_ref, vs_ref, kw_ref, vw_ref, gn_ref,
                ovt_ref, gexp_ref, dbias_ref, wbias_ref, o_ref, imp_ref):
    tq = dbias_ref.shape[0]
    S = q_ref.shape[1]
    R = NSA_REP
    M = R * tq
    n_sel = imp_ref.shape[0]
    ncp = kc_ref.shape[2]
    span = WINDOW + tq
    grp = pl.program_id(1)
    dbias = dbias_ref[...]
    kf = lambda a, b: ks_ref[0, 0, a:b, :]
    vf = lambda a, b: vs_ref[0, 0, a:b, :]
    row = lax.broadcasted_iota(jnp.int32, (M, 1), 0)
    n_idx = lax.broadcasted_iota(jnp.int32, (M, ncp), 1)
    j = lax.broadcasted_iota(jnp.int32, (n_sel, tq), 0)
    head_q = lambda i, r: q_ref[0, i * tq:(i + 1) * tq, HEAD_PAD * r:HEAD_PAD * (r + 1)]
    tile_q = lambda i: jnp.concatenate([head_q(i, r) for r in range(R)], axis=0)
    tiles = {}

    def compressed_and_select(i):
        q0 = i * tq
        t = q0 + jnp.bitwise_and(row, tq - 1)
        s = _dot_nt(tile_q(i), kc_ref[0, 0])
        valid = (n_idx * CMP_STRIDE + (CMP_LEN - 1)) <= t
        sm = jnp.where(valid, s, NEG)
        e = jnp.where(valid, jnp.exp2(sm - _rowmax(sm)), 0.0)
        den = jnp.sum(e, axis=-1, keepdims=True)
        p_c = e / jnp.where(den > 0.0, den, 1.0)
        o_c = _dot(p_c.astype(BF16), vc_ref[0, 0])
        psum = p_c[0:tq]
        for r in range(1, R):
            psum = psum + p_c[r * tq:(r + 1) * tq]
        hi, lo = _split_hilo(psum.T)
        imp = (_dot(ovt_ref[...], hi) + _dot(ovt_ref[...], lo))[0:n_sel]
        cur = lax.shift_right_logical(q0 + lax.broadcasted_iota(jnp.int32, (1, tq), 1),
                                      SEL_LEN.bit_length() - 1)
        forced = (j == 0) | (j == cur) | (j == cur - 1)
        imp = jnp.where(forced, imp + FORCE_BONUS, imp)
        imp = jnp.where(j <= cur, imp, NEG)
        imp_ref[...] = imp
        cnt = jnp.zeros((n_sel, tq), F32)
        for jj in range(n_sel):
            other = imp_ref[jj:jj + 1, :]
            beats = (other > imp) | ((other == imp) & (j > jj))
            cnt = cnt + jnp.where(beats, 1.0, 0.0)
        nsel = jnp.where((cnt < float(SEL_TOP)) & (j <= cur), 0.0, 1.0)
        nsel = jnp.concatenate([jnp.zeros((NSA_HEAD, tq), F32), nsel,
                                jnp.zeros((LANES - NSA_HEAD - n_sel, tq), F32)], axis=0).T.astype(BF16)
        tiles[i] = dict(o_c=o_c, nsel=nsel, o_s=[], o_w=[])

    def scores(u):
        kind, i, r = u
        if kind == "cmp":
            compressed_and_select(i)
            return None
        if kind == "sel":
            return _attention_scores(head_q(i, r) + tiles[i]["nsel"], kf, (i + 1) * tq, dbias)
        w0 = max(i * tq - WINDOW, 0)
        wb = wbias_ref[min(i, 1)]
        sw = _dot_nt(head_q(i, r), kw_ref[0, 0, w0:w0 + span, :]) + wb
        return sw, _rowmax(sw)

    def values(u, sc):
        kind, i, r = u
        if kind == "sel":
            tiles[i]["o_s"].append(_attention_values(sc, vf, (i + 1) * tq))
        elif kind == "win":
            sw, mw = sc
            w0 = max(i * tq - WINDOW, 0)
            acc_w = _dot(jnp.exp2(sw - mw).astype(BF16), vw_ref[0, 0, w0:w0 + span, :])
            tiles[i]["o_w"].append(acc_w[:, :NSA_HEAD] / acc_w[:, NSA_HEAD:NSA_HEAD + 1])
        if kind == "win" and r == R - 1:
            tile = tiles.pop(i)
            g_hi, g_lo = _split_hilo(gn_ref[0, i * tq:(i + 1) * tq, :])
            o_c = jnp.concatenate([tile["o_c"][r * tq:(r + 1) * tq] for r in range(R)], axis=-1)
            branches = (o_c, jnp.concatenate(tile["o_s"], axis=-1), jnp.concatenate(tile["o_w"], axis=-1))
            out = None
            for br, o_b in enumerate(branches):
                gate = _dot(g_hi, gexp_ref[grp, br]) + _dot(g_lo, gexp_ref[grp, br])
                out = gate * o_b if out is None else out + gate * o_b
            o_ref[0, i * tq:(i + 1) * tq, :] = out.astype(BF16)

    nq = S // tq
    units = [("cmp", 0, 0)]
    for i in range(nq):
        units += [("cmp", i + 1, 0)] if i + 1 < nq else []
        for r in range(R):
            units += [("sel", i, r), ("win", i, r)]
    pending = [scores(u) for u in units[:ATT_LOOKAHEAD]]
    for n, u in enumerate(units):
        if n + ATT_LOOKAHEAD < len(units):
            pending.append(scores(units[n + ATT_LOOKAHEAD]))
        values(u, pending.pop(0))


def _nsa_attention(qn, kc, vc, ks, vs, kw, vw, gn, ovt, gexp, dbias, wbias):
    B, S, _ = qn.shape
    G, R, Dh = NSA_KV_GROUPS, NSA_REP, NSA_HEAD
    ncp = kc.shape[2]
    full = pl.BlockSpec((1, 1, S, HEAD_PAD), lambda b, g: (b, g, 0, 0))
    cmp_spec = lambda w: pl.BlockSpec((1, 1, ncp, w), lambda b, g: (b, g, 0, 0))
    return pl.pallas_call(
        _nsa_kernel,
        grid=(B, G),
        in_specs=[pl.BlockSpec((1, S, R * HEAD_PAD), lambda b, g: (b, 0, g)),
                  cmp_spec(HEAD_PAD), cmp_spec(Dh), full, full, full, full,
                  pl.BlockSpec((1, S, LANES), lambda b, g: (b, 0, 0)),
                  _const_spec(ovt.shape), _const_spec(gexp.shape),
                  _const_spec(dbias.shape), _const_spec(wbias.shape)],
        out_specs=pl.BlockSpec((1, S, R * Dh), lambda b, g: (b, 0, g)),
        out_shape=jax.ShapeDtypeStruct((B, S, G * R * Dh), BF16),
        scratch_shapes=[pltpu.VMEM((S // SEL_LEN, TQ_ATT), F32)],
        compiler_params=pltpu.CompilerParams(dimension_semantics=("arbitrary",) * 2,
                                             vmem_limit_bytes=VMEM_LIMIT),
        name="nsa_attention",
    )(qn, kc, vc, ks, vs, kw, vw, gn, ovt, gexp, dbias, wbias)


def _mla_kernel(q_ref, k_ref, v_ref, dbias_ref, o_ref):
    tq = dbias_ref.shape[0]
    S = q_ref.shape[1]
    dbias = dbias_ref[...]
    units = [(i, hh) for i in range(S // tq) for hh in range(2)]

    def scores(u):
        i, hh = u
        cols = slice(HEAD_PAD * hh, HEAD_PAD * (hh + 1))
        q = q_ref[0, i * tq:(i + 1) * tq, cols]
        return _attention_scores(q, lambda a, b: k_ref[0, a:b, cols], (i + 1) * tq, dbias)

    def values(u, sc):
        i, hh = u
        cols = slice(HEAD_PAD * hh, HEAD_PAD * (hh + 1))
        o = _attention_values(sc, lambda a, b: v_ref[0, a:b, cols], (i + 1) * tq)
        o_ref[0, i * tq:(i + 1) * tq, MLA_V * hh:MLA_V * (hh + 1)] = o.astype(BF16)

    pending = [scores(u) for u in units[:ATT_LOOKAHEAD]]
    for n, u in enumerate(units):
        if n + ATT_LOOKAHEAD < len(units):
            pending.append(scores(units[n + ATT_LOOKAHEAD]))
        values(u, pending.pop(0))


def _mla_attention(qm, km, vm, dbias):
    B, S, _ = qm.shape
    pair = pl.BlockSpec((1, S, 2 * HEAD_PAD), lambda b, h: (b, 0, h))
    return pl.pallas_call(
        _mla_kernel,
        grid=(B, MLA_HEADS // 2),
        in_specs=[pair, pair, pair, _const_spec(dbias.shape)],
        out_specs=pl.BlockSpec((1, S, 2 * MLA_V), lambda b, h: (b, 0, h)),
        out_shape=jax.ShapeDtypeStruct((B, S, MLA_HEADS * MLA_V), BF16),
        compiler_params=pltpu.CompilerParams(dimension_semantics=("arbitrary",) * 2,
                                             vmem_limit_bytes=VMEM_LIMIT),
        name="mla_attention",
    )(qm, km, vm, dbias)


def _out_ffn_kernel(x_ref, om_ref, on_ref, gm_ref, mod_ref, g2_ref,
                    wom_ref, won_ref, wout_ref, wg_ref, wu_ref, wd_ref, o_ref):
    x = x_ref[0]
    mod = mod_ref[0]
    gt1, sh2, sc2, gt2 = mod[2:3], mod[3:4], mod[4:5], mod[5:6]
    ym = _dot(om_ref[0], wom_ref[...])
    yn = _dot(on_ref[0], won_ref[...])
    merged = gm_ref[0, :, :D_MODEL] * ym + gm_ref[0, :, D_MODEL:] * yn
    x1 = x + gt1 * _dot(merged.astype(BF16), wout_ref[...])
    h2 = (_rms(x1, D_MODEL) * g2_ref[...] * (1.0 + sc2) + sh2).astype(BF16)
    acc = jnp.zeros(x.shape, F32)
    for c in range(D_FF // FF_CHUNK):
        sl = slice(c * FF_CHUNK, (c + 1) * FF_CHUNK)
        g = _dot(h2, wg_ref[:, sl])
        u = _dot(h2, wu_ref[:, sl])
        a = (g * _sigmoid(g) * u).astype(BF16)
        acc = acc + _dot(a, wd_ref[sl, :])
    o_ref[0] = x1 + gt2 * acc


def _out_ffn(x, om, on, gm, mod, g2, wom, won, wout, wg, wu, wd):
    B, S, D = x.shape
    tm = TM_OUT
    tok = lambda w: pl.BlockSpec((1, tm, w), lambda b, i: (b, i, 0))
    wspec = lambda w: pl.BlockSpec(w.shape, lambda b, i: (0, 0), pipeline_mode=pl.Buffered(1))
    return pl.pallas_call(
        _out_ffn_kernel,
        grid=(B, S // tm),
        in_specs=[tok(D), tok(om.shape[2]), tok(on.shape[2]), tok(2 * D),
                  pl.BlockSpec((1, N_MOD, D), lambda b, i: (b, 0, 0)),
                  _const_spec(g2.shape)] + [wspec(w) for w in (wom, won, wout, wg, wu, wd)],
        out_specs=tok(D),
        out_shape=jax.ShapeDtypeStruct((B, S, D), F32),
        compiler_params=pltpu.CompilerParams(dimension_semantics=("arbitrary", "arbitrary"),
                                             vmem_limit_bytes=VMEM_LIMIT),
        name="out_ffn",
    )(x, om, on, gm, mod, g2, wom, won, wout, wg, wu, wd)


def _rope_expansion():
    texp = np.zeros((LANES, 4 * LANES), np.float32)
    trow = np.zeros((1, 4 * LANES), np.float32)
    hm, hn = MLA_ROPE // 2, NSA_ROT // 2
    trow[0, 0:LANES] = 1.0
    trow[0, 2 * LANES:3 * LANES] = 1.0
    for i in range(hm):
        for off, sgn in ((MLA_NOPE + i, -1.0), (MLA_NOPE + hm + i, 1.0)):
            texp[i, off] = 1.0
            trow[0, off] = 0.0
            texp[N_FREQ + i, LANES + off] = sgn
    for i in range(hn):
        for off, sgn in ((i, -1.0), (hn + i, 1.0)):
            texp[hm + i, 2 * LANES + off] = 1.0
            trow[0, 2 * LANES + off] = 0.0
            texp[N_FREQ + hm + i, 3 * LANES + off] = sgn
    return jnp.asarray(texp, BF16), jnp.asarray(trow, F32)


def _mask_tables(S):
    tq = TQ_ATT
    n_chunk = S // CMP_STRIDE
    n_sel = S // SEL_LEN
    starts = np.arange(n_chunk) * CMP_STRIDE
    sel_start = np.arange(LANES) * SEL_LEN
    ovt = ((starts[None, :] < sel_start[:, None] + SEL_LEN) &
           (starts[None, :] + CMP_LEN > sel_start[:, None]) &
           (np.arange(n_chunk)[None, :] < n_chunk - 1) &
           (np.arange(LANES)[:, None] < n_sel))
    gcol = np.arange(LANES)[:, None]
    head = np.arange(NSA_REP * NSA_HEAD)[None, :] // NSA_HEAD
    gexp = np.stack([np.stack([gcol == (g * NSA_REP + head) * N_NSA_BRANCH + br
                               for br in range(N_NSA_BRANCH)]) for g in range(NSA_KV_GROUPS)])
    qi = np.arange(tq)[:, None]
    dbias = np.where(np.arange(tq)[None, :] <= qi, 0.0, NEG)
    kk = np.arange(WINDOW + tq)[None, :]
    band = lambda d: np.where((d >= 0) & (d < WINDOW), 0.0, NEG)
    wbias = np.stack([band(qi - kk), band(qi + WINDOW - kk)])
    return (jnp.asarray(ovt, BF16), jnp.asarray(gexp, BF16),
            jnp.asarray(dbias, F32), jnp.asarray(wbias, F32))


def _pad_heads(w, n_heads, width):
    k = w.shape[0]
    w = w.reshape(k, n_heads, width)
    return jnp.pad(w, ((0, 0), (0, 0), (0, HEAD_PAD - width))).reshape(k, n_heads * HEAD_PAD)


def _pad_row(g, lo=0):
    return jnp.pad(g, (lo, HEAD_PAD - lo - g.shape[0])).reshape(1, HEAD_PAD)


def _layer(x, mod, cs, p):
    B, S, D = x.shape
    w_in = p["w_in"]
    o = 0
    cols = {}
    for name, wdt in (("cq", MLA_Q_LORA), ("ckv", MLA_KV_LORA), ("kpe", MLA_ROPE),
                      ("qn", NSA_HEADS * NSA_HEAD), ("kc", KV_W), ("vc", KV_W), ("ks", KV_W),
                      ("vs", KV_W), ("kw", KV_W), ("vw", KV_W),
                      ("gn", NSA_HEADS * N_NSA_BRANCH), ("gm", 2 * D)):
        cols[name] = w_in[:, o:o + wdt]
        o += wdt
    G = NSA_KV_GROUPS
    n_gate = NSA_HEADS * N_NSA_BRANCH
    zc = lambda n: jnp.zeros((D, n), F32)
    wsm = jnp.concatenate([cols["gn"], zc(MLA_NOPE - n_gate), cols["kpe"], zc(LANES - MLA_QK)], axis=1)
    wkv6 = jnp.concatenate([cols["kc"], cols["vc"]] +
                           [_pad_heads(cols[k], G, NSA_HEAD) for k in ("ks", "vs", "kw", "vw")], axis=1)
    wkvb = p["mla_w_kv_b"].reshape(MLA_KV_LORA, MLA_HEADS, MLA_NOPE + MLA_V)
    wkvb = jnp.concatenate([_pad_heads(wkvb[:, :, :MLA_NOPE].reshape(MLA_KV_LORA, -1), MLA_HEADS, MLA_NOPE),
                            _pad_heads(wkvb[:, :, MLA_NOPE:].reshape(MLA_KV_LORA, -1), MLA_HEADS, MLA_V)], axis=1)
    bf = lambda w: w.astype(BF16)
    row = lambda g: g.reshape(1, -1)
    weights = tuple(bf(w) for w in (cols["cq"], cols["ckv"], wsm, _pad_heads(cols["qn"], NSA_HEADS, NSA_HEAD),
                                    wkv6, cols["gm"],
                                    _pad_heads(p["mla_w_q_b"], MLA_HEADS, MLA_QK), wkvb))
    vone = jnp.tile(jnp.zeros((1, HEAD_PAD), F32).at[0, MLA_V].set(1.0), (1, MLA_HEADS))
    rows = (row(p["mla_q_a_gain"]), row(p["mla_kv_a_gain"]),
            _pad_row(p["mla_q_gain"]), _pad_row(p["mla_k_gain"][:MLA_NOPE]),
            _pad_row(p["mla_k_gain"][MLA_NOPE:], MLA_NOPE),
            _pad_row(p["nsa_q_gain"]), _pad_row(p["nsa_ks_gain"]), _pad_row(p["nsa_kw_gain"]), vone)
    texp, trow = _rope_expansion()
    (qm, km, vm, qn, ks, kw, vs, vw, kc_raw, vc_raw, gn, gm) = _inproj(
        x, mod, cs, (row(p["norm1_gain"]), texp, trow), weights, rows)

    n_chunk = S // CMP_STRIDE
    half = CMP_STRIDE * NSA_HEAD
    ck = kc_raw.reshape(B, G, n_chunk, half)
    cv = vc_raw.reshape(B, G, n_chunk, half)
    cs_end = cs[:, CMP_LEN - 1::CMP_STRIDE]
    cs_end = jnp.pad(cs_end, ((0, 0), (0, n_chunk - cs_end.shape[1]), (0, 0)))
    w2k = jnp.pad(p["cmp_w2_k"], ((0, 0), (0, HEAD_PAD - NSA_HEAD)))
    kc, vc = _compress(ck, cv, p["cmp_pos_k"].reshape(2, half), p["cmp_pos_v"].reshape(2, half),
                       bf(p["cmp_w1_k"]), bf(w2k), bf(p["cmp_w1_v"]), bf(p["cmp_w2_v"]),
                       _pad_row(p["nsa_kc_gain"]), cs_end, texp, trow)

    ovt, gexp, dbias, wbias = _mask_tables(S)
    o_nsa = _nsa_attention(qn, kc, vc, ks, vs, kw, vw, gn, ovt, gexp, dbias, wbias)
    o_mla = _mla_attention(qm, km, vm, dbias)

    return _out_ffn(x, o_mla, o_nsa, gm, mod, row(p["norm2_gain"]),
                    bf(p["w_o_mla"]), bf(p["w_o_nsa"]), bf(p["w_out"]),
                    bf(p["ffn_w_gate"]), bf(p["ffn_w_up"]), bf(p["ffn_w_down"]))


def kernel(x, c, positions, ada_w, ada_b, norm1_gain, w_in, mla_q_a_gain, mla_w_q_b, mla_kv_a_gain, mla_w_kv_b, mla_q_gain, mla_k_gain, nsa_q_gain, nsa_kc_gain, nsa_ks_gain, nsa_kw_gain, cmp_pos_k, cmp_w1_k, cmp_w2_k, cmp_pos_v, cmp_w1_v, cmp_w2_v, w_o_mla, w_o_nsa, w_out, norm2_gain, ffn_w_gate, ffn_w_up, ffn_w_down):
    params = dict(norm1_gain=norm1_gain, w_in=w_in, mla_q_a_gain=mla_q_a_gain, mla_w_q_b=mla_w_q_b,
                  mla_kv_a_gain=mla_kv_a_gain, mla_w_kv_b=mla_w_kv_b, mla_q_gain=mla_q_gain,
                  mla_k_gain=mla_k_gain, nsa_q_gain=nsa_q_gain, nsa_kc_gain=nsa_kc_gain,
                  nsa_ks_gain=nsa_ks_gain, nsa_kw_gain=nsa_kw_gain, cmp_pos_k=cmp_pos_k,
                  cmp_w1_k=cmp_w1_k, cmp_w2_k=cmp_w2_k, cmp_pos_v=cmp_pos_v, cmp_w1_v=cmp_w1_v,
                  cmp_w2_v=cmp_w2_v, w_o_mla=w_o_mla, w_o_nsa=w_o_nsa, w_out=w_out,
                  norm2_gain=norm2_gain, ffn_w_gate=ffn_w_gate, ffn_w_up=ffn_w_up, ffn_w_down=ffn_w_down)
    B = x.shape[0]
    inv_m = ROPE_THETA ** (-jnp.arange(0, MLA_ROPE, 2, dtype=F32) / MLA_ROPE)
    inv_n = ROPE_THETA ** (-jnp.arange(0, NSA_ROT, 2, dtype=F32) / NSA_ROT)
    n_unused = N_FREQ - inv_m.shape[0] - inv_n.shape[0]
    cs = _rope_tables(positions, jnp.concatenate([inv_m, inv_n, jnp.zeros((n_unused,), F32)]))
    depth = ada_w.shape[0]
    for l in range(depth):
        mod = _ada(c, ada_w[l], ada_b[l]).reshape(B, N_MOD, D_MODEL)
        x = _layer(x, mod, cs, {k: v[l] for k, v in params.items()})
    return x
```

```python
import numpy as np
import jax
import jax.numpy as jnp
from jax import lax
from jax.experimental import pallas as pl
from jax.experimental.pallas import tpu as pltpu

F32 = jnp.float32
BF16 = jnp.bfloat16

D_MODEL = 1024
ROPE_THETA = 500000.0
EPS = 1e-6
NEG = -1e30
LOG2E = 1.4426950408889634

MLA_HEADS = 8
MLA_NOPE = 64
MLA_ROPE = 32
MLA_QK = MLA_NOPE + MLA_ROPE
MLA_V = 64
MLA_Q_LORA = 768
MLA_KV_LORA = 256

NSA_HEADS = 8
NSA_KV_GROUPS = 2
NSA_REP = NSA_HEADS // NSA_KV_GROUPS
NSA_HEAD = 64
NSA_ROT = NSA_HEAD // 4
CMP_LEN = 32
CMP_STRIDE = 16
CMP_HIDDEN = 256
SEL_LEN = 64
SEL_TOP = 8
WINDOW = 256
N_NSA_BRANCH = 3
FORCE_BONUS = 1e4
KV_W = NSA_KV_GROUPS * NSA_HEAD

D_FF = -(-8 * D_MODEL // (3 * 256)) * 256
N_MOD = 6
LANES = 128
HEAD_PAD = LANES
N_FREQ = 32

TM_IN = 256
TQ_ATT = 256
TM_OUT = 512
FF_CHUNK = D_FF // 2
ATT_LOOKAHEAD = 2
VMEM_LIMIT = 56 * 1024 * 1024


def _dot(a, b):
    return jnp.dot(a, b, preferred_element_type=F32)


def _dot_nt(a, b):
    return lax.dot_general(a, b, (((1,), (1,)), ((), ())), preferred_element_type=F32)


def _split_hilo(a):
    hi = a.astype(BF16)
    return hi, (a - hi.astype(F32)).astype(BF16)


def _dot_hilo(a, m):
    hi, lo = _split_hilo(a)
    return _dot(hi, m) + _dot(lo, m)


def _sigmoid(v):
    return 1.0 / (1.0 + jnp.exp(-v))


def _rms(v, n):
    return v * lax.rsqrt(jnp.sum(v * v, axis=-1, keepdims=True) * (1.0 / n) + EPS)


def _rope(v, cos_v, sin_v, lo, half):
    lane = lax.broadcasted_iota(jnp.int32, v.shape, 1)
    is_x1 = (lane >= lo) & (lane < lo + half)
    rot = jnp.where(is_x1, pltpu.roll(v, LANES - half, 1), pltpu.roll(v, half, 1))
    return v * cos_v + rot * sin_v


def _rope_multipliers(cs, texp_ref, trow_ref):
    tabs = _dot_hilo(cs, texp_ref[...]) + trow_ref[...]
    return tuple(tabs[:, LANES * i:LANES * (i + 1)] for i in range(4))


def _const_spec(shape):
    nd = len(shape)
    return pl.BlockSpec(shape, lambda *_: (0,) * nd)


def _rowmax(s):
    return jnp.max(s, axis=-1, keepdims=True)


def _attention_scores(q, k_ref, kmax, dbias):
    k0 = kmax - dbias.shape[1]
    s_d = _dot_nt(q, k_ref(k0, kmax)) + dbias
    m = _rowmax(s_d)
    s_m = None
    if k0 > 0:
        s_m = _dot_nt(q, k_ref(0, k0))
        m = jnp.maximum(m, _rowmax(s_m))
    return s_m, s_d, m


def _attention_values(scores, v_ref, kmax):
    s_m, s_d, m = scores
    k0 = kmax - s_d.shape[1]
    acc = _dot(jnp.exp2(s_d - m).astype(BF16), v_ref(k0, kmax))
    if s_m is not None:
        acc = acc + _dot(jnp.exp2(s_m - m).astype(BF16), v_ref(0, k0))
    return acc[:, :NSA_HEAD] / acc[:, NSA_HEAD:NSA_HEAD + 1]


def _rope_kernel(pos_ref, inv_ref, cs_ref):
    ang = pos_ref[0].astype(F32) * inv_ref[...]
    nf, S = ang.shape
    rows = jnp.concatenate([jnp.cos(ang), jnp.sin(ang), jnp.zeros((LANES - 2 * nf, S), F32)], axis=0)
    cs_ref[0] = rows.T


def _rope_tables(positions, inv):
    B, S = positions.shape
    nf = inv.shape[0]
    return pl.pallas_call(
        _rope_kernel,
        grid=(B,),
        in_specs=[pl.BlockSpec((1, 1, S), lambda b: (b, 0, 0)),
                  _const_spec((nf, 1))],
        out_specs=pl.BlockSpec((1, S, LANES), lambda b: (b, 0, 0)),
        out_shape=jax.ShapeDtypeStruct((B, S, LANES), F32),
        name="rope_tables",
    )(positions.reshape(B, 1, S), inv.reshape(nf, 1))


def _ada_kernel(c_ref, w_ref, b_ref, o_ref):
    c = c_ref[...]
    sc = c * _sigmoid(c)
    o_ref[...] = jnp.dot(sc, w_ref[0], preferred_element_type=F32,
                         precision=lax.Precision.HIGHEST) + b_ref[0]


def _ada(c, w, b, layer):
    B, D = c.shape
    N = w.shape[2]
    tn = D_MODEL
    return pl.pallas_call(
        _ada_kernel,
        grid=(N // tn,),
        in_specs=[_const_spec((B, D)),
                  pl.BlockSpec((1, D, tn), lambda j: (layer, 0, j)),
                  pl.BlockSpec((1, 1, tn), lambda j: (layer, 0, j))],
        out_specs=pl.BlockSpec((B, tn), lambda j: (0, j)),
        out_shape=jax.ShapeDtypeStruct((B, N), F32),
        name="ada_mod",
    )(c, w, b.reshape(b.shape[0], 1, N))


def _inproj_kernel(x_ref, mod_ref, cs_ref, g1_ref, texp_ref, trow_ref,
                   wcq_ref, wckv_ref, wsm_ref, wqn_ref, wkv6_ref, wgm_ref,
                   qag_ref, wqb_ref, kvag_ref, wkvb_ref,
                   mqg_ref, mkn_ref, mkr_ref, nqg_ref, nksg_ref, nkwg_ref, vone_ref,
                   qm_ref, km_ref, vm_ref, qn_ref, ks_ref, kw_ref, vs_ref, vw_ref,
                   kcin_ref, vcin_ref, gn_ref, gm_ref):
    x = x_ref[0]
    tm = x.shape[0]
    mod = mod_ref[0]
    sh1, sc1 = mod[0:1], mod[1:2]
    h = _rms(x, D_MODEL) * g1_ref[...] * (1.0 + sc1) + sh1
    hb = h.astype(BF16)

    cos_m, sin_m, cos_n, sin_n = _rope_multipliers(cs_ref[0], texp_ref, trow_ref)
    lane = lax.broadcasted_iota(jnp.int32, (tm, LANES), 1)
    blk = lambda a, i: a[:, HEAD_PAD * i:HEAD_PAD * (i + 1)]
    hm = MLA_ROPE // 2
    hn = NSA_ROT // 2

    cq = _dot(hb, wcq_ref[...])
    cqn = (_rms(cq, MLA_Q_LORA) * qag_ref[...]).astype(BF16)
    q = _dot(cqn, wqb_ref[...])
    mqg = mqg_ref[...]
    m_scale = MLA_QK ** -0.5 * LOG2E
    for hd in range(MLA_HEADS):
        qh = _rope(_rms(blk(q, hd), MLA_QK) * mqg, cos_m, sin_m, MLA_NOPE, hm) * m_scale
        qm_ref[0, :, HEAD_PAD * hd:HEAD_PAD * (hd + 1)] = qh.astype(BF16)

    zs = _dot(hb, wsm_ref[...])
    kpe = jnp.where((lane >= MLA_NOPE) & (lane < MLA_QK), zs, 0.0)
    kpe_ss = jnp.sum(kpe * kpe, axis=-1, keepdims=True)
    kr = _rope(kpe * mkr_ref[...], cos_m, sin_m, MLA_NOPE, hm)
    ckv = _dot(hb, wckv_ref[...])
    ckvn = (_rms(ckv, MLA_KV_LORA) * kvag_ref[...]).astype(BF16)
    kv = _dot(ckvn, wkvb_ref[...])
    mkn = mkn_ref[...]
    for hd in range(MLA_HEADS):
        kn = blk(kv, hd)
        inv = lax.rsqrt((jnp.sum(kn * kn, axis=-1, keepdims=True) + kpe_ss) * (1.0 / MLA_QK) + EPS)
        km_ref[0, :, HEAD_PAD * hd:HEAD_PAD * (hd + 1)] = ((kn * mkn + kr) * inv).astype(BF16)
    vm_ref[0] = (kv[:, MLA_HEADS * HEAD_PAD:] + vone_ref[...]).astype(BF16)

    qn = _dot(hb, wqn_ref[...])
    nqg = nqg_ref[...]
    n_scale = NSA_HEAD ** -0.5 * LOG2E
    for hd in range(NSA_HEADS):
        qh = _rope(_rms(blk(qn, hd), NSA_HEAD) * nqg, cos_n, sin_n, 0, hn) * n_scale
        qn_ref[0, :, HEAD_PAD * hd:HEAD_PAD * (hd + 1)] = qh.astype(BF16)

    kv6 = _dot(hb, wkv6_ref[...])
    tok = pl.program_id(1) * tm + lax.broadcasted_iota(jnp.int32, (tm, 1), 0)
    sblk = lax.shift_right_logical(tok, SEL_LEN.bit_length() - 1)
    ind = jnp.where(lane - NSA_HEAD == sblk, NEG, 0.0)
    vone = vone_ref[:, 0:HEAD_PAD]
    nksg, nkwg = nksg_ref[...], nkwg_ref[...]
    kcin_ref[0] = blk(kv6, 0)
    vcin_ref[0] = blk(kv6, 1)
    for g in range(NSA_KV_GROUPS):
        ks =_rope(_rms(blk(kv6, 2 + g), NSA_HEAD) * nksg, cos_n, sin_n, 0, hn)
        ks_ref[0, g] = (ks + ind).astype(BF16)
        vs_ref[0, g] = (blk(kv6, 4 + g) + vone).astype(BF16)
        kw = _rope(_rms(blk(kv6, 6 + g), NSA_HEAD) * nkwg, cos_n, sin_n, 0, hn)
        kw_ref[0, g] = kw.astype(BF16)
        vw_ref[0, g] = (blk(kv6, 8 + g) + vone).astype(BF16)

    gn_ref[0] = _sigmoid(zs)
    gm_ref[0] = _sigmoid(_dot(hb, wgm_ref[...])).astype(BF16)


def _inproj(x, mod, cs, consts, weights, rows):
    B, S, D = x.shape
    tm = TM_IN
    tok = lambda w: pl.BlockSpec((1, tm, w), lambda b, i: (b, i, 0))
    head = lambda n, w: pl.BlockSpec((1, n, tm, w), lambda b, i: (b, 0, i, 0))
    operands = list(consts) + list(weights[:6]) + [rows[0], weights[6], rows[1], weights[7]] + list(rows[2:])
    in_specs = [tok(D), pl.BlockSpec((1, N_MOD, D), lambda b, i: (b, 0, 0)), tok(LANES)]
    in_specs += [_const_spec(a.shape) for a in operands]
    G = NSA_KV_GROUPS
    sds = jax.ShapeDtypeStruct
    wide = MLA_HEADS * HEAD_PAD
    outs = [
        (tok(wide), sds((B, S, wide), BF16)),
        (tok(wide), sds((B, S, wide), BF16)),
        (tok(wide), sds((B, S, wide), BF16)),
        (tok(wide), sds((B, S, wide), BF16)),
        (head(G, HEAD_PAD), sds((B, G, S, HEAD_PAD), BF16)),
        (head(G, HEAD_PAD), sds((B, G, S, HEAD_PAD), BF16)),
        (head(G, HEAD_PAD), sds((B, G, S, HEAD_PAD), BF16)),
        (head(G, HEAD_PAD), sds((B, G, S, HEAD_PAD), BF16)),
        (tok(KV_W), sds((B, S, KV_W), F32)),
        (tok(KV_W), sds((B, S, KV_W), F32)),
        (tok(LANES), sds((B, S, LANES), F32)),
        (tok(2 * D), sds((B, S, 2 * D), BF16)),
    ]
    return pl.pallas_call(
        _inproj_kernel,
        grid=(B, S // tm),
        in_specs=in_specs,
        out_specs=[o[0] for o in outs],
        out_shape=[o[1] for o in outs],
        compiler_params=pltpu.CompilerParams(dimension_semantics=("arbitrary", "arbitrary"),
                                             vmem_limit_bytes=VMEM_LIMIT),
        name="inproj_prep",
    )(x, mod, cs, *operands)


def _compress_kernel(kcin_ref, vcin_ref, pk_ref, pv_ref, w1k_ref, w2k_ref, w1v_ref, w2v_ref,
                     kcg_ref, cs_ref, texp_ref, trow_ref, kc_ref, vc_ref):
    n = kcin_ref.shape[1] // CMP_STRIDE

    def hidden(cin_ref, pos_ref, w1_ref):
        a = b = None
        for l in range(0, CMP_STRIDE, 2):
            t0 = cin_ref[0, pl.ds(l, n, stride=CMP_STRIDE), :]
            t1 = cin_ref[0, pl.ds(l + 1, n, stride=CMP_STRIDE), :]
            pair = lambda o: jnp.concatenate([t0 + pos_ref[o + l:o + l + 1],
                                              t1 + pos_ref[o + l + 1:o + l + 2]], axis=-1).astype(BF16)
            da = _dot(pair(0), w1_ref[l // 2])
            db = _dot(pair(CMP_STRIDE), w1_ref[(CMP_STRIDE + l) // 2])
            a, b = (da, db) if a is None else (a + da, b + db)
        hid = a + pltpu.roll(b, n - 1, 0)
        return (hid * _sigmoid(hid)).astype(BF16)

    kc = _dot(hidden(kcin_ref, pk_ref, w1k_ref), w2k_ref[...])
    _, _, cos_n, sin_n = _rope_multipliers(cs_ref[0], texp_ref, trow_ref)
    vc = _dot(hidden(vcin_ref, pv_ref, w1v_ref), w2v_ref[...])
    for g in range(NSA_KV_GROUPS):
        kg = _rms(kc[:, HEAD_PAD * g:HEAD_PAD * (g + 1)], NSA_HEAD) * kcg_ref[...]
        kc_ref[0, g] = _rope(kg, cos_n, sin_n, 0, NSA_ROT // 2).astype(BF16)
        vc_ref[0, g] = vc[:, NSA_HEAD * g:NSA_HEAD * (g + 1)].astype(BF16)


def _compress(kcin, vcin, pk, pv, w1k, w2k, w1v, w2v, kcg, cs_end, texp, trow):
    B, S, w = kcin.shape
    G = NSA_KV_GROUPS
    n = S // CMP_STRIDE
    oblk = lambda wd: pl.BlockSpec((1, G, n, wd), lambda b: (b, 0, 0, 0))
    consts = (pk, pv, w1k, w2k, w1v, w2v, kcg)
    return pl.pallas_call(
        _compress_kernel,
        grid=(B,),
        in_specs=[pl.BlockSpec((1, S, w), lambda b: (b, 0, 0))] * 2 + [_const_spec(a.shape) for a in consts] +
                 [pl.BlockSpec((1, n, LANES), lambda b: (b, 0, 0)),
                  _const_spec(texp.shape), _const_spec(trow.shape)],
        out_specs=[oblk(HEAD_PAD), oblk(NSA_HEAD)],
        out_shape=[jax.ShapeDtypeStruct((B, G, n, HEAD_PAD), BF16),
                   jax.ShapeDtypeStruct((B, G, n, NSA_HEAD), BF16)],
        name="nsa_compress",
    )(kcin, vcin, *consts, cs_end, texp, trow)


def _nsa_kernel(q_ref, kc_ref, vc_ref, ks_ref, vs_ref, kw_ref, vw_ref, gn_ref,
                ovt_ref, gexp_ref, dbias_ref, wbias_ref, o_ref, imp_ref):
    tq = dbias_ref.shape[0]
    S = q_ref.shape[1]
    R = NSA_REP
    M = R * tq
    n_sel = imp_ref.shape[0]
    ncp = kc_ref.shape[2]
    span = WINDOW + tq
    grp = pl.program_id(1)
    dbias = dbias_ref[...]
    kf = lambda a, b: ks_ref[0, 0, a:b, :]
    vf = lambda a, b: vs_ref[0, 0, a:b, :]
    row = lax.broadcasted_iota(jnp.int32, (M, 1), 0)
    n_idx = lax.broadcasted_iota(jnp.int32, (M, ncp), 1)
    j = lax.broadcasted_iota(jnp.int32, (n_sel, tq), 0)
    head_q = lambda i, r: q_ref[0, i * tq:(i + 1) * tq, HEAD_PAD * r:HEAD_PAD * (r + 1)]
    tile_q = lambda i: jnp.concatenate([head_q(i, r) for r in range(R)], axis=0)
    tiles = {}

    def compressed_and_select(i):
        q0 = i * tq
        t = q0 + jnp.bitwise_and(row, tq - 1)
        s = _dot_nt(tile_q(i), kc_ref[0, 0])
        valid = (n_idx * CMP_STRIDE + (CMP_LEN - 1)) <= t
        sm = jnp.where(valid, s, NEG)
        e = jnp.where(valid, jnp.exp2(sm - _rowmax(sm)), 0.0)
        den = jnp.sum(e, axis=-1, keepdims=True)
        p_c = e / jnp.where(den > 0.0, den, 1.0)
        o_c = _dot(p_c.astype(BF16), vc_ref[0, 0])
        psum = p_c[0:tq]
        for r in range(1, R):
            psum = psum + p_c[r * tq:(r + 1) * tq]
        hi, lo = _split_hilo(psum.T)
        imp = (_dot(ovt_ref[...], hi) + _dot(ovt_ref[...], lo))[0:n_sel]
        cur = lax.shift_right_logical(q0 + lax.broadcasted_iota(jnp.int32, (1, tq), 1),
                                      SEL_LEN.bit_length() - 1)
        forced = (j == 0) | (j == cur) | (j == cur - 1)
        imp = jnp.where(forced, imp + FORCE_BONUS, imp)
        imp = jnp.where(j <= cur, imp, NEG)
        imp_ref[...] = imp
        cnt = jnp.zeros((n_sel, tq), F32)
        for jj in range(n_sel):
            other = imp_ref[jj:jj + 1, :]
            beats = (other > imp) | ((other == imp) & (j > jj))
            cnt = cnt + jnp.where(beats, 1.0, 0.0)
        nsel = jnp.where((cnt < float(SEL_TOP)) & (j <= cur), 0.0, 1.0)
        nsel = jnp.concatenate([jnp.zeros((NSA_HEAD, tq), F32), nsel,
                                jnp.zeros((LANES - NSA_HEAD - n_sel, tq), F32)], axis=0).T.astype(BF16)
        tiles[i] = dict(o_c=o_c, nsel=nsel, o_s=[], o_w=[])

    def scores(u):
        kind, i, r = u
        if kind == "cmp":
            compressed_and_select(i)
            return None
        if kind == "sel":
            return _attention_scores(head_q(i, r) + tiles[i]["nsel"], kf, (i + 1) * tq, dbias)
        w0 = max(i * tq - WINDOW, 0)
        wb = wbias_ref[min(i, 1)]
        sw = _dot_nt(head_q(i, r), kw_ref[0, 0, w0:w0 + span, :]) + wb
        return sw, _rowmax(sw)

    def values(u, sc):
        kind, i, r = u
        if kind == "sel":
            tiles[i]["o_s"].append(_attention_values(sc, vf, (i + 1) * tq))
        elif kind == "win":
            sw, mw = sc
            w0 = max(i * tq - WINDOW, 0)
            acc_w = _dot(jnp.exp2(sw - mw).astype(BF16), vw_ref[0, 0, w0:w0 + span, :])
            tiles[i]["o_w"].append(acc_w[:, :NSA_HEAD] / acc_w[:, NSA_HEAD:NSA_HEAD + 1])
        if kind == "win" and r == R - 1:
            tile = tiles.pop(i)
            g_hi, g_lo = _split_hilo(gn_ref[0, i * tq:(i + 1) * tq, :])
            o_c = jnp.concatenate([tile["o_c"][r * tq:(r + 1) * tq] for r in range(R)], axis=-1)
            branches = (o_c, jnp.concatenate(tile["o_s"], axis=-1), jnp.concatenate(tile["o_w"], axis=-1))
            out = None
            for br, o_b in enumerate(branches):
                gate = _dot(g_hi, gexp_ref[grp, br]) + _dot(g_lo, gexp_ref[grp, br])
                out = gate * o_b if out is None else out + gate * o_b
            o_ref[0, i * tq:(i + 1) * tq, :] = out.astype(BF16)

    nq = S // tq
    units = [("cmp", 0, 0)]
    for i in range(nq):
        units += [("cmp", i + 1, 0)] if i + 1 < nq else []
        for r in range(R):
            units += [("sel", i, r), ("win", i, r)]
    pending = [scores(u) for u in units[:ATT_LOOKAHEAD]]
    for n, u in enumerate(units):
        if n + ATT_LOOKAHEAD < len(units):
            pending.append(scores(units[n + ATT_LOOKAHEAD]))
        values(u, pending.pop(0))


def _nsa_attention(qn, kc, vc, ks, vs, kw, vw, gn, ovt, gexp, dbias, wbias):
    B, S, _ = qn.shape
    G, R, Dh = NSA_KV_GROUPS, NSA_REP, NSA_HEAD
    ncp = kc.shape[2]
    full = pl.BlockSpec((1, 1, S, HEAD_PAD), lambda b, g: (b, g, 0, 0))
    cmp_spec = lambda w: pl.BlockSpec((1, 1, ncp, w), lambda b, g: (b, g, 0, 0))
    return pl.pallas_call(
        _nsa_kernel,
        grid=(B, G),
        in_specs=[pl.BlockSpec((1, S, R * HEAD_PAD), lambda b, g: (b, 0, g)),
                  cmp_spec(HEAD_PAD), cmp_spec(Dh), full, full, full, full,
                  pl.BlockSpec((1, S, LANES), lambda b, g: (b, 0, 0)),
                  _const_spec(ovt.shape), _const_spec(gexp.shape),
                  _const_spec(dbias.shape), _const_spec(wbias.shape)],
        out_specs=pl.BlockSpec((1, S, R * Dh), lambda b, g: (b, 0, g)),
        out_shape=jax.ShapeDtypeStruct((B, S, G * R * Dh), BF16),
        scratch_shapes=[pltpu.VMEM((S // SEL_LEN, TQ_ATT), F32)],
        compiler_params=pltpu.CompilerParams(dimension_semantics=("arbitrary",) * 2,
                                             vmem_limit_bytes=VMEM_LIMIT),
        name="nsa_attention",
    )(qn, kc, vc, ks, vs, kw, vw, gn, ovt, gexp, dbias, wbias)


def _mla_kernel(q_ref, k_ref, v_ref, dbias_ref, o_ref):
    tq = dbias_ref.shape[0]
    S = q_ref.shape[1]
    dbias = dbias_ref[...]
    units = [(i, hh) for i in range(S // tq) for hh in range(2)]

    def scores(u):
        i, hh = u
        cols = slice(HEAD_PAD * hh, HEAD_PAD * (hh + 1))
        q = q_ref[0, i * tq:(i + 1) * tq, cols]
        return _attention_scores(q, lambda a, b: k_ref[0, a:b, cols], (i + 1) * tq, dbias)

    def values(u, sc):
        i, hh = u
        cols = slice(HEAD_PAD * hh, HEAD_PAD * (hh + 1))
        o = _attention_values(sc, lambda a, b: v_ref[0, a:b, cols], (i + 1) * tq)
        o_ref[0, i * tq:(i + 1) * tq, MLA_V * hh:MLA_V * (hh + 1)] = o.astype(BF16)

    pending = [scores(u) for u in units[:ATT_LOOKAHEAD]]
    for n, u in enumerate(units):
        if n + ATT_LOOKAHEAD < len(units):
            pending.append(scores(units[n + ATT_LOOKAHEAD]))
        values(u, pending.pop(0))


def _mla_attention(qm, km, vm, dbias):
    B, S, _ = qm.shape
    pair = pl.BlockSpec((1, S, 2 * HEAD_PAD), lambda b, h: (b, 0, h))
    return pl.pallas_call(
        _mla_kernel,
        grid=(B, MLA_HEADS // 2),
        in_specs=[pair, pair, pair, _const_spec(dbias.shape)],
        out_specs=pl.BlockSpec((1, S, 2 * MLA_V), lambda b, h: (b, 0, h)),
        out_shape=jax.ShapeDtypeStruct((B, S, MLA_HEADS * MLA_V), BF16),
        compiler_params=pltpu.CompilerParams(dimension_semantics=("arbitrary",) * 2,
                                             vmem_limit_bytes=VMEM_LIMIT),
        name="mla_attention",
    )(qm, km, vm, dbias)


def _out_ffn_kernel(x_ref, om_ref, on_ref, gm_ref, mod_ref, g2_ref,
                    wom_ref, won_ref, wout_ref, wg_ref, wu_ref, wd_ref, o_ref):
    x = x_ref[0]
    mod = mod_ref[0]
    gt1, sh2, sc2, gt2 = mod[2:3], mod[3:4], mod[4:5], mod[5:6]
    ym = _dot(om_ref[0], wom_ref[...])
    yn = _dot(on_ref[0], won_ref[...])
    merged = gm_ref[0, :, :D_MODEL] * ym + gm_ref[0, :, D_MODEL:] * yn
    x1 = x + gt1 * _dot(merged.astype(BF16), wout_ref[...])
    h2 = (_rms(x1, D_MODEL) * g2_ref[...] * (1.0 + sc2) + sh2).astype(BF16)
    acc = jnp.zeros(x.shape, F32)
    for c in range(D_FF // FF_CHUNK):
        sl = slice(c * FF_CHUNK, (c + 1) * FF_CHUNK)
        g = _dot(h2, wg_ref[:, sl])
        u = _dot(h2, wu_ref[:, sl])
        a = (g * _sigmoid(g) * u).astype(BF16)
        acc = acc + _dot(a, wd_ref[sl, :])
    o_ref[0] = x1 + gt2 * acc


def _out_ffn(x, om, on, gm, mod, g2, wom, won, wout, wg, wu, wd):
    B, S, D = x.shape
    tm = TM_OUT
    tok = lambda w: pl.BlockSpec((1, tm, w), lambda b, i: (b, i, 0))
    wspec = lambda w: pl.BlockSpec(w.shape, lambda b, i: (0, 0), pipeline_mode=pl.Buffered(1))
    return pl.pallas_call(
        _out_ffn_kernel,
        grid=(B, S // tm),
        in_specs=[tok(D), tok(om.shape[2]), tok(on.shape[2]), tok(2 * D),
                  pl.BlockSpec((1, N_MOD, D), lambda b, i: (b, 0, 0)),
                  _const_spec(g2.shape)] + [wspec(w) for w in (wom, won, wout, wg, wu, wd)],
        out_specs=tok(D),
        out_shape=jax.ShapeDtypeStruct((B, S, D), F32),
        compiler_params=pltpu.CompilerParams(dimension_semantics=("arbitrary", "arbitrary"),
                                             vmem_limit_bytes=VMEM_LIMIT),
        name="out_ffn",
    )(x, om, on, gm, mod, g2, wom, won, wout, wg, wu, wd)


def _rope_expansion():
    texp = np.zeros((LANES, 4 * LANES), np.float32)
    trow = np.zeros((1, 4 * LANES), np.float32)
    hm, hn = MLA_ROPE // 2, NSA_ROT // 2
    trow[0, 0:LANES] = 1.0
    trow[0, 2 * LANES:3 * LANES] = 1.0
    for i in range(hm):
        for off, sgn in ((MLA_NOPE + i, -1.0), (MLA_NOPE + hm + i, 1.0)):
            texp[i, off] = 1.0
            trow[0, off] = 0.0
            texp[N_FREQ + i, LANES + off] = sgn
    for i in range(hn):
        for off, sgn in ((i, -1.0), (hn + i, 1.0)):
            texp[hm + i, 2 * LANES + off] = 1.0
            trow[0, 2 * LANES + off] = 0.0
            texp[N_FREQ + hm + i, 3 * LANES + off] = sgn
    return jnp.asarray(texp, BF16), jnp.asarray(trow, F32)


def _mask_tables(S):
    tq = TQ_ATT
    n_chunk = S // CMP_STRIDE
    n_sel = S // SEL_LEN
    starts = np.arange(n_chunk) * CMP_STRIDE
    sel_start = np.arange(LANES) * SEL_LEN
    ovt = ((starts[None, :] < sel_start[:, None] + SEL_LEN) &
           (starts[None, :] + CMP_LEN > sel_start[:, None]) &
           (np.arange(n_chunk)[None, :] < n_chunk - 1) &
           (np.arange(LANES)[:, None] < n_sel))
    gcol = np.arange(LANES)[:, None]
    head = np.arange(NSA_REP * NSA_HEAD)[None, :] // NSA_HEAD
    gexp = np.stack([np.stack([gcol == (g * NSA_REP + head) * N_NSA_BRANCH + br
                               for br in range(N_NSA_BRANCH)]) for g in range(NSA_KV_GROUPS)])
    qi = np.arange(tq)[:, None]
    dbias = np.where(np.arange(tq)[None, :] <= qi, 0.0, NEG)
    kk = np.arange(WINDOW + tq)[None, :]
    band = lambda d: np.where((d >= 0) & (d < WINDOW), 0.0, NEG)
    wbias = np.stack([band(qi - kk), band(qi + WINDOW - kk)])
    return (jnp.asarray(ovt, BF16), jnp.asarray(gexp, BF16),
            jnp.asarray(dbias, F32), jnp.asarray(wbias, F32))


def _pad_heads(w, n_heads, width):
    k = w.shape[0]
    w = w.reshape(k, n_heads, width)
    return jnp.pad(w, ((0, 0), (0, 0), (0, HEAD_PAD - width))).reshape(k, n_heads * HEAD_PAD)


def _block_diag(w, n):
    k, m = w.shape
    eye = jnp.eye(n, dtype=w.dtype)
    return (eye[:, None, :, None] * w[None, :, None, :]).reshape(n * k, n * m)


def _cmp_w1_pairs(w1):
    per_tok = w1.reshape(CMP_LEN, NSA_HEAD, w1.shape[1])
    bd = jax.vmap(lambda w: _block_diag(w, NSA_KV_GROUPS))(per_tok)
    return bd.reshape(CMP_LEN // 2, 2 * KV_W, NSA_KV_GROUPS * w1.shape[1])


def _pad_row(g, lo=0):
    return jnp.pad(g, (lo, HEAD_PAD - lo - g.shape[0])).reshape(1, HEAD_PAD)


def _layer(x, mod, cs, p):
    B, S, D = x.shape
    w_in = p["w_in"]
    o = 0
    cols = {}
    for name, wdt in (("cq", MLA_Q_LORA), ("ckv", MLA_KV_LORA), ("kpe", MLA_ROPE),
                      ("qn", NSA_HEADS * NSA_HEAD), ("kc", KV_W), ("vc", KV_W), ("ks", KV_W),
                      ("vs", KV_W), ("kw", KV_W), ("vw", KV_W),
                      ("gn", NSA_HEADS * N_NSA_BRANCH), ("gm", 2 * D)):
        cols[name] = w_in[:, o:o + wdt]
        o += wdt
    G = NSA_KV_GROUPS
    n_gate = NSA_HEADS * N_NSA_BRANCH
    zc = lambda n: jnp.zeros((D, n), F32)
    wsm = jnp.concatenate([cols["gn"], zc(MLA_NOPE - n_gate), cols["kpe"], zc(LANES - MLA_QK)], axis=1)
    wkv6 = jnp.concatenate([cols["kc"], cols["vc"]] +
                           [_pad_heads(cols[k], G, NSA_HEAD) for k in ("ks", "vs", "kw", "vw")], axis=1)
    wkvb = p["mla_w_kv_b"].reshape(MLA_KV_LORA, MLA_HEADS, MLA_NOPE + MLA_V)
    wkvb = jnp.concatenate([_pad_heads(wkvb[:, :, :MLA_NOPE].reshape(MLA_KV_LORA, -1), MLA_HEADS, MLA_NOPE),
                            _pad_heads(wkvb[:, :, MLA_NOPE:].reshape(MLA_KV_LORA, -1), MLA_HEADS, MLA_V)], axis=1)
    bf = lambda w: w.astype(BF16)
    row = lambda g: g.reshape(1, -1)
    weights = tuple(bf(w) for w in (cols["cq"], cols["ckv"], wsm, _pad_heads(cols["qn"], NSA_HEADS, NSA_HEAD),
                                    wkv6, cols["gm"],
                                    _pad_heads(p["mla_w_q_b"], MLA_HEADS, MLA_QK), wkvb))
    vone = jnp.tile(jnp.zeros((1, HEAD_PAD), F32).at[0, MLA_V].set(1.0), (1, MLA_HEADS))
    rows = (row(p["mla_q_a_gain"]), row(p["mla_kv_a_gain"]),
            _pad_row(p["mla_q_gain"]), _pad_row(p["mla_k_gain"][:MLA_NOPE]),
            _pad_row(p["mla_k_gain"][MLA_NOPE:], MLA_NOPE),
            _pad_row(p["nsa_q_gain"]), _pad_row(p["nsa_ks_gain"]), _pad_row(p["nsa_kw_gain"]), vone)
    texp, trow = _rope_expansion()
    (qm, km, vm, qn, ks, kw, vs, vw, kcin, vcin, gn, gm) = _inproj(
        x, mod, cs, (row(p["norm1_gain"]), texp, trow), weights, rows)

    n_chunk = S // CMP_STRIDE
    cs_end = cs[:, CMP_LEN - 1::CMP_STRIDE]
    cs_end = jnp.pad(cs_end, ((0, 0), (0, n_chunk - cs_end.shape[1]), (0, 0)))
    w2k = jnp.pad(p["cmp_w2_k"], ((0, 0), (0, HEAD_PAD - NSA_HEAD)))
    kc, vc = _compress(kcin, vcin, jnp.tile(p["cmp_pos_k"], (1, G)), jnp.tile(p["cmp_pos_v"], (1, G)),
                       bf(_cmp_w1_pairs(p["cmp_w1_k"])), bf(_block_diag(w2k, G)),
                       bf(_cmp_w1_pairs(p["cmp_w1_v"])), bf(_block_diag(p["cmp_w2_v"], G)),
                       _pad_row(p["nsa_kc_gain"]), cs_end, texp, trow)

    ovt, gexp, dbias, wbias = _mask_tables(S)
    o_nsa = _nsa_attention(qn, kc, vc, ks, vs, kw, vw, gn, ovt, gexp, dbias, wbias)
    o_mla = _mla_attention(qm, km, vm, dbias)

    return _out_ffn(x, o_mla, o_nsa, gm, mod, row(p["norm2_gain"]),
                    bf(p["w_o_mla"]), bf(p["w_o_nsa"]), bf(p["w_out"]),
                    bf(p["ffn_w_gate"]), bf(p["ffn_w_up"]), bf(p["ffn_w_down"]))


def kernel(x, c, positions, ada_w, ada_b, norm1_gain, w_in, mla_q_a_gain, mla_w_q_b, mla_kv_a_gain, mla_w_kv_b, mla_q_gain, mla_k_gain, nsa_q_gain, nsa_kc_gain, nsa_ks_gain, nsa_kw_gain, cmp_pos_k, cmp_w1_k, cmp_w2_k, cmp_pos_v, cmp_w1_v, cmp_w2_v, w_o_mla, w_o_nsa, w_out, norm2_gain, ffn_w_gate, ffn_w_up, ffn_w_down):
    params = dict(norm1_gain=norm1_gain, w_in=w_in, mla_q_a_gain=mla_q_a_gain, mla_w_q_b=mla_w_q_b,
                  mla_kv_a_gain=mla_kv_a_gain, mla_w_kv_b=mla_w_kv_b, mla_q_gain=mla_q_gain,
                  mla_k_gain=mla_k_gain, nsa_q_gain=nsa_q_gain, nsa_kc_gain=nsa_kc_gain,
                  nsa_ks_gain=nsa_ks_gain, nsa_kw_gain=nsa_kw_gain, cmp_pos_k=cmp_pos_k,
                  cmp_w1_k=cmp_w1_k, cmp_w2_k=cmp_w2_k, cmp_pos_v=cmp_pos_v, cmp_w1_v=cmp_w1_v,
                  cmp_w2_v=cmp_w2_v, w_o_mla=w_o_mla, w_o_nsa=w_o_nsa, w_out=w_out,
                  norm2_gain=norm2_gain, ffn_w_gate=ffn_w_gate, ffn_w_up=ffn_w_up, ffn_w_down=ffn_w_down)
    B = x.shape[0]
    inv_m = ROPE_THETA ** (-jnp.arange(0, MLA_ROPE, 2, dtype=F32) / MLA_ROPE)
    inv_n = ROPE_THETA ** (-jnp.arange(0, NSA_ROT, 2, dtype=F32) / NSA_ROT)
    n_unused = N_FREQ - inv_m.shape[0] - inv_n.shape[0]
    cs = _rope_tables(positions, jnp.concatenate([inv_m, inv_n, jnp.zeros((n_unused,), F32)]))
    depth = ada_w.shape[0]
    for l in range(depth):
        mod = _ada(c, ada_w, ada_b, l).reshape(B, N_MOD, D_MODEL)
        x = _layer(x, mod, cs, {k: v[l] for k, v in params.items()})
    return x
```

```python
import numpy as np
import jax
import jax.numpy as jnp
from jax import lax
from jax.experimental import pallas as pl
from jax.experimental.pallas import tpu as pltpu

F32 = jnp.float32
BF16 = jnp.bfloat16

D_MODEL = 1024
ROPE_THETA = 500000.0
EPS = 1e-6
NEG = -1e30
LOG2E = 1.4426950408889634

MLA_HEADS = 8
MLA_NOPE = 64
MLA_ROPE = 32
MLA_QK = MLA_NOPE + MLA_ROPE
MLA_V = 64
MLA_Q_LORA = 768
MLA_KV_LORA = 256

NSA_HEADS = 8
NSA_KV_GROUPS = 2
NSA_REP = NSA_HEADS // NSA_KV_GROUPS
NSA_HEAD = 64
NSA_ROT = NSA_HEAD // 4
CMP_LEN = 32
CMP_STRIDE = 16
CMP_HIDDEN = 256
SEL_LEN = 64
SEL_TOP = 8
WINDOW = 256
N_NSA_BRANCH = 3
FORCE_BONUS = 1e4
KV_W = NSA_KV_GROUPS * NSA_HEAD

D_FF = -(-8 * D_MODEL // (3 * 256)) * 256
N_MOD = 6
LANES = 128
HEAD_PAD = LANES
N_FREQ = 32

TM_IN = 256
TQ_ATT = 256
TM_OUT = 512
FF_CHUNK = D_FF // 2
ATT_LOOKAHEAD = 2
VMEM_LIMIT = 56 * 1024 * 1024


def _dot(a, b):
    return jnp.dot(a, b, preferred_element_type=F32)


def _dot_nt(a, b):
    return lax.dot_general(a, b, (((1,), (1,)), ((), ())), preferred_element_type=F32)


def _split_hilo(a):
    hi = a.astype(BF16)
    return hi, (a - hi.astype(F32)).astype(BF16)


def _dot_hilo(a, m):
    hi, lo = _split_hilo(a)
    return _dot(hi, m) + _dot(lo, m)


def _sigmoid(v):
    return 1.0 / (1.0 + jnp.exp(-v))


def _rms(v, n):
    return v * lax.rsqrt(jnp.sum(v * v, axis=-1, keepdims=True) * (1.0 / n) + EPS)


def _rope(v, cos_v, sin_v, lo, half):
    lane = lax.broadcasted_iota(jnp.int32, v.shape, 1)
    is_x1 = (lane >= lo) & (lane < lo + half)
    rot = jnp.where(is_x1, pltpu.roll(v, LANES - half, 1), pltpu.roll(v, half, 1))
    return v * cos_v + rot * sin_v


def _rope_multipliers(cs, texp_ref, trow_ref):
    tabs = _dot_hilo(cs, texp_ref[...]) + trow_ref[...]
    return tuple(tabs[:, LANES * i:LANES * (i + 1)] for i in range(4))


def _const_spec(shape):
    nd = len(shape)
    return pl.BlockSpec(shape, lambda *_: (0,) * nd)


def _rowmax(s):
    return jnp.max(s, axis=-1, keepdims=True)


def _attention_scores(q, k_ref, kmax, dbias):
    k0 = kmax - dbias.shape[1]
    s_d = _dot_nt(q, k_ref(k0, kmax)) + dbias
    m = _rowmax(s_d)
    s_m = None
    if k0 > 0:
        s_m = _dot_nt(q, k_ref(0, k0))
        m = jnp.maximum(m, _rowmax(s_m))
    return s_m, s_d, m


def _attention_values(scores, v_ref, kmax):
    s_m, s_d, m = scores
    k0 = kmax - s_d.shape[1]
    acc = _dot(jnp.exp2(s_d - m).astype(BF16), v_ref(k0, kmax))
    if s_m is not None:
        acc = acc + _dot(jnp.exp2(s_m - m).astype(BF16), v_ref(0, k0))
    return acc[:, :NSA_HEAD] / acc[:, NSA_HEAD:NSA_HEAD + 1]


def _rope_kernel(pos_ref, inv_ref, cs_ref):
    ang = pos_ref[0].astype(F32) * inv_ref[...]
    nf, S = ang.shape
    rows = jnp.concatenate([jnp.cos(ang), jnp.sin(ang), jnp.zeros((LANES - 2 * nf, S), F32)], axis=0)
    cs_ref[0] = rows.T


def _rope_tables(positions, inv):
    B, S = positions.shape
    nf = inv.shape[0]
    return pl.pallas_call(
        _rope_kernel,
        grid=(B,),
        in_specs=[pl.BlockSpec((1, 1, S), lambda b: (b, 0, 0)),
                  _const_spec((nf, 1))],
        out_specs=pl.BlockSpec((1, S, LANES), lambda b: (b, 0, 0)),
        out_shape=jax.ShapeDtypeStruct((B, S, LANES), F32),
        name="rope_tables",
    )(positions.reshape(B, 1, S), inv.reshape(nf, 1))


def _ada_kernel(c_ref, w_ref, b_ref, o_ref):
    c = c_ref[...]
    sc = c * _sigmoid(c)
    o_ref[...] = jnp.dot(sc, w_ref[0], preferred_element_type=F32,
                         precision=lax.Precision.HIGHEST) + b_ref[0]


def _ada(c, w, b, layer):
    B, D = c.shape
    N = w.shape[2]
    tn = D_MODEL
    return pl.pallas_call(
        _ada_kernel,
        grid=(N // tn,),
        in_specs=[_const_spec((B, D)),
                  pl.BlockSpec((1, D, tn), lambda j: (layer, 0, j)),
                  pl.BlockSpec((1, 1, tn), lambda j: (layer, 0, j))],
        out_specs=pl.BlockSpec((B, tn), lambda j: (0, j)),
        out_shape=jax.ShapeDtypeStruct((B, N), F32),
        name="ada_mod",
    )(c, w, b.reshape(b.shape[0], 1, N))


def _inproj_kernel(x_ref, mod_ref, cs_ref, g1_ref, texp_ref, trow_ref,
                   wcq_ref, wckv_ref, wsm_ref, wqn_ref, wkv6_ref, wgm_ref,
                   qag_ref, wqb_ref, kvag_ref, wkvb_ref,
                   mqg_ref, mkn_ref, mkr_ref, nqg_ref, nksg_ref, nkwg_ref, vone_ref,
                   qm_ref, km_ref, vm_ref, qn_ref, ks_ref, kw_ref, vs_ref, vw_ref,
                   kcin_ref, vcin_ref, gn_ref, gm_ref):
    x = x_ref[0]
    tm = x.shape[0]
    mod = mod_ref[0]
    sh1, sc1 = mod[0:1], mod[1:2]
    h = _rms(x, D_MODEL) * g1_ref[...] * (1.0 + sc1) + sh1
    hb = h.astype(BF16)

    cos_m, sin_m, cos_n, sin_n = _rope_multipliers(cs_ref[0], texp_ref, trow_ref)
    lane = lax.broadcasted_iota(jnp.int32, (tm, LANES), 1)
    blk = lambda a, i: a[:, HEAD_PAD * i:HEAD_PAD * (i + 1)]
    hm = MLA_ROPE // 2
    hn = NSA_ROT // 2

    cq = _dot(hb, wcq_ref[...])
    ckv = _dot(hb, wckv_ref[...])
    zs = _dot(hb, wsm_ref[...])
    qn = _dot(hb, wqn_ref[...])
    kv6 = _dot(hb, wkv6_ref[...])
    cqn = (_rms(cq, MLA_Q_LORA) * qag_ref[...]).astype(BF16)
    q = _dot(cqn, wqb_ref[...])
    gates = _dot(hb, wgm_ref[...])
    ckvn = (_rms(ckv, MLA_KV_LORA) * kvag_ref[...]).astype(BF16)
    kv = _dot(ckvn, wkvb_ref[...])

    nqg = nqg_ref[...]
    n_scale = NSA_HEAD ** -0.5 * LOG2E
    for hd in range(NSA_HEADS):
        qh = _rope(_rms(blk(qn, hd), NSA_HEAD) * nqg, cos_n, sin_n, 0, hn) * n_scale
        qn_ref[0, :, HEAD_PAD * hd:HEAD_PAD * (hd + 1)] = qh.astype(BF16)

    tok = pl.program_id(1) * tm + lax.broadcasted_iota(jnp.int32, (tm, 1), 0)
    sblk = lax.shift_right_logical(tok, SEL_LEN.bit_length() - 1)
    ind = jnp.where(lane - NSA_HEAD == sblk, NEG, 0.0)
    vone = vone_ref[:, 0:HEAD_PAD]
    nksg, nkwg = nksg_ref[...], nkwg_ref[...]
    kcin_ref[0] = blk(kv6, 0)
    vcin_ref[0] = blk(kv6, 1)
    for g in range(NSA_KV_GROUPS):
        ks =_rope(_rms(blk(kv6, 2 + g), NSA_HEAD) * nksg, cos_n, sin_n, 0, hn)
        ks_ref[0, g] = (ks + ind).astype(BF16)
        vs_ref[0, g] = (blk(kv6, 4 + g) + vone).astype(BF16)
        kw = _rope(_rms(blk(kv6, 6 + g), NSA_HEAD) * nkwg, cos_n, sin_n, 0, hn)
        kw_ref[0, g] = kw.astype(BF16)
        vw_ref[0, g] = (blk(kv6, 8 + g) + vone).astype(BF16)

    mqg = mqg_ref[...]
    m_scale = MLA_QK ** -0.5 * LOG2E
    for hd in range(MLA_HEADS):
        qh = _rope(_rms(blk(q, hd), MLA_QK) * mqg, cos_m, sin_m, MLA_NOPE, hm) * m_scale
        qm_ref[0, :, HEAD_PAD * hd:HEAD_PAD * (hd + 1)] = qh.astype(BF16)

    gn_ref[0] = _sigmoid(zs)
    gm_ref[0] = _sigmoid(gates).astype(BF16)

    kpe = jnp.where((lane >= MLA_NOPE) & (lane < MLA_QK), zs, 0.0)
    kpe_ss = jnp.sum(kpe * kpe, axis=-1, keepdims=True)
    kr = _rope(kpe * mkr_ref[...], cos_m, sin_m, MLA_NOPE, hm)
    mkn = mkn_ref[...]
    for hd in range(MLA_HEADS):
        kn = blk(kv, hd)
        inv = lax.rsqrt((jnp.sum(kn * kn, axis=-1, keepdims=True) + kpe_ss) * (1.0 / MLA_QK) + EPS)
        km_ref[0, :, HEAD_PAD * hd:HEAD_PAD * (hd + 1)] = ((kn * mkn + kr) * inv).astype(BF16)
    vm_ref[0] = (kv[:, MLA_HEADS * HEAD_PAD:] + vone_ref[...]).astype(BF16)


def _inproj(x, mod, cs, consts, weights, rows):
    B, S, D = x.shape
    tm = TM_IN
    tok = lambda w: pl.BlockSpec((1, tm, w), lambda b, i: (b, i, 0))
    head = lambda n, w: pl.BlockSpec((1, n, tm, w), lambda b, i: (b, 0, i, 0))
    operands = list(consts) + list(weights[:6]) + [rows[0], weights[6], rows[1], weights[7]] + list(rows[2:])
    in_specs = [tok(D), pl.BlockSpec((1, N_MOD, D), lambda b, i: (b, 0, 0)), tok(LANES)]
    in_specs += [_const_spec(a.shape) for a in operands]
    G = NSA_KV_GROUPS
    sds = jax.ShapeDtypeStruct
    wide = MLA_HEADS * HEAD_PAD
    outs = [
        (tok(wide), sds((B, S, wide), BF16)),
        (tok(wide), sds((B, S, wide), BF16)),
        (tok(wide), sds((B, S, wide), BF16)),
        (tok(wide), sds((B, S, wide), BF16)),
        (head(G, HEAD_PAD), sds((B, G, S, HEAD_PAD), BF16)),
        (head(G, HEAD_PAD), sds((B, G, S, HEAD_PAD), BF16)),
        (head(G, HEAD_PAD), sds((B, G, S, HEAD_PAD), BF16)),
        (head(G, HEAD_PAD), sds((B, G, S, HEAD_PAD), BF16)),
        (tok(KV_W), sds((B, S, KV_W), F32)),
        (tok(KV_W), sds((B, S, KV_W), F32)),
        (tok(LANES), sds((B, S, LANES), F32)),
        (tok(2 * D), sds((B, S, 2 * D), BF16)),
    ]
    return pl.pallas_call(
        _inproj_kernel,
        grid=(B, S // tm),
        in_specs=in_specs,
        out_specs=[o[0] for o in outs],
        out_shape=[o[1] for o in outs],
        compiler_params=pltpu.CompilerParams(dimension_semantics=("arbitrary", "arbitrary"),
                                             vmem_limit_bytes=VMEM_LIMIT),
        name="inproj_prep",
    )(x, mod, cs, *operands)


def _compress_kernel(kcin_ref, vcin_ref, pk_ref, pv_ref, w1k_ref, w2k_ref, w1v_ref, w2v_ref,
                     kcg_ref, cs_ref, texp_ref, trow_ref, kc_ref, vc_ref):
    n = kcin_ref.shape[1] // CMP_STRIDE

    def hidden(cin_ref, pos_ref, w1_ref):
        a = b = None
        for l in range(0, CMP_STRIDE, 2):
            t0 = cin_ref[0, pl.ds(l, n, stride=CMP_STRIDE), :]
            t1 = cin_ref[0, pl.ds(l + 1, n, stride=CMP_STRIDE), :]
            pair = lambda o: jnp.concatenate([t0 + pos_ref[o + l:o + l + 1],
                                              t1 + pos_ref[o + l + 1:o + l + 2]], axis=-1).astype(BF16)
            da = _dot(pair(0), w1_ref[l // 2])
            db = _dot(pair(CMP_STRIDE), w1_ref[(CMP_STRIDE + l) // 2])
            a, b = (da, db) if a is None else (a + da, b + db)
        hid = a + pltpu.roll(b, n - 1, 0)
        return (hid * _sigmoid(hid)).astype(BF16)

    kc = _dot(hidden(kcin_ref, pk_ref, w1k_ref), w2k_ref[...])
    _, _, cos_n, sin_n = _rope_multipliers(cs_ref[0], texp_ref, trow_ref)
    vc = _dot(hidden(vcin_ref, pv_ref, w1v_ref), w2v_ref[...])
    for g in range(NSA_KV_GROUPS):
        kg = _rms(kc[:, HEAD_PAD * g:HEAD_PAD * (g + 1)], NSA_HEAD) * kcg_ref[...]
        kc_ref[0, g] = _rope(kg, cos_n, sin_n, 0, NSA_ROT // 2).astype(BF16)
        vc_ref[0, g] = vc[:, NSA_HEAD * g:NSA_HEAD * (g + 1)].astype(BF16)


def _compress(kcin, vcin, pk, pv, w1k, w2k, w1v, w2v, kcg, cs_end, texp, trow):
    B, S, w = kcin.shape
    G = NSA_KV_GROUPS
    n = S // CMP_STRIDE
    oblk = lambda wd: pl.BlockSpec((1, G, n, wd), lambda b: (b, 0, 0, 0))
    consts = (pk, pv, w1k, w2k, w1v, w2v, kcg)
    return pl.pallas_call(
        _compress_kernel,
        grid=(B,),
        in_specs=[pl.BlockSpec((1, S, w), lambda b: (b, 0, 0))] * 2 + [_const_spec(a.shape) for a in consts] +
                 [pl.BlockSpec((1, n, LANES), lambda b: (b, 0, 0)),
                  _const_spec(texp.shape), _const_spec(trow.shape)],
        out_specs=[oblk(HEAD_PAD), oblk(NSA_HEAD)],
        out_shape=[jax.ShapeDtypeStruct((B, G, n, HEAD_PAD), BF16),
                   jax.ShapeDtypeStruct((B, G, n, NSA_HEAD), BF16)],
        name="nsa_compress",
    )(kcin, vcin, *consts, cs_end, texp, trow)


def _nsa_kernel(q_ref, kc_ref, vc_ref, ks_ref, vs_ref, kw_ref, vw_ref, gn_ref,
                ovt_ref, gexp_ref, dbias_ref, wbias_ref, o_ref, imp_ref):
    tq = dbias_ref.shape[0]
    S = q_ref.shape[1]
    R = NSA_REP
    M = R * tq
    n_sel = imp_ref.shape[0]
    ncp = kc_ref.shape[2]
    span = WINDOW + tq
    grp = pl.program_id(1)
    dbias = dbias_ref[...]
    kf = lambda a, b: ks_ref[0, 0, a:b, :]
    vf = lambda a, b: vs_ref[0, 0, a:b, :]
    row = lax.broadcasted_iota(jnp.int32, (M, 1), 0)
    n_idx = lax.broadcasted_iota(jnp.int32, (M, ncp), 1)
    j = lax.broadcasted_iota(jnp.int32, (n_sel, tq), 0)
    head_q = lambda i, r: q_ref[0, i * tq:(i + 1) * tq, HEAD_PAD * r:HEAD_PAD * (r + 1)]
    tile_q = lambda i: jnp.concatenate([head_q(i, r) for r in range(R)], axis=0)
    tiles = {}

    def compressed_and_select(i):
        q0 = i * tq
        t = q0 + jnp.bitwise_and(row, tq - 1)
        s = _dot_nt(tile_q(i), kc_ref[0, 0])
        valid = (n_idx * CMP_STRIDE + (CMP_LEN - 1)) <= t
        sm = jnp.where(valid, s, NEG)
        e = jnp.where(valid, jnp.exp2(sm - _rowmax(sm)), 0.0)
        den = jnp.sum(e, axis=-1, keepdims=True)
        p_c = e / jnp.where(den > 0.0, den, 1.0)
        o_c = _dot(p_c.astype(BF16), vc_ref[0, 0])
        psum = p_c[0:tq]
        for r in range(1, R):
            psum = psum + p_c[r * tq:(r + 1) * tq]
        hi, lo = _split_hilo(psum.T)
        imp = (_dot(ovt_ref[...], hi) + _dot(ovt_ref[...], lo))[0:n_sel]
        cur = lax.shift_right_logical(q0 + lax.broadcasted_iota(jnp.int32, (1, tq), 1),
                                      SEL_LEN.bit_length() - 1)
        forced = (j == 0) | (j == cur) | (j == cur - 1)
        imp = jnp.where(forced, imp + FORCE_BONUS, imp)
        imp = jnp.where(j <= cur, imp, NEG)
        imp_ref[...] = imp
        cnt = jnp.zeros((n_sel, tq), F32)
        for jj in range(n_sel):
            other = imp_ref[jj:jj + 1, :]
            beats = (other > imp) | ((other == imp) & (j > jj))
            cnt = cnt + jnp.where(beats, 1.0, 0.0)
        nsel = jnp.where((cnt < float(SEL_TOP)) & (j <= cur), 0.0, 1.0)
        nsel = jnp.concatenate([jnp.zeros((NSA_HEAD, tq), F32), nsel,
                                jnp.zeros((LANES - NSA_HEAD - n_sel, tq), F32)], axis=0).T.astype(BF16)
        tiles[i] = dict(o_c=o_c, nsel=nsel, o_s=[], o_w=[])

    def scores(u):
        kind, i, r = u
        if kind == "cmp":
            compressed_and_select(i)
            return None
        if kind == "sel":
            return _attention_scores(head_q(i, r) + tiles[i]["nsel"], kf, (i + 1) * tq, dbias)
        w0 = max(i * tq - WINDOW, 0)
        wb = wbias_ref[min(i, 1)]
        sw = _dot_nt(head_q(i, r), kw_ref[0, 0, w0:w0 + span, :]) + wb
        return sw, _rowmax(sw)

    def values(u, sc):
        kind, i, r = u
        if kind == "sel":
            tiles[i]["o_s"].append(_attention_values(sc, vf, (i + 1) * tq))
        elif kind == "win":
            sw, mw = sc
            w0 = max(i * tq - WINDOW, 0)
            acc_w = _dot(jnp.exp2(sw - mw).astype(BF16), vw_ref[0, 0, w0:w0 + span, :])
            tiles[i]["o_w"].append(acc_w[:, :NSA_HEAD] / acc_w[:, NSA_HEAD:NSA_HEAD + 1])
        if kind == "win" and r == R - 1:
            tile = tiles.pop(i)
            g_hi, g_lo = _split_hilo(gn_ref[0, i * tq:(i + 1) * tq, :])
            o_c = jnp.concatenate([tile["o_c"][r * tq:(r + 1) * tq] for r in range(R)], axis=-1)
            branches = (o_c, jnp.concatenate(tile["o_s"], axis=-1), jnp.concatenate(tile["o_w"], axis=-1))
            out = None
            for br, o_b in enumerate(branches):
                gate = _dot(g_hi, gexp_ref[grp, br]) + _dot(g_lo, gexp_ref[grp, br])
                out = gate * o_b if out is None else out + gate * o_b
            o_ref[0, i * tq:(i + 1) * tq, :] = out.astype(BF16)

    nq = S // tq
    units = [("cmp", 0, 0)]
    for i in range(nq):
        units += [("cmp", i + 1, 0)] if i + 1 < nq else []
        for r in range(R):
            units += [("sel", i, r), ("win", i, r)]
    pending = [scores(u) for u in units[:ATT_LOOKAHEAD]]
    for n, u in enumerate(units):
        if n + ATT_LOOKAHEAD < len(units):
            pending.append(scores(units[n + ATT_LOOKAHEAD]))
        values(u, pending.pop(0))


def _nsa_attention(qn, kc, vc, ks, vs, kw, vw, gn, ovt, gexp, dbias, wbias):
    B, S, _ = qn.shape
    G, R, Dh = NSA_KV_GROUPS, NSA_REP, NSA_HEAD
    ncp = kc.shape[2]
    full = pl.BlockSpec((1, 1, S, HEAD_PAD), lambda b, g: (b, g, 0, 0))
    cmp_spec = lambda w: pl.BlockSpec((1, 1, ncp, w), lambda b, g: (b, g, 0, 0))
    return pl.pallas_call(
        _nsa_kernel,
        grid=(B, G),
        in_specs=[pl.BlockSpec((1, S, R * HEAD_PAD), lambda b, g: (b, 0, g)),
                  cmp_spec(HEAD_PAD), cmp_spec(Dh), full, full, full, full,
                  pl.BlockSpec((1, S, LANES), lambda b, g: (b, 0, 0)),
                  _const_spec(ovt.shape), _const_spec(gexp.shape),
                  _const_spec(dbias.shape), _const_spec(wbias.shape)],
        out_specs=pl.BlockSpec((1, S, R * Dh), lambda b, g: (b, 0, g)),
        out_shape=jax.ShapeDtypeStruct((B, S, G * R * Dh), BF16),
        scratch_shapes=[pltpu.VMEM((S // SEL_LEN, TQ_ATT), F32)],
        compiler_params=pltpu.CompilerParams(dimension_semantics=("arbitrary",) * 2,
                                             vmem_limit_bytes=VMEM_LIMIT),
        name="nsa_attention",
    )(qn, kc, vc, ks, vs, kw, vw, gn, ovt, gexp, dbias, wbias)


def _mla_kernel(q_ref, k_ref, v_ref, dbias_ref, o_ref):
    tq = dbias_ref.shape[0]
    S = q_ref.shape[1]
    dbias = dbias_ref[...]
    units = [(i, hh) for i in range(S // tq) for hh in range(2)]

    def scores(u):
        i, hh = u
        cols = slice(HEAD_PAD * hh, HEAD_PAD * (hh + 1))
        q = q_ref[0, i * tq:(i + 1) * tq, cols]
        return _attention_scores(q, lambda a, b: k_ref[0, a:b, cols], (i + 1) * tq, dbias)

    def values(u, sc):
        i, hh = u
        cols = slice(HEAD_PAD * hh, HEAD_PAD * (hh + 1))
        o = _attention_values(sc, lambda a, b: v_ref[0, a:b, cols], (i + 1) * tq)
        o_ref[0, i * tq:(i + 1) * tq, MLA_V * hh:MLA_V * (hh + 1)] = o.astype(BF16)

    pending = [scores(u) for u in units[:ATT_LOOKAHEAD]]
    for n, u in enumerate(units):
        if n + ATT_LOOKAHEAD < len(units):
            pending.append(scores(units[n + ATT_LOOKAHEAD]))
        values(u, pending.pop(0))


def _mla_attention(qm, km, vm, dbias):
    B, S, _ = qm.shape
    pair = pl.BlockSpec((1, S, 2 * HEAD_PAD), lambda b, h: (b, 0, h))
    return pl.pallas_call(
        _mla_kernel,
        grid=(B, MLA_HEADS // 2),
        in_specs=[pair, pair, pair, _const_spec(dbias.shape)],
        out_specs=pl.BlockSpec((1, S, 2 * MLA_V), lambda b, h: (b, 0, h)),
        out_shape=jax.ShapeDtypeStruct((B, S, MLA_HEADS * MLA_V), BF16),
        compiler_params=pltpu.CompilerParams(dimension_semantics=("arbitrary",) * 2,
                                             vmem_limit_bytes=VMEM_LIMIT),
        name="mla_attention",
    )(qm, km, vm, dbias)


def _out_ffn_kernel(x_ref, om_ref, on_ref, gm_ref, mod_ref, g2_ref,
                    wom_ref, won_ref, wout_ref, wg_ref, wu_ref, wd_ref, o_ref):
    x = x_ref[0]
    mod = mod_ref[0]
    gt1, sh2, sc2, gt2 = mod[2:3], mod[3:4], mod[4:5], mod[5:6]
    ym = _dot(om_ref[0], wom_ref[...])
    yn = _dot(on_ref[0], won_ref[...])
    merged = gm_ref[0, :, :D_MODEL] * ym + gm_ref[0, :, D_MODEL:] * yn
    x1 = x + gt1 * _dot(merged.astype(BF16), wout_ref[...])
    h2 = (_rms(x1, D_MODEL) * g2_ref[...] * (1.0 + sc2) + sh2).astype(BF16)
    acc = jnp.zeros(x.shape, F32)
    for c in range(D_FF // FF_CHUNK):
        sl = slice(c * FF_CHUNK, (c + 1) * FF_CHUNK)
        g = _dot(h2, wg_ref[:, sl])
        u = _dot(h2, wu_ref[:, sl])
        a = (g * _sigmoid(g) * u).astype(BF16)
        acc = acc + _dot(a, wd_ref[sl, :])
    o_ref[0] = x1 + gt2 * acc


def _out_ffn(x, om, on, gm, mod, g2, wom, won, wout, wg, wu, wd):
    B, S, D = x.shape
    tm = TM_OUT
    tok = lambda w: pl.BlockSpec((1, tm, w), lambda b, i: (b, i, 0))
    wspec = lambda w: pl.BlockSpec(w.shape, lambda b, i: (0, 0), pipeline_mode=pl.Buffered(1))
    return pl.pallas_call(
        _out_ffn_kernel,
        grid=(B, S // tm),
        in_specs=[tok(D), tok(om.shape[2]), tok(on.shape[2]), tok(2 * D),
                  pl.BlockSpec((1, N_MOD, D), lambda b, i: (b, 0, 0)),
                  _const_spec(g2.shape)] + [wspec(w) for w in (wom, won, wout, wg, wu, wd)],
        out_specs=tok(D),
        out_shape=jax.ShapeDtypeStruct((B, S, D), F32),
        compiler_params=pltpu.CompilerParams(dimension_semantics=("arbitrary", "arbitrary"),
                                             vmem_limit_bytes=VMEM_LIMIT),
        name="out_ffn",
    )(x, om, on, gm, mod, g2, wom, won, wout, wg, wu, wd)


def _rope_expansion():
    texp = np.zeros((LANES, 4 * LANES), np.float32)
    trow = np.zeros((1, 4 * LANES), np.float32)
    hm, hn = MLA_ROPE // 2, NSA_ROT // 2
    trow[0, 0:LANES] = 1.0
    trow[0, 2 * LANES:3 * LANES] = 1.0
    for i in range(hm):
        for off, sgn in ((MLA_NOPE + i, -1.0), (MLA_NOPE + hm + i, 1.0)):
            texp[i, off] = 1.0
            trow[0, off] = 0.0
            texp[N_FREQ + i, LANES + off] = sgn
    for i in range(hn):
        for off, sgn in ((i, -1.0), (hn + i, 1.0)):
            texp[hm + i, 2 * LANES + off] = 1.0
            trow[0, 2 * LANES + off] = 0.0
            texp[N_FREQ + hm + i, 3 * LANES + off] = sgn
    return jnp.asarray(texp, BF16), jnp.asarray(trow, F32)


def _mask_tables(S):
    tq = TQ_ATT
    n_chunk = S // CMP_STRIDE
    n_sel = S // SEL_LEN
    starts = np.arange(n_chunk) * CMP_STRIDE
    sel_start = np.arange(LANES) * SEL_LEN
    ovt = ((starts[None, :] < sel_start[:, None] + SEL_LEN) &
           (starts[None, :] + CMP_LEN > sel_start[:, None]) &
           (np.arange(n_chunk)[None, :] < n_chunk - 1) &
           (np.arange(LANES)[:, None] < n_sel))
    gcol = np.arange(LANES)[:, None]
    head = np.arange(NSA_REP * NSA_HEAD)[None, :] // NSA_HEAD
    gexp = np.stack([np.stack([gcol == (g * NSA_REP + head) * N_NSA_BRANCH + br
                               for br in range(N_NSA_BRANCH)]) for g in range(NSA_KV_GROUPS)])
    qi = np.arange(tq)[:, None]
    dbias = np.where(np.arange(tq)[None, :] <= qi, 0.0, NEG)
    kk = np.arange(WINDOW + tq)[None, :]
    band = lambda d: np.where((d >= 0) & (d < WINDOW), 0.0, NEG)
    wbias = np.stack([band(qi - kk), band(qi + WINDOW - kk)])
    return (jnp.asarray(ovt, BF16), jnp.asarray(gexp, BF16),
            jnp.asarray(dbias, F32), jnp.asarray(wbias, F32))


def _pad_heads(w, n_heads, width):
    k = w.shape[0]
    w = w.reshape(k, n_heads, width)
    return jnp.pad(w, ((0, 0), (0, 0), (0, HEAD_PAD - width))).reshape(k, n_heads * HEAD_PAD)


def _block_diag(w, n):
    k, m = w.shape
    eye = jnp.eye(n, dtype=w.dtype)
    return (eye[:, None, :, None] * w[None, :, None, :]).reshape(n * k, n * m)


def _cmp_w1_pairs(w1):
    per_tok = w1.reshape(CMP_LEN, NSA_HEAD, w1.shape[1])
    bd = jax.vmap(lambda w: _block_diag(w, NSA_KV_GROUPS))(per_tok)
    return bd.reshape(CMP_LEN // 2, 2 * KV_W, NSA_KV_GROUPS * w1.shape[1])


def _pad_row(g, lo=0):
    return jnp.pad(g, (lo, HEAD_PAD - lo - g.shape[0])).reshape(1, HEAD_PAD)


def _layer(x, mod, cs, p):
    B, S, D = x.shape
    w_in = p["w_in"]
    o = 0
    cols = {}
    for name, wdt in (("cq", MLA_Q_LORA), ("ckv", MLA_KV_LORA), ("kpe", MLA_ROPE),
                      ("qn", NSA_HEADS * NSA_HEAD), ("kc", KV_W), ("vc", KV_W), ("ks", KV_W),
                      ("vs", KV_W), ("kw", KV_W), ("vw", KV_W),
                      ("gn", NSA_HEADS * N_NSA_BRANCH), ("gm", 2 * D)):
        cols[name] = w_in[:, o:o + wdt]
        o += wdt
    G = NSA_KV_GROUPS
    n_gate = NSA_HEADS * N_NSA_BRANCH
    zc = lambda n: jnp.zeros((D, n), F32)
    wsm = jnp.concatenate([cols["gn"], zc(MLA_NOPE - n_gate), cols["kpe"], zc(LANES - MLA_QK)], axis=1)
    wkv6 = jnp.concatenate([cols["kc"], cols["vc"]] +
                           [_pad_heads(cols[k], G, NSA_HEAD) for k in ("ks", "vs", "kw", "vw")], axis=1)
    wkvb = p["mla_w_kv_b"].reshape(MLA_KV_LORA, MLA_HEADS, MLA_NOPE + MLA_V)
    wkvb = jnp.concatenate([_pad_heads(wkvb[:, :, :MLA_NOPE].reshape(MLA_KV_LORA, -1), MLA_HEADS, MLA_NOPE),
                            _pad_heads(wkvb[:, :, MLA_NOPE:].reshape(MLA_KV_LORA, -1), MLA_HEADS, MLA_V)], axis=1)
    bf = lambda w: w.astype(BF16)
    row = lambda g: g.reshape(1, -1)
    weights = tuple(bf(w) for w in (cols["cq"], cols["ckv"], wsm, _pad_heads(cols["qn"], NSA_HEADS, NSA_HEAD),
                                    wkv6, cols["gm"],
                                    _pad_heads(p["mla_w_q_b"], MLA_HEADS, MLA_QK), wkvb))
    vone = jnp.tile(jnp.zeros((1, HEAD_PAD), F32).at[0, MLA_V].set(1.0), (1, MLA_HEADS))
    rows = (row(p["mla_q_a_gain"]), row(p["mla_kv_a_gain"]),
            _pad_row(p["mla_q_gain"]), _pad_row(p["mla_k_gain"][:MLA_NOPE]),
            _pad_row(p["mla_k_gain"][MLA_NOPE:], MLA_NOPE),
            _pad_row(p["nsa_q_gain"]), _pad_row(p["nsa_ks_gain"]), _pad_row(p["nsa_kw_gain"]), vone)
    texp, trow = _rope_expansion()
    (qm, km, vm, qn, ks, kw, vs, vw, kcin, vcin, gn, gm) = _inproj(
        x, mod, cs, (row(p["norm1_gain"]), texp, trow), weights, rows)

    n_chunk = S // CMP_STRIDE
    cs_end = cs[:, CMP_LEN - 1::CMP_STRIDE]
    cs_end = jnp.pad(cs_end, ((0, 0), (0, n_chunk - cs_end.shape[1]), (0, 0)))
    w2k = jnp.pad(p["cmp_w2_k"], ((0, 0), (0, HEAD_PAD - NSA_HEAD)))
    kc, vc = _compress(kcin, vcin, jnp.tile(p["cmp_pos_k"], (1, G)), jnp.tile(p["cmp_pos_v"], (1, G)),
                       bf(_cmp_w1_pairs(p["cmp_w1_k"])), bf(_block_diag(w2k, G)),
                       bf(_cmp_w1_pairs(p["cmp_w1_v"])), bf(_block_diag(p["cmp_w2_v"], G)),
                       _pad_row(p["nsa_kc_gain"]), cs_end, texp, trow)

    ovt, gexp, dbias, wbias = _mask_tables(S)
    o_nsa = _nsa_attention(qn, kc, vc, ks, vs, kw, vw, gn, ovt, gexp, dbias, wbias)
    o_mla = _mla_attention(qm, km, vm, dbias)

    return _out_ffn(x, o_mla, o_nsa, gm, mod, row(p["norm2_gain"]),
                    bf(p["w_o_mla"]), bf(p["w_o_nsa"]), bf(p["w_out"]),
                    bf(p["ffn_w_gate"]), bf(p["ffn_w_up"]), bf(p["ffn_w_down"]))


def kernel(x, c, positions, ada_w, ada_b, norm1_gain, w_in, mla_q_a_gain, mla_w_q_b, mla_kv_a_gain, mla_w_kv_b, mla_q_gain, mla_k_gain, nsa_q_gain, nsa_kc_gain, nsa_ks_gain, nsa_kw_gain, cmp_pos_k, cmp_w1_k, cmp_w2_k, cmp_pos_v, cmp_w1_v, cmp_w2_v, w_o_mla, w_o_nsa, w_out, norm2_gain, ffn_w_gate, ffn_w_up, ffn_w_down):
    params = dict(norm1_gain=norm1_gain, w_in=w_in, mla_q_a_gain=mla_q_a_gain, mla_w_q_b=mla_w_q_b,
                  mla_kv_a_gain=mla_kv_a_gain, mla_w_kv_b=mla_w_kv_b, mla_q_gain=mla_q_gain,
                  mla_k_gain=mla_k_gain, nsa_q_gain=nsa_q_gain, nsa_kc_gain=nsa_kc_gain,
                  nsa_ks_gain=nsa_ks_gain, nsa_kw_gain=nsa_kw_gain, cmp_pos_k=cmp_pos_k,
                  cmp_w1_k=cmp_w1_k, cmp_w2_k=cmp_w2_k, cmp_pos_v=cmp_pos_v, cmp_w1_v=cmp_w1_v,
                  cmp_w2_v=cmp_w2_v, w_o_mla=w_o_mla, w_o_nsa=w_o_nsa, w_out=w_out,
                  norm2_gain=norm2_gain, ffn_w_gate=ffn_w_gate, ffn_w_up=ffn_w_up, ffn_w_down=ffn_w_down)
    B = x.shape[0]
    inv_m = ROPE_THETA ** (-jnp.arange(0, MLA_ROPE, 2, dtype=F32) / MLA_ROPE)
    inv_n = ROPE_THETA ** (-jnp.arange(0, NSA_ROT, 2, dtype=F32) / NSA_ROT)
    n_unused = N_FREQ - inv_m.shape[0] - inv_n.shape[0]
    cs = _rope_tables(positions, jnp.concatenate([inv_m, inv_n, jnp.zeros((n_unused,), F32)]))
    depth = ada_w.shape[0]
    for l in range(depth):
        mod = _ada(c, ada_w, ada_b, l).reshape(B, N_MOD, D_MODEL)
        x = _layer(x, mod, cs, {k: v[l] for k, v in params.items()})
    return x
```

```python
import numpy as np
import jax
import jax.numpy as jnp
from jax import lax
from jax.experimental import pallas as pl
from jax.experimental.pallas import tpu as pltpu

F32 = jnp.float32
BF16 = jnp.bfloat16

D_MODEL = 1024
ROPE_THETA = 500000.0
EPS = 1e-6
NEG = -1e30
LOG2E = 1.4426950408889634

MLA_HEADS = 8
MLA_NOPE = 64
MLA_ROPE = 32
MLA_QK = MLA_NOPE + MLA_ROPE
MLA_V = 64
MLA_Q_LORA = 768
MLA_KV_LORA = 256

NSA_HEADS = 8
NSA_KV_GROUPS = 2
NSA_REP = NSA_HEADS // NSA_KV_GROUPS
NSA_HEAD = 64
NSA_ROT = NSA_HEAD // 4
CMP_LEN = 32
CMP_STRIDE = 16
CMP_HIDDEN = 256
SEL_LEN = 64
SEL_TOP = 8
WINDOW = 256
N_NSA_BRANCH = 3
FORCE_BONUS = 1e4
KV_W = NSA_KV_GROUPS * NSA_HEAD

D_FF = -(-8 * D_MODEL // (3 * 256)) * 256
N_MOD = 6
LANES = 128
HEAD_PAD = LANES
N_FREQ = 32

TM_IN = 256
TQ_ATT = 256
TM_OUT = 512
MXU_TILE = 256
FF_SPLITS = (0, 6 * MXU_TILE, D_FF)
ATT_LOOKAHEAD = 2
VMEM_LIMIT = 56 * 1024 * 1024


def _dot(a, b):
    return jnp.dot(a, b, preferred_element_type=F32)


def _dot_nt(a, b):
    return lax.dot_general(a, b, (((1,), (1,)), ((), ())), preferred_element_type=F32)


def _split_hilo(a):
    hi = a.astype(BF16)
    return hi, (a - hi.astype(F32)).astype(BF16)


def _dot_hilo(a, m):
    hi, lo = _split_hilo(a)
    return _dot(hi, m) + _dot(lo, m)


def _sigmoid(v):
    return 1.0 / (1.0 + jnp.exp(-v))


def _rms(v, n):
    return v * lax.rsqrt(jnp.sum(v * v, axis=-1, keepdims=True) * (1.0 / n) + EPS)


def _rope(v, cos_v, sin_v, lo, half):
    lane = lax.broadcasted_iota(jnp.int32, v.shape, 1)
    is_x1 = (lane >= lo) & (lane < lo + half)
    rot = jnp.where(is_x1, pltpu.roll(v, LANES - half, 1), pltpu.roll(v, half, 1))
    return v * cos_v + rot * sin_v


def _rope_multipliers(cs, texp_ref, trow_ref):
    tabs = _dot_hilo(cs, texp_ref[...]) + trow_ref[...]
    return tuple(tabs[:, LANES * i:LANES * (i + 1)] for i in range(4))


def _const_spec(shape):
    nd = len(shape)
    return pl.BlockSpec(shape, lambda *_: (0,) * nd)


def _rowmax(s):
    return jnp.max(s, axis=-1, keepdims=True)


def _attention_scores(q, k_ref, kmax, dbias):
    k0 = kmax - dbias.shape[1]
    s_d = _dot_nt(q, k_ref(k0, kmax)) + dbias
    m = _rowmax(s_d)
    s_m = None
    if k0 > 0:
        s_m = _dot_nt(q, k_ref(0, k0))
        m = jnp.maximum(m, _rowmax(s_m))
    return s_m, s_d, m


def _attention_values(scores, v_ref, kmax):
    s_m, s_d, m = scores
    k0 = kmax - s_d.shape[1]
    acc = _dot(jnp.exp2(s_d - m).astype(BF16), v_ref(k0, kmax))
    if s_m is not None:
        acc = acc + _dot(jnp.exp2(s_m - m).astype(BF16), v_ref(0, k0))
    return acc[:, :NSA_HEAD] / acc[:, NSA_HEAD:NSA_HEAD + 1]


def _rope_kernel(pos_ref, inv_ref, cs_ref):
    ang = pos_ref[0].astype(F32) * inv_ref[...]
    nf, S = ang.shape
    rows = jnp.concatenate([jnp.cos(ang), jnp.sin(ang), jnp.zeros((LANES - 2 * nf, S), F32)], axis=0)
    cs_ref[0] = rows.T


def _rope_tables(positions, inv):
    B, S = positions.shape
    nf = inv.shape[0]
    return pl.pallas_call(
        _rope_kernel,
        grid=(B,),
        in_specs=[pl.BlockSpec((1, 1, S), lambda b: (b, 0, 0)),
                  _const_spec((nf, 1))],
        out_specs=pl.BlockSpec((1, S, LANES), lambda b: (b, 0, 0)),
        out_shape=jax.ShapeDtypeStruct((B, S, LANES), F32),
        name="rope_tables",
    )(positions.reshape(B, 1, S), inv.reshape(nf, 1))


def _ada_kernel(c_ref, w_ref, b_ref, o_ref):
    c = c_ref[...]
    sc = c * _sigmoid(c)
    o_ref[...] = jnp.dot(sc, w_ref[0], preferred_element_type=F32,
                         precision=lax.Precision.HIGHEST) + b_ref[0]


def _ada(c, w, b, layer):
    B, D = c.shape
    N = w.shape[2]
    tn = D_MODEL
    return pl.pallas_call(
        _ada_kernel,
        grid=(N // tn,),
        in_specs=[_const_spec((B, D)),
                  pl.BlockSpec((1, D, tn), lambda j: (layer, 0, j)),
                  pl.BlockSpec((1, 1, tn), lambda j: (layer, 0, j))],
        out_specs=pl.BlockSpec((B, tn), lambda j: (0, j)),
        out_shape=jax.ShapeDtypeStruct((B, N), F32),
        name="ada_mod",
    )(c, w, b.reshape(b.shape[0], 1, N))


def _inproj_kernel(x_ref, mod_ref, cs_ref, g1_ref, texp_ref, trow_ref,
                   wcq_ref, wckv_ref, wsm_ref, wqn_ref, wkv6_ref, wgm_ref,
                   qag_ref, wqb_ref, kvag_ref, wkvb_ref,
                   mqg_ref, mkn_ref, mkr_ref, nqg_ref, nksg_ref, nkwg_ref, vone_ref,
                   qm_ref, km_ref, vm_ref, qn_ref, ks_ref, kw_ref, vs_ref, vw_ref,
                   kcin_ref, vcin_ref, gn_ref, gm_ref):
    x = x_ref[0]
    tm = x.shape[0]
    mod = mod_ref[0]
    sh1, sc1 = mod[0:1], mod[1:2]
    h = _rms(x, D_MODEL) * g1_ref[...] * (1.0 + sc1) + sh1
    hb = h.astype(BF16)

    cos_m, sin_m, cos_n, sin_n = _rope_multipliers(cs_ref[0], texp_ref, trow_ref)
    lane = lax.broadcasted_iota(jnp.int32, (tm, LANES), 1)
    blk = lambda a, i: a[:, HEAD_PAD * i:HEAD_PAD * (i + 1)]
    hm = MLA_ROPE // 2
    hn = NSA_ROT // 2

    cq = _dot(hb, wcq_ref[...])
    ckv = _dot(hb, wckv_ref[...])
    zs = _dot(hb, wsm_ref[...])
    qn = _dot(hb, wqn_ref[...])
    kv6 = _dot(hb, wkv6_ref[...])
    cqn = (_rms(cq, MLA_Q_LORA) * qag_ref[...]).astype(BF16)
    q = _dot(cqn, wqb_ref[...])
    gates = _dot(hb, wgm_ref[...])
    ckvn = (_rms(ckv, MLA_KV_LORA) * kvag_ref[...]).astype(BF16)
    kv = _dot(ckvn, wkvb_ref[...])

    nqg = nqg_ref[...]
    n_scale = NSA_HEAD ** -0.5 * LOG2E
    for hd in range(NSA_HEADS):
        qh = _rope(_rms(blk(qn, hd), NSA_HEAD) * nqg, cos_n, sin_n, 0, hn) * n_scale
        qn_ref[0, :, HEAD_PAD * hd:HEAD_PAD * (hd + 1)] = qh.astype(BF16)

    tok = pl.program_id(1) * tm + lax.broadcasted_iota(jnp.int32, (tm, 1), 0)
    sblk = lax.shift_right_logical(tok, SEL_LEN.bit_length() - 1)
    ind = jnp.where(lane - NSA_HEAD == sblk, NEG, 0.0)
    vone = vone_ref[:, 0:HEAD_PAD]
    nksg, nkwg = nksg_ref[...], nkwg_ref[...]
    kcin_ref[0] = blk(kv6, 0)
    vcin_ref[0] = blk(kv6, 1)
    for g in range(NSA_KV_GROUPS):
        ks =_rope(_rms(blk(kv6, 2 + g), NSA_HEAD) * nksg, cos_n, sin_n, 0, hn)
        ks_ref[0, g] = (ks + ind).astype(BF16)
        vs_ref[0, g] = (blk(kv6, 4 + g) + vone).astype(BF16)
        kw = _rope(_rms(blk(kv6, 6 + g), NSA_HEAD) * nkwg, cos_n, sin_n, 0, hn)
        kw_ref[0, g] = kw.astype(BF16)
        vw_ref[0, g] = (blk(kv6, 8 + g) + vone).astype(BF16)

    mqg = mqg_ref[...]
    m_scale = MLA_QK ** -0.5 * LOG2E
    for hd in range(MLA_HEADS):
        qh = _rope(_rms(blk(q, hd), MLA_QK) * mqg, cos_m, sin_m, MLA_NOPE, hm) * m_scale
        qm_ref[0, :, HEAD_PAD * hd:HEAD_PAD * (hd + 1)] = qh.astype(BF16)

    gn_ref[0] = _sigmoid(zs)
    gm_ref[0] = _sigmoid(gates).astype(BF16)

    kpe = jnp.where((lane >= MLA_NOPE) & (lane < MLA_QK), zs, 0.0)
    kpe_ss = jnp.sum(kpe * kpe, axis=-1, keepdims=True)
    kr = _rope(kpe * mkr_ref[...], cos_m, sin_m, MLA_NOPE, hm)
    mkn = mkn_ref[...]
    for hd in range(MLA_HEADS):
        kn = blk(kv, hd)
        inv = lax.rsqrt((jnp.sum(kn * kn, axis=-1, keepdims=True) + kpe_ss) * (1.0 / MLA_QK) + EPS)
        km_ref[0, :, HEAD_PAD * hd:HEAD_PAD * (hd + 1)] = ((kn * mkn + kr) * inv).astype(BF16)
    vm_ref[0] = (kv[:, MLA_HEADS * HEAD_PAD:] + vone_ref[...]).astype(BF16)


def _inproj(x, mod, cs, consts, weights, rows):
    B, S, D = x.shape
    tm = TM_IN
    tok = lambda w: pl.BlockSpec((1, tm, w), lambda b, i: (b, i, 0))
    head = lambda n, w: pl.BlockSpec((1, n, tm, w), lambda b, i: (b, 0, i, 0))
    operands = list(consts) + list(weights[:6]) + [rows[0], weights[6], rows[1], weights[7]] + list(rows[2:])
    in_specs = [tok(D), pl.BlockSpec((1, N_MOD, D), lambda b, i: (b, 0, 0)), tok(LANES)]
    in_specs += [_const_spec(a.shape) for a in operands]
    G = NSA_KV_GROUPS
    sds = jax.ShapeDtypeStruct
    wide = MLA_HEADS * HEAD_PAD
    outs = [
        (tok(wide), sds((B, S, wide), BF16)),
        (tok(wide), sds((B, S, wide), BF16)),
        (tok(wide), sds((B, S, wide), BF16)),
        (tok(wide), sds((B, S, wide), BF16)),
        (head(G, HEAD_PAD), sds((B, G, S, HEAD_PAD), BF16)),
        (head(G, HEAD_PAD), sds((B, G, S, HEAD_PAD), BF16)),
        (head(G, HEAD_PAD), sds((B, G, S, HEAD_PAD), BF16)),
        (head(G, HEAD_PAD), sds((B, G, S, HEAD_PAD), BF16)),
        (tok(KV_W), sds((B, S, KV_W), F32)),
        (tok(KV_W), sds((B, S, KV_W), F32)),
        (tok(LANES), sds((B, S, LANES), F32)),
        (tok(2 * D), sds((B, S, 2 * D), BF16)),
    ]
    return pl.pallas_call(
        _inproj_kernel,
        grid=(B, S // tm),
        in_specs=in_specs,
        out_specs=[o[0] for o in outs],
        out_shape=[o[1] for o in outs],
        compiler_params=pltpu.CompilerParams(dimension_semantics=("arbitrary", "arbitrary"),
                                             vmem_limit_bytes=VMEM_LIMIT),
        name="inproj_prep",
    )(x, mod, cs, *operands)


def _compress_kernel(kcin_ref, vcin_ref, pk_ref, pv_ref, w1k_ref, w2k_ref, w1v_ref, w2v_ref,
                     kcg_ref, cs_ref, texp_ref, trow_ref, kc_ref, vc_ref):
    n = kcin_ref.shape[1] // CMP_STRIDE

    def hidden(cin_ref, pos_ref, w1_ref):
        a = b = None
        for l in range(0, CMP_STRIDE, 2):
            t0 = cin_ref[0, pl.ds(l, n, stride=CMP_STRIDE), :]
            t1 = cin_ref[0, pl.ds(l + 1, n, stride=CMP_STRIDE), :]
            pair = lambda o: jnp.concatenate([t0 + pos_ref[o + l:o + l + 1],
                                              t1 + pos_ref[o + l + 1:o + l + 2]], axis=-1).astype(BF16)
            da = _dot(pair(0), w1_ref[l // 2])
            db = _dot(pair(CMP_STRIDE), w1_ref[(CMP_STRIDE + l) // 2])
            a, b = (da, db) if a is None else (a + da, b + db)
        hid = a + pltpu.roll(b, n - 1, 0)
        return (hid * _sigmoid(hid)).astype(BF16)

    kc = _dot(hidden(kcin_ref, pk_ref, w1k_ref), w2k_ref[...])
    _, _, cos_n, sin_n = _rope_multipliers(cs_ref[0], texp_ref, trow_ref)
    vc = _dot(hidden(vcin_ref, pv_ref, w1v_ref), w2v_ref[...])
    for g in range(NSA_KV_GROUPS):
        kg = _rms(kc[:, HEAD_PAD * g:HEAD_PAD * (g + 1)], NSA_HEAD) * kcg_ref[...]
        kc_ref[0, g] = _rope(kg, cos_n, sin_n, 0, NSA_ROT // 2).astype(BF16)
        vc_ref[0, g] = vc[:, NSA_HEAD * g:NSA_HEAD * (g + 1)].astype(BF16)


def _compress(kcin, vcin, pk, pv, w1k, w2k, w1v, w2v, kcg, cs_end, texp, trow):
    B, S, w = kcin.shape
    G = NSA_KV_GROUPS
    n = S // CMP_STRIDE
    oblk = lambda wd: pl.BlockSpec((1, G, n, wd), lambda b: (b, 0, 0, 0))
    consts = (pk, pv, w1k, w2k, w1v, w2v, kcg)
    return pl.pallas_call(
        _compress_kernel,
        grid=(B,),
        in_specs=[pl.BlockSpec((1, S, w), lambda b: (b, 0, 0))] * 2 + [_const_spec(a.shape) for a in consts] +
                 [pl.BlockSpec((1, n, LANES), lambda b: (b, 0, 0)),
                  _const_spec(texp.shape), _const_spec(trow.shape)],
        out_specs=[oblk(HEAD_PAD), oblk(NSA_HEAD)],
        out_shape=[jax.ShapeDtypeStruct((B, G, n, HEAD_PAD), BF16),
                   jax.ShapeDtypeStruct((B, G, n, NSA_HEAD), BF16)],
        name="nsa_compress",
    )(kcin, vcin, *consts, cs_end, texp, trow)


def _nsa_kernel(q_ref, kc_ref, vc_ref, ks_ref, vs_ref, kw_ref, vw_ref, gn_ref,
                ovt_ref, gexp_ref, dbias_ref, wbias_ref, o_ref, imp_ref):
    tq = dbias_ref.shape[0]
    S = q_ref.shape[1]
    R = NSA_REP
    M = R * tq
    n_sel = imp_ref.shape[0]
    ncp = kc_ref.shape[2]
    span = WINDOW + tq
    grp = pl.program_id(1)
    dbias = dbias_ref[...]
    kf = lambda a, b: ks_ref[0, 0, a:b, :]
    vf = lambda a, b: vs_ref[0, 0, a:b, :]
    row = lax.broadcasted_iota(jnp.int32, (M, 1), 0)
    n_idx = lax.broadcasted_iota(jnp.int32, (M, ncp), 1)
    j = lax.broadcasted_iota(jnp.int32, (n_sel, tq), 0)
    head_q = lambda i, r: q_ref[0, i * tq:(i + 1) * tq, HEAD_PAD * r:HEAD_PAD * (r + 1)]
    tile_q = lambda i: jnp.concatenate([head_q(i, r) for r in range(R)], axis=0)
    tiles = {}

    def compressed_and_select(i):
        q0 = i * tq
        t = q0 + jnp.bitwise_and(row, tq - 1)
        s = _dot_nt(tile_q(i), kc_ref[0, 0])
        valid = (n_idx * CMP_STRIDE + (CMP_LEN - 1)) <= t
        sm = jnp.where(valid, s, NEG)
        e = jnp.where(valid, jnp.exp2(sm - _rowmax(sm)), 0.0)
        den = jnp.sum(e, axis=-1, keepdims=True)
        p_c = e / jnp.where(den > 0.0, den, 1.0)
        o_c = _dot(p_c.astype(BF16), vc_ref[0, 0])
        psum = p_c[0:tq]
        for r in range(1, R):
            psum = psum + p_c[r * tq:(r + 1) * tq]
        hi, lo = _split_hilo(psum.T)
        imp = (_dot(ovt_ref[...], hi) + _dot(ovt_ref[...], lo))[0:n_sel]
        cur = lax.shift_right_logical(q0 + lax.broadcasted_iota(jnp.int32, (1, tq), 1),
                                      SEL_LEN.bit_length() - 1)
        forced = (j == 0) | (j == cur) | (j == cur - 1)
        imp = jnp.where(forced, imp + FORCE_BONUS, imp)
        imp = jnp.where(j <= cur, imp, NEG)
        imp_ref[...] = imp
        cnt = jnp.zeros((n_sel, tq), F32)
        for jj in range(n_sel):
            other = imp_ref[jj:jj + 1, :]
            beats = (other > imp) | ((other == imp) & (j > jj))
            cnt = cnt + jnp.where(beats, 1.0, 0.0)
        nsel = jnp.where((cnt < float(SEL_TOP)) & (j <= cur), 0.0, 1.0)
        nsel = jnp.concatenate([jnp.zeros((NSA_HEAD, tq), F32), nsel,
                                jnp.zeros((LANES - NSA_HEAD - n_sel, tq), F32)], axis=0).T.astype(BF16)
        tiles[i] = dict(o_c=o_c, nsel=nsel, o_s=[], o_w=[])

    def scores(u):
        kind, i, r = u
        if kind == "cmp":
            compressed_and_select(i)
            return None
        if kind == "sel":
            return _attention_scores(head_q(i, r) + tiles[i]["nsel"], kf, (i + 1) * tq, dbias)
        w0 = max(i * tq - WINDOW, 0)
        wb = wbias_ref[min(i, 1)]
        sw = _dot_nt(head_q(i, r), kw_ref[0, 0, w0:w0 + span, :]) + wb
        return sw, _rowmax(sw)

    def values(u, sc):
        kind, i, r = u
        if kind == "sel":
            tiles[i]["o_s"].append(_attention_values(sc, vf, (i + 1) * tq))
        elif kind == "win":
            sw, mw = sc
            w0 = max(i * tq - WINDOW, 0)
            acc_w = _dot(jnp.exp2(sw - mw).astype(BF16), vw_ref[0, 0, w0:w0 + span, :])
            tiles[i]["o_w"].append(acc_w[:, :NSA_HEAD] / acc_w[:, NSA_HEAD:NSA_HEAD + 1])
        if kind == "win" and r == R - 1:
            tile = tiles.pop(i)
            g_hi, g_lo = _split_hilo(gn_ref[0, i * tq:(i + 1) * tq, :])
            o_c = jnp.concatenate([tile["o_c"][r * tq:(r + 1) * tq] for r in range(R)], axis=-1)
            branches = (o_c, jnp.concatenate(tile["o_s"], axis=-1), jnp.concatenate(tile["o_w"], axis=-1))
            out = None
            for br, o_b in enumerate(branches):
                gate = _dot(g_hi, gexp_ref[grp, br]) + _dot(g_lo, gexp_ref[grp, br])
                out = gate * o_b if out is None else out + gate * o_b
            o_ref[0, i * tq:(i + 1) * tq, :] = out.astype(BF16)

    nq = S // tq
    units = [("cmp", 0, 0)]
    for i in range(nq):
        units += [("cmp", i + 1, 0)] if i + 1 < nq else []
        for r in range(R):
            units += [("sel", i, r), ("win", i, r)]
    pending = [scores(u) for u in units[:ATT_LOOKAHEAD]]
    for n, u in enumerate(units):
        if n + ATT_LOOKAHEAD < len(units):
            pending.append(scores(units[n + ATT_LOOKAHEAD]))
        values(u, pending.pop(0))


def _nsa_attention(qn, kc, vc, ks, vs, kw, vw, gn, ovt, gexp, dbias, wbias):
    B, S, _ = qn.shape
    G, R, Dh = NSA_KV_GROUPS, NSA_REP, NSA_HEAD
    ncp = kc.shape[2]
    full = pl.BlockSpec((1, 1, S, HEAD_PAD), lambda b, g: (b, g, 0, 0))
    cmp_spec = lambda w: pl.BlockSpec((1, 1, ncp, w), lambda b, g: (b, g, 0, 0))
    return pl.pallas_call(
        _nsa_kernel,
        grid=(B, G),
        in_specs=[pl.BlockSpec((1, S, R * HEAD_PAD), lambda b, g: (b, 0, g)),
                  cmp_spec(HEAD_PAD), cmp_spec(Dh), full, full, full, full,
                  pl.BlockSpec((1, S, LANES), lambda b, g: (b, 0, 0)),
                  _const_spec(ovt.shape), _const_spec(gexp.shape),
                  _const_spec(dbias.shape), _const_spec(wbias.shape)],
        out_specs=pl.BlockSpec((1, S, R * Dh), lambda b, g: (b, 0, g)),
        out_shape=jax.ShapeDtypeStruct((B, S, G * R * Dh), BF16),
        scratch_shapes=[pltpu.VMEM((S // SEL_LEN, TQ_ATT), F32)],
        compiler_params=pltpu.CompilerParams(dimension_semantics=("arbitrary",) * 2,
                                             vmem_limit_bytes=VMEM_LIMIT),
        name="nsa_attention",
    )(qn, kc, vc, ks, vs, kw, vw, gn, ovt, gexp, dbias, wbias)


def _mla_kernel(q_ref, k_ref, v_ref, dbias_ref, o_ref):
    tq = dbias_ref.shape[0]
    S = q_ref.shape[1]
    dbias = dbias_ref[...]
    units = [(i, hh) for i in range(S // tq) for hh in range(2)]

    def scores(u):
        i, hh = u
        cols = slice(HEAD_PAD * hh, HEAD_PAD * (hh + 1))
        q = q_ref[0, i * tq:(i + 1) * tq, cols]
        return _attention_scores(q, lambda a, b: k_ref[0, a:b, cols], (i + 1) * tq, dbias)

    def values(u, sc):
        i, hh = u
        cols = slice(HEAD_PAD * hh, HEAD_PAD * (hh + 1))
        o = _attention_values(sc, lambda a, b: v_ref[0, a:b, cols], (i + 1) * tq)
        o_ref[0, i * tq:(i + 1) * tq, MLA_V * hh:MLA_V * (hh + 1)] = o.astype(BF16)

    pending = [scores(u) for u in units[:ATT_LOOKAHEAD]]
    for n, u in enumerate(units):
        if n + ATT_LOOKAHEAD < len(units):
            pending.append(scores(units[n + ATT_LOOKAHEAD]))
        values(u, pending.pop(0))


def _mla_attention(qm, km, vm, dbias):
    B, S, _ = qm.shape
    pair = pl.BlockSpec((1, S, 2 * HEAD_PAD), lambda b, h: (b, 0, h))
    return pl.pallas_call(
        _mla_kernel,
        grid=(B, MLA_HEADS // 2),
        in_specs=[pair, pair, pair, _const_spec(dbias.shape)],
        out_specs=pl.BlockSpec((1, S, 2 * MLA_V), lambda b, h: (b, 0, h)),
        out_shape=jax.ShapeDtypeStruct((B, S, MLA_HEADS * MLA_V), BF16),
        compiler_params=pltpu.CompilerParams(dimension_semantics=("arbitrary",) * 2,
                                             vmem_limit_bytes=VMEM_LIMIT),
        name="mla_attention",
    )(qm, km, vm, dbias)


def _out_ffn_kernel(x_ref, om_ref, on_ref, gm_ref, mod_ref, g2_ref,
                    wom_ref, won_ref, wout_ref, wg_ref, wu_ref, wd_ref, o_ref):
    x = x_ref[0]
    mod = mod_ref[0]
    gt1, sh2, sc2, gt2 = mod[2:3], mod[3:4], mod[4:5], mod[5:6]
    ym = _dot(om_ref[0], wom_ref[...])
    yn = _dot(on_ref[0], won_ref[...])
    merged = gm_ref[0, :, :D_MODEL] * ym + gm_ref[0, :, D_MODEL:] * yn
    x1 = x + gt1 * _dot(merged.astype(BF16), wout_ref[...])
    h2 = (_rms(x1, D_MODEL) * g2_ref[...] * (1.0 + sc2) + sh2).astype(BF16)
    chunks = [slice(lo, hi) for lo, hi in zip(FF_SPLITS[:-1], FF_SPLITS[1:])]
    gu = [(_dot(h2, wg_ref[:, sl]), _dot(h2, wu_ref[:, sl])) for sl in chunks]
    acc = None
    for sl, (g, u) in zip(chunks, gu):
        d = _dot((g * _sigmoid(g) * u).astype(BF16), wd_ref[sl, :])
        acc = d if acc is None else acc + d
    o_ref[0] = x1 + gt2 * acc


def _out_ffn(x, om, on, gm, mod, g2, wom, won, wout, wg, wu, wd):
    B, S, D = x.shape
    tm = TM_OUT
    tok = lambda w: pl.BlockSpec((1, tm, w), lambda b, i: (b, i, 0))
    wspec = lambda w: pl.BlockSpec(w.shape, lambda b, i: (0, 0), pipeline_mode=pl.Buffered(1))
    return pl.pallas_call(
        _out_ffn_kernel,
        grid=(B, S // tm),
        in_specs=[tok(D), tok(om.shape[2]), tok(on.shape[2]), tok(2 * D),
                  pl.BlockSpec((1, N_MOD, D), lambda b, i: (b, 0, 0)),
                  _const_spec(g2.shape)] + [wspec(w) for w in (wom, won, wout, wg, wu, wd)],
        out_specs=tok(D),
        out_shape=jax.ShapeDtypeStruct((B, S, D), F32),
        compiler_params=pltpu.CompilerParams(dimension_semantics=("arbitrary", "arbitrary"),
                                             vmem_limit_bytes=VMEM_LIMIT),
        name="out_ffn",
    )(x, om, on, gm, mod, g2, wom, won, wout, wg, wu, wd)


def _rope_expansion():
    texp = np.zeros((LANES, 4 * LANES), np.float32)
    trow = np.zeros((1, 4 * LANES), np.float32)
    hm, hn = MLA_ROPE // 2, NSA_ROT // 2
    trow[0, 0:LANES] = 1.0
    trow[0, 2 * LANES:3 * LANES] = 1.0
    for i in range(hm):
        for off, sgn in ((MLA_NOPE + i, -1.0), (MLA_NOPE + hm + i, 1.0)):
            texp[i, off] = 1.0
            trow[0, off] = 0.0
            texp[N_FREQ + i, LANES + off] = sgn
    for i in range(hn):
        for off, sgn in ((i, -1.0), (hn + i, 1.0)):
            texp[hm + i, 2 * LANES + off] = 1.0
            trow[0, 2 * LANES + off] = 0.0
            texp[N_FREQ + hm + i, 3 * LANES + off] = sgn
    return jnp.asarray(texp, BF16), jnp.asarray(trow, F32)


def _mask_tables(S):
    tq = TQ_ATT
    n_chunk = S // CMP_STRIDE
    n_sel = S // SEL_LEN
    starts = np.arange(n_chunk) * CMP_STRIDE
    sel_start = np.arange(LANES) * SEL_LEN
    ovt = ((starts[None, :] < sel_start[:, None] + SEL_LEN) &
           (starts[None, :] + CMP_LEN > sel_start[:, None]) &
           (np.arange(n_chunk)[None, :] < n_chunk - 1) &
           (np.arange(LANES)[:, None] < n_sel))
    gcol = np.arange(LANES)[:, None]
    head = np.arange(NSA_REP * NSA_HEAD)[None, :] // NSA_HEAD
    gexp = np.stack([np.stack([gcol == (g * NSA_REP + head) * N_NSA_BRANCH + br
                               for br in range(N_NSA_BRANCH)]) for g in range(NSA_KV_GROUPS)])
    qi = np.arange(tq)[:, None]
    dbias = np.where(np.arange(tq)[None, :] <= qi, 0.0, NEG)
    kk = np.arange(WINDOW + tq)[None, :]
    band = lambda d: np.where((d >= 0) & (d < WINDOW), 0.0, NEG)
    wbias = np.stack([band(qi - kk), band(qi + WINDOW - kk)])
    return (jnp.asarray(ovt, BF16), jnp.asarray(gexp, BF16),
            jnp.asarray(dbias, F32), jnp.asarray(wbias, F32))


def _pad_heads(w, n_heads, width):
    k = w.shape[0]
    w = w.reshape(k, n_heads, width)
    return jnp.pad(w, ((0, 0), (0, 0), (0, HEAD_PAD - width))).reshape(k, n_heads * HEAD_PAD)


def _block_diag(w, n):
    k, m = w.shape
    eye = jnp.eye(n, dtype=w.dtype)
    return (eye[:, None, :, None] * w[None, :, None, :]).reshape(n * k, n * m)


def _cmp_w1_pairs(w1):
    per_tok = w1.reshape(CMP_LEN, NSA_HEAD, w1.shape[1])
    bd = jax.vmap(lambda w: _block_diag(w, NSA_KV_GROUPS))(per_tok)
    return bd.reshape(CMP_LEN // 2, 2 * KV_W, NSA_KV_GROUPS * w1.shape[1])


def _pad_row(g, lo=0):
    return jnp.pad(g, (lo, HEAD_PAD - lo - g.shape[0])).reshape(1, HEAD_PAD)


def _layer(x, mod, cs, p):
    B, S, D = x.shape
    w_in = p["w_in"]
    o = 0
    cols = {}
    for name, wdt in (("cq", MLA_Q_LORA), ("ckv", MLA_KV_LORA), ("kpe", MLA_ROPE),
                      ("qn", NSA_HEADS * NSA_HEAD), ("kc", KV_W), ("vc", KV_W), ("ks", KV_W),
                      ("vs", KV_W), ("kw", KV_W), ("vw", KV_W),
                      ("gn", NSA_HEADS * N_NSA_BRANCH), ("gm", 2 * D)):
        cols[name] = w_in[:, o:o + wdt]
        o += wdt
    G = NSA_KV_GROUPS
    n_gate = NSA_HEADS * N_NSA_BRANCH
    zc = lambda n: jnp.zeros((D, n), F32)
    wsm = jnp.concatenate([cols["gn"], zc(MLA_NOPE - n_gate), cols["kpe"], zc(LANES - MLA_QK)], axis=1)
    wkv6 = jnp.concatenate([cols["kc"], cols["vc"]] +
                           [_pad_heads(cols[k], G, NSA_HEAD) for k in ("ks", "vs", "kw", "vw")], axis=1)
    wkvb = p["mla_w_kv_b"].reshape(MLA_KV_LORA, MLA_HEADS, MLA_NOPE + MLA_V)
    wkvb = jnp.concatenate([_pad_heads(wkvb[:, :, :MLA_NOPE].reshape(MLA_KV_LORA, -1), MLA_HEADS, MLA_NOPE),
                            _pad_heads(wkvb[:, :, MLA_NOPE:].reshape(MLA_KV_LORA, -1), MLA_HEADS, MLA_V)], axis=1)
    bf = lambda w: w.astype(BF16)
    row = lambda g: g.reshape(1, -1)
    weights = tuple(bf(w) for w in (cols["cq"], cols["ckv"], wsm, _pad_heads(cols["qn"], NSA_HEADS, NSA_HEAD),
                                    wkv6, cols["gm"],
                                    _pad_heads(p["mla_w_q_b"], MLA_HEADS, MLA_QK), wkvb))
    vone = jnp.tile(jnp.zeros((1, HEAD_PAD), F32).at[0, MLA_V].set(1.0), (1, MLA_HEADS))
    rows = (row(p["mla_q_a_gain"]), row(p["mla_kv_a_gain"]),
            _pad_row(p["mla_q_gain"]), _pad_row(p["mla_k_gain"][:MLA_NOPE]),
            _pad_row(p["mla_k_gain"][MLA_NOPE:], MLA_NOPE),
            _pad_row(p["nsa_q_gain"]), _pad_row(p["nsa_ks_gain"]), _pad_row(p["nsa_kw_gain"]), vone)
    texp, trow = _rope_expansion()
    (qm, km, vm, qn, ks, kw, vs, vw, kcin, vcin, gn, gm) = _inproj(
        x, mod, cs, (row(p["norm1_gain"]), texp, trow), weights, rows)

    n_chunk = S // CMP_STRIDE
    cs_end = cs[:, CMP_LEN - 1::CMP_STRIDE]
    cs_end = jnp.pad(cs_end, ((0, 0), (0, n_chunk - cs_end.shape[1]), (0, 0)))
    w2k = jnp.pad(p["cmp_w2_k"], ((0, 0), (0, HEAD_PAD - NSA_HEAD)))
    kc, vc = _compress(kcin, vcin, jnp.tile(p["cmp_pos_k"], (1, G)), jnp.tile(p["cmp_pos_v"], (1, G)),
                       bf(_cmp_w1_pairs(p["cmp_w1_k"])), bf(_block_diag(w2k, G)),
                       bf(_cmp_w1_pairs(p["cmp_w1_v"])), bf(_block_diag(p["cmp_w2_v"], G)),
                       _pad_row(p["nsa_kc_gain"]), cs_end, texp, trow)

    ovt, gexp, dbias, wbias = _mask_tables(S)
    o_nsa = _nsa_attention(qn, kc, vc, ks, vs, kw, vw, gn, ovt, gexp, dbias, wbias)
    o_mla = _mla_attention(qm, km, vm, dbias)

    return _out_ffn(x, o_mla, o_nsa, gm, mod, row(p["norm2_gain"]),
                    bf(p["w_o_mla"]), bf(p["w_o_nsa"]), bf(p["w_out"]),
                    bf(p["ffn_w_gate"]), bf(p["ffn_w_up"]), bf(p["ffn_w_down"]))


def kernel(x, c, positions, ada_w, ada_b, norm1_gain, w_in, mla_q_a_gain, mla_w_q_b, mla_kv_a_gain, mla_w_kv_b, mla_q_gain, mla_k_gain, nsa_q_gain, nsa_kc_gain, nsa_ks_gain, nsa_kw_gain, cmp_pos_k, cmp_w1_k, cmp_w2_k, cmp_pos_v, cmp_w1_v, cmp_w2_v, w_o_mla, w_o_nsa, w_out, norm2_gain, ffn_w_gate, ffn_w_up, ffn_w_down):
    params = dict(norm1_gain=norm1_gain, w_in=w_in, mla_q_a_gain=mla_q_a_gain, mla_w_q_b=mla_w_q_b,
                  mla_kv_a_gain=mla_kv_a_gain, mla_w_kv_b=mla_w_kv_b, mla_q_gain=mla_q_gain,
                  mla_k_gain=mla_k_gain, nsa_q_gain=nsa_q_gain, nsa_kc_gain=nsa_kc_gain,
                  nsa_ks_gain=nsa_ks_gain, nsa_kw_gain=nsa_kw_gain, cmp_pos_k=cmp_pos_k,
                  cmp_w1_k=cmp_w1_k, cmp_w2_k=cmp_w2_k, cmp_pos_v=cmp_pos_v, cmp_w1_v=cmp_w1_v,
                  cmp_w2_v=cmp_w2_v, w_o_mla=w_o_mla, w_o_nsa=w_o_nsa, w_out=w_out,
                  norm2_gain=norm2_gain, ffn_w_gate=ffn_w_gate, ffn_w_up=ffn_w_up, ffn_w_down=ffn_w_down)
    B = x.shape[0]
    inv_m = ROPE_THETA ** (-jnp.arange(0, MLA_ROPE, 2, dtype=F32) / MLA_ROPE)
    inv_n = ROPE_THETA ** (-jnp.arange(0, NSA_ROT, 2, dtype=F32) / NSA_ROT)
    n_unused = N_FREQ - inv_m.shape[0] - inv_n.shape[0]
    cs = _rope_tables(positions, jnp.concatenate([inv_m, inv_n, jnp.zeros((n_unused,), F32)]))
    depth = ada_w.shape[0]
    for l in range(depth):
        mod = _ada(c, ada_w, ada_b, l).reshape(B, N_MOD, D_MODEL)
        x = _layer(x, mod, cs, {k: v[l] for k, v in params.items()})
    return x
```

```python
import numpy as np
import jax
import jax.numpy as jnp
from jax import lax
from jax.experimental import pallas as pl
from jax.experimental.pallas import tpu as pltpu

F32 = jnp.float32
BF16 = jnp.bfloat16

D_MODEL = 1024
ROPE_THETA = 500000.0
EPS = 1e-6
NEG = -1e30
LOG2E = 1.4426950408889634

MLA_HEADS = 8
MLA_NOPE = 64
MLA_ROPE = 32
MLA_QK = MLA_NOPE + MLA_ROPE
MLA_V = 64
MLA_Q_LORA = 768
MLA_KV_LORA = 256

NSA_HEADS = 8
NSA_KV_GROUPS = 2
NSA_REP = NSA_HEADS // NSA_KV_GROUPS
NSA_HEAD = 64
NSA_ROT = NSA_HEAD // 4
CMP_LEN = 32
CMP_STRIDE = 16
CMP_HIDDEN = 256
SEL_LEN = 64
SEL_TOP = 8
WINDOW = 256
N_NSA_BRANCH = 3
FORCE_BONUS = 1e4
KV_W = NSA_KV_GROUPS * NSA_HEAD

D_FF = -(-8 * D_MODEL // (3 * 256)) * 256
N_MOD = 6
LANES = 128
HEAD_PAD = LANES
N_FREQ = 32

TM_IN = 256
TQ_ATT = 256
TM_OUT = 512
MXU_TILE = 256
FF_SPLITS = (0, 6 * MXU_TILE, D_FF)
ATT_LOOKAHEAD = 2
VMEM_LIMIT = 56 * 1024 * 1024


def _dot(a, b):
    return jnp.dot(a, b, preferred_element_type=F32)


def _dot_nt(a, b):
    return lax.dot_general(a, b, (((1,), (1,)), ((), ())), preferred_element_type=F32)


def _split_hilo(a):
    hi = a.astype(BF16)
    return hi, (a - hi.astype(F32)).astype(BF16)


def _dot_hilo(a, m):
    hi, lo = _split_hilo(a)
    return _dot(hi, m) + _dot(lo, m)


def _sigmoid(v):
    return 1.0 / (1.0 + jnp.exp(-v))


def _rms(v, n):
    return v * lax.rsqrt(jnp.sum(v * v, axis=-1, keepdims=True) * (1.0 / n) + EPS)


def _rope(v, cos_v, sin_v, lo, half):
    lane = lax.broadcasted_iota(jnp.int32, v.shape, 1)
    is_x1 = (lane >= lo) & (lane < lo + half)
    rot = jnp.where(is_x1, pltpu.roll(v, LANES - half, 1), pltpu.roll(v, half, 1))
    return v * cos_v + rot * sin_v


def _rope_multipliers(cs, texp_ref, trow_ref):
    tabs = _dot_hilo(cs, texp_ref[...]) + trow_ref[...]
    return tuple(tabs[:, LANES * i:LANES * (i + 1)] for i in range(4))


def _const_spec(shape):
    nd = len(shape)
    return pl.BlockSpec(shape, lambda *_: (0,) * nd)


def _rowmax(s):
    return jnp.max(s, axis=-1, keepdims=True)


def _attention_scores(q, k_ref, kmax, dbias):
    k0 = kmax - dbias.shape[1]
    s_d = _dot_nt(q, k_ref(k0, kmax)) + dbias
    m = _rowmax(s_d)
    s_m = None
    if k0 > 0:
        s_m = _dot_nt(q, k_ref(0, k0))
        m = jnp.maximum(m, _rowmax(s_m))
    return s_m, s_d, m


def _attention_values(scores, v_ref, kmax):
    s_m, s_d, m = scores
    k0 = kmax - s_d.shape[1]
    acc = _dot(jnp.exp2(s_d - m).astype(BF16), v_ref(k0, kmax))
    if s_m is not None:
        acc = acc + _dot(jnp.exp2(s_m - m).astype(BF16), v_ref(0, k0))
    return acc[:, :NSA_HEAD] / acc[:, NSA_HEAD:NSA_HEAD + 1]


def _rope_kernel(pos_ref, inv_ref, cs_ref):
    ang = pos_ref[0].astype(F32) * inv_ref[...]
    nf, S = ang.shape
    rows = jnp.concatenate([jnp.cos(ang), jnp.sin(ang), jnp.zeros((LANES - 2 * nf, S), F32)], axis=0)
    cs_ref[0] = rows.T


def _rope_tables(positions, inv):
    B, S = positions.shape
    nf = inv.shape[0]
    return pl.pallas_call(
        _rope_kernel,
        grid=(B,),
        in_specs=[pl.BlockSpec((1, 1, S), lambda b: (b, 0, 0)),
                  _const_spec((nf, 1))],
        out_specs=pl.BlockSpec((1, S, LANES), lambda b: (b, 0, 0)),
        out_shape=jax.ShapeDtypeStruct((B, S, LANES), F32),
        name="rope_tables",
    )(positions.reshape(B, 1, S), inv.reshape(nf, 1))


def _ada_kernel(c_ref, w_ref, b_ref, o_ref):
    c = c_ref[...]
    sc = c * _sigmoid(c)
    o_ref[...] = jnp.dot(sc, w_ref[0], preferred_element_type=F32,
                         precision=lax.Precision.HIGHEST) + b_ref[0]


def _ada(c, w, b, layer):
    B, D = c.shape
    N = w.shape[2]
    tn = D_MODEL
    return pl.pallas_call(
        _ada_kernel,
        grid=(N // tn,),
        in_specs=[_const_spec((B, D)),
                  pl.BlockSpec((1, D, tn), lambda j: (layer, 0, j)),
                  pl.BlockSpec((1, 1, tn), lambda j: (layer, 0, j))],
        out_specs=pl.BlockSpec((B, tn), lambda j: (0, j)),
        out_shape=jax.ShapeDtypeStruct((B, N), F32),
        name="ada_mod",
    )(c, w, b.reshape(b.shape[0], 1, N))


def _inproj_kernel(x_ref, mod_ref, cs_ref, g1_ref, texp_ref, trow_ref,
                   wcq_ref, wckv_ref, wsm_ref, wqn_ref, wkv6_ref, wgm_ref,
                   qag_ref, wqb_ref, kvag_ref, wkvb_ref,
                   mqg_ref, mkn_ref, mkr_ref, nqg_ref, nksg_ref, nkwg_ref, vone_ref,
                   qm_ref, km_ref, vm_ref, qn_ref, ks_ref, kw_ref, vs_ref, vw_ref,
                   kcin_ref, vcin_ref, gn_ref, gm_ref):
    x = x_ref[0]
    tm = x.shape[0]
    mod = mod_ref[0]
    sh1, sc1 = mod[0:1], mod[1:2]
    h = _rms(x, D_MODEL) * g1_ref[...] * (1.0 + sc1) + sh1
    hb = h.astype(BF16)

    cos_m, sin_m, cos_n, sin_n = _rope_multipliers(cs_ref[0], texp_ref, trow_ref)
    lane = lax.broadcasted_iota(jnp.int32, (tm, LANES), 1)
    blk = lambda a, i: a[:, HEAD_PAD * i:HEAD_PAD * (i + 1)]
    hm = MLA_ROPE // 2
    hn = NSA_ROT // 2

    cq = _dot(hb, wcq_ref[...])
    ckv = _dot(hb, wckv_ref[...])
    zs = _dot(hb, wsm_ref[...])
    qn = _dot(hb, wqn_ref[...])
    kv6 = _dot(hb, wkv6_ref[...])
    cqn = (_rms(cq, MLA_Q_LORA) * qag_ref[...]).astype(BF16)
    q = _dot(cqn, wqb_ref[...])
    gates = _dot(hb, wgm_ref[...])
    ckvn = (_rms(ckv, MLA_KV_LORA) * kvag_ref[...]).astype(BF16)
    kv = _dot(ckvn, wkvb_ref[...])

    nqg = nqg_ref[...]
    n_scale = NSA_HEAD ** -0.5 * LOG2E
    for hd in range(NSA_HEADS):
        qh = _rope(_rms(blk(qn, hd), NSA_HEAD) * nqg, cos_n, sin_n, 0, hn) * n_scale
        qn_ref[0, :, HEAD_PAD * hd:HEAD_PAD * (hd + 1)] = qh.astype(BF16)

    tok = pl.program_id(1) * tm + lax.broadcasted_iota(jnp.int32, (tm, 1), 0)
    sblk = lax.shift_right_logical(tok, SEL_LEN.bit_length() - 1)
    ind = jnp.where(lane - NSA_HEAD == sblk, NEG, 0.0)
    vone = vone_ref[:, 0:HEAD_PAD]
    nksg, nkwg = nksg_ref[...], nkwg_ref[...]
    kcin_ref[0] = blk(kv6, 0)
    vcin_ref[0] = blk(kv6, 1)
    for g in range(NSA_KV_GROUPS):
        ks =_rope(_rms(blk(kv6, 2 + g), NSA_HEAD) * nksg, cos_n, sin_n, 0, hn)
        ks_ref[0, g] = (ks + ind).astype(BF16)
        vs_ref[0, g] = (blk(kv6, 4 + g) + vone).astype(BF16)
        kw = _rope(_rms(blk(kv6, 6 + g), NSA_HEAD) * nkwg, cos_n, sin_n, 0, hn)
        kw_ref[0, g] = kw.astype(BF16)
        vw_ref[0, g] = (blk(kv6, 8 + g) + vone).astype(BF16)

    mqg = mqg_ref[...]
    m_scale = MLA_QK ** -0.5 * LOG2E
    for hd in range(MLA_HEADS):
        qh = _rope(_rms(blk(q, hd), MLA_QK) * mqg, cos_m, sin_m, MLA_NOPE, hm) * m_scale
        qm_ref[0, :, HEAD_PAD * hd:HEAD_PAD * (hd + 1)] = qh.astype(BF16)

    gn_ref[0] = _sigmoid(zs)
    gm_ref[0] = _sigmoid(gates).astype(BF16)

    kpe = jnp.where((lane >= MLA_NOPE) & (lane < MLA_QK), zs, 0.0)
    kpe_ss = jnp.sum(kpe * kpe, axis=-1, keepdims=True)
    kr = _rope(kpe * mkr_ref[...], cos_m, sin_m, MLA_NOPE, hm)
    mkn = mkn_ref[...]
    for hd in range(MLA_HEADS):
        kn = blk(kv, hd)
        inv = lax.rsqrt((jnp.sum(kn * kn, axis=-1, keepdims=True) + kpe_ss) * (1.0 / MLA_QK) + EPS)
        km_ref[0, :, HEAD_PAD * hd:HEAD_PAD * (hd + 1)] = ((kn * mkn + kr) * inv).astype(BF16)
    vm_ref[0] = (kv[:, MLA_HEADS * HEAD_PAD:] + vone_ref[...]).astype(BF16)


def _inproj(x, mod, cs, consts, weights, rows):
    B, S, D = x.shape
    tm = TM_IN
    tok = lambda w: pl.BlockSpec((1, tm, w), lambda b, i: (b, i, 0))
    head = lambda n, w: pl.BlockSpec((1, n, tm, w), lambda b, i: (b, 0, i, 0))
    operands = list(consts) + list(weights[:6]) + [rows[0], weights[6], rows[1], weights[7]] + list(rows[2:])
    in_specs = [tok(D), pl.BlockSpec((1, N_MOD, D), lambda b, i: (b, 0, 0)), tok(LANES)]
    in_specs += [_const_spec(a.shape) for a in operands]
    G = NSA_KV_GROUPS
    sds = jax.ShapeDtypeStruct
    wide = MLA_HEADS * HEAD_PAD
    outs = [
        (tok(wide), sds((B, S, wide), BF16)),
        (tok(wide), sds((B, S, wide), BF16)),
        (tok(wide), sds((B, S, wide), BF16)),
        (tok(wide), sds((B, S, wide), BF16)),
        (head(G, HEAD_PAD), sds((B, G, S, HEAD_PAD), BF16)),
        (head(G, HEAD_PAD), sds((B, G, S, HEAD_PAD), BF16)),
        (head(G, HEAD_PAD), sds((B, G, S, HEAD_PAD), BF16)),
        (head(G, HEAD_PAD), sds((B, G, S, HEAD_PAD), BF16)),
        (tok(KV_W), sds((B, S, KV_W), F32)),
        (tok(KV_W), sds((B, S, KV_W), F32)),
        (tok(LANES), sds((B, S, LANES), F32)),
        (tok(2 * D), sds((B, S, 2 * D), BF16)),
    ]
    return pl.pallas_call(
        _inproj_kernel,
        grid=(B, S // tm),
        in_specs=in_specs,
        out_specs=[o[0] for o in outs],
        out_shape=[o[1] for o in outs],
        compiler_params=pltpu.CompilerParams(dimension_semantics=("arbitrary", "arbitrary"),
                                             vmem_limit_bytes=VMEM_LIMIT),
        name="inproj_prep",
    )(x, mod, cs, *operands)


def _compress_kernel(kcin_ref, vcin_ref, pk_ref, pv_ref, w1k_ref, w2k_ref, w1v_ref, w2v_ref,
                     kcg_ref, cs_ref, texp_ref, trow_ref, kc_ref, vc_ref):
    n = kcin_ref.shape[1] // CMP_STRIDE

    def hidden(cin_ref, pos_ref, w1_ref):
        a = b = None
        for l in range(0, CMP_STRIDE, 2):
            t0 = cin_ref[0, pl.ds(l, n, stride=CMP_STRIDE), :]
            t1 = cin_ref[0, pl.ds(l + 1, n, stride=CMP_STRIDE), :]
            pair = lambda o: jnp.concatenate([t0 + pos_ref[o + l:o + l + 1],
                                              t1 + pos_ref[o + l + 1:o + l + 2]], axis=-1).astype(BF16)
            da = _dot(pair(0), w1_ref[l // 2])
            db = _dot(pair(CMP_STRIDE), w1_ref[(CMP_STRIDE + l) // 2])
            a, b = (da, db) if a is None else (a + da, b + db)
        hid = a + pltpu.roll(b, n - 1, 0)
        return (hid * _sigmoid(hid)).astype(BF16)

    kc = _dot(hidden(kcin_ref, pk_ref, w1k_ref), w2k_ref[...])
    _, _, cos_n, sin_n = _rope_multipliers(cs_ref[0], texp_ref, trow_ref)
    vc = _dot(hidden(vcin_ref, pv_ref, w1v_ref), w2v_ref[...])
    for g in range(NSA_KV_GROUPS):
        kg = _rms(kc[:, HEAD_PAD * g:HEAD_PAD * (g + 1)], NSA_HEAD) * kcg_ref[...]
        kc_ref[0, g] = _rope(kg, cos_n, sin_n, 0, NSA_ROT // 2).astype(BF16)
        vc_ref[0, g] = vc[:, NSA_HEAD * g:NSA_HEAD * (g + 1)].astype(BF16)


def _compress(kcin, vcin, pk, pv, w1k, w2k, w1v, w2v, kcg, cs_end, texp, trow):
    B, S, w = kcin.shape
    G = NSA_KV_GROUPS
    n = S // CMP_STRIDE
    oblk = lambda wd: pl.BlockSpec((1, G, n, wd), lambda b: (b, 0, 0, 0))
    consts = (pk, pv, w1k, w2k, w1v, w2v, kcg)
    return pl.pallas_call(
        _compress_kernel,
        grid=(B,),
        in_specs=[pl.BlockSpec((1, S, w), lambda b: (b, 0, 0))] * 2 + [_const_spec(a.shape) for a in consts] +
                 [pl.BlockSpec((1, n, LANES), lambda b: (b, 0, 0)),
                  _const_spec(texp.shape), _const_spec(trow.shape)],
        out_specs=[oblk(HEAD_PAD), oblk(NSA_HEAD)],
        out_shape=[jax.ShapeDtypeStruct((B, G, n, HEAD_PAD), BF16),
                   jax.ShapeDtypeStruct((B, G, n, NSA_HEAD), BF16)],
        name="nsa_compress",
    )(kcin, vcin, *consts, cs_end, texp, trow)


def _nsa_kernel(q_ref, kc_ref, vc_ref, ks_ref, vs_ref, kw_ref, vw_ref, gn_ref,
                ovt_ref, gexp_ref, dbias_ref, wbias_ref, o_ref, imp_ref):
    tq = dbias_ref.shape[0]
    S = q_ref.shape[1]
    R = NSA_REP
    M = R * tq
    n_sel = imp_ref.shape[0]
    ncp = kc_ref.shape[2]
    span = WINDOW + tq
    grp = pl.program_id(1)
    dbias = dbias_ref[...]
    kf = lambda a, b: ks_ref[0, 0, a:b, :]
    vf = lambda a, b: vs_ref[0, 0, a:b, :]
    row = lax.broadcasted_iota(jnp.int32, (M, 1), 0)
    n_idx = lax.broadcasted_iota(jnp.int32, (M, ncp), 1)
    j = lax.broadcasted_iota(jnp.int32, (n_sel, tq), 0)
    head_q = lambda i, r: q_ref[0, i * tq:(i + 1) * tq, HEAD_PAD * r:HEAD_PAD * (r + 1)]
    tile_q = lambda i: jnp.concatenate([head_q(i, r) for r in range(R)], axis=0)
    tiles = {}

    def compressed_and_select(i, s):
        q0 = i * tq
        t = q0 + jnp.bitwise_and(row, tq - 1)
        valid =(n_idx * CMP_STRIDE + (CMP_LEN - 1)) <= t
        sm = jnp.where(valid, s, NEG)
        e = jnp.where(valid, jnp.exp2(sm - _rowmax(sm)), 0.0)
        den = jnp.sum(e, axis=-1, keepdims=True)
        p_c = e / jnp.where(den > 0.0, den, 1.0)
        o_c = _dot(p_c.astype(BF16), vc_ref[0, 0])
        psum = p_c[0:tq]
        for r in range(1, R):
            psum = psum + p_c[r * tq:(r + 1) * tq]
        hi, lo = _split_hilo(psum.T)
        imp = (_dot(ovt_ref[...], hi) + _dot(ovt_ref[...], lo))[0:n_sel]
        cur = lax.shift_right_logical(q0 + lax.broadcasted_iota(jnp.int32, (1, tq), 1),
                                      SEL_LEN.bit_length() - 1)
        forced = (j == 0) | (j == cur) | (j == cur - 1)
        imp = jnp.where(forced, imp + FORCE_BONUS, imp)
        imp = jnp.where(j <= cur, imp, NEG)
        imp_ref[...] = imp
        cnt = jnp.zeros((n_sel, tq), F32)
        for jj in range(n_sel):
            other = imp_ref[jj:jj + 1, :]
            beats = (other > imp) | ((other == imp) & (j > jj))
            cnt = cnt + jnp.where(beats, 1.0, 0.0)
        nsel = jnp.where((cnt < float(SEL_TOP)) & (j <= cur), 0.0, 1.0)
        nsel = jnp.concatenate([jnp.zeros((NSA_HEAD, tq), F32), nsel,
                                jnp.zeros((LANES - NSA_HEAD - n_sel, tq), F32)], axis=0).T.astype(BF16)
        tiles[i] = dict(o_c=o_c, nsel=nsel, o_s=[], o_w=[])

    def scores(u):
        kind, i, r = u
        if kind == "cmp":
            return _dot_nt(tile_q(i), kc_ref[0, 0])
        if kind == "sel":
            return _attention_scores(head_q(i, r) + tiles[i]["nsel"], kf, (i + 1) * tq, dbias)
        w0 = max(i * tq - WINDOW, 0)
        wb = wbias_ref[min(i, 1)]
        sw = _dot_nt(head_q(i, r), kw_ref[0, 0, w0:w0 + span, :]) + wb
        return sw, _rowmax(sw)

    def values(u, sc):
        kind, i, r = u
        if kind == "cmp":
            compressed_and_select(i, sc)
        elif kind == "sel":
            tiles[i]["o_s"].append(_attention_values(sc, vf, (i + 1) * tq))
        elif kind == "win":
            sw, mw = sc
            w0 = max(i * tq - WINDOW, 0)
            acc_w = _dot(jnp.exp2(sw - mw).astype(BF16), vw_ref[0, 0, w0:w0 + span, :])
            tiles[i]["o_w"].append(acc_w[:, :NSA_HEAD] / acc_w[:, NSA_HEAD:NSA_HEAD + 1])
        if kind == "win" and r == R - 1:
            tile = tiles.pop(i)
            g_hi, g_lo = _split_hilo(gn_ref[0, i * tq:(i + 1) * tq, :])
            o_c = jnp.concatenate([tile["o_c"][r * tq:(r + 1) * tq] for r in range(R)], axis=-1)
            branches = (o_c, jnp.concatenate(tile["o_s"], axis=-1), jnp.concatenate(tile["o_w"], axis=-1))
            out = None
            for br, o_b in enumerate(branches):
                gate = _dot(g_hi, gexp_ref[grp, br]) + _dot(g_lo, gexp_ref[grp, br])
                out = gate * o_b if out is None else out + gate * o_b
            o_ref[0, i * tq:(i + 1) * tq, :] = out.astype(BF16)

    nq = S // tq
    first = ("cmp", 0, 0)
    values(first, scores(first))
    units = []
    for i in range(nq):
        units += [("cmp", i + 1, 0)] if i + 1 < nq else []
        for r in range(R):
            units += [("sel", i, r), ("win", i, r)]
    pending = [scores(u) for u in units[:ATT_LOOKAHEAD]]
    for n, u in enumerate(units):
        if n + ATT_LOOKAHEAD < len(units):
            pending.append(scores(units[n + ATT_LOOKAHEAD]))
        values(u, pending.pop(0))


def _nsa_attention(qn, kc, vc, ks, vs, kw, vw, gn, ovt, gexp, dbias, wbias):
    B, S, _ = qn.shape
    G, R, Dh = NSA_KV_GROUPS, NSA_REP, NSA_HEAD
    ncp = kc.shape[2]
    full = pl.BlockSpec((1, 1, S, HEAD_PAD), lambda b, g: (b, g, 0, 0))
    cmp_spec = lambda w: pl.BlockSpec((1, 1, ncp, w), lambda b, g: (b, g, 0, 0))
    return pl.pallas_call(
        _nsa_kernel,
        grid=(B, G),
        in_specs=[pl.BlockSpec((1, S, R * HEAD_PAD), lambda b, g: (b, 0, g)),
                  cmp_spec(HEAD_PAD), cmp_spec(Dh), full, full, full, full,
                  pl.BlockSpec((1, S, LANES), lambda b, g: (b, 0, 0)),
                  _const_spec(ovt.shape), _const_spec(gexp.shape),
                  _const_spec(dbias.shape), _const_spec(wbias.shape)],
        out_specs=pl.BlockSpec((1, S, R * Dh), lambda b, g: (b, 0, g)),
        out_shape=jax.ShapeDtypeStruct((B, S, G * R * Dh), BF16),
        scratch_shapes=[pltpu.VMEM((S // SEL_LEN, TQ_ATT), F32)],
        compiler_params=pltpu.CompilerParams(dimension_semantics=("arbitrary",) * 2,
                                             vmem_limit_bytes=VMEM_LIMIT),
        name="nsa_attention",
    )(qn, kc, vc, ks, vs, kw, vw, gn, ovt, gexp, dbias, wbias)


def _mla_kernel(q_ref, k_ref, v_ref, dbias_ref, o_ref):
    tq = dbias_ref.shape[0]
    S = q_ref.shape[1]
    dbias = dbias_ref[...]
    units = [(i, hh) for i in range(S // tq) for hh in range(2)]

    def scores(u):
        i, hh = u
        cols = slice(HEAD_PAD * hh, HEAD_PAD * (hh + 1))
        q = q_ref[0, i * tq:(i + 1) * tq, cols]
        return _attention_scores(q, lambda a, b: k_ref[0, a:b, cols], (i + 1) * tq, dbias)

    def values(u, sc):
        i, hh = u
        cols = slice(HEAD_PAD * hh, HEAD_PAD * (hh + 1))
        o = _attention_values(sc, lambda a, b: v_ref[0, a:b, cols], (i + 1) * tq)
        o_ref[0, i * tq:(i + 1) * tq, MLA_V * hh:MLA_V * (hh + 1)] = o.astype(BF16)

    pending = [scores(u) for u in units[:ATT_LOOKAHEAD]]
    for n, u in enumerate(units):
        if n + ATT_LOOKAHEAD < len(units):
            pending.append(scores(units[n + ATT_LOOKAHEAD]))
        values(u, pending.pop(0))


def _mla_attention(qm, km, vm, dbias):
    B, S, _ = qm.shape
    pair = pl.BlockSpec((1, S, 2 * HEAD_PAD), lambda b, h: (b, 0, h))
    return pl.pallas_call(
        _mla_kernel,
        grid=(B, MLA_HEADS // 2),
        in_specs=[pair, pair, pair, _const_spec(dbias.shape)],
        out_specs=pl.BlockSpec((1, S, 2 * MLA_V), lambda b, h: (b, 0, h)),
        out_shape=jax.ShapeDtypeStruct((B, S, MLA_HEADS * MLA_V), BF16),
        compiler_params=pltpu.CompilerParams(dimension_semantics=("arbitrary",) * 2,
                                             vmem_limit_bytes=VMEM_LIMIT),
        name="mla_attention",
    )(qm, km, vm, dbias)


def _out_ffn_kernel(x_ref, om_ref, on_ref, gm_ref, mod_ref, g2_ref,
                    wom_ref, won_ref, wout_ref, wg_ref, wu_ref, wd_ref, o_ref):
    x = x_ref[0]
    mod = mod_ref[0]
    gt1, sh2, sc2, gt2 = mod[2:3], mod[3:4], mod[4:5], mod[5:6]
    ym = _dot(om_ref[0], wom_ref[...])
    yn = _dot(on_ref[0], won_ref[...])
    merged = gm_ref[0, :, :D_MODEL] * ym + gm_ref[0, :, D_MODEL:] * yn
    x1 = x + gt1 * _dot(merged.astype(BF16), wout_ref[...])
    h2 = (_rms(x1, D_MODEL) * g2_ref[...] * (1.0 + sc2) + sh2).astype(BF16)
    chunks = [slice(lo, hi) for lo, hi in zip(FF_SPLITS[:-1], FF_SPLITS[1:])]
    gu = [(_dot(h2, wg_ref[:, sl]), _dot(h2, wu_ref[:, sl])) for sl in chunks]
    acc = None
    for sl, (g, u) in zip(chunks, gu):
        d = _dot((g * _sigmoid(g) * u).astype(BF16), wd_ref[sl, :])
        acc = d if acc is None else acc + d
    o_ref[0] = x1 + gt2 * acc


def _out_ffn(x, om, on, gm, mod, g2, wom, won, wout, wg, wu, wd):
    B, S, D = x.shape
    tm = TM_OUT
    tok = lambda w: pl.BlockSpec((1, tm, w), lambda b, i: (b, i, 0))
    wspec = lambda w: pl.BlockSpec(w.shape, lambda b, i: (0, 0), pipeline_mode=pl.Buffered(1))
    return pl.pallas_call(
        _out_ffn_kernel,
        grid=(B, S // tm),
        in_specs=[tok(D), tok(om.shape[2]), tok(on.shape[2]), tok(2 * D),
                  pl.BlockSpec((1, N_MOD, D), lambda b, i: (b, 0, 0)),
                  _const_spec(g2.shape)] + [wspec(w) for w in (wom, won, wout, wg, wu, wd)],
        out_specs=tok(D),
        out_shape=jax.ShapeDtypeStruct((B, S, D), F32),
        compiler_params=pltpu.CompilerParams(dimension_semantics=("arbitrary", "arbitrary"),
                                             vmem_limit_bytes=VMEM_LIMIT),
        name="out_ffn",
    )(x, om, on, gm, mod, g2, wom, won, wout, wg, wu, wd)


def _rope_expansion():
    texp = np.zeros((LANES, 4 * LANES), np.float32)
    trow = np.zeros((1, 4 * LANES), np.float32)
    hm, hn = MLA_ROPE // 2, NSA_ROT // 2
    trow[0, 0:LANES] = 1.0
    trow[0, 2 * LANES:3 * LANES] = 1.0
    for i in range(hm):
        for off, sgn in ((MLA_NOPE + i, -1.0), (MLA_NOPE + hm + i, 1.0)):
            texp[i, off] = 1.0
            trow[0, off] = 0.0
            texp[N_FREQ + i, LANES + off] = sgn
    for i in range(hn):
        for off, sgn in ((i, -1.0), (hn + i, 1.0)):
            texp[hm + i, 2 * LANES + off] = 1.0
            trow[0, 2 * LANES + off] = 0.0
            texp[N_FREQ + hm + i, 3 * LANES + off] = sgn
    return jnp.asarray(texp, BF16), jnp.asarray(trow, F32)


def _mask_tables(S):
    tq = TQ_ATT
    n_chunk = S // CMP_STRIDE
    n_sel = S // SEL_LEN
    starts = np.arange(n_chunk) * CMP_STRIDE
    sel_start = np.arange(LANES) * SEL_LEN
    ovt = ((starts[None, :] < sel_start[:, None] + SEL_LEN) &
           (starts[None, :] + CMP_LEN > sel_start[:, None]) &
           (np.arange(n_chunk)[None, :] < n_chunk - 1) &
           (np.arange(LANES)[:, None] < n_sel))
    gcol = np.arange(LANES)[:, None]
    head = np.arange(NSA_REP * NSA_HEAD)[None, :] // NSA_HEAD
    gexp = np.stack([np.stack([gcol == (g * NSA_REP + head) * N_NSA_BRANCH + br
                               for br in range(N_NSA_BRANCH)]) for g in range(NSA_KV_GROUPS)])
    qi = np.arange(tq)[:, None]
    dbias = np.where(np.arange(tq)[None, :] <= qi, 0.0, NEG)
    kk = np.arange(WINDOW + tq)[None, :]
    band = lambda d: np.where((d >= 0) & (d < WINDOW), 0.0, NEG)
    wbias = np.stack([band(qi - kk), band(qi + WINDOW - kk)])
    return (jnp.asarray(ovt, BF16), jnp.asarray(gexp, BF16),
            jnp.asarray(dbias, F32), jnp.asarray(wbias, F32))


def _pad_heads(w, n_heads, width):
    k = w.shape[0]
    w = w.reshape(k, n_heads, width)
    return jnp.pad(w, ((0, 0), (0, 0), (0, HEAD_PAD - width))).reshape(k, n_heads * HEAD_PAD)


def _block_diag(w, n):
    k, m = w.shape
    eye = jnp.eye(n, dtype=w.dtype)
    return (eye[:, None, :, None] * w[None, :, None, :]).reshape(n * k, n * m)


def _cmp_w1_pairs(w1):
    per_tok = w1.reshape(CMP_LEN, NSA_HEAD, w1.shape[1])
    bd = jax.vmap(lambda w: _block_diag(w, NSA_KV_GROUPS))(per_tok)
    return bd.reshape(CMP_LEN // 2, 2 * KV_W, NSA_KV_GROUPS * w1.shape[1])


def _pad_row(g, lo=0):
    return jnp.pad(g, (lo, HEAD_PAD - lo - g.shape[0])).reshape(1, HEAD_PAD)


def _layer(x, mod, cs, p):
    B, S, D = x.shape
    w_in = p["w_in"]
    o = 0
    cols = {}
    for name, wdt in (("cq", MLA_Q_LORA), ("ckv", MLA_KV_LORA), ("kpe", MLA_ROPE),
                      ("qn", NSA_HEADS * NSA_HEAD), ("kc", KV_W), ("vc", KV_W), ("ks", KV_W),
                      ("vs", KV_W), ("kw", KV_W), ("vw", KV_W),
                      ("gn", NSA_HEADS * N_NSA_BRANCH), ("gm", 2 * D)):
        cols[name] = w_in[:, o:o + wdt]
        o += wdt
    G = NSA_KV_GROUPS
    n_gate = NSA_HEADS * N_NSA_BRANCH
    zc = lambda n: jnp.zeros((D, n), F32)
    wsm = jnp.concatenate([cols["gn"], zc(MLA_NOPE - n_gate), cols["kpe"], zc(LANES - MLA_QK)], axis=1)
    wkv6 = jnp.concatenate([cols["kc"], cols["vc"]] +
                           [_pad_heads(cols[k], G, NSA_HEAD) for k in ("ks", "vs", "kw", "vw")], axis=1)
    wkvb = p["mla_w_kv_b"].reshape(MLA_KV_LORA, MLA_HEADS, MLA_NOPE + MLA_V)
    wkvb = jnp.concatenate([_pad_heads(wkvb[:, :, :MLA_NOPE].reshape(MLA_KV_LORA, -1), MLA_HEADS, MLA_NOPE),
                            _pad_heads(wkvb[:, :, MLA_NOPE:].reshape(MLA_KV_LORA, -1), MLA_HEADS, MLA_V)], axis=1)
    bf = lambda w: w.astype(BF16)
    row = lambda g: g.reshape(1, -1)
    weights = tuple(bf(w) for w in (cols["cq"], cols["ckv"], wsm, _pad_heads(cols["qn"], NSA_HEADS, NSA_HEAD),
                                    wkv6, cols["gm"],
                                    _pad_heads(p["mla_w_q_b"], MLA_HEADS, MLA_QK), wkvb))
    vone = jnp.tile(jnp.zeros((1, HEAD_PAD), F32).at[0, MLA_V].set(1.0), (1, MLA_HEADS))
    rows = (row(p["mla_q_a_gain"]), row(p["mla_kv_a_gain"]),
            _pad_row(p["mla_q_gain"]), _pad_row(p["mla_k_gain"][:MLA_NOPE]),
            _pad_row(p["mla_k_gain"][MLA_NOPE:], MLA_NOPE),
            _pad_row(p["nsa_q_gain"]), _pad_row(p["nsa_ks_gain"]), _pad_row(p["nsa_kw_gain"]), vone)
    texp, trow = _rope_expansion()
    (qm, km, vm, qn, ks, kw, vs, vw, kcin, vcin, gn, gm) = _inproj(
        x, mod, cs, (row(p["norm1_gain"]), texp, trow), weights, rows)

    n_chunk = S // CMP_STRIDE
    cs_end = cs[:, CMP_LEN - 1::CMP_STRIDE]
    cs_end = jnp.pad(cs_end, ((0, 0), (0, n_chunk - cs_end.shape[1]), (0, 0)))
    w2k = jnp.pad(p["cmp_w2_k"], ((0, 0), (0, HEAD_PAD - NSA_HEAD)))
    kc, vc = _compress(kcin, vcin, jnp.tile(p["cmp_pos_k"], (1, G)), jnp.tile(p["cmp_pos_v"], (1, G)),
                       bf(_cmp_w1_pairs(p["cmp_w1_k"])), bf(_block_diag(w2k, G)),
                       bf(_cmp_w1_pairs(p["cmp_w1_v"])), bf(_block_diag(p["cmp_w2_v"], G)),
                       _pad_row(p["nsa_kc_gain"]), cs_end, texp, trow)

    ovt, gexp, dbias, wbias = _mask_tables(S)
    o_nsa = _nsa_attention(qn, kc, vc, ks, vs, kw, vw, gn, ovt, gexp, dbias, wbias)
    o_mla = _mla_attention(qm, km, vm, dbias)

    return _out_ffn(x, o_mla, o_nsa, gm, mod, row(p["norm2_gain"]),
                    bf(p["w_o_mla"]), bf(p["w_o_nsa"]), bf(p["w_out"]),
                    bf(p["ffn_w_gate"]), bf(p["ffn_w_up"]), bf(p["ffn_w_down"]))


def kernel(x, c, positions, ada_w, ada_b, norm1_gain, w_in, mla_q_a_gain, mla_w_q_b, mla_kv_a_gain, mla_w_kv_b, mla_q_gain, mla_k_gain, nsa_q_gain, nsa_kc_gain, nsa_ks_gain, nsa_kw_gain, cmp_pos_k, cmp_w1_k, cmp_w2_k, cmp_pos_v, cmp_w1_v, cmp_w2_v, w_o_mla, w_o_nsa, w_out, norm2_gain, ffn_w_gate, ffn_w_up, ffn_w_down):
    params = dict(norm1_gain=norm1_gain, w_in=w_in, mla_q_a_gain=mla_q_a_gain, mla_w_q_b=mla_w_q_b,
                  mla_kv_a_gain=mla_kv_a_gain, mla_w_kv_b=mla_w_kv_b, mla_q_gain=mla_q_gain,
                  mla_k_gain=mla_k_gain, nsa_q_gain=nsa_q_gain, nsa_kc_gain=nsa_kc_gain,
                  nsa_ks_gain=nsa_ks_gain, nsa_kw_gain=nsa_kw_gain, cmp_pos_k=cmp_pos_k,
                  cmp_w1_k=cmp_w1_k, cmp_w2_k=cmp_w2_k, cmp_pos_v=cmp_pos_v, cmp_w1_v=cmp_w1_v,
                  cmp_w2_v=cmp_w2_v, w_o_mla=w_o_mla, w_o_nsa=w_o_nsa, w_out=w_out,
                  norm2_gain=norm2_gain, ffn_w_gate=ffn_w_gate, ffn_w_up=ffn_w_up, ffn_w_down=ffn_w_down)
    B = x.shape[0]
    inv_m = ROPE_THETA ** (-jnp.arange(0, MLA_ROPE, 2, dtype=F32) / MLA_ROPE)
    inv_n = ROPE_THETA ** (-jnp.arange(0, NSA_ROT, 2, dtype=F32) / NSA_ROT)
    n_unused = N_FREQ - inv_m.shape[0] - inv_n.shape[0]
    cs = _rope_tables(positions, jnp.concatenate([inv_m, inv_n, jnp.zeros((n_unused,), F32)]))
    depth = ada_w.shape[0]
    for l in range(depth):
        mod = _ada(c, ada_w, ada_b, l).reshape(B, N_MOD, D_MODEL)
        x = _layer(x, mod, cs, {k: v[l] for k, v in params.items()})
    return x
```

```python
import numpy as np
import jax
import jax.numpy as jnp
from jax import lax
from jax.experimental import pallas as pl
from jax.experimental.pallas import tpu as pltpu

F32 = jnp.float32
BF16 = jnp.bfloat16

D_MODEL = 1024
ROPE_THETA = 500000.0
EPS = 1e-6
NEG = -1e30
LOG2E = 1.4426950408889634

MLA_HEADS = 8
MLA_NOPE = 64
MLA_ROPE = 32
MLA_QK = MLA_NOPE + MLA_ROPE
MLA_V = 64
MLA_Q_LORA = 768
MLA_KV_LORA = 256

NSA_HEADS = 8
NSA_KV_GROUPS = 2
NSA_REP = NSA_HEADS // NSA_KV_GROUPS
NSA_HEAD = 64
NSA_ROT = NSA_HEAD // 4
CMP_LEN = 32
CMP_STRIDE = 16
CMP_HIDDEN = 256
SEL_LEN = 64
SEL_TOP = 8
WINDOW = 256
N_NSA_BRANCH = 3
FORCE_BONUS = 1e4
KV_W = NSA_KV_GROUPS * NSA_HEAD

D_FF = -(-8 * D_MODEL // (3 * 256)) * 256
N_MOD = 6
LANES = 128
HEAD_PAD = LANES
N_FREQ = 32

TM_IN = 512
IN_ROWS = 256
TQ_ATT = 256
TM_OUT = 512
MXU_TILE = 256
FF_SPLITS = (0, 6 * MXU_TILE, D_FF)
ATT_LOOKAHEAD = 2
VMEM_LIMIT = 56 * 1024 * 1024


def _dot(a, b):
    return jnp.dot(a, b, preferred_element_type=F32)


def _dot_nt(a, b):
    return lax.dot_general(a, b, (((1,), (1,)), ((), ())), preferred_element_type=F32)


def _split_hilo(a):
    hi = a.astype(BF16)
    return hi, (a - hi.astype(F32)).astype(BF16)


def _dot_hilo(a, m):
    hi, lo = _split_hilo(a)
    return _dot(hi, m) + _dot(lo, m)


def _sigmoid(v):
    return 1.0 / (1.0 + jnp.exp(-v))


def _rms(v, n):
    return v * lax.rsqrt(jnp.sum(v * v, axis=-1, keepdims=True) * (1.0 / n) + EPS)


def _rope(v, cos_v, sin_v, lo, half):
    lane = lax.broadcasted_iota(jnp.int32, v.shape, 1)
    is_x1 = (lane >= lo) & (lane < lo + half)
    rot = jnp.where(is_x1, pltpu.roll(v, LANES - half, 1), pltpu.roll(v, half, 1))
    return v * cos_v + rot * sin_v


def _rope_multipliers(cs, texp_ref, trow_ref):
    tabs = _dot_hilo(cs, texp_ref[...]) + trow_ref[...]
    return tuple(tabs[:, LANES * i:LANES * (i + 1)] for i in range(4))


def _const_spec(shape, single_buffer=False):
    nd = len(shape)
    mode = {"pipeline_mode": pl.Buffered(1)} if single_buffer else {}
    return pl.BlockSpec(shape, lambda *_: (0,) * nd, **mode)


def _rowmax(s):
    return jnp.max(s, axis=-1, keepdims=True)


def _attention_scores(q, k_ref, kmax, dbias):
    k0 = kmax - dbias.shape[1]
    s_d = _dot_nt(q, k_ref(k0, kmax)) + dbias
    m = _rowmax(s_d)
    s_m = None
    if k0 > 0:
        s_m = _dot_nt(q, k_ref(0, k0))
        m = jnp.maximum(m, _rowmax(s_m))
    return s_m, s_d, m


def _attention_values(scores, v_ref, kmax):
    s_m, s_d, m = scores
    k0 = kmax - s_d.shape[1]
    acc = _dot(jnp.exp2(s_d - m).astype(BF16), v_ref(k0, kmax))
    if s_m is not None:
        acc = acc + _dot(jnp.exp2(s_m - m).astype(BF16), v_ref(0, k0))
    return acc[:, :NSA_HEAD] / acc[:, NSA_HEAD:NSA_HEAD + 1]


def _rope_kernel(pos_ref, inv_ref, cs_ref):
    ang = pos_ref[0].astype(F32) * inv_ref[...]
    nf, S = ang.shape
    rows = jnp.concatenate([jnp.cos(ang), jnp.sin(ang), jnp.zeros((LANES - 2 * nf, S), F32)], axis=0)
    cs_ref[0] = rows.T


def _rope_tables(positions, inv):
    B, S = positions.shape
    nf = inv.shape[0]
    return pl.pallas_call(
        _rope_kernel,
        grid=(B,),
        in_specs=[pl.BlockSpec((1, 1, S), lambda b: (b, 0, 0)),
                  _const_spec((nf, 1))],
        out_specs=pl.BlockSpec((1, S, LANES), lambda b: (b, 0, 0)),
        out_shape=jax.ShapeDtypeStruct((B, S, LANES), F32),
        name="rope_tables",
    )(positions.reshape(B, 1, S), inv.reshape(nf, 1))


def _ada_kernel(c_ref, w_ref, b_ref, o_ref):
    c = c_ref[...]
    sc = c * _sigmoid(c)
    o_ref[...] = jnp.dot(sc, w_ref[0], preferred_element_type=F32,
                         precision=lax.Precision.HIGHEST) + b_ref[0]


def _ada(c, w, b, layer):
    B, D = c.shape
    N = w.shape[2]
    tn = D_MODEL
    return pl.pallas_call(
        _ada_kernel,
        grid=(N // tn,),
        in_specs=[_const_spec((B, D)),
                  pl.BlockSpec((1, D, tn), lambda j: (layer, 0, j)),
                  pl.BlockSpec((1, 1, tn), lambda j: (layer, 0, j))],
        out_specs=pl.BlockSpec((B, tn), lambda j: (0, j)),
        out_shape=jax.ShapeDtypeStruct((B, N), F32),
        name="ada_mod",
    )(c, w, b.reshape(b.shape[0], 1, N))


def _inproj_kernel(x_ref, mod_ref, cs_ref, g1_ref, texp_ref, trow_ref,
                   wcq_ref, wckv_ref, wsm_ref, wqn_ref, wkv6_ref, wgm_ref,
                   qag_ref, wqb_ref, kvag_ref, wkvb_ref,
                   mqg_ref, mkn_ref, mkr_ref, nqg_ref, nksg_ref, nkwg_ref, vone_ref,
                   qm_ref, km_ref, vm_ref, qn_ref, ks_ref, kw_ref, vs_ref, vw_ref,
                   kcin_ref, vcin_ref, gn_ref, gm_ref):
    tm = x_ref.shape[1]
    mod = mod_ref[0]
    sh1, sc1 = mod[0:1], mod[1:2]
    lane = lax.broadcasted_iota(jnp.int32, (IN_ROWS, LANES), 1)
    blk = lambda a, i: a[:, HEAD_PAD * i:HEAD_PAD * (i + 1)]
    hm = MLA_ROPE // 2
    hn = NSA_ROT // 2

    def front(rows):
        h = _rms(x_ref[0, rows], D_MODEL) * g1_ref[...] * (1.0 + sc1) + sh1
        hb = h.astype(BF16)
        return dict(
            hb=hb,
            tabs=_rope_multipliers(cs_ref[0, rows], texp_ref, trow_ref),
            cq=_dot(hb, wcq_ref[...]),
            ckv=_dot(hb, wckv_ref[...]),
            zs=_dot(hb, wsm_ref[...]),
            qn=_dot(hb, wqn_ref[...]),
            kv6=_dot(hb, wkv6_ref[...]))

    def middle(st):
        cqn = (_rms(st["cq"], MLA_Q_LORA) * qag_ref[...]).astype(BF16)
        st["q"] = _dot(cqn, wqb_ref[...])
        st["gates"] = _dot(st["hb"], wgm_ref[...])
        ckvn = (_rms(st["ckv"], MLA_KV_LORA) * kvag_ref[...]).astype(BF16)
        st["kv"] = _dot(ckvn, wkvb_ref[...])

    def back(rows, st):
        cos_m, sin_m, cos_n, sin_n = st["tabs"]
        zs, qn, kv6, q, kv = st["zs"], st["qn"], st["kv6"], st["q"], st["kv"]

        nqg = nqg_ref[...]
        n_scale = NSA_HEAD ** -0.5 * LOG2E
        for hd in range(NSA_HEADS):
            qh = _rope(_rms(blk(qn, hd), NSA_HEAD) * nqg, cos_n, sin_n, 0, hn) * n_scale
            qn_ref[0, rows, HEAD_PAD * hd:HEAD_PAD * (hd + 1)] = qh.astype(BF16)

        tok = pl.program_id(1) * tm + rows.start + lax.broadcasted_iota(jnp.int32, (IN_ROWS, 1), 0)
        sblk = lax.shift_right_logical(tok, SEL_LEN.bit_length() - 1)
        ind = jnp.where(lane - NSA_HEAD == sblk, NEG, 0.0)
        vone = vone_ref[:, 0:HEAD_PAD]
        nksg, nkwg = nksg_ref[...], nkwg_ref[...]
        kcin_ref[0, rows] = blk(kv6, 0)
        vcin_ref[0, rows] = blk(kv6, 1)
        for g in range(NSA_KV_GROUPS):
            ks = _rope(_rms(blk(kv6, 2 + g), NSA_HEAD) * nksg, cos_n, sin_n, 0, hn)
            ks_ref[0, g, rows] = (ks + ind).astype(BF16)
            vs_ref[0, g, rows] = (blk(kv6, 4 + g) + vone).astype(BF16)
            kw = _rope(_rms(blk(kv6, 6 + g), NSA_HEAD) * nkwg, cos_n, sin_n, 0, hn)
            kw_ref[0, g, rows] = kw.astype(BF16)
            vw_ref[0, g, rows] = (blk(kv6, 8 + g) + vone).astype(BF16)

        mqg = mqg_ref[...]
        m_scale = MLA_QK ** -0.5 * LOG2E
        for hd in range(MLA_HEADS):
            qh = _rope(_rms(blk(q, hd), MLA_QK) * mqg, cos_m, sin_m, MLA_NOPE, hm) * m_scale
            qm_ref[0, rows, HEAD_PAD * hd:HEAD_PAD * (hd + 1)] = qh.astype(BF16)

        gn_ref[0, rows] = _sigmoid(zs)
        gm_ref[0, rows] = _sigmoid(st["gates"]).astype(BF16)

        kpe = jnp.where((lane >= MLA_NOPE) & (lane < MLA_QK), zs, 0.0)
        kpe_ss = jnp.sum(kpe * kpe, axis=-1, keepdims=True)
        kr = _rope(kpe * mkr_ref[...], cos_m, sin_m, MLA_NOPE, hm)
        mkn = mkn_ref[...]
        for hd in range(MLA_HEADS):
            kn = blk(kv, hd)
            inv = lax.rsqrt((jnp.sum(kn * kn, axis=-1, keepdims=True) + kpe_ss) * (1.0 / MLA_QK) + EPS)
            km_ref[0, rows, HEAD_PAD * hd:HEAD_PAD * (hd + 1)] = ((kn * mkn + kr) * inv).astype(BF16)
        vm_ref[0, rows] = (kv[:, MLA_HEADS * HEAD_PAD:] + vone_ref[...]).astype(BF16)

    groups = [slice(s, s + IN_ROWS) for s in range(0, tm, IN_ROWS)]
    states = [front(rows) for rows in groups]
    for st in states:
        middle(st)
    for rows, st in zip(groups, states):
        back(rows, st)


def _inproj(x, mod, cs, consts, weights, rows):
    B, S, D = x.shape
    tm = TM_IN
    tok = lambda w: pl.BlockSpec((1, tm, w), lambda b, i: (b, i, 0))
    head = lambda n, w: pl.BlockSpec((1, n, tm, w), lambda b, i: (b, 0, i, 0))
    operands = list(consts) + list(weights[:6]) + [rows[0], weights[6], rows[1], weights[7]] + list(rows[2:])
    in_specs = [tok(D), pl.BlockSpec((1, N_MOD, D), lambda b, i: (b, 0, 0)), tok(LANES)]
    in_specs += [_const_spec(a.shape, single_buffer=True) for a in operands]
    G = NSA_KV_GROUPS
    sds = jax.ShapeDtypeStruct
    wide = MLA_HEADS * HEAD_PAD
    outs = [
        (tok(wide), sds((B, S, wide), BF16)),
        (tok(wide), sds((B, S, wide), BF16)),
        (tok(wide), sds((B, S, wide), BF16)),
        (tok(wide), sds((B, S, wide), BF16)),
        (head(G, HEAD_PAD), sds((B, G, S, HEAD_PAD), BF16)),
        (head(G, HEAD_PAD), sds((B, G, S, HEAD_PAD), BF16)),
        (head(G, HEAD_PAD), sds((B, G, S, HEAD_PAD), BF16)),
        (head(G, HEAD_PAD), sds((B, G, S, HEAD_PAD), BF16)),
        (tok(KV_W), sds((B, S, KV_W), F32)),
        (tok(KV_W), sds((B, S, KV_W), F32)),
        (tok(LANES), sds((B, S, LANES), F32)),
        (tok(2 * D), sds((B, S, 2 * D), BF16)),
    ]
    return pl.pallas_call(
        _inproj_kernel,
        grid=(B, S // tm),
        in_specs=in_specs,
        out_specs=[o[0] for o in outs],
        out_shape=[o[1] for o in outs],
        compiler_params=pltpu.CompilerParams(dimension_semantics=("arbitrary", "arbitrary"),
                                             vmem_limit_bytes=VMEM_LIMIT),
        name="inproj_prep",
    )(x, mod, cs, *operands)


def _compress_kernel(kcin_ref, vcin_ref, pk_ref, pv_ref, w1k_ref, w2k_ref, w1v_ref, w2v_ref,
                     kcg_ref, cs_ref, texp_ref, trow_ref, kc_ref, vc_ref):
    n = kcin_ref.shape[1] // CMP_STRIDE

    def hidden(cin_ref, pos_ref, w1_ref):
        a = b = None
        for l in range(0, CMP_STRIDE, 2):
            t0 = cin_ref[0, pl.ds(l, n, stride=CMP_STRIDE), :]
            t1 = cin_ref[0, pl.ds(l + 1, n, stride=CMP_STRIDE), :]
            pair = lambda o: jnp.concatenate([t0 + pos_ref[o + l:o + l + 1],
                                              t1 + pos_ref[o + l + 1:o + l + 2]], axis=-1).astype(BF16)
            da = _dot(pair(0), w1_ref[l // 2])
            db = _dot(pair(CMP_STRIDE), w1_ref[(CMP_STRIDE + l) // 2])
            a, b = (da, db) if a is None else (a + da, b + db)
        hid = a + pltpu.roll(b, n - 1, 0)
        return (hid * _sigmoid(hid)).astype(BF16)

    kc = _dot(hidden(kcin_ref, pk_ref, w1k_ref), w2k_ref[...])
    _, _, cos_n, sin_n = _rope_multipliers(cs_ref[0], texp_ref, trow_ref)
    vc = _dot(hidden(vcin_ref, pv_ref, w1v_ref), w2v_ref[...])
    for g in range(NSA_KV_GROUPS):
        kg = _rms(kc[:, HEAD_PAD * g:HEAD_PAD * (g + 1)], NSA_HEAD) * kcg_ref[...]
        kc_ref[0, g] = _rope(kg, cos_n, sin_n, 0, NSA_ROT // 2).astype(BF16)
        vc_ref[0, g] = vc[:, NSA_HEAD * g:NSA_HEAD * (g + 1)].astype(BF16)


def _compress(kcin, vcin, pk, pv, w1k, w2k, w1v, w2v, kcg, cs_end, texp, trow):
    B, S, w = kcin.shape
    G = NSA_KV_GROUPS
    n = S // CMP_STRIDE
    oblk = lambda wd: pl.BlockSpec((1, G, n, wd), lambda b: (b, 0, 0, 0))
    consts = (pk, pv, w1k, w2k, w1v, w2v, kcg)
    return pl.pallas_call(
        _compress_kernel,
        grid=(B,),
        in_specs=[pl.BlockSpec((1, S, w), lambda b: (b, 0, 0))] * 2 + [_const_spec(a.shape) for a in consts] +
                 [pl.BlockSpec((1, n, LANES), lambda b: (b, 0, 0)),
                  _const_spec(texp.shape), _const_spec(trow.shape)],
        out_specs=[oblk(HEAD_PAD), oblk(NSA_HEAD)],
        out_shape=[jax.ShapeDtypeStruct((B, G, n, HEAD_PAD), BF16),
                   jax.ShapeDtypeStruct((B, G, n, NSA_HEAD), BF16)],
        name="nsa_compress",
    )(kcin, vcin, *consts, cs_end, texp, trow)


def _nsa_kernel(q_ref, kc_ref, vc_ref, ks_ref, vs_ref, kw_ref, vw_ref, gn_ref,
                ovt_ref, gexp_ref, dbias_ref, wbias_ref, o_ref, imp_ref):
    tq = dbias_ref.shape[0]
    S = q_ref.shape[1]
    R = NSA_REP
    M = R * tq
    n_sel = imp_ref.shape[0]
    ncp = kc_ref.shape[2]
    span = WINDOW + tq
    grp = pl.program_id(1)
    dbias = dbias_ref[...]
    kf = lambda a, b: ks_ref[0, 0, a:b, :]
    vf = lambda a, b: vs_ref[0, 0, a:b, :]
    row = lax.broadcasted_iota(jnp.int32, (M, 1), 0)
    n_idx = lax.broadcasted_iota(jnp.int32, (M, ncp), 1)
    j = lax.broadcasted_iota(jnp.int32, (n_sel, tq), 0)
    head_q = lambda i, r: q_ref[0, i * tq:(i + 1) * tq, HEAD_PAD * r:HEAD_PAD * (r + 1)]
    tile_q = lambda i: jnp.concatenate([head_q(i, r) for r in range(R)], axis=0)
    tiles = {}

    def compressed_and_select(i, s):
        q0 = i * tq
        t = q0 + jnp.bitwise_and(row, tq - 1)
        valid = (n_idx * CMP_STRIDE + (CMP_LEN - 1)) <= t
        sm = jnp.where(valid, s, NEG)
        e = jnp.where(valid, jnp.exp2(sm - _rowmax(sm)), 0.0)
        den = jnp.sum(e, axis=-1, keepdims=True)
        p_c = e / jnp.where(den > 0.0, den, 1.0)
        o_c = _dot(p_c.astype(BF16), vc_ref[0, 0])
        psum = p_c[0:tq]
        for r in range(1, R):
            psum = psum + p_c[r * tq:(r + 1) * tq]
        hi, lo = _split_hilo(psum.T)
        imp = (_dot(ovt_ref[...], hi) + _dot(ovt_ref[...], lo))[0:n_sel]
        cur = lax.shift_right_logical(q0 + lax.broadcasted_iota(jnp.int32, (1, tq), 1),
                                      SEL_LEN.bit_length() - 1)
        forced = (j == 0) | (j == cur) | (j == cur - 1)
        imp = jnp.where(forced, imp + FORCE_BONUS, imp)
        imp = jnp.where(j <= cur, imp, NEG)
        imp_ref[...] = imp
        cnt = jnp.zeros((n_sel, tq), F32)
        for jj in range(n_sel):
            other = imp_ref[jj:jj + 1, :]
            beats = (other > imp) | ((other == imp) & (j > jj))
            cnt = cnt + jnp.where(beats, 1.0, 0.0)
        nsel = jnp.where((cnt < float(SEL_TOP)) & (j <= cur), 0.0, 1.0)
        nsel = jnp.concatenate([jnp.zeros((NSA_HEAD, tq), F32), nsel,
                                jnp.zeros((LANES - NSA_HEAD - n_sel, tq), F32)], axis=0).T.astype(BF16)
        tiles[i] = dict(o_c=o_c, nsel=nsel, o_s=[], o_w=[])

    def scores(u):
        kind, i, r = u
        if kind == "cmp":
            return _dot_nt(tile_q(i), kc_ref[0, 0])
        if kind == "sel":
            return _attention_scores(head_q(i, r) + tiles[i]["nsel"], kf, (i + 1) * tq, dbias)
        w0 = max(i * tq - WINDOW, 0)
        wb = wbias_ref[min(i, 1)]
        sw = _dot_nt(head_q(i, r), kw_ref[0, 0, w0:w0 + span, :]) + wb
        return sw, _rowmax(sw)

    def values(u, sc):
        kind, i, r = u
        if kind == "cmp":
            compressed_and_select(i, sc)
        elif kind == "sel":
            tiles[i]["o_s"].append(_attention_values(sc, vf, (i + 1) * tq))
        elif kind == "win":
            sw, mw = sc
            w0 = max(i * tq - WINDOW, 0)
            acc_w = _dot(jnp.exp2(sw - mw).astype(BF16), vw_ref[0, 0, w0:w0 + span, :])
            tiles[i]["o_w"].append(acc_w[:, :NSA_HEAD] / acc_w[:, NSA_HEAD:NSA_HEAD + 1])
        if kind == "win" and r == R - 1:
            tile = tiles.pop(i)
            g_hi, g_lo = _split_hilo(gn_ref[0, i * tq:(i + 1) * tq, :])
            o_c = jnp.concatenate([tile["o_c"][r * tq:(r + 1) * tq] for r in range(R)], axis=-1)
            branches = (o_c, jnp.concatenate(tile["o_s"], axis=-1), jnp.concatenate(tile["o_w"], axis=-1))
            out = None
            for br, o_b in enumerate(branches):
                gate = _dot(g_hi, gexp_ref[grp, br]) + _dot(g_lo, gexp_ref[grp, br])
                out = gate * o_b if out is None else out + gate * o_b
            o_ref[0, i * tq:(i + 1) * tq, :] = out.astype(BF16)

    nq = S // tq
    first = ("cmp", 0, 0)
    values(first, scores(first))
    units = []
    for i in range(nq):
        units += [("cmp", i + 1, 0)] if i + 1 < nq else []
        for r in range(R):
            units += [("sel", i, r), ("win", i, r)]
    pending = [scores(u) for u in units[:ATT_LOOKAHEAD]]
    for n, u in enumerate(units):
        if n + ATT_LOOKAHEAD < len(units):
            pending.append(scores(units[n + ATT_LOOKAHEAD]))
        values(u, pending.pop(0))


def _nsa_attention(qn, kc, vc, ks, vs, kw, vw, gn, ovt, gexp, dbias, wbias):
    B, S, _ = qn.shape
    G, R, Dh = NSA_KV_GROUPS, NSA_REP, NSA_HEAD
    ncp = kc.shape[2]
    full = pl.BlockSpec((1, 1, S, HEAD_PAD), lambda b, g: (b, g, 0, 0))
    cmp_spec = lambda w: pl.BlockSpec((1, 1, ncp, w), lambda b, g: (b, g, 0, 0))
    return pl.pallas_call(
        _nsa_kernel,
        grid=(B, G),
        in_specs=[pl.BlockSpec((1, S, R * HEAD_PAD), lambda b, g: (b, 0, g)),
                  cmp_spec(HEAD_PAD), cmp_spec(Dh), full, full, full, full,
                  pl.BlockSpec((1, S, LANES), lambda b, g: (b, 0, 0)),
                  _const_spec(ovt.shape), _const_spec(gexp.shape),
                  _const_spec(dbias.shape), _const_spec(wbias.shape)],
        out_specs=pl.BlockSpec((1, S, R * Dh), lambda b, g: (b, 0, g)),
        out_shape=jax.ShapeDtypeStruct((B, S, G * R * Dh), BF16),
        scratch_shapes=[pltpu.VMEM((S // SEL_LEN, TQ_ATT), F32)],
        compiler_params=pltpu.CompilerParams(dimension_semantics=("arbitrary",) * 2,
                                             vmem_limit_bytes=VMEM_LIMIT),
        name="nsa_attention",
    )(qn, kc, vc, ks, vs, kw, vw, gn, ovt, gexp, dbias, wbias)


def _mla_kernel(q_ref, k_ref, v_ref, dbias_ref, o_ref):
    tq = dbias_ref.shape[0]
    S = q_ref.shape[1]
    dbias = dbias_ref[...]
    units = [(i, hh) for i in range(S // tq) for hh in range(2)]

    def scores(u):
        i, hh = u
        cols = slice(HEAD_PAD * hh, HEAD_PAD * (hh + 1))
        q = q_ref[0, i * tq:(i + 1) * tq, cols]
        return _attention_scores(q, lambda a, b: k_ref[0, a:b, cols], (i + 1) * tq, dbias)

    def values(u, sc):
        i, hh = u
        cols = slice(HEAD_PAD * hh, HEAD_PAD * (hh + 1))
        o = _attention_values(sc, lambda a, b: v_ref[0, a:b, cols], (i + 1) * tq)
        o_ref[0, i * tq:(i + 1) * tq, MLA_V * hh:MLA_V * (hh + 1)] = o.astype(BF16)

    pending = [scores(u) for u in units[:ATT_LOOKAHEAD]]
    for n, u in enumerate(units):
        if n + ATT_LOOKAHEAD < len(units):
            pending.append(scores(units[n + ATT_LOOKAHEAD]))
        values(u, pending.pop(0))


def _mla_attention(qm, km, vm, dbias):
    B, S, _ = qm.shape
    pair = pl.BlockSpec((1, S, 2 * HEAD_PAD), lambda b, h: (b, 0, h))
    return pl.pallas_call(
        _mla_kernel,
        grid=(B, MLA_HEADS // 2),
        in_specs=[pair, pair, pair, _const_spec(dbias.shape)],
        out_specs=pl.BlockSpec((1, S, 2 * MLA_V), lambda b, h: (b, 0, h)),
        out_shape=jax.ShapeDtypeStruct((B, S, MLA_HEADS * MLA_V), BF16),
        compiler_params=pltpu.CompilerParams(dimension_semantics=("arbitrary",) * 2,
                                             vmem_limit_bytes=VMEM_LIMIT),
        name="mla_attention",
    )(qm, km, vm, dbias)


def _out_ffn_kernel(x_ref, om_ref, on_ref, gm_ref, mod_ref, g2_ref,
                    wom_ref, won_ref, wout_ref, wg_ref, wu_ref, wd_ref, o_ref):
    x = x_ref[0]
    mod = mod_ref[0]
    gt1, sh2, sc2, gt2 = mod[2:3], mod[3:4], mod[4:5], mod[5:6]
    ym = _dot(om_ref[0], wom_ref[...])
    yn = _dot(on_ref[0], won_ref[...])
    merged = gm_ref[0, :, :D_MODEL] * ym + gm_ref[0, :, D_MODEL:] * yn
    x1 = x + gt1 * _dot(merged.astype(BF16), wout_ref[...])
    h2 = (_rms(x1, D_MODEL) * g2_ref[...] * (1.0 + sc2) + sh2).astype(BF16)
    chunks = [slice(lo, hi) for lo, hi in zip(FF_SPLITS[:-1], FF_SPLITS[1:])]
    gu = [(_dot(h2, wg_ref[:, sl]), _dot(h2, wu_ref[:, sl])) for sl in chunks]
    acc = None
    for sl, (g, u) in zip(chunks, gu):
        d = _dot((g * _sigmoid(g) * u).astype(BF16), wd_ref[sl, :])
        acc = d if acc is None else acc + d
    o_ref[0] = x1 + gt2 * acc


def _out_ffn(x, om, on, gm, mod, g2, wom, won, wout, wg, wu, wd):
    B, S, D = x.shape
    tm = TM_OUT
    tok = lambda w: pl.BlockSpec((1, tm, w), lambda b, i: (b, i, 0))
    wspec = lambda w: pl.BlockSpec(w.shape, lambda b, i: (0, 0), pipeline_mode=pl.Buffered(1))
    return pl.pallas_call(
        _out_ffn_kernel,
        grid=(B, S // tm),
        in_specs=[tok(D), tok(om.shape[2]), tok(on.shape[2]), tok(2 * D),
                  pl.BlockSpec((1, N_MOD, D), lambda b, i: (b, 0, 0)),
                  _const_spec(g2.shape)] + [wspec(w) for w in (wom, won, wout, wg, wu, wd)],
        out_specs=tok(D),
        out_shape=jax.ShapeDtypeStruct((B, S, D), F32),
        compiler_params=pltpu.CompilerParams(dimension_semantics=("arbitrary", "arbitrary"),
                                             vmem_limit_bytes=VMEM_LIMIT),
        name="out_ffn",
    )(x, om, on, gm, mod, g2, wom, won, wout, wg, wu, wd)


def _rope_expansion():
    texp = np.zeros((LANES, 4 * LANES), np.float32)
    trow = np.zeros((1, 4 * LANES), np.float32)
    hm, hn = MLA_ROPE // 2, NSA_ROT // 2
    trow[0, 0:LANES] = 1.0
    trow[0, 2 * LANES:3 * LANES] = 1.0
    for i in range(hm):
        for off, sgn in ((MLA_NOPE + i, -1.0), (MLA_NOPE + hm + i, 1.0)):
            texp[i, off] = 1.0
            trow[0, off] = 0.0
            texp[N_FREQ + i, LANES + off] = sgn
    for i in range(hn):
        for off, sgn in ((i, -1.0), (hn + i, 1.0)):
            texp[hm + i, 2 * LANES + off] = 1.0
            trow[0, 2 * LANES + off] = 0.0
            texp[N_FREQ + hm + i, 3 * LANES + off] = sgn
    return jnp.asarray(texp, BF16), jnp.asarray(trow, F32)


def _mask_tables(S):
    tq = TQ_ATT
    n_chunk = S // CMP_STRIDE
    n_sel = S // SEL_LEN
    starts = np.arange(n_chunk) * CMP_STRIDE
    sel_start = np.arange(LANES) * SEL_LEN
    ovt = ((starts[None, :] < sel_start[:, None] + SEL_LEN) &
           (starts[None, :] + CMP_LEN > sel_start[:, None]) &
           (np.arange(n_chunk)[None, :] < n_chunk - 1) &
           (np.arange(LANES)[:, None] < n_sel))
    gcol = np.arange(LANES)[:, None]
    head = np.arange(NSA_REP * NSA_HEAD)[None, :] // NSA_HEAD
    gexp = np.stack([np.stack([gcol == (g * NSA_REP + head) * N_NSA_BRANCH + br
                               for br in range(N_NSA_BRANCH)]) for g in range(NSA_KV_GROUPS)])
    qi = np.arange(tq)[:, None]
    dbias = np.where(np.arange(tq)[None, :] <= qi, 0.0, NEG)
    kk = np.arange(WINDOW + tq)[None, :]
    band = lambda d: np.where((d >= 0) & (d < WINDOW), 0.0, NEG)
    wbias = np.stack([band(qi - kk), band(qi + WINDOW - kk)])
    return (jnp.asarray(ovt, BF16), jnp.asarray(gexp, BF16),
            jnp.asarray(dbias, F32), jnp.asarray(wbias, F32))


def _pad_heads(w, n_heads, width):
    k = w.shape[0]
    w = w.reshape(k, n_heads, width)
    return jnp.pad(w, ((0, 0), (0, 0), (0, HEAD_PAD - width))).reshape(k, n_heads * HEAD_PAD)


def _block_diag(w, n):
    k, m = w.shape
    eye = jnp.eye(n, dtype=w.dtype)
    return (eye[:, None, :, None] * w[None, :, None, :]).reshape(n * k, n * m)


def _cmp_w1_pairs(w1):
    per_tok = w1.reshape(CMP_LEN, NSA_HEAD, w1.shape[1])
    bd = jax.vmap(lambda w: _block_diag(w, NSA_KV_GROUPS))(per_tok)
    return bd.reshape(CMP_LEN // 2, 2 * KV_W, NSA_KV_GROUPS * w1.shape[1])


def _pad_row(g, lo=0):
    return jnp.pad(g, (lo, HEAD_PAD - lo - g.shape[0])).reshape(1, HEAD_PAD)


def _layer(x, mod, cs, p):
    B, S, D = x.shape
    w_in = p["w_in"]
    o = 0
    cols = {}
    for name, wdt in (("cq", MLA_Q_LORA), ("ckv", MLA_KV_LORA), ("kpe", MLA_ROPE),
                      ("qn", NSA_HEADS * NSA_HEAD), ("kc", KV_W), ("vc", KV_W), ("ks", KV_W),
                      ("vs", KV_W), ("kw", KV_W), ("vw", KV_W),
                      ("gn", NSA_HEADS * N_NSA_BRANCH), ("gm", 2 * D)):
        cols[name] = w_in[:, o:o + wdt]
        o += wdt
    G = NSA_KV_GROUPS
    n_gate = NSA_HEADS * N_NSA_BRANCH
    zc = lambda n: jnp.zeros((D, n), F32)
    wsm = jnp.concatenate([cols["gn"], zc(MLA_NOPE - n_gate), cols["kpe"], zc(LANES - MLA_QK)], axis=1)
    wkv6 = jnp.concatenate([cols["kc"], cols["vc"]] +
                           [_pad_heads(cols[k], G, NSA_HEAD) for k in ("ks", "vs", "kw", "vw")], axis=1)
    wkvb = p["mla_w_kv_b"].reshape(MLA_KV_LORA, MLA_HEADS, MLA_NOPE + MLA_V)
    wkvb = jnp.concatenate([_pad_heads(wkvb[:, :, :MLA_NOPE].reshape(MLA_KV_LORA, -1), MLA_HEADS, MLA_NOPE),
                            _pad_heads(wkvb[:, :, MLA_NOPE:].reshape(MLA_KV_LORA, -1), MLA_HEADS, MLA_V)], axis=1)
    bf = lambda w: w.astype(BF16)
    row = lambda g: g.reshape(1, -1)
    weights = tuple(bf(w) for w in (cols["cq"], cols["ckv"], wsm, _pad_heads(cols["qn"], NSA_HEADS, NSA_HEAD),
                                    wkv6, cols["gm"],
                                    _pad_heads(p["mla_w_q_b"], MLA_HEADS, MLA_QK), wkvb))
    vone = jnp.tile(jnp.zeros((1, HEAD_PAD), F32).at[0, MLA_V].set(1.0), (1, MLA_HEADS))
    rows = (row(p["mla_q_a_gain"]), row(p["mla_kv_a_gain"]),
            _pad_row(p["mla_q_gain"]), _pad_row(p["mla_k_gain"][:MLA_NOPE]),
            _pad_row(p["mla_k_gain"][MLA_NOPE:], MLA_NOPE),
            _pad_row(p["nsa_q_gain"]), _pad_row(p["nsa_ks_gain"]), _pad_row(p["nsa_kw_gain"]), vone)
    texp, trow = _rope_expansion()
    (qm, km, vm, qn, ks, kw, vs, vw, kcin, vcin, gn, gm) = _inproj(
        x, mod, cs, (row(p["norm1_gain"]), texp, trow), weights, rows)

    n_chunk = S // CMP_STRIDE
    cs_end = cs[:, CMP_LEN - 1::CMP_STRIDE]
    cs_end = jnp.pad(cs_end, ((0, 0), (0, n_chunk - cs_end.shape[1]), (0, 0)))
    w2k = jnp.pad(p["cmp_w2_k"], ((0, 0), (0, HEAD_PAD - NSA_HEAD)))
    kc, vc = _compress(kcin, vcin, jnp.tile(p["cmp_pos_k"], (1, G)), jnp.tile(p["cmp_pos_v"], (1, G)),
                       bf(_cmp_w1_pairs(p["cmp_w1_k"])), bf(_block_diag(w2k, G)),
                       bf(_cmp_w1_pairs(p["cmp_w1_v"])), bf(_block_diag(p["cmp_w2_v"], G)),
                       _pad_row(p["nsa_kc_gain"]), cs_end, texp, trow)

    ovt, gexp, dbias, wbias = _mask_tables(S)
    o_nsa = _nsa_attention(qn, kc, vc, ks, vs, kw, vw, gn, ovt, gexp, dbias, wbias)
    o_mla = _mla_attention(qm, km, vm, dbias)

    return _out_ffn(x, o_mla, o_nsa, gm, mod, row(p["norm2_gain"]),
                    bf(p["w_o_mla"]), bf(p["w_o_nsa"]), bf(p["w_out"]),
                    bf(p["ffn_w_gate"]), bf(p["ffn_w_up"]), bf(p["ffn_w_down"]))


def kernel(x, c, positions, ada_w, ada_b, norm1_gain, w_in, mla_q_a_gain, mla_w_q_b, mla_kv_a_gain, mla_w_kv_b, mla_q_gain, mla_k_gain, nsa_q_gain, nsa_kc_gain, nsa_ks_gain, nsa_kw_gain, cmp_pos_k, cmp_w1_k, cmp_w2_k, cmp_pos_v, cmp_w1_v, cmp_w2_v, w_o_mla, w_o_nsa, w_out, norm2_gain, ffn_w_gate, ffn_w_up, ffn_w_down):
    params = dict(norm1_gain=norm1_gain, w_in=w_in, mla_q_a_gain=mla_q_a_gain, mla_w_q_b=mla_w_q_b,
                  mla_kv_a_gain=mla_kv_a_gain, mla_w_kv_b=mla_w_kv_b, mla_q_gain=mla_q_gain,
                  mla_k_gain=mla_k_gain, nsa_q_gain=nsa_q_gain, nsa_kc_gain=nsa_kc_gain,
                  nsa_ks_gain=nsa_ks_gain, nsa_kw_gain=nsa_kw_gain, cmp_pos_k=cmp_pos_k,
                  cmp_w1_k=cmp_w1_k, cmp_w2_k=cmp_w2_k, cmp_pos_v=cmp_pos_v, cmp_w1_v=cmp_w1_v,
                  cmp_w2_v=cmp_w2_v, w_o_mla=w_o_mla, w_o_nsa=w_o_nsa, w_out=w_out,
                  norm2_gain=norm2_gain, ffn_w_gate=ffn_w_gate, ffn_w_up=ffn_w_up, ffn_w_down=ffn_w_down)
    B = x.shape[0]
    inv_m = ROPE_THETA ** (-jnp.arange(0, MLA_ROPE, 2, dtype=F32) / MLA_ROPE)
    inv_n = ROPE_THETA ** (-jnp.arange(0, NSA_ROT, 2, dtype=F32) / NSA_ROT)
    n_unused = N_FREQ - inv_m.shape[0] - inv_n.shape[0]
    cs = _rope_tables(positions, jnp.concatenate([inv_m, inv_n, jnp.zeros((n_unused,), F32)]))
    depth = ada_w.shape[0]
    for l in range(depth):
        mod = _ada(c, ada_w, ada_b, l).reshape(B, N_MOD, D_MODEL)
        x = _layer(x, mod, cs, {k: v[l] for k, v in params.items()})
    return x
```

```python
import numpy as np
import jax
import jax.numpy as jnp
from jax import lax
from jax.experimental import pallas as pl
from jax.experimental.pallas import tpu as pltpu

F32 = jnp.float32
BF16 = jnp.bfloat16

D_MODEL = 1024
ROPE_THETA = 500000.0
EPS = 1e-6
NEG = -1e30
LOG2E = 1.4426950408889634

MLA_HEADS = 8
MLA_NOPE = 64
MLA_ROPE = 32
MLA_QK = MLA_NOPE + MLA_ROPE
MLA_V = 64
MLA_Q_LORA = 768
MLA_KV_LORA = 256

NSA_HEADS = 8
NSA_KV_GROUPS = 2
NSA_REP = NSA_HEADS // NSA_KV_GROUPS
NSA_HEAD = 64
NSA_ROT = NSA_HEAD // 4
CMP_LEN = 32
CMP_STRIDE = 16
CMP_HIDDEN = 256
SEL_LEN = 64
SEL_TOP = 8
WINDOW = 256
N_NSA_BRANCH = 3
FORCE_BONUS = 1e4
KV_W = NSA_KV_GROUPS * NSA_HEAD

D_FF = -(-8 * D_MODEL // (3 * 256)) * 256
N_MOD = 6
LANES = 128
HEAD_PAD = LANES
N_FREQ = 32

TM_IN = 512
IN_ROWS = 256
TQ_ATT = 256
TM_OUT = 512
OUT_ROWS = 256
MXU_TILE = 256
FF_SPLITS = (0, 6 * MXU_TILE, D_FF)
ATT_LOOKAHEAD = 2
VMEM_LIMIT = 56 * 1024 * 1024


def _dot(a, b):
    return jnp.dot(a, b, preferred_element_type=F32)


def _dot_nt(a, b):
    return lax.dot_general(a, b, (((1,), (1,)), ((), ())), preferred_element_type=F32)


def _split_hilo(a):
    hi = a.astype(BF16)
    return hi, (a - hi.astype(F32)).astype(BF16)


def _dot_hilo(a, m):
    hi, lo = _split_hilo(a)
    return _dot(hi, m) + _dot(lo, m)


def _sigmoid(v):
    return 1.0 / (1.0 + jnp.exp(-v))


def _rms(v, n):
    return v * lax.rsqrt(jnp.sum(v * v, axis=-1, keepdims=True) * (1.0 / n) + EPS)


def _rope(v, cos_v, sin_v, lo, half):
    lane = lax.broadcasted_iota(jnp.int32, v.shape, 1)
    is_x1 = (lane >= lo) & (lane < lo + half)
    rot = jnp.where(is_x1, pltpu.roll(v, LANES - half, 1), pltpu.roll(v, half, 1))
    return v * cos_v + rot * sin_v


def _rope_multipliers(cs, texp_ref, trow_ref):
    tabs = _dot_hilo(cs, texp_ref[...]) + trow_ref[...]
    return tuple(tabs[:, LANES * i:LANES * (i + 1)] for i in range(4))


def _const_spec(shape, single_buffer=False):
    nd = len(shape)
    mode = {"pipeline_mode": pl.Buffered(1)} if single_buffer else {}
    return pl.BlockSpec(shape, lambda *_: (0,) * nd, **mode)


def _rowmax(s):
    return jnp.max(s, axis=-1, keepdims=True)


def _attention_scores(q, k_ref, kmax, dbias):
    k0 = kmax - dbias.shape[1]
    s_d = _dot_nt(q, k_ref(k0, kmax)) + dbias
    m = _rowmax(s_d)
    s_m = None
    if k0 > 0:
        s_m = _dot_nt(q, k_ref(0, k0))
        m = jnp.maximum(m, _rowmax(s_m))
    return s_m, s_d, m


def _attention_values(scores, v_ref, kmax):
    s_m, s_d, m = scores
    k0 = kmax - s_d.shape[1]
    acc = _dot(jnp.exp2(s_d - m).astype(BF16), v_ref(k0, kmax))
    if s_m is not None:
        acc = acc + _dot(jnp.exp2(s_m - m).astype(BF16), v_ref(0, k0))
    return acc[:, :NSA_HEAD] / acc[:, NSA_HEAD:NSA_HEAD + 1]


def _rope_kernel(pos_ref, inv_ref, cs_ref):
    ang = pos_ref[0].astype(F32) * inv_ref[...]
    nf, S = ang.shape
    rows = jnp.concatenate([jnp.cos(ang), jnp.sin(ang), jnp.zeros((LANES - 2 * nf, S), F32)], axis=0)
    cs_ref[0] = rows.T


def _rope_tables(positions, inv):
    B, S = positions.shape
    nf = inv.shape[0]
    return pl.pallas_call(
        _rope_kernel,
        grid=(B,),
        in_specs=[pl.BlockSpec((1, 1, S), lambda b: (b, 0, 0)),
                  _const_spec((nf, 1))],
        out_specs=pl.BlockSpec((1, S, LANES), lambda b: (b, 0, 0)),
        out_shape=jax.ShapeDtypeStruct((B, S, LANES), F32),
        name="rope_tables",
    )(positions.reshape(B, 1, S), inv.reshape(nf, 1))


def _ada_kernel(c_ref, w_ref, b_ref, o_ref):
    c = c_ref[...]
    sc = c * _sigmoid(c)
    o_ref[...] = jnp.dot(sc, w_ref[0], preferred_element_type=F32,
                         precision=lax.Precision.HIGHEST) + b_ref[0]


def _ada(c, w, b, layer):
    B, D = c.shape
    N = w.shape[2]
    tn = D_MODEL
    return pl.pallas_call(
        _ada_kernel,
        grid=(N // tn,),
        in_specs=[_const_spec((B, D)),
                  pl.BlockSpec((1, D, tn), lambda j: (layer, 0, j)),
                  pl.BlockSpec((1, 1, tn), lambda j: (layer, 0, j))],
        out_specs=pl.BlockSpec((B, tn), lambda j: (0, j)),
        out_shape=jax.ShapeDtypeStruct((B, N), F32),
        name="ada_mod",
    )(c, w, b.reshape(b.shape[0], 1, N))


def _inproj_kernel(x_ref, mod_ref, cs_ref, g1_ref, texp_ref, trow_ref,
                   wcq_ref, wckv_ref, wsm_ref, wqn_ref, wkv6_ref, wgm_ref,
                   qag_ref, wqb_ref, kvag_ref, wkvb_ref,
                   mqg_ref, mkn_ref, mkr_ref, nqg_ref, nksg_ref, nkwg_ref, vone_ref,
                   qm_ref, km_ref, vm_ref, qn_ref, ks_ref, kw_ref, vs_ref, vw_ref,
                   kcin_ref, vcin_ref, gn_ref, gm_ref):
    tm = x_ref.shape[1]
    mod = mod_ref[0]
    sh1, sc1 = mod[0:1], mod[1:2]
    lane = lax.broadcasted_iota(jnp.int32, (IN_ROWS, LANES), 1)
    blk = lambda a, i: a[:, HEAD_PAD * i:HEAD_PAD * (i + 1)]
    hm = MLA_ROPE // 2
    hn = NSA_ROT // 2

    def front(rows):
        h = _rms(x_ref[0, rows], D_MODEL) * g1_ref[...] * (1.0 + sc1) + sh1
        hb = h.astype(BF16)
        return dict(
            hb=hb,
            tabs=_rope_multipliers(cs_ref[0, rows], texp_ref, trow_ref),
            cq=_dot(hb, wcq_ref[...]),
            ckv=_dot(hb, wckv_ref[...]),
            zs=_dot(hb, wsm_ref[...]),
            qn=_dot(hb, wqn_ref[...]),
            kv6=_dot(hb, wkv6_ref[...]))

    def middle(st):
        cqn = (_rms(st["cq"], MLA_Q_LORA) * qag_ref[...]).astype(BF16)
        st["q"] = _dot(cqn, wqb_ref[...])
        st["gates"] = _dot(st["hb"], wgm_ref[...])
        ckvn = (_rms(st["ckv"], MLA_KV_LORA) * kvag_ref[...]).astype(BF16)
        st["kv"] = _dot(ckvn, wkvb_ref[...])

    def back(rows, st):
        cos_m, sin_m, cos_n, sin_n = st["tabs"]
        zs, qn, kv6, q, kv = st["zs"], st["qn"], st["kv6"], st["q"], st["kv"]

        nqg = nqg_ref[...]
        n_scale = NSA_HEAD ** -0.5 * LOG2E
        for hd in range(NSA_HEADS):
            qh = _rope(_rms(blk(qn, hd), NSA_HEAD) * nqg, cos_n, sin_n, 0, hn) * n_scale
            qn_ref[0, rows, HEAD_PAD * hd:HEAD_PAD * (hd + 1)] = qh.astype(BF16)

        tok = pl.program_id(1) * tm + rows.start + lax.broadcasted_iota(jnp.int32, (IN_ROWS, 1), 0)
        sblk = lax.shift_right_logical(tok, SEL_LEN.bit_length() - 1)
        ind = jnp.where(lane - NSA_HEAD == sblk, NEG, 0.0)
        vone = vone_ref[:, 0:HEAD_PAD]
        nksg, nkwg = nksg_ref[...], nkwg_ref[...]
        kcin_ref[0, rows] = blk(kv6, 0)
        vcin_ref[0, rows] = blk(kv6, 1)
        for g in range(NSA_KV_GROUPS):
            ks = _rope(_rms(blk(kv6, 2 + g), NSA_HEAD) * nksg, cos_n, sin_n, 0, hn)
            ks_ref[0, g, rows] = (ks + ind).astype(BF16)
            vs_ref[0, g, rows] = (blk(kv6, 4 + g) + vone).astype(BF16)
            kw = _rope(_rms(blk(kv6, 6 + g), NSA_HEAD) * nkwg, cos_n, sin_n, 0, hn)
            kw_ref[0, g, rows] = kw.astype(BF16)
            vw_ref[0, g, rows] = (blk(kv6, 8 + g) + vone).astype(BF16)

        mqg = mqg_ref[...]
        m_scale = MLA_QK ** -0.5 * LOG2E
        for hd in range(MLA_HEADS):
            qh = _rope(_rms(blk(q, hd), MLA_QK) * mqg, cos_m, sin_m, MLA_NOPE, hm) * m_scale
            qm_ref[0, rows, HEAD_PAD * hd:HEAD_PAD * (hd + 1)] = qh.astype(BF16)

        gn_ref[0, rows] = _sigmoid(zs)
        gm_ref[0, rows] = _sigmoid(st["gates"]).astype(BF16)

        kpe = jnp.where((lane >= MLA_NOPE) & (lane < MLA_QK), zs, 0.0)
        kpe_ss = jnp.sum(kpe * kpe, axis=-1, keepdims=True)
        kr = _rope(kpe * mkr_ref[...], cos_m, sin_m, MLA_NOPE, hm)
        mkn = mkn_ref[...]
        for hd in range(MLA_HEADS):
            kn = blk(kv, hd)
            inv = lax.rsqrt((jnp.sum(kn * kn, axis=-1, keepdims=True) + kpe_ss) * (1.0 / MLA_QK) + EPS)
            km_ref[0, rows, HEAD_PAD * hd:HEAD_PAD * (hd + 1)] = ((kn * mkn + kr) * inv).astype(BF16)
        vm_ref[0, rows] = (kv[:, MLA_HEADS * HEAD_PAD:] + vone_ref[...]).astype(BF16)

    groups = [slice(s, s + IN_ROWS) for s in range(0, tm, IN_ROWS)]
    states = [front(rows) for rows in groups]
    for st in states:
        middle(st)
    for rows, st in zip(groups, states):
        back(rows, st)


def _inproj(x, mod, cs, consts, weights, rows):
    B, S, D = x.shape
    tm = TM_IN
    tok = lambda w: pl.BlockSpec((1, tm, w), lambda b, i: (b, i, 0))
    head = lambda n, w: pl.BlockSpec((1, n, tm, w), lambda b, i: (b, 0, i, 0))
    operands = list(consts) + list(weights[:6]) + [rows[0], weights[6], rows[1], weights[7]] + list(rows[2:])
    in_specs = [tok(D), pl.BlockSpec((1, N_MOD, D), lambda b, i: (b, 0, 0)), tok(LANES)]
    in_specs += [_const_spec(a.shape, single_buffer=True) for a in operands]
    G = NSA_KV_GROUPS
    sds = jax.ShapeDtypeStruct
    wide = MLA_HEADS * HEAD_PAD
    outs = [
        (tok(wide), sds((B, S, wide), BF16)),
        (tok(wide), sds((B, S, wide), BF16)),
        (tok(wide), sds((B, S, wide), BF16)),
        (tok(wide), sds((B, S, wide), BF16)),
        (head(G, HEAD_PAD), sds((B, G, S, HEAD_PAD), BF16)),
        (head(G, HEAD_PAD), sds((B, G, S, HEAD_PAD), BF16)),
        (head(G, HEAD_PAD), sds((B, G, S, HEAD_PAD), BF16)),
        (head(G, HEAD_PAD), sds((B, G, S, HEAD_PAD), BF16)),
        (tok(KV_W), sds((B, S, KV_W), F32)),
        (tok(KV_W), sds((B, S, KV_W), F32)),
        (tok(LANES), sds((B, S, LANES), F32)),
        (tok(2 * D), sds((B, S, 2 * D), BF16)),
    ]
    return pl.pallas_call(
        _inproj_kernel,
        grid=(B, S // tm),
        in_specs=in_specs,
        out_specs=[o[0] for o in outs],
        out_shape=[o[1] for o in outs],
        compiler_params=pltpu.CompilerParams(dimension_semantics=("arbitrary", "arbitrary"),
                                             vmem_limit_bytes=VMEM_LIMIT),
        name="inproj_prep",
    )(x, mod, cs, *operands)


def _compress_kernel(kcin_ref, vcin_ref, pk_ref, pv_ref, w1k_ref, w2k_ref, w1v_ref, w2v_ref,
                     kcg_ref, cs_ref, texp_ref, trow_ref, kc_ref, vc_ref):
    n = kcin_ref.shape[1] // CMP_STRIDE

    def hidden(cin_ref, pos_ref, w1_ref):
        a = b = None
        for l in range(0, CMP_STRIDE, 2):
            t0 = cin_ref[0, pl.ds(l, n, stride=CMP_STRIDE), :]
            t1 = cin_ref[0, pl.ds(l + 1, n, stride=CMP_STRIDE), :]
            pair = lambda o: jnp.concatenate([t0 + pos_ref[o + l:o + l + 1],
                                              t1 + pos_ref[o + l + 1:o + l + 2]], axis=-1).astype(BF16)
            da = _dot(pair(0), w1_ref[l // 2])
            db = _dot(pair(CMP_STRIDE), w1_ref[(CMP_STRIDE + l) // 2])
            a, b = (da, db) if a is None else (a + da, b + db)
        hid = a + pltpu.roll(b, n - 1, 0)
        return (hid * _sigmoid(hid)).astype(BF16)

    kc = _dot(hidden(kcin_ref, pk_ref, w1k_ref), w2k_ref[...])
    _, _, cos_n, sin_n = _rope_multipliers(cs_ref[0], texp_ref, trow_ref)
    vc = _dot(hidden(vcin_ref, pv_ref, w1v_ref), w2v_ref[...])
    for g in range(NSA_KV_GROUPS):
        kg = _rms(kc[:, HEAD_PAD * g:HEAD_PAD * (g + 1)], NSA_HEAD) * kcg_ref[...]
        kc_ref[0, g] = _rope(kg, cos_n, sin_n, 0, NSA_ROT // 2).astype(BF16)
        vc_ref[0, g] = vc[:, NSA_HEAD * g:NSA_HEAD * (g + 1)].astype(BF16)


def _compress(kcin, vcin, pk, pv, w1k, w2k, w1v, w2v, kcg, cs_end, texp, trow):
    B, S, w = kcin.shape
    G = NSA_KV_GROUPS
    n = S // CMP_STRIDE
    oblk = lambda wd: pl.BlockSpec((1, G, n, wd), lambda b: (b, 0, 0, 0))
    consts = (pk, pv, w1k, w2k, w1v, w2v, kcg)
    return pl.pallas_call(
        _compress_kernel,
        grid=(B,),
        in_specs=[pl.BlockSpec((1, S, w), lambda b: (b, 0, 0))] * 2 + [_const_spec(a.shape) for a in consts] +
                 [pl.BlockSpec((1, n, LANES), lambda b: (b, 0, 0)),
                  _const_spec(texp.shape), _const_spec(trow.shape)],
        out_specs=[oblk(HEAD_PAD), oblk(NSA_HEAD)],
        out_shape=[jax.ShapeDtypeStruct((B, G, n, HEAD_PAD), BF16),
                   jax.ShapeDtypeStruct((B, G, n, NSA_HEAD), BF16)],
        name="nsa_compress",
    )(kcin, vcin, *consts, cs_end, texp, trow)


def _nsa_kernel(q_ref, kc_ref, vc_ref, ks_ref, vs_ref, kw_ref, vw_ref, gn_ref,
                ovt_ref, gexp_ref, dbias_ref, wbias_ref, o_ref, imp_ref):
    tq = dbias_ref.shape[0]
    S = q_ref.shape[1]
    R = NSA_REP
    M = R * tq
    n_sel = imp_ref.shape[0]
    ncp = kc_ref.shape[2]
    span = WINDOW + tq
    grp = pl.program_id(1)
    dbias = dbias_ref[...]
    kf = lambda a, b: ks_ref[0, 0, a:b, :]
    vf = lambda a, b: vs_ref[0, 0, a:b, :]
    row = lax.broadcasted_iota(jnp.int32, (M, 1), 0)
    n_idx = lax.broadcasted_iota(jnp.int32, (M, ncp), 1)
    j = lax.broadcasted_iota(jnp.int32, (n_sel, tq), 0)
    head_q = lambda i, r: q_ref[0, i * tq:(i + 1) * tq, HEAD_PAD * r:HEAD_PAD * (r + 1)]
    tile_q = lambda i: jnp.concatenate([head_q(i, r) for r in range(R)], axis=0)
    tiles = {}

    def compressed_and_select(i, s):
        q0 = i * tq
        t = q0 + jnp.bitwise_and(row, tq - 1)
        valid = (n_idx * CMP_STRIDE + (CMP_LEN - 1)) <= t
        sm = jnp.where(valid, s, NEG)
        e = jnp.where(valid, jnp.exp2(sm - _rowmax(sm)), 0.0)
        den = jnp.sum(e, axis=-1, keepdims=True)
        p_c = e / jnp.where(den > 0.0, den, 1.0)
        o_c = _dot(p_c.astype(BF16), vc_ref[0, 0])
        psum = p_c[0:tq]
        for r in range(1, R):
            psum = psum + p_c[r * tq:(r + 1) * tq]
        hi, lo = _split_hilo(psum.T)
        imp = (_dot(ovt_ref[...], hi) + _dot(ovt_ref[...], lo))[0:n_sel]
        cur = lax.shift_right_logical(q0 + lax.broadcasted_iota(jnp.int32, (1, tq), 1),
                                      SEL_LEN.bit_length() - 1)
        forced = (j == 0) | (j == cur) | (j == cur - 1)
        imp = jnp.where(forced, imp + FORCE_BONUS, imp)
        imp = jnp.where(j <= cur, imp, NEG)
        imp_ref[...] = imp
        cnt = jnp.zeros((n_sel, tq), F32)
        for jj in range(n_sel):
            other = imp_ref[jj:jj + 1, :]
            beats = (other > imp) | ((other == imp) & (j > jj))
            cnt = cnt + jnp.where(beats, 1.0, 0.0)
        nsel = jnp.where((cnt < float(SEL_TOP)) & (j <= cur), 0.0, 1.0)
        nsel = jnp.concatenate([jnp.zeros((NSA_HEAD, tq), F32), nsel,
                                jnp.zeros((LANES - NSA_HEAD - n_sel, tq), F32)], axis=0).T.astype(BF16)
        tiles[i] = dict(o_c=o_c, nsel=nsel, o_s=[], o_w=[])

    def scores(u):
        kind, i, r = u
        if kind == "cmp":
            return _dot_nt(tile_q(i), kc_ref[0, 0])
        if kind == "sel":
            return _attention_scores(head_q(i, r) + tiles[i]["nsel"], kf, (i + 1) * tq, dbias)
        w0 = max(i * tq - WINDOW, 0)
        wb = wbias_ref[min(i, 1)]
        sw = _dot_nt(head_q(i, r), kw_ref[0, 0, w0:w0 + span, :]) + wb
        return sw, _rowmax(sw)

    def values(u, sc):
        kind, i, r = u
        if kind == "cmp":
            compressed_and_select(i, sc)
        elif kind == "sel":
            tiles[i]["o_s"].append(_attention_values(sc, vf, (i + 1) * tq))
        elif kind == "win":
            sw, mw = sc
            w0 = max(i * tq - WINDOW, 0)
            acc_w = _dot(jnp.exp2(sw - mw).astype(BF16), vw_ref[0, 0, w0:w0 + span, :])
            tiles[i]["o_w"].append(acc_w[:, :NSA_HEAD] / acc_w[:, NSA_HEAD:NSA_HEAD + 1])
        if kind == "win" and r == R - 1:
            tile = tiles.pop(i)
            g_hi, g_lo = _split_hilo(gn_ref[0, i * tq:(i + 1) * tq, :])
            o_c = jnp.concatenate([tile["o_c"][r * tq:(r + 1) * tq] for r in range(R)], axis=-1)
            branches = (o_c, jnp.concatenate(tile["o_s"], axis=-1), jnp.concatenate(tile["o_w"], axis=-1))
            out = None
            for br, o_b in enumerate(branches):
                gate = _dot(g_hi, gexp_ref[grp, br]) + _dot(g_lo, gexp_ref[grp, br])
                out = gate * o_b if out is None else out + gate * o_b
            o_ref[0, i * tq:(i + 1) * tq, :] = out.astype(BF16)

    nq = S // tq
    first = ("cmp", 0, 0)
    values(first, scores(first))
    units = []
    for i in range(nq):
        units += [("cmp", i + 1, 0)] if i + 1 < nq else []
        for r in range(R):
            units += [("sel", i, r), ("win", i, r)]
    pending = [scores(u) for u in units[:ATT_LOOKAHEAD]]
    for n, u in enumerate(units):
        if n + ATT_LOOKAHEAD < len(units):
            pending.append(scores(units[n + ATT_LOOKAHEAD]))
        values(u, pending.pop(0))


def _nsa_attention(qn, kc, vc, ks, vs, kw, vw, gn, ovt, gexp, dbias, wbias):
    B, S, _ = qn.shape
    G, R, Dh = NSA_KV_GROUPS, NSA_REP, NSA_HEAD
    ncp = kc.shape[2]
    full = pl.BlockSpec((1, 1, S, HEAD_PAD), lambda b, g: (b, g, 0, 0))
    cmp_spec = lambda w: pl.BlockSpec((1, 1, ncp, w), lambda b, g: (b, g, 0, 0))
    return pl.pallas_call(
        _nsa_kernel,
        grid=(B, G),
        in_specs=[pl.BlockSpec((1, S, R * HEAD_PAD), lambda b, g: (b, 0, g)),
                  cmp_spec(HEAD_PAD), cmp_spec(Dh), full, full, full, full,
                  pl.BlockSpec((1, S, LANES), lambda b, g: (b, 0, 0)),
                  _const_spec(ovt.shape), _const_spec(gexp.shape),
                  _const_spec(dbias.shape), _const_spec(wbias.shape)],
        out_specs=pl.BlockSpec((1, S, R * Dh), lambda b, g: (b, 0, g)),
        out_shape=jax.ShapeDtypeStruct((B, S, G * R * Dh), BF16),
        scratch_shapes=[pltpu.VMEM((S // SEL_LEN, TQ_ATT), F32)],
        compiler_params=pltpu.CompilerParams(dimension_semantics=("arbitrary",) * 2,
                                             vmem_limit_bytes=VMEM_LIMIT),
        name="nsa_attention",
    )(qn, kc, vc, ks, vs, kw, vw, gn, ovt, gexp, dbias, wbias)


def _mla_kernel(q_ref, k_ref, v_ref, dbias_ref, o_ref):
    tq = dbias_ref.shape[0]
    S = q_ref.shape[1]
    dbias = dbias_ref[...]
    units = [(i, hh) for i in range(S // tq) for hh in range(2)]

    def scores(u):
        i, hh = u
        cols = slice(HEAD_PAD * hh, HEAD_PAD * (hh + 1))
        q = q_ref[0, i * tq:(i + 1) * tq, cols]
        return _attention_scores(q, lambda a, b: k_ref[0, a:b, cols], (i + 1) * tq, dbias)

    def values(u, sc):
        i, hh = u
        cols = slice(HEAD_PAD * hh, HEAD_PAD * (hh + 1))
        o = _attention_values(sc, lambda a, b: v_ref[0, a:b, cols], (i + 1) * tq)
        o_ref[0, i * tq:(i + 1) * tq, MLA_V * hh:MLA_V * (hh + 1)] = o.astype(BF16)

    pending = [scores(u) for u in units[:ATT_LOOKAHEAD]]
    for n, u in enumerate(units):
        if n + ATT_LOOKAHEAD < len(units):
            pending.append(scores(units[n + ATT_LOOKAHEAD]))
        values(u, pending.pop(0))


def _mla_attention(qm, km, vm, dbias):
    B, S, _ = qm.shape
    pair = pl.BlockSpec((1, S, 2 * HEAD_PAD), lambda b, h: (b, 0, h))
    return pl.pallas_call(
        _mla_kernel,
        grid=(B, MLA_HEADS // 2),
        in_specs=[pair, pair, pair, _const_spec(dbias.shape)],
        out_specs=pl.BlockSpec((1, S, 2 * MLA_V), lambda b, h: (b, 0, h)),
        out_shape=jax.ShapeDtypeStruct((B, S, MLA_HEADS * MLA_V), BF16),
        compiler_params=pltpu.CompilerParams(dimension_semantics=("arbitrary",) * 2,
                                             vmem_limit_bytes=VMEM_LIMIT),
        name="mla_attention",
    )(qm, km, vm, dbias)


def _out_ffn_kernel(x_ref, om_ref, on_ref, gm_ref, mod_ref, g2_ref,
                    wom_ref, won_ref, wout_ref, wg_ref, wu_ref, wd_ref, o_ref):
    mod = mod_ref[0]
    gt1, sh2, sc2, gt2 = mod[2:3], mod[3:4], mod[4:5], mod[5:6]
    groups = [slice(s, s + OUT_ROWS) for s in range(0, x_ref.shape[1], OUT_ROWS)]
    chunks = [slice(lo, hi) for lo, hi in zip(FF_SPLITS[:-1], FF_SPLITS[1:])]
    heads = [(_dot(om_ref[0, rows], wom_ref[...]), _dot(on_ref[0, rows], won_ref[...])) for rows in groups]
    x1s = []
    for rows, (ym, yn) in zip(groups, heads):
        merged = gm_ref[0, rows, :D_MODEL] * ym + gm_ref[0, rows, D_MODEL:] * yn
        x1s.append(x_ref[0, rows] + gt1 * _dot(merged.astype(BF16), wout_ref[...]))
    gus = []
    for x1 in x1s:
        h2 = (_rms(x1, D_MODEL) * g2_ref[...] * (1.0 + sc2) + sh2).astype(BF16)
        gus.append([(_dot(h2, wg_ref[:, sl]), _dot(h2, wu_ref[:, sl])) for sl in chunks])
    for rows, x1, gu in zip(groups, x1s, gus):
        acc = None
        for sl, (g, u) in zip(chunks, gu):
            d = _dot((g * _sigmoid(g) * u).astype(BF16), wd_ref[sl, :])
            acc = d if acc is None else acc + d
        o_ref[0, rows] = x1 + gt2 * acc


def _out_ffn(x, om, on, gm, mod, g2, wom, won, wout, wg, wu, wd):
    B, S, D = x.shape
    tm = TM_OUT
    tok = lambda w: pl.BlockSpec((1, tm, w), lambda b, i: (b, i, 0))
    wspec = lambda w: pl.BlockSpec(w.shape, lambda b, i: (0, 0), pipeline_mode=pl.Buffered(1))
    return pl.pallas_call(
        _out_ffn_kernel,
        grid=(B, S // tm),
        in_specs=[tok(D), tok(om.shape[2]), tok(on.shape[2]), tok(2 * D),
                  pl.BlockSpec((1, N_MOD, D), lambda b, i: (b, 0, 0)),
                  _const_spec(g2.shape)] + [wspec(w) for w in (wom, won, wout, wg, wu, wd)],
        out_specs=tok(D),
        out_shape=jax.ShapeDtypeStruct((B, S, D), F32),
        compiler_params=pltpu.CompilerParams(dimension_semantics=("arbitrary", "arbitrary"),
                                             vmem_limit_bytes=VMEM_LIMIT),
        name="out_ffn",
    )(x, om, on, gm, mod, g2, wom, won, wout, wg, wu, wd)


def _rope_expansion():
    texp = np.zeros((LANES, 4 * LANES), np.float32)
    trow = np.zeros((1, 4 * LANES), np.float32)
    hm, hn = MLA_ROPE // 2, NSA_ROT // 2
    trow[0, 0:LANES] = 1.0
    trow[0, 2 * LANES:3 * LANES] = 1.0
    for i in range(hm):
        for off, sgn in ((MLA_NOPE + i, -1.0), (MLA_NOPE + hm + i, 1.0)):
            texp[i, off] = 1.0
            trow[0, off] = 0.0
            texp[N_FREQ + i, LANES + off] = sgn
    for i in range(hn):
        for off, sgn in ((i, -1.0), (hn + i, 1.0)):
            texp[hm + i, 2 * LANES + off] = 1.0
            trow[0, 2 * LANES + off] = 0.0
            texp[N_FREQ + hm + i, 3 * LANES + off] = sgn
    return jnp.asarray(texp, BF16), jnp.asarray(trow, F32)


def _mask_tables(S):
    tq = TQ_ATT
    n_chunk = S // CMP_STRIDE
    n_sel = S // SEL_LEN
    starts = np.arange(n_chunk) * CMP_STRIDE
    sel_start = np.arange(LANES) * SEL_LEN
    ovt = ((starts[None, :] < sel_start[:, None] + SEL_LEN) &
           (starts[None, :] + CMP_LEN > sel_start[:, None]) &
           (np.arange(n_chunk)[None, :] < n_chunk - 1) &
           (np.arange(LANES)[:, None] < n_sel))
    gcol = np.arange(LANES)[:, None]
    head = np.arange(NSA_REP * NSA_HEAD)[None, :] // NSA_HEAD
    gexp = np.stack([np.stack([gcol == (g * NSA_REP + head) * N_NSA_BRANCH + br
                               for br in range(N_NSA_BRANCH)]) for g in range(NSA_KV_GROUPS)])
    qi = np.arange(tq)[:, None]
    dbias = np.where(np.arange(tq)[None, :] <= qi, 0.0, NEG)
    kk = np.arange(WINDOW + tq)[None, :]
    band = lambda d: np.where((d >= 0) & (d < WINDOW), 0.0, NEG)
    wbias = np.stack([band(qi - kk), band(qi + WINDOW - kk)])
    return (jnp.asarray(ovt, BF16), jnp.asarray(gexp, BF16),
            jnp.asarray(dbias, F32), jnp.asarray(wbias, F32))


def _pad_heads(w, n_heads, width):
    k = w.shape[0]
    w = w.reshape(k, n_heads, width)
    return jnp.pad(w, ((0, 0), (0, 0), (0, HEAD_PAD - width))).reshape(k, n_heads * HEAD_PAD)


def _block_diag(w, n):
    k, m = w.shape
    eye = jnp.eye(n, dtype=w.dtype)
    return (eye[:, None, :, None] * w[None, :, None, :]).reshape(n * k, n * m)


def _cmp_w1_pairs(w1):
    per_tok = w1.reshape(CMP_LEN, NSA_HEAD, w1.shape[1])
    bd = jax.vmap(lambda w: _block_diag(w, NSA_KV_GROUPS))(per_tok)
    return bd.reshape(CMP_LEN // 2, 2 * KV_W, NSA_KV_GROUPS * w1.shape[1])


def _pad_row(g, lo=0):
    return jnp.pad(g, (lo, HEAD_PAD - lo - g.shape[0])).reshape(1, HEAD_PAD)


def _layer(x, mod, cs, p):
    B, S, D = x.shape
    w_in = p["w_in"]
    o = 0
    cols = {}
    for name, wdt in (("cq", MLA_Q_LORA), ("ckv", MLA_KV_LORA), ("kpe", MLA_ROPE),
                      ("qn", NSA_HEADS * NSA_HEAD), ("kc", KV_W), ("vc", KV_W), ("ks", KV_W),
                      ("vs", KV_W), ("kw", KV_W), ("vw", KV_W),
                      ("gn", NSA_HEADS * N_NSA_BRANCH), ("gm", 2 * D)):
        cols[name] = w_in[:, o:o + wdt]
        o += wdt
    G = NSA_KV_GROUPS
    n_gate = NSA_HEADS * N_NSA_BRANCH
    zc = lambda n: jnp.zeros((D, n), F32)
    wsm = jnp.concatenate([cols["gn"], zc(MLA_NOPE - n_gate), cols["kpe"], zc(LANES - MLA_QK)], axis=1)
    wkv6 = jnp.concatenate([cols["kc"], cols["vc"]] +
                           [_pad_heads(cols[k], G, NSA_HEAD) for k in ("ks", "vs", "kw", "vw")], axis=1)
    wkvb = p["mla_w_kv_b"].reshape(MLA_KV_LORA, MLA_HEADS, MLA_NOPE + MLA_V)
    wkvb = jnp.concatenate([_pad_heads(wkvb[:, :, :MLA_NOPE].reshape(MLA_KV_LORA, -1), MLA_HEADS, MLA_NOPE),
                            _pad_heads(wkvb[:, :, MLA_NOPE:].reshape(MLA_KV_LORA, -1), MLA_HEADS, MLA_V)], axis=1)
    bf = lambda w: w.astype(BF16)
    row = lambda g: g.reshape(1, -1)
    weights = tuple(bf(w) for w in (cols["cq"], cols["ckv"], wsm, _pad_heads(cols["qn"], NSA_HEADS, NSA_HEAD),
                                    wkv6, cols["gm"],
                                    _pad_heads(p["mla_w_q_b"], MLA_HEADS, MLA_QK), wkvb))
    vone = jnp.tile(jnp.zeros((1, HEAD_PAD), F32).at[0, MLA_V].set(1.0), (1, MLA_HEADS))
    rows = (row(p["mla_q_a_gain"]), row(p["mla_kv_a_gain"]),
            _pad_row(p["mla_q_gain"]), _pad_row(p["mla_k_gain"][:MLA_NOPE]),
            _pad_row(p["mla_k_gain"][MLA_NOPE:], MLA_NOPE),
            _pad_row(p["nsa_q_gain"]), _pad_row(p["nsa_ks_gain"]), _pad_row(p["nsa_kw_gain"]), vone)
    texp, trow = _rope_expansion()
    (qm, km, vm, qn, ks, kw, vs, vw, kcin, vcin, gn, gm) = _inproj(
        x, mod, cs, (row(p["norm1_gain"]), texp, trow), weights, rows)

    n_chunk = S // CMP_STRIDE
    cs_end = cs[:, CMP_LEN - 1::CMP_STRIDE]
    cs_end = jnp.pad(cs_end, ((0, 0), (0, n_chunk - cs_end.shape[1]), (0, 0)))
    w2k = jnp.pad(p["cmp_w2_k"], ((0, 0), (0, HEAD_PAD - NSA_HEAD)))
    kc, vc = _compress(kcin, vcin, jnp.tile(p["cmp_pos_k"], (1, G)), jnp.tile(p["cmp_pos_v"], (1, G)),
                       bf(_cmp_w1_pairs(p["cmp_w1_k"])), bf(_block_diag(w2k, G)),
                       bf(_cmp_w1_pairs(p["cmp_w1_v"])), bf(_block_diag(p["cmp_w2_v"], G)),
                       _pad_row(p["nsa_kc_gain"]), cs_end, texp, trow)

    ovt, gexp, dbias, wbias = _mask_tables(S)
    o_nsa = _nsa_attention(qn, kc, vc, ks, vs, kw, vw, gn, ovt, gexp, dbias, wbias)
    o_mla = _mla_attention(qm, km, vm, dbias)

    return _out_ffn(x, o_mla, o_nsa, gm, mod, row(p["norm2_gain"]),
                    bf(p["w_o_mla"]), bf(p["w_o_nsa"]), bf(p["w_out"]),
                    bf(p["ffn_w_gate"]), bf(p["ffn_w_up"]), bf(p["ffn_w_down"]))


def kernel(x, c, positions, ada_w, ada_b, norm1_gain, w_in, mla_q_a_gain, mla_w_q_b, mla_kv_a_gain, mla_w_kv_b, mla_q_gain, mla_k_gain, nsa_q_gain, nsa_kc_gain, nsa_ks_gain, nsa_kw_gain, cmp_pos_k, cmp_w1_k, cmp_w2_k, cmp_pos_v, cmp_w1_v, cmp_w2_v, w_o_mla, w_o_nsa, w_out, norm2_gain, ffn_w_gate, ffn_w_up, ffn_w_down):
    params = dict(norm1_gain=norm1_gain, w_in=w_in, mla_q_a_gain=mla_q_a_gain, mla_w_q_b=mla_w_q_b,
                  mla_kv_a_gain=mla_kv_a_gain, mla_w_kv_b=mla_w_kv_b, mla_q_gain=mla_q_gain,
                  mla_k_gain=mla_k_gain, nsa_q_gain=nsa_q_gain, nsa_kc_gain=nsa_kc_gain,
                  nsa_ks_gain=nsa_ks_gain, nsa_kw_gain=nsa_kw_gain, cmp_pos_k=cmp_pos_k,
                  cmp_w1_k=cmp_w1_k, cmp_w2_k=cmp_w2_k, cmp_pos_v=cmp_pos_v, cmp_w1_v=cmp_w1_v,
                  cmp_w2_v=cmp_w2_v, w_o_mla=w_o_mla, w_o_nsa=w_o_nsa, w_out=w_out,
                  norm2_gain=norm2_gain, ffn_w_gate=ffn_w_gate, ffn_w_up=ffn_w_up, ffn_w_down=ffn_w_down)
    B = x.shape[0]
    inv_m = ROPE_THETA ** (-jnp.arange(0, MLA_ROPE, 2, dtype=F32) / MLA_ROPE)
    inv_n = ROPE_THETA ** (-jnp.arange(0, NSA_ROT, 2, dtype=F32) / NSA_ROT)
    n_unused = N_FREQ - inv_m.shape[0] - inv_n.shape[0]
    cs = _rope_tables(positions, jnp.concatenate([inv_m, inv_n, jnp.zeros((n_unused,), F32)]))
    depth = ada_w.shape[0]
    for l in range(depth):
        mod = _ada(c, ada_w, ada_b, l).reshape(B, N_MOD, D_MODEL)
        x = _layer(x, mod, cs, {k: v[l] for k, v in params.items()})
    return x
```

```python
import numpy as np
import jax
import jax.numpy as jnp
from jax import lax
from jax.experimental import pallas as pl
from jax.experimental.pallas import tpu as pltpu

F32 = jnp.float32
BF16 = jnp.bfloat16

D_MODEL = 1024
ROPE_THETA = 500000.0
EPS = 1e-6
NEG = -1e30
LOG2E = 1.4426950408889634

MLA_HEADS = 8
MLA_NOPE = 64
MLA_ROPE = 32
MLA_QK = MLA_NOPE + MLA_ROPE
MLA_V = 64
MLA_Q_LORA = 768
MLA_KV_LORA = 256

NSA_HEADS = 8
NSA_KV_GROUPS = 2
NSA_REP = NSA_HEADS // NSA_KV_GROUPS
NSA_HEAD = 64
NSA_ROT = NSA_HEAD // 4
CMP_LEN = 32
CMP_STRIDE = 16
CMP_HIDDEN = 256
SEL_LEN = 64
SEL_TOP = 8
WINDOW = 256
N_NSA_BRANCH = 3
FORCE_BONUS = 1e4
KV_W = NSA_KV_GROUPS * NSA_HEAD

D_FF = -(-8 * D_MODEL // (3 * 256)) * 256
N_MOD = 6
LANES = 128
HEAD_PAD = LANES
N_FREQ = 32

TM_IN = 512
IN_ROWS = 256
TQ_ATT = 256
TM_OUT = 512
OUT_ROWS = 256
MXU_TILE = 256
CMP_PACK = MXU_TILE // NSA_HEAD
FF_SPLITS = (0, 6 * MXU_TILE, D_FF)
ATT_LOOKAHEAD = 2
VMEM_LIMIT = 56 * 1024 * 1024


def _dot(a, b):
    return jnp.dot(a, b, preferred_element_type=F32)


def _dot_nt(a, b):
    return lax.dot_general(a, b, (((1,), (1,)), ((), ())), preferred_element_type=F32)


def _split_hilo(a):
    hi = a.astype(BF16)
    return hi, (a - hi.astype(F32)).astype(BF16)


def _dot_hilo(a, m):
    hi, lo = _split_hilo(a)
    return _dot(hi, m) + _dot(lo, m)


def _sigmoid(v):
    return 1.0 / (1.0 + jnp.exp(-v))


def _rms(v, n):
    return v * lax.rsqrt(jnp.sum(v * v, axis=-1, keepdims=True) * (1.0 / n) + EPS)


def _rope(v, cos_v, sin_v, lo, half):
    lane = lax.broadcasted_iota(jnp.int32, v.shape, 1)
    is_x1 = (lane >= lo) & (lane < lo + half)
    rot = jnp.where(is_x1, pltpu.roll(v, LANES - half, 1), pltpu.roll(v, half, 1))
    return v * cos_v + rot * sin_v


def _rope_multipliers(cs, texp_ref, trow_ref):
    tabs = _dot_hilo(cs, texp_ref[...]) + trow_ref[...]
    return tuple(tabs[:, LANES * i:LANES * (i + 1)] for i in range(4))


def _const_spec(shape, single_buffer=False):
    nd = len(shape)
    mode = {"pipeline_mode": pl.Buffered(1)} if single_buffer else {}
    return pl.BlockSpec(shape, lambda *_: (0,) * nd, **mode)


def _rowmax(s):
    return jnp.max(s, axis=-1, keepdims=True)


def _attention_scores(q, k_ref, kmax, dbias):
    k0 = kmax - dbias.shape[1]
    s_d = _dot_nt(q, k_ref(k0, kmax)) + dbias
    m = _rowmax(s_d)
    s_m = None
    if k0 > 0:
        s_m = _dot_nt(q, k_ref(0, k0))
        m = jnp.maximum(m, _rowmax(s_m))
    return s_m, s_d, m


def _attention_values(scores, v_ref, kmax):
    s_m, s_d, m = scores
    k0 = kmax - s_d.shape[1]
    acc = _dot(jnp.exp2(s_d - m).astype(BF16), v_ref(k0, kmax))
    if s_m is not None:
        acc = acc + _dot(jnp.exp2(s_m - m).astype(BF16), v_ref(0, k0))
    return acc[:, :NSA_HEAD] / acc[:, NSA_HEAD:NSA_HEAD + 1]


def _rope_kernel(pos_ref, inv_ref, cs_ref):
    ang = pos_ref[0].astype(F32) * inv_ref[...]
    nf, S = ang.shape
    rows = jnp.concatenate([jnp.cos(ang), jnp.sin(ang), jnp.zeros((LANES - 2 * nf, S), F32)], axis=0)
    cs_ref[0] = rows.T


def _rope_tables(positions, inv):
    B, S = positions.shape
    nf = inv.shape[0]
    return pl.pallas_call(
        _rope_kernel,
        grid=(B,),
        in_specs=[pl.BlockSpec((1, 1, S), lambda b: (b, 0, 0)),
                  _const_spec((nf, 1))],
        out_specs=pl.BlockSpec((1, S, LANES), lambda b: (b, 0, 0)),
        out_shape=jax.ShapeDtypeStruct((B, S, LANES), F32),
        name="rope_tables",
    )(positions.reshape(B, 1, S), inv.reshape(nf, 1))


def _ada_kernel(c_ref, w_ref, b_ref, o_ref):
    c = c_ref[...]
    sc = c * _sigmoid(c)
    o_ref[...] = jnp.dot(sc, w_ref[0], preferred_element_type=F32,
                         precision=lax.Precision.HIGHEST) + b_ref[0]


def _ada(c, w, b, layer):
    B, D = c.shape
    N = w.shape[2]
    tn = D_MODEL
    return pl.pallas_call(
        _ada_kernel,
        grid=(N // tn,),
        in_specs=[_const_spec((B, D)),
                  pl.BlockSpec((1, D, tn), lambda j: (layer, 0, j)),
                  pl.BlockSpec((1, 1, tn), lambda j: (layer, 0, j))],
        out_specs=pl.BlockSpec((B, tn), lambda j: (0, j)),
        out_shape=jax.ShapeDtypeStruct((B, N), F32),
        name="ada_mod",
    )(c, w, b.reshape(b.shape[0], 1, N))


def _inproj_kernel(x_ref, mod_ref, cs_ref, g1_ref, texp_ref, trow_ref,
                   wcq_ref, wckv_ref, wsm_ref, wqn_ref, wkv6_ref, wgm_ref,
                   qag_ref, wqb_ref, kvag_ref, wkvb_ref,
                   mqg_ref, mkn_ref, mkr_ref, nqg_ref, nksg_ref, nkwg_ref, vone_ref,
                   qm_ref, km_ref, vm_ref, qn_ref, ks_ref, kw_ref, vs_ref, vw_ref,
                   kcin_ref, vcin_ref, gn_ref, gm_ref):
    tm = x_ref.shape[1]
    mod = mod_ref[0]
    sh1, sc1 = mod[0:1], mod[1:2]
    lane = lax.broadcasted_iota(jnp.int32, (IN_ROWS, LANES), 1)
    blk = lambda a, i: a[:, HEAD_PAD * i:HEAD_PAD * (i + 1)]
    hm = MLA_ROPE // 2
    hn = NSA_ROT // 2

    def front(rows):
        h = _rms(x_ref[0, rows], D_MODEL) * g1_ref[...] * (1.0 + sc1) + sh1
        hb = h.astype(BF16)
        return dict(
            hb=hb,
            tabs=_rope_multipliers(cs_ref[0, rows], texp_ref, trow_ref),
            cq=_dot(hb, wcq_ref[...]),
            ckv=_dot(hb, wckv_ref[...]),
            zs=_dot(hb, wsm_ref[...]),
            qn=_dot(hb, wqn_ref[...]),
            kv6=_dot(hb, wkv6_ref[...]))

    def middle(st):
        cqn = (_rms(st["cq"], MLA_Q_LORA) * qag_ref[...]).astype(BF16)
        st["q"] = _dot(cqn, wqb_ref[...])
        st["gates"] = _dot(st["hb"], wgm_ref[...])
        ckvn = (_rms(st["ckv"], MLA_KV_LORA) * kvag_ref[...]).astype(BF16)
        st["kv"] = _dot(ckvn, wkvb_ref[...])

    def back(rows, st):
        cos_m, sin_m, cos_n, sin_n = st["tabs"]
        zs, qn, kv6, q, kv = st["zs"], st["qn"], st["kv6"], st["q"], st["kv"]

        nqg = nqg_ref[...]
        n_scale = NSA_HEAD ** -0.5 * LOG2E
        for hd in range(NSA_HEADS):
            qh = _rope(_rms(blk(qn, hd), NSA_HEAD) * nqg, cos_n, sin_n, 0, hn) * n_scale
            qn_ref[0, rows, HEAD_PAD * hd:HEAD_PAD * (hd + 1)] = qh.astype(BF16)

        tok = pl.program_id(1) * tm + rows.start + lax.broadcasted_iota(jnp.int32, (IN_ROWS, 1), 0)
        sblk = lax.shift_right_logical(tok, SEL_LEN.bit_length() - 1)
        ind = jnp.where(lane - NSA_HEAD == sblk, NEG, 0.0)
        vone = vone_ref[:, 0:HEAD_PAD]
        nksg, nkwg = nksg_ref[...], nkwg_ref[...]
        kcin_ref[0, rows] = blk(kv6, 0)
        vcin_ref[0, rows] = blk(kv6, 1)
        for g in range(NSA_KV_GROUPS):
            ks = _rope(_rms(blk(kv6, 2 + g), NSA_HEAD) * nksg, cos_n, sin_n, 0, hn)
            ks_ref[0, g, rows] = (ks + ind).astype(BF16)
            vs_ref[0, g, rows] = (blk(kv6, 4 + g) + vone).astype(BF16)
            kw = _rope(_rms(blk(kv6, 6 + g), NSA_HEAD) * nkwg, cos_n, sin_n, 0, hn)
            kw_ref[0, g, rows] = kw.astype(BF16)
            vw_ref[0, g, rows] = (blk(kv6, 8 + g) + vone).astype(BF16)

        mqg = mqg_ref[...]
        m_scale = MLA_QK ** -0.5 * LOG2E
        for hd in range(MLA_HEADS):
            qh = _rope(_rms(blk(q, hd), MLA_QK) * mqg, cos_m, sin_m, MLA_NOPE, hm) * m_scale
            qm_ref[0, rows, HEAD_PAD * hd:HEAD_PAD * (hd + 1)] = qh.astype(BF16)

        gn_ref[0, rows] = _sigmoid(zs)
        gm_ref[0, rows] = _sigmoid(st["gates"]).astype(BF16)

        kpe = jnp.where((lane >= MLA_NOPE) & (lane < MLA_QK), zs, 0.0)
        kpe_ss = jnp.sum(kpe * kpe, axis=-1, keepdims=True)
        kr = _rope(kpe * mkr_ref[...], cos_m, sin_m, MLA_NOPE, hm)
        mkn = mkn_ref[...]
        for hd in range(MLA_HEADS):
            kn = blk(kv, hd)
            inv = lax.rsqrt((jnp.sum(kn * kn, axis=-1, keepdims=True) + kpe_ss) * (1.0 / MLA_QK) + EPS)
            km_ref[0, rows, HEAD_PAD * hd:HEAD_PAD * (hd + 1)] = ((kn * mkn + kr) * inv).astype(BF16)
        vm_ref[0, rows] = (kv[:, MLA_HEADS * HEAD_PAD:] + vone_ref[...]).astype(BF16)

    groups = [slice(s, s + IN_ROWS) for s in range(0, tm, IN_ROWS)]
    states = [front(rows) for rows in groups]
    for st in states:
        middle(st)
    for rows, st in zip(groups, states):
        back(rows, st)


def _inproj(x, mod, cs, consts, weights, rows):
    B, S, D = x.shape
    tm = TM_IN
    tok = lambda w: pl.BlockSpec((1, tm, w), lambda b, i: (b, i, 0))
    head = lambda n, w: pl.BlockSpec((1, n, tm, w), lambda b, i: (b, 0, i, 0))
    operands = list(consts) + list(weights[:6]) + [rows[0], weights[6], rows[1], weights[7]] + list(rows[2:])
    in_specs = [tok(D), pl.BlockSpec((1, N_MOD, D), lambda b, i: (b, 0, 0)), tok(LANES)]
    in_specs += [_const_spec(a.shape, single_buffer=True) for a in operands]
    G = NSA_KV_GROUPS
    sds = jax.ShapeDtypeStruct
    wide = MLA_HEADS * HEAD_PAD
    outs = [
        (tok(wide), sds((B, S, wide), BF16)),
        (tok(wide), sds((B, S, wide), BF16)),
        (tok(wide), sds((B, S, wide), BF16)),
        (tok(wide), sds((B, S, wide), BF16)),
        (head(G, HEAD_PAD), sds((B, G, S, HEAD_PAD), BF16)),
        (head(G, HEAD_PAD), sds((B, G, S, HEAD_PAD), BF16)),
        (head(G, HEAD_PAD), sds((B, G, S, HEAD_PAD), BF16)),
        (head(G, HEAD_PAD), sds((B, G, S, HEAD_PAD), BF16)),
        (tok(KV_W), sds((B, S, KV_W), F32)),
        (tok(KV_W), sds((B, S, KV_W), F32)),
        (tok(LANES), sds((B, S, LANES), F32)),
        (tok(2 * D), sds((B, S, 2 * D), BF16)),
    ]
    return pl.pallas_call(
        _inproj_kernel,
        grid=(B, S // tm),
        in_specs=in_specs,
        out_specs=[o[0] for o in outs],
        out_shape=[o[1] for o in outs],
        compiler_params=pltpu.CompilerParams(dimension_semantics=("arbitrary", "arbitrary"),
                                             vmem_limit_bytes=VMEM_LIMIT),
        name="inproj_prep",
    )(x, mod, cs, *operands)


def _compress_kernel(kcin_ref, vcin_ref, pk_ref, pv_ref, w1k_ref, w2k_ref, w1v_ref, w2v_ref,
                     kcg_ref, cs_ref, texp_ref, trow_ref, kc_ref, vc_ref):
    n = kcin_ref.shape[1] // CMP_STRIDE
    G = NSA_KV_GROUPS

    def hidden(cin_ref, pos_ref, w1_ref):
        a, b = [None] * G, [None] * G
        for l in range(0, CMP_STRIDE, CMP_PACK):
            toks = [cin_ref[0, pl.ds(l + j, n, stride=CMP_STRIDE), :] for j in range(CMP_PACK)]
            for g in range(G):
                lanes = slice(NSA_HEAD * g, NSA_HEAD * (g + 1))
                pack = lambda o: jnp.concatenate(
                    [toks[j][:, lanes] + pos_ref[o + l + j:o + l + j + 1] for j in range(CMP_PACK)],
                    axis=-1).astype(BF16)
                rows = lambda o: slice(NSA_HEAD * (o + l), NSA_HEAD * (o + l + CMP_PACK))
                da = _dot(pack(0), w1_ref[rows(0), :])
                db = _dot(pack(CMP_STRIDE), w1_ref[rows(CMP_STRIDE), :])
                a[g], b[g] = (da, db) if a[g] is None else (a[g] + da, b[g] + db)
        hid = [a[g] + pltpu.roll(b[g], n - 1, 0) for g in range(G)]
        return [(h * _sigmoid(h)).astype(BF16) for h in hid]

    _, _, cos_n, sin_n = _rope_multipliers(cs_ref[0], texp_ref, trow_ref)
    hk = hidden(kcin_ref, pk_ref, w1k_ref)
    hv = hidden(vcin_ref, pv_ref, w1v_ref)
    for g in range(G):
        kg = _rms(_dot(hk[g], w2k_ref[...]), NSA_HEAD) * kcg_ref[...]
        kc_ref[0, g] = _rope(kg, cos_n, sin_n, 0, NSA_ROT // 2).astype(BF16)
        vc_ref[0, g] = _dot(hv[g], w2v_ref[...]).astype(BF16)


def _compress(kcin, vcin, pk, pv, w1k, w2k, w1v, w2v, kcg, cs_end, texp, trow):
    B, S, w = kcin.shape
    G = NSA_KV_GROUPS
    n = S // CMP_STRIDE
    oblk = lambda wd: pl.BlockSpec((1, G, n, wd), lambda b: (b, 0, 0, 0))
    consts = (pk, pv, w1k, w2k, w1v, w2v, kcg)
    return pl.pallas_call(
        _compress_kernel,
        grid=(B,),
        in_specs=[pl.BlockSpec((1, S, w), lambda b: (b, 0, 0))] * 2 + [_const_spec(a.shape) for a in consts] +
                 [pl.BlockSpec((1, n, LANES), lambda b: (b, 0, 0)),
                  _const_spec(texp.shape), _const_spec(trow.shape)],
        out_specs=[oblk(HEAD_PAD), oblk(NSA_HEAD)],
        out_shape=[jax.ShapeDtypeStruct((B, G, n, HEAD_PAD), BF16),
                   jax.ShapeDtypeStruct((B, G, n, NSA_HEAD), BF16)],
        name="nsa_compress",
    )(kcin, vcin, *consts, cs_end, texp, trow)


def _nsa_kernel(q_ref, kc_ref, vc_ref, ks_ref, vs_ref, kw_ref, vw_ref, gn_ref,
                ovt_ref, gexp_ref, dbias_ref, wbias_ref, o_ref, imp_ref):
    tq = dbias_ref.shape[0]
    S = q_ref.shape[1]
    R = NSA_REP
    M = R * tq
    n_sel = imp_ref.shape[0]
    ncp = kc_ref.shape[2]
    span = WINDOW + tq
    grp = pl.program_id(1)
    dbias = dbias_ref[...]
    kf = lambda a, b: ks_ref[0, 0, a:b, :]
    vf = lambda a, b: vs_ref[0, 0, a:b, :]
    row = lax.broadcasted_iota(jnp.int32, (M, 1), 0)
    n_idx = lax.broadcasted_iota(jnp.int32, (M, ncp), 1)
    j = lax.broadcasted_iota(jnp.int32, (n_sel, tq), 0)
    head_q = lambda i, r: q_ref[0, i * tq:(i + 1) * tq, HEAD_PAD * r:HEAD_PAD * (r + 1)]
    tile_q = lambda i: jnp.concatenate([head_q(i, r) for r in range(R)], axis=0)
    tiles = {}

    def compressed_and_select(i, s):
        q0 = i * tq
        t = q0 + jnp.bitwise_and(row, tq - 1)
        valid = (n_idx * CMP_STRIDE + (CMP_LEN - 1)) <= t
        sm = jnp.where(valid, s, NEG)
        e = jnp.where(valid, jnp.exp2(sm - _rowmax(sm)), 0.0)
        den = jnp.sum(e, axis=-1, keepdims=True)
        p_c = e / jnp.where(den > 0.0, den, 1.0)
        o_c = _dot(p_c.astype(BF16), vc_ref[0, 0])
        psum = p_c[0:tq]
        for r in range(1, R):
            psum = psum + p_c[r * tq:(r + 1) * tq]
        hi, lo = _split_hilo(psum.T)
        imp = (_dot(ovt_ref[...], hi) + _dot(ovt_ref[...], lo))[0:n_sel]
        cur = lax.shift_right_logical(q0 + lax.broadcasted_iota(jnp.int32, (1, tq), 1),
                                      SEL_LEN.bit_length() - 1)
        forced = (j == 0) | (j == cur) | (j == cur - 1)
        imp = jnp.where(forced, imp + FORCE_BONUS, imp)
        imp = jnp.where(j <= cur, imp, NEG)
        imp_ref[...] = imp
        cnt = jnp.zeros((n_sel, tq), F32)
        for jj in range(n_sel):
            other = imp_ref[jj:jj + 1, :]
            beats = (other > imp) | ((other == imp) & (j > jj))
            cnt = cnt + jnp.where(beats, 1.0, 0.0)
        nsel = jnp.where((cnt < float(SEL_TOP)) & (j <= cur), 0.0, 1.0)
        nsel = jnp.concatenate([jnp.zeros((NSA_HEAD, tq), F32), nsel,
                                jnp.zeros((LANES - NSA_HEAD - n_sel, tq), F32)], axis=0).T.astype(BF16)
        tiles[i] = dict(o_c=o_c, nsel=nsel, o_s=[], o_w=[])

    def scores(u):
        kind, i, r = u
        if kind == "cmp":
            return _dot_nt(tile_q(i), kc_ref[0, 0])
        if kind == "sel":
            return _attention_scores(head_q(i, r) + tiles[i]["nsel"], kf, (i + 1) * tq, dbias)
        w0 = max(i * tq - WINDOW, 0)
        wb = wbias_ref[min(i, 1)]
        sw = _dot_nt(head_q(i, r), kw_ref[0, 0, w0:w0 + span, :]) + wb
        return sw, _rowmax(sw)

    def values(u, sc):
        kind, i, r = u
        if kind == "cmp":
            compressed_and_select(i, sc)
        elif kind == "sel":
            tiles[i]["o_s"].append(_attention_values(sc, vf, (i + 1) * tq))
        elif kind == "win":
            sw, mw = sc
            w0 = max(i * tq - WINDOW, 0)
            acc_w = _dot(jnp.exp2(sw - mw).astype(BF16), vw_ref[0, 0, w0:w0 + span, :])
            tiles[i]["o_w"].append(acc_w[:, :NSA_HEAD] / acc_w[:, NSA_HEAD:NSA_HEAD + 1])
        if kind == "win" and r == R - 1:
            tile = tiles.pop(i)
            g_hi, g_lo = _split_hilo(gn_ref[0, i * tq:(i + 1) * tq, :])
            o_c = jnp.concatenate([tile["o_c"][r * tq:(r + 1) * tq] for r in range(R)], axis=-1)
            branches = (o_c, jnp.concatenate(tile["o_s"], axis=-1), jnp.concatenate(tile["o_w"], axis=-1))
            out = None
            for br, o_b in enumerate(branches):
                gate = _dot(g_hi, gexp_ref[grp, br]) + _dot(g_lo, gexp_ref[grp, br])
                out = gate * o_b if out is None else out + gate * o_b
            o_ref[0, i * tq:(i + 1) * tq, :] = out.astype(BF16)

    nq = S // tq
    first = ("cmp", 0, 0)
    values(first, scores(first))
    units = []
    for i in range(nq):
        units += [("cmp", i + 1, 0)] if i + 1 < nq else []
        for r in range(R):
            units += [("sel", i, r), ("win", i, r)]
    pending = [scores(u) for u in units[:ATT_LOOKAHEAD]]
    for n, u in enumerate(units):
        if n + ATT_LOOKAHEAD < len(units):
            pending.append(scores(units[n + ATT_LOOKAHEAD]))
        values(u, pending.pop(0))


def _nsa_attention(qn, kc, vc, ks, vs, kw, vw, gn, ovt, gexp, dbias, wbias):
    B, S, _ = qn.shape
    G, R, Dh = NSA_KV_GROUPS, NSA_REP, NSA_HEAD
    ncp = kc.shape[2]
    full = pl.BlockSpec((1, 1, S, HEAD_PAD), lambda b, g: (b, g, 0, 0))
    cmp_spec = lambda w: pl.BlockSpec((1, 1, ncp, w), lambda b, g: (b, g, 0, 0))
    return pl.pallas_call(
        _nsa_kernel,
        grid=(B, G),
        in_specs=[pl.BlockSpec((1, S, R * HEAD_PAD), lambda b, g: (b, 0, g)),
                  cmp_spec(HEAD_PAD), cmp_spec(Dh), full, full, full, full,
                  pl.BlockSpec((1, S, LANES), lambda b, g: (b, 0, 0)),
                  _const_spec(ovt.shape), _const_spec(gexp.shape),
                  _const_spec(dbias.shape), _const_spec(wbias.shape)],
        out_specs=pl.BlockSpec((1, S, R * Dh), lambda b, g: (b, 0, g)),
        out_shape=jax.ShapeDtypeStruct((B, S, G * R * Dh), BF16),
        scratch_shapes=[pltpu.VMEM((S // SEL_LEN, TQ_ATT), F32)],
        compiler_params=pltpu.CompilerParams(dimension_semantics=("arbitrary",) * 2,
                                             vmem_limit_bytes=VMEM_LIMIT),
        name="nsa_attention",
    )(qn, kc, vc, ks, vs, kw, vw, gn, ovt, gexp, dbias, wbias)


def _mla_kernel(q_ref, k_ref, v_ref, dbias_ref, o_ref):
    tq = dbias_ref.shape[0]
    S = q_ref.shape[1]
    dbias = dbias_ref[...]
    units = [(i, hh) for i in range(S // tq) for hh in range(2)]

    def scores(u):
        i, hh = u
        cols = slice(HEAD_PAD * hh, HEAD_PAD * (hh + 1))
        q = q_ref[0, i * tq:(i + 1) * tq, cols]
        return _attention_scores(q, lambda a, b: k_ref[0, a:b, cols], (i + 1) * tq, dbias)

    def values(u, sc):
        i, hh = u
        cols = slice(HEAD_PAD * hh, HEAD_PAD * (hh + 1))
        o = _attention_values(sc, lambda a, b: v_ref[0, a:b, cols], (i + 1) * tq)
        o_ref[0, i * tq:(i + 1) * tq, MLA_V * hh:MLA_V * (hh + 1)] = o.astype(BF16)

    pending = [scores(u) for u in units[:ATT_LOOKAHEAD]]
    for n, u in enumerate(units):
        if n + ATT_LOOKAHEAD < len(units):
            pending.append(scores(units[n + ATT_LOOKAHEAD]))
        values(u, pending.pop(0))


def _mla_attention(qm, km, vm, dbias):
    B, S, _ = qm.shape
    pair = pl.BlockSpec((1, S, 2 * HEAD_PAD), lambda b, h: (b, 0, h))
    return pl.pallas_call(
        _mla_kernel,
        grid=(B, MLA_HEADS // 2),
        in_specs=[pair, pair, pair, _const_spec(dbias.shape)],
        out_specs=pl.BlockSpec((1, S, 2 * MLA_V), lambda b, h: (b, 0, h)),
        out_shape=jax.ShapeDtypeStruct((B, S, MLA_HEADS * MLA_V), BF16),
        compiler_params=pltpu.CompilerParams(dimension_semantics=("arbitrary",) * 2,
                                             vmem_limit_bytes=VMEM_LIMIT),
        name="mla_attention",
    )(qm, km, vm, dbias)


def _out_ffn_kernel(x_ref, om_ref, on_ref, gm_ref, mod_ref, g2_ref,
                    wom_ref, won_ref, wout_ref, wg_ref, wu_ref, wd_ref, o_ref):
    mod = mod_ref[0]
    gt1, sh2, sc2, gt2 = mod[2:3], mod[3:4], mod[4:5], mod[5:6]
    groups = [slice(s, s + OUT_ROWS) for s in range(0, x_ref.shape[1], OUT_ROWS)]
    chunks = [slice(lo, hi) for lo, hi in zip(FF_SPLITS[:-1], FF_SPLITS[1:])]
    heads = [(_dot(om_ref[0, rows], wom_ref[...]), _dot(on_ref[0, rows], won_ref[...])) for rows in groups]
    x1s = []
    for rows, (ym, yn) in zip(groups, heads):
        merged = gm_ref[0, rows, :D_MODEL] * ym + gm_ref[0, rows, D_MODEL:] * yn
        x1s.append(x_ref[0, rows] + gt1 * _dot(merged.astype(BF16), wout_ref[...]))
    gus = []
    for x1 in x1s:
        h2 = (_rms(x1, D_MODEL) * g2_ref[...] * (1.0 + sc2) + sh2).astype(BF16)
        gus.append([(_dot(h2, wg_ref[:, sl]), _dot(h2, wu_ref[:, sl])) for sl in chunks])
    for rows, x1, gu in zip(groups, x1s, gus):
        acc = None
        for sl, (g, u) in zip(chunks, gu):
            d = _dot((g * _sigmoid(g) * u).astype(BF16), wd_ref[sl, :])
            acc = d if acc is None else acc + d
        o_ref[0, rows] = x1 + gt2 * acc


def _out_ffn(x, om, on, gm, mod, g2, wom, won, wout, wg, wu, wd):
    B, S, D = x.shape
    tm = TM_OUT
    tok = lambda w: pl.BlockSpec((1, tm, w), lambda b, i: (b, i, 0))
    wspec = lambda w: pl.BlockSpec(w.shape, lambda b, i: (0, 0), pipeline_mode=pl.Buffered(1))
    return pl.pallas_call(
        _out_ffn_kernel,
        grid=(B, S // tm),
        in_specs=[tok(D), tok(om.shape[2]), tok(on.shape[2]), tok(2 * D),
                  pl.BlockSpec((1, N_MOD, D), lambda b, i: (b, 0, 0)),
                  _const_spec(g2.shape)] + [wspec(w) for w in (wom, won, wout, wg, wu, wd)],
        out_specs=tok(D),
        out_shape=jax.ShapeDtypeStruct((B, S, D), F32),
        compiler_params=pltpu.CompilerParams(dimension_semantics=("arbitrary", "arbitrary"),
                                             vmem_limit_bytes=VMEM_LIMIT),
        name="out_ffn",
    )(x, om, on, gm, mod, g2, wom, won, wout, wg, wu, wd)


def _rope_expansion():
    texp = np.zeros((LANES, 4 * LANES), np.float32)
    trow = np.zeros((1, 4 * LANES), np.float32)
    hm, hn = MLA_ROPE // 2, NSA_ROT // 2
    trow[0, 0:LANES] = 1.0
    trow[0, 2 * LANES:3 * LANES] = 1.0
    for i in range(hm):
        for off, sgn in ((MLA_NOPE + i, -1.0), (MLA_NOPE + hm + i, 1.0)):
            texp[i, off] = 1.0
            trow[0, off] = 0.0
            texp[N_FREQ + i, LANES + off] = sgn
    for i in range(hn):
        for off, sgn in ((i, -1.0), (hn + i, 1.0)):
            texp[hm + i, 2 * LANES + off] = 1.0
            trow[0, 2 * LANES + off] = 0.0
            texp[N_FREQ + hm + i, 3 * LANES + off] = sgn
    return jnp.asarray(texp, BF16), jnp.asarray(trow, F32)


def _mask_tables(S):
    tq = TQ_ATT
    n_chunk = S // CMP_STRIDE
    n_sel = S // SEL_LEN
    starts = np.arange(n_chunk) * CMP_STRIDE
    sel_start = np.arange(LANES) * SEL_LEN
    ovt = ((starts[None, :] < sel_start[:, None] + SEL_LEN) &
           (starts[None, :] + CMP_LEN > sel_start[:, None]) &
           (np.arange(n_chunk)[None, :] < n_chunk - 1) &
           (np.arange(LANES)[:, None] < n_sel))
    gcol = np.arange(LANES)[:, None]
    head = np.arange(NSA_REP * NSA_HEAD)[None, :] // NSA_HEAD
    gexp = np.stack([np.stack([gcol == (g * NSA_REP + head) * N_NSA_BRANCH + br
                               for br in range(N_NSA_BRANCH)]) for g in range(NSA_KV_GROUPS)])
    qi = np.arange(tq)[:, None]
    dbias = np.where(np.arange(tq)[None, :] <= qi, 0.0, NEG)
    kk = np.arange(WINDOW + tq)[None, :]
    band = lambda d: np.where((d >= 0) & (d < WINDOW), 0.0, NEG)
    wbias = np.stack([band(qi - kk), band(qi + WINDOW - kk)])
    return (jnp.asarray(ovt, BF16), jnp.asarray(gexp, BF16),
            jnp.asarray(dbias, F32), jnp.asarray(wbias, F32))


def _pad_heads(w, n_heads, width):
    k = w.shape[0]
    w = w.reshape(k, n_heads, width)
    return jnp.pad(w, ((0, 0), (0, 0), (0, HEAD_PAD - width))).reshape(k, n_heads * HEAD_PAD)


def _pad_row(g, lo=0):
    return jnp.pad(g, (lo, HEAD_PAD - lo - g.shape[0])).reshape(1, HEAD_PAD)


def _layer(x, mod, cs, p):
    B, S, D = x.shape
    w_in = p["w_in"]
    o = 0
    cols = {}
    for name, wdt in (("cq", MLA_Q_LORA), ("ckv", MLA_KV_LORA), ("kpe", MLA_ROPE),
                      ("qn", NSA_HEADS * NSA_HEAD), ("kc", KV_W), ("vc", KV_W), ("ks", KV_W),
                      ("vs", KV_W), ("kw", KV_W), ("vw", KV_W),
                      ("gn", NSA_HEADS * N_NSA_BRANCH), ("gm", 2 * D)):
        cols[name] = w_in[:, o:o + wdt]
        o += wdt
    G = NSA_KV_GROUPS
    n_gate = NSA_HEADS * N_NSA_BRANCH
    zc = lambda n: jnp.zeros((D, n), F32)
    wsm = jnp.concatenate([cols["gn"], zc(MLA_NOPE - n_gate), cols["kpe"], zc(LANES - MLA_QK)], axis=1)
    wkv6 = jnp.concatenate([cols["kc"], cols["vc"]] +
                           [_pad_heads(cols[k], G, NSA_HEAD) for k in ("ks", "vs", "kw", "vw")], axis=1)
    wkvb = p["mla_w_kv_b"].reshape(MLA_KV_LORA, MLA_HEADS, MLA_NOPE + MLA_V)
    wkvb = jnp.concatenate([_pad_heads(wkvb[:, :, :MLA_NOPE].reshape(MLA_KV_LORA, -1), MLA_HEADS, MLA_NOPE),
                            _pad_heads(wkvb[:, :, MLA_NOPE:].reshape(MLA_KV_LORA, -1), MLA_HEADS, MLA_V)], axis=1)
    bf = lambda w: w.astype(BF16)
    row = lambda g: g.reshape(1, -1)
    weights = tuple(bf(w) for w in (cols["cq"], cols["ckv"], wsm, _pad_heads(cols["qn"], NSA_HEADS, NSA_HEAD),
                                    wkv6, cols["gm"],
                                    _pad_heads(p["mla_w_q_b"], MLA_HEADS, MLA_QK), wkvb))
    vone = jnp.tile(jnp.zeros((1, HEAD_PAD), F32).at[0, MLA_V].set(1.0), (1, MLA_HEADS))
    rows = (row(p["mla_q_a_gain"]), row(p["mla_kv_a_gain"]),
            _pad_row(p["mla_q_gain"]), _pad_row(p["mla_k_gain"][:MLA_NOPE]),
            _pad_row(p["mla_k_gain"][MLA_NOPE:], MLA_NOPE),
            _pad_row(p["nsa_q_gain"]), _pad_row(p["nsa_ks_gain"]), _pad_row(p["nsa_kw_gain"]), vone)
    texp, trow = _rope_expansion()
    (qm, km, vm, qn, ks, kw, vs, vw, kcin, vcin, gn, gm) = _inproj(
        x, mod, cs, (row(p["norm1_gain"]), texp, trow), weights, rows)

    n_chunk = S // CMP_STRIDE
    cs_end = cs[:, CMP_LEN - 1::CMP_STRIDE]
    cs_end = jnp.pad(cs_end, ((0, 0), (0, n_chunk - cs_end.shape[1]), (0, 0)))
    w2k = jnp.pad(p["cmp_w2_k"], ((0, 0), (0, HEAD_PAD - NSA_HEAD)))
    kc, vc = _compress(kcin, vcin, p["cmp_pos_k"], p["cmp_pos_v"],
                       bf(p["cmp_w1_k"]), bf(w2k), bf(p["cmp_w1_v"]), bf(p["cmp_w2_v"]),
                       _pad_row(p["nsa_kc_gain"]), cs_end, texp, trow)

    ovt, gexp, dbias, wbias = _mask_tables(S)
    o_nsa = _nsa_attention(qn, kc, vc, ks, vs, kw, vw, gn, ovt, gexp, dbias, wbias)
    o_mla = _mla_attention(qm, km, vm, dbias)

    return _out_ffn(x, o_mla, o_nsa, gm, mod, row(p["norm2_gain"]),
                    bf(p["w_o_mla"]), bf(p["w_o_nsa"]), bf(p["w_out"]),
                    bf(p["ffn_w_gate"]), bf(p["ffn_w_up"]), bf(p["ffn_w_down"]))


def kernel(x, c, positions, ada_w, ada_b, norm1_gain, w_in, mla_q_a_gain, mla_w_q_b, mla_kv_a_gain, mla_w_kv_b, mla_q_gain, mla_k_gain, nsa_q_gain, nsa_kc_gain, nsa_ks_gain, nsa_kw_gain, cmp_pos_k, cmp_w1_k, cmp_w2_k, cmp_pos_v, cmp_w1_v, cmp_w2_v, w_o_mla, w_o_nsa, w_out, norm2_gain, ffn_w_gate, ffn_w_up, ffn_w_down):
    params = dict(norm1_gain=norm1_gain, w_in=w_in, mla_q_a_gain=mla_q_a_gain, mla_w_q_b=mla_w_q_b,
                  mla_kv_a_gain=mla_kv_a_gain, mla_w_kv_b=mla_w_kv_b, mla_q_gain=mla_q_gain,
                  mla_k_gain=mla_k_gain, nsa_q_gain=nsa_q_gain, nsa_kc_gain=nsa_kc_gain,
                  nsa_ks_gain=nsa_ks_gain, nsa_kw_gain=nsa_kw_gain, cmp_pos_k=cmp_pos_k,
                  cmp_w1_k=cmp_w1_k, cmp_w2_k=cmp_w2_k, cmp_pos_v=cmp_pos_v, cmp_w1_v=cmp_w1_v,
                  cmp_w2_v=cmp_w2_v, w_o_mla=w_o_mla, w_o_nsa=w_o_nsa, w_out=w_out,
                  norm2_gain=norm2_gain, ffn_w_gate=ffn_w_gate, ffn_w_up=ffn_w_up, ffn_w_down=ffn_w_down)
    B = x.shape[0]
    inv_m = ROPE_THETA ** (-jnp.arange(0, MLA_ROPE, 2, dtype=F32) / MLA_ROPE)
    inv_n = ROPE_THETA ** (-jnp.arange(0, NSA_ROT, 2, dtype=F32) / NSA_ROT)
    n_unused = N_FREQ - inv_m.shape[0] - inv_n.shape[0]
    cs = _rope_tables(positions, jnp.concatenate([inv_m, inv_n, jnp.zeros((n_unused,), F32)]))
    depth = ada_w.shape[0]
    for l in range(depth):
        mod = _ada(c, ada_w, ada_b, l).reshape(B, N_MOD, D_MODEL)
        x = _layer(x, mod, cs, {k: v[l] for k, v in params.items()})
    return x
```

```python
import numpy as np
import jax
import jax.numpy as jnp
from jax import lax
from jax.experimental import pallas as pl
from jax.experimental.pallas import tpu as pltpu

F32 = jnp.float32
BF16 = jnp.bfloat16

D_MODEL = 1024
ROPE_THETA = 500000.0
EPS = 1e-6
NEG = -1e30
LOG2E = 1.4426950408889634

MLA_HEADS = 8
MLA_NOPE = 64
MLA_ROPE = 32
MLA_QK = MLA_NOPE + MLA_ROPE
MLA_V = 64
MLA_Q_LORA = 768
MLA_KV_LORA = 256

NSA_HEADS = 8
NSA_KV_GROUPS = 2
NSA_REP = NSA_HEADS // NSA_KV_GROUPS
NSA_HEAD = 64
NSA_ROT = NSA_HEAD // 4
CMP_LEN = 32
CMP_STRIDE = 16
CMP_HIDDEN = 256
SEL_LEN = 64
SEL_TOP = 8
WINDOW = 256
N_NSA_BRANCH = 3
FORCE_BONUS = 1e4
KV_W = NSA_KV_GROUPS * NSA_HEAD

D_FF = -(-8 * D_MODEL // (3 * 256)) * 256
N_MOD = 6
LANES = 128
HEAD_PAD = LANES
N_FREQ = 32

TM_IN = 512
IN_ROWS = 256
TQ_ATT = 256
TM_OUT = 512
OUT_ROWS = 256
MXU_TILE = 256
CMP_PACK = MXU_TILE // NSA_HEAD
FF_SPLITS = (0, 6 * MXU_TILE, D_FF)
VMEM_LIMIT = 56 * 1024 * 1024


def _dot(a, b):
    return jnp.dot(a, b, preferred_element_type=F32)


def _dot_nt(a, b):
    return lax.dot_general(a, b, (((1,), (1,)), ((), ())), preferred_element_type=F32)


def _split_hilo(a):
    hi = a.astype(BF16)
    return hi, (a - hi.astype(F32)).astype(BF16)


def _dot_hilo(a, m):
    hi, lo = _split_hilo(a)
    return _dot(hi, m) + _dot(lo, m)


def _sigmoid(v):
    return 1.0 / (1.0 + jnp.exp(-v))


def _rms(v, n):
    return v * lax.rsqrt(jnp.sum(v * v, axis=-1, keepdims=True) * (1.0 / n) + EPS)


def _rope(v, cos_v, sin_v, lo, half):
    lane = lax.broadcasted_iota(jnp.int32, v.shape, 1)
    is_x1 = (lane >= lo) & (lane < lo + half)
    rot = jnp.where(is_x1, pltpu.roll(v, LANES - half, 1), pltpu.roll(v, half, 1))
    return v * cos_v + rot * sin_v


def _rope_multipliers(cs, texp_ref, trow_ref):
    tabs = _dot_hilo(cs, texp_ref[...]) + trow_ref[...]
    return tuple(tabs[:, LANES * i:LANES * (i + 1)] for i in range(4))


def _const_spec(shape, single_buffer=False):
    nd = len(shape)
    mode = {"pipeline_mode": pl.Buffered(1)} if single_buffer else {}
    return pl.BlockSpec(shape, lambda *_: (0,) * nd, **mode)


def _rowmax(s):
    return jnp.max(s, axis=-1, keepdims=True)


def _attention_scores(q, k_ref, kmax, dbias):
    k0 = kmax - dbias.shape[1]
    s_d = _dot_nt(q, k_ref(k0, kmax)) + dbias
    m = _rowmax(s_d)
    s_m = None
    if k0 > 0:
        s_m = _dot_nt(q, k_ref(0, k0))
        m = jnp.maximum(m, _rowmax(s_m))
    return s_m, s_d, m


def _attention_probs(scores):
    s_m, s_d, m = scores
    p_m = None if s_m is None else jnp.exp2(s_m - m).astype(BF16)
    return p_m, jnp.exp2(s_d - m).astype(BF16)


def _attention_out(probs, v_ref, kmax):
    p_m, p_d = probs
    k0 = kmax - p_d.shape[1]
    acc = _dot(p_d, v_ref(k0, kmax))
    if p_m is not None:
        acc = acc + _dot(p_m, v_ref(0, k0))
    return acc[:, :NSA_HEAD] / acc[:, NSA_HEAD:NSA_HEAD + 1]


def _software_pipeline(units, stages):
    last = len(stages) - 1
    carry = {}
    for step in range(-last, len(units)):
        for s, stage in enumerate(stages):
            n = step + last - s
            if 0 <= n < len(units):
                carry[n] = stage(units[n], carry.get(n))


def _rope_kernel(pos_ref, inv_ref, cs_ref):
    ang = pos_ref[0].astype(F32) * inv_ref[...]
    nf, S = ang.shape
    rows = jnp.concatenate([jnp.cos(ang), jnp.sin(ang), jnp.zeros((LANES - 2 * nf, S), F32)], axis=0)
    cs_ref[0] = rows.T


def _rope_tables(positions, inv):
    B, S = positions.shape
    nf = inv.shape[0]
    return pl.pallas_call(
        _rope_kernel,
        grid=(B,),
        in_specs=[pl.BlockSpec((1, 1, S), lambda b: (b, 0, 0)),
                  _const_spec((nf, 1))],
        out_specs=pl.BlockSpec((1, S, LANES), lambda b: (b, 0, 0)),
        out_shape=jax.ShapeDtypeStruct((B, S, LANES), F32),
        name="rope_tables",
    )(positions.reshape(B, 1, S), inv.reshape(nf, 1))


def _ada_kernel(c_ref, w_ref, b_ref, o_ref):
    c = c_ref[...]
    sc = c * _sigmoid(c)
    o_ref[...] = jnp.dot(sc, w_ref[0], preferred_element_type=F32,
                         precision=lax.Precision.HIGHEST) + b_ref[0]


def _ada(c, w, b, layer):
    B, D = c.shape
    N = w.shape[2]
    tn = D_MODEL
    return pl.pallas_call(
        _ada_kernel,
        grid=(N // tn,),
        in_specs=[_const_spec((B, D)),
                  pl.BlockSpec((1, D, tn), lambda j: (layer, 0, j)),
                  pl.BlockSpec((1, 1, tn), lambda j: (layer, 0, j))],
        out_specs=pl.BlockSpec((B, tn), lambda j: (0, j)),
        out_shape=jax.ShapeDtypeStruct((B, N), F32),
        name="ada_mod",
    )(c, w, b.reshape(b.shape[0], 1, N))


def _inproj_kernel(x_ref, mod_ref, cs_ref, g1_ref, texp_ref, trow_ref,
                   wcq_ref, wckv_ref, wsm_ref, wqn_ref, wkv6_ref, wgm_ref,
                   qag_ref, wqb_ref, kvag_ref, wkvb_ref,
                   mqg_ref, mkn_ref, mkr_ref, nqg_ref, nksg_ref, nkwg_ref, vone_ref,
                   qm_ref, km_ref, vm_ref, qn_ref, ks_ref, kw_ref, vs_ref, vw_ref,
                   kcin_ref, vcin_ref, gn_ref, gm_ref):
    tm = x_ref.shape[1]
    mod = mod_ref[0]
    sh1, sc1 = mod[0:1], mod[1:2]
    lane = lax.broadcasted_iota(jnp.int32, (IN_ROWS, LANES), 1)
    blk = lambda a, i: a[:, HEAD_PAD * i:HEAD_PAD * (i + 1)]
    hm = MLA_ROPE // 2
    hn = NSA_ROT // 2

    def front(rows):
        h = _rms(x_ref[0, rows], D_MODEL) * g1_ref[...] * (1.0 + sc1) + sh1
        hb = h.astype(BF16)
        return dict(
            hb=hb,
            tabs=_rope_multipliers(cs_ref[0, rows], texp_ref, trow_ref),
            cq=_dot(hb, wcq_ref[...]),
            ckv=_dot(hb, wckv_ref[...]),
            zs=_dot(hb, wsm_ref[...]),
            qn=_dot(hb, wqn_ref[...]),
            kv6=_dot(hb, wkv6_ref[...]))

    def middle(st):
        cqn = (_rms(st["cq"], MLA_Q_LORA) * qag_ref[...]).astype(BF16)
        st["q"] = _dot(cqn, wqb_ref[...])
        st["gates"] = _dot(st["hb"], wgm_ref[...])
        ckvn = (_rms(st["ckv"], MLA_KV_LORA) * kvag_ref[...]).astype(BF16)
        st["kv"] = _dot(ckvn, wkvb_ref[...])

    def back(rows, st):
        cos_m, sin_m, cos_n, sin_n = st["tabs"]
        zs, qn, kv6, q, kv = st["zs"], st["qn"], st["kv6"], st["q"], st["kv"]

        nqg = nqg_ref[...]
        n_scale = NSA_HEAD ** -0.5 * LOG2E
        for hd in range(NSA_HEADS):
            qh = _rope(_rms(blk(qn, hd), NSA_HEAD) * nqg, cos_n, sin_n, 0, hn) * n_scale
            qn_ref[0, rows, HEAD_PAD * hd:HEAD_PAD * (hd + 1)] = qh.astype(BF16)

        tok = pl.program_id(1) * tm + rows.start + lax.broadcasted_iota(jnp.int32, (IN_ROWS, 1), 0)
        sblk = lax.shift_right_logical(tok, SEL_LEN.bit_length() - 1)
        ind = jnp.where(lane - NSA_HEAD == sblk, NEG, 0.0)
        vone = vone_ref[:, 0:HEAD_PAD]
        nksg, nkwg = nksg_ref[...], nkwg_ref[...]
        kcin_ref[0, rows] = blk(kv6, 0)
        vcin_ref[0, rows] = blk(kv6, 1)
        for g in range(NSA_KV_GROUPS):
            ks = _rope(_rms(blk(kv6, 2 + g), NSA_HEAD) * nksg, cos_n, sin_n, 0, hn)
            ks_ref[0, g, rows] = (ks + ind).astype(BF16)
            vs_ref[0, g, rows] = (blk(kv6, 4 + g) + vone).astype(BF16)
            kw = _rope(_rms(blk(kv6, 6 + g), NSA_HEAD) * nkwg, cos_n, sin_n, 0, hn)
            kw_ref[0, g, rows] = kw.astype(BF16)
            vw_ref[0, g, rows] = (blk(kv6, 8 + g) + vone).astype(BF16)

        mqg = mqg_ref[...]
        m_scale = MLA_QK ** -0.5 * LOG2E
        for hd in range(MLA_HEADS):
            qh = _rope(_rms(blk(q, hd), MLA_QK) * mqg, cos_m, sin_m, MLA_NOPE, hm) * m_scale
            qm_ref[0, rows, HEAD_PAD * hd:HEAD_PAD * (hd + 1)] = qh.astype(BF16)

        gn_ref[0, rows] = _sigmoid(zs)
        gm_ref[0, rows] = _sigmoid(st["gates"]).astype(BF16)

        kpe = jnp.where((lane >= MLA_NOPE) & (lane < MLA_QK), zs, 0.0)
        kpe_ss = jnp.sum(kpe * kpe, axis=-1, keepdims=True)
        kr = _rope(kpe * mkr_ref[...], cos_m, sin_m, MLA_NOPE, hm)
        mkn = mkn_ref[...]
        for hd in range(MLA_HEADS):
            kn = blk(kv, hd)
            inv = lax.rsqrt((jnp.sum(kn * kn, axis=-1, keepdims=True) + kpe_ss) * (1.0 / MLA_QK) + EPS)
            km_ref[0, rows, HEAD_PAD * hd:HEAD_PAD * (hd + 1)] = ((kn * mkn + kr) * inv).astype(BF16)
        vm_ref[0, rows] = (kv[:, MLA_HEADS * HEAD_PAD:] + vone_ref[...]).astype(BF16)

    groups = [slice(s, s + IN_ROWS) for s in range(0, tm, IN_ROWS)]
    states = [front(rows) for rows in groups]
    for st in states:
        middle(st)
    for rows, st in zip(groups, states):
        back(rows, st)


def _inproj(x, mod, cs, consts, weights, rows):
    B, S, D = x.shape
    tm = TM_IN
    tok = lambda w: pl.BlockSpec((1, tm, w), lambda b, i: (b, i, 0))
    head = lambda n, w: pl.BlockSpec((1, n, tm, w), lambda b, i: (b, 0, i, 0))
    operands = list(consts) + list(weights[:6]) + [rows[0], weights[6], rows[1], weights[7]] + list(rows[2:])
    in_specs = [tok(D), pl.BlockSpec((1, N_MOD, D), lambda b, i: (b, 0, 0)), tok(LANES)]
    in_specs += [_const_spec(a.shape, single_buffer=True) for a in operands]
    G = NSA_KV_GROUPS
    sds = jax.ShapeDtypeStruct
    wide = MLA_HEADS * HEAD_PAD
    outs = [
        (tok(wide), sds((B, S, wide), BF16)),
        (tok(wide), sds((B, S, wide), BF16)),
        (tok(wide), sds((B, S, wide), BF16)),
        (tok(wide), sds((B, S, wide), BF16)),
        (head(G, HEAD_PAD), sds((B, G, S, HEAD_PAD), BF16)),
        (head(G, HEAD_PAD), sds((B, G, S, HEAD_PAD), BF16)),
        (head(G, HEAD_PAD), sds((B, G, S, HEAD_PAD), BF16)),
        (head(G, HEAD_PAD), sds((B, G, S, HEAD_PAD), BF16)),
        (tok(KV_W), sds((B, S, KV_W), F32)),
        (tok(KV_W), sds((B, S, KV_W), F32)),
        (tok(LANES), sds((B, S, LANES), F32)),
        (tok(2 * D), sds((B, S, 2 * D), BF16)),
    ]
    return pl.pallas_call(
        _inproj_kernel,
        grid=(B, S // tm),
        in_specs=in_specs,
        out_specs=[o[0] for o in outs],
        out_shape=[o[1] for o in outs],
        compiler_params=pltpu.CompilerParams(dimension_semantics=("arbitrary", "arbitrary"),
                                             vmem_limit_bytes=VMEM_LIMIT),
        name="inproj_prep",
    )(x, mod, cs, *operands)


def _compress_kernel(kcin_ref, vcin_ref, pk_ref, pv_ref, w1k_ref, w2k_ref, w1v_ref, w2v_ref,
                     kcg_ref, cs_ref, texp_ref, trow_ref, kc_ref, vc_ref):
    n = kcin_ref.shape[1] // CMP_STRIDE
    G = NSA_KV_GROUPS

    def hidden(cin_ref, pos_ref, w1_ref):
        a, b = [None] * G, [None] * G
        for l in range(0, CMP_STRIDE, CMP_PACK):
            toks = [cin_ref[0, pl.ds(l + j, n, stride=CMP_STRIDE), :] for j in range(CMP_PACK)]
            for g in range(G):
                lanes = slice(NSA_HEAD * g, NSA_HEAD * (g + 1))
                pack = lambda o: jnp.concatenate(
                    [toks[j][:, lanes] + pos_ref[o + l + j:o + l + j + 1] for j in range(CMP_PACK)],
                    axis=-1).astype(BF16)
                rows = lambda o: slice(NSA_HEAD * (o + l), NSA_HEAD * (o + l + CMP_PACK))
                da = _dot(pack(0), w1_ref[rows(0), :])
                db = _dot(pack(CMP_STRIDE), w1_ref[rows(CMP_STRIDE), :])
                a[g], b[g] = (da, db) if a[g] is None else (a[g] + da, b[g] + db)
        hid = [a[g] + pltpu.roll(b[g], n - 1, 0) for g in range(G)]
        return [(h * _sigmoid(h)).astype(BF16) for h in hid]

    _, _, cos_n, sin_n = _rope_multipliers(cs_ref[0], texp_ref, trow_ref)
    hk = hidden(kcin_ref, pk_ref, w1k_ref)
    hv = hidden(vcin_ref, pv_ref, w1v_ref)
    for g in range(G):
        kg = _rms(_dot(hk[g], w2k_ref[...]), NSA_HEAD) * kcg_ref[...]
        kc_ref[0, g] = _rope(kg, cos_n, sin_n, 0, NSA_ROT // 2).astype(BF16)
        vc_ref[0, g] = _dot(hv[g], w2v_ref[...]).astype(BF16)


def _compress(kcin, vcin, pk, pv, w1k, w2k, w1v, w2v, kcg, cs_end, texp, trow):
    B, S, w = kcin.shape
    G = NSA_KV_GROUPS
    n = S // CMP_STRIDE
    oblk = lambda wd: pl.BlockSpec((1, G, n, wd), lambda b: (b, 0, 0, 0))
    consts = (pk, pv, w1k, w2k, w1v, w2v, kcg)
    return pl.pallas_call(
        _compress_kernel,
        grid=(B,),
        in_specs=[pl.BlockSpec((1, S, w), lambda b: (b, 0, 0))] * 2 + [_const_spec(a.shape) for a in consts] +
                 [pl.BlockSpec((1, n, LANES), lambda b: (b, 0, 0)),
                  _const_spec(texp.shape), _const_spec(trow.shape)],
        out_specs=[oblk(HEAD_PAD), oblk(NSA_HEAD)],
        out_shape=[jax.ShapeDtypeStruct((B, G, n, HEAD_PAD), BF16),
                   jax.ShapeDtypeStruct((B, G, n, NSA_HEAD), BF16)],
        name="nsa_compress",
    )(kcin, vcin, *consts, cs_end, texp, trow)


def _nsa_kernel(q_ref, kc_ref, vc_ref, ks_ref, vs_ref, kw_ref, vw_ref, gn_ref,
                ovt_ref, gexp_ref, dbias_ref, wbias_ref, o_ref, imp_ref):
    tq = dbias_ref.shape[0]
    S = q_ref.shape[1]
    R = NSA_REP
    M = R * tq
    n_sel = imp_ref.shape[0]
    ncp = kc_ref.shape[2]
    span = WINDOW + tq
    grp = pl.program_id(1)
    dbias = dbias_ref[...]
    kf = lambda a, b: ks_ref[0, 0, a:b, :]
    vf = lambda a, b: vs_ref[0, 0, a:b, :]
    row = lax.broadcasted_iota(jnp.int32, (M, 1), 0)
    n_idx = lax.broadcasted_iota(jnp.int32, (M, ncp), 1)
    j = lax.broadcasted_iota(jnp.int32, (n_sel, tq), 0)
    head_q = lambda i, r: q_ref[0, i * tq:(i + 1) * tq, HEAD_PAD * r:HEAD_PAD * (r + 1)]
    tile_q = lambda i: jnp.concatenate([head_q(i, r) for r in range(R)], axis=0)
    tiles = {}

    def compressed_and_select(i, s):
        q0 = i * tq
        t = q0 + jnp.bitwise_and(row, tq - 1)
        valid = (n_idx * CMP_STRIDE + (CMP_LEN - 1)) <= t
        sm = jnp.where(valid, s, NEG)
        e = jnp.where(valid, jnp.exp2(sm - _rowmax(sm)), 0.0)
        den = jnp.sum(e, axis=-1, keepdims=True)
        p_c = e / jnp.where(den > 0.0, den, 1.0)
        o_c = _dot(p_c.astype(BF16), vc_ref[0, 0])
        psum = p_c[0:tq]
        for r in range(1, R):
            psum = psum + p_c[r * tq:(r + 1) * tq]
        hi, lo = _split_hilo(psum.T)
        imp = (_dot(ovt_ref[...], hi) + _dot(ovt_ref[...], lo))[0:n_sel]
        cur = lax.shift_right_logical(q0 + lax.broadcasted_iota(jnp.int32, (1, tq), 1),
                                      SEL_LEN.bit_length() - 1)
        forced = (j == 0) | (j == cur) | (j == cur - 1)
        imp = jnp.where(forced, imp + FORCE_BONUS, imp)
        imp = jnp.where(j <= cur, imp, NEG)
        imp_ref[...] = imp
        cnt = jnp.zeros((n_sel, tq), F32)
        for jj in range(n_sel):
            other = imp_ref[jj:jj + 1, :]
            beats = (other > imp) | ((other == imp) & (j > jj))
            cnt = cnt + jnp.where(beats, 1.0, 0.0)
        nsel = jnp.where((cnt < float(SEL_TOP)) & (j <= cur), 0.0, 1.0)
        nsel = jnp.concatenate([jnp.zeros((NSA_HEAD, tq), F32), nsel,
                                jnp.zeros((LANES - NSA_HEAD - n_sel, tq), F32)], axis=0).T.astype(BF16)
        tiles[i] = dict(o_c=o_c, nsel=nsel, o_s=[], o_w=[])

    def scores(u, _):
        kind, i, r = u
        if kind == "cmp":
            return _dot_nt(tile_q(i), kc_ref[0, 0])
        if kind == "sel":
            return _attention_scores(head_q(i, r) + tiles[i]["nsel"], kf, (i + 1) * tq, dbias)
        w0 = max(i * tq - WINDOW, 0)
        wb = wbias_ref[min(i, 1)]
        sw = _dot_nt(head_q(i, r), kw_ref[0, 0, w0:w0 + span, :]) + wb
        return sw, _rowmax(sw)

    def probs(u, sc):
        kind, i, r = u
        if kind == "cmp":
            return sc
        if kind == "sel":
            return _attention_probs(sc)
        sw, mw = sc
        return jnp.exp2(sw - mw).astype(BF16)

    def finish(u, ps):
        kind, i, r = u
        if kind == "cmp":
            compressed_and_select(i, ps)
        elif kind == "sel":
            tiles[i]["o_s"].append(_attention_out(ps, vf, (i + 1) * tq))
        elif kind == "win":
            w0 = max(i * tq - WINDOW, 0)
            acc_w = _dot(ps, vw_ref[0, 0, w0:w0 + span, :])
            tiles[i]["o_w"].append(acc_w[:, :NSA_HEAD] / acc_w[:, NSA_HEAD:NSA_HEAD + 1])
        if kind == "win" and r == R - 1:
            tile = tiles.pop(i)
            g_hi, g_lo = _split_hilo(gn_ref[0, i * tq:(i + 1) * tq, :])
            o_c = jnp.concatenate([tile["o_c"][r * tq:(r + 1) * tq] for r in range(R)], axis=-1)
            branches = (o_c, jnp.concatenate(tile["o_s"], axis=-1), jnp.concatenate(tile["o_w"], axis=-1))
            out = None
            for br, o_b in enumerate(branches):
                gate = _dot(g_hi, gexp_ref[grp, br]) + _dot(g_lo, gexp_ref[grp, br])
                out = gate * o_b if out is None else out + gate * o_b
            o_ref[0, i * tq:(i + 1) * tq, :] = out.astype(BF16)

    nq = S // tq
    first = ("cmp", 0, 0)
    finish(first, scores(first, None))
    units = []
    for i in range(nq):
        units += [("cmp", i + 1, 0)] if i + 1 < nq else []
        for r in range(R):
            units += [("sel", i, r), ("win", i, r)]
    _software_pipeline(units, (scores, probs, finish))


def _nsa_attention(qn, kc, vc, ks, vs, kw, vw, gn, ovt, gexp, dbias, wbias):
    B, S, _ = qn.shape
    G, R, Dh = NSA_KV_GROUPS, NSA_REP, NSA_HEAD
    ncp = kc.shape[2]
    full = pl.BlockSpec((1, 1, S, HEAD_PAD), lambda b, g: (b, g, 0, 0))
    cmp_spec = lambda w: pl.BlockSpec((1, 1, ncp, w), lambda b, g: (b, g, 0, 0))
    return pl.pallas_call(
        _nsa_kernel,
        grid=(B, G),
        in_specs=[pl.BlockSpec((1, S, R * HEAD_PAD), lambda b, g: (b, 0, g)),
                  cmp_spec(HEAD_PAD), cmp_spec(Dh), full, full, full, full,
                  pl.BlockSpec((1, S, LANES), lambda b, g: (b, 0, 0)),
                  _const_spec(ovt.shape), _const_spec(gexp.shape),
                  _const_spec(dbias.shape), _const_spec(wbias.shape)],
        out_specs=pl.BlockSpec((1, S, R * Dh), lambda b, g: (b, 0, g)),
        out_shape=jax.ShapeDtypeStruct((B, S, G * R * Dh), BF16),
        scratch_shapes=[pltpu.VMEM((S // SEL_LEN, TQ_ATT), F32)],
        compiler_params=pltpu.CompilerParams(dimension_semantics=("arbitrary",) * 2,
                                             vmem_limit_bytes=VMEM_LIMIT),
        name="nsa_attention",
    )(qn, kc, vc, ks, vs, kw, vw, gn, ovt, gexp, dbias, wbias)


def _mla_kernel(q_ref, k_ref, v_ref, dbias_ref, o_ref):
    tq = dbias_ref.shape[0]
    S = q_ref.shape[1]
    dbias = dbias_ref[...]
    units = [(i, hh) for i in range(S // tq) for hh in range(2)]

    def scores(u, _):
        i, hh = u
        cols = slice(HEAD_PAD * hh, HEAD_PAD * (hh + 1))
        q = q_ref[0, i * tq:(i + 1) * tq, cols]
        return _attention_scores(q, lambda a, b: k_ref[0, a:b, cols], (i + 1) * tq, dbias)

    def probs(u, sc):
        return _attention_probs(sc)

    def finish(u, ps):
        i, hh = u
        cols = slice(HEAD_PAD * hh, HEAD_PAD * (hh + 1))
        o = _attention_out(ps, lambda a, b: v_ref[0, a:b, cols], (i + 1) * tq)
        o_ref[0, i * tq:(i + 1) * tq, MLA_V * hh:MLA_V * (hh + 1)] = o.astype(BF16)

    _software_pipeline(units, (scores, probs, finish))


def _mla_attention(qm, km, vm, dbias):
    B, S, _ = qm.shape
    pair = pl.BlockSpec((1, S, 2 * HEAD_PAD), lambda b, h: (b, 0, h))
    return pl.pallas_call(
        _mla_kernel,
        grid=(B, MLA_HEADS // 2),
        in_specs=[pair, pair, pair, _const_spec(dbias.shape)],
        out_specs=pl.BlockSpec((1, S, 2 * MLA_V), lambda b, h: (b, 0, h)),
        out_shape=jax.ShapeDtypeStruct((B, S, MLA_HEADS * MLA_V), BF16),
        compiler_params=pltpu.CompilerParams(dimension_semantics=("arbitrary",) * 2,
                                             vmem_limit_bytes=VMEM_LIMIT),
        name="mla_attention",
    )(qm, km, vm, dbias)


def _out_ffn_kernel(x_ref, om_ref, on_ref, gm_ref, mod_ref, g2_ref,
                    wom_ref, won_ref, wout_ref, wg_ref, wu_ref, wd_ref, o_ref):
    mod = mod_ref[0]
    gt1, sh2, sc2, gt2 = mod[2:3], mod[3:4], mod[4:5], mod[5:6]
    groups = [slice(s, s + OUT_ROWS) for s in range(0, x_ref.shape[1], OUT_ROWS)]
    chunks = [slice(lo, hi) for lo, hi in zip(FF_SPLITS[:-1], FF_SPLITS[1:])]
    heads = [(_dot(om_ref[0, rows], wom_ref[...]), _dot(on_ref[0, rows], won_ref[...])) for rows in groups]
    x1s = []
    for rows, (ym, yn) in zip(groups, heads):
        merged = gm_ref[0, rows, :D_MODEL] * ym + gm_ref[0, rows, D_MODEL:] * yn
        x1s.append(x_ref[0, rows] + gt1 * _dot(merged.astype(BF16), wout_ref[...]))
    gus = []
    for x1 in x1s:
        h2 = (_rms(x1, D_MODEL) * g2_ref[...] * (1.0 + sc2) + sh2).astype(BF16)
        gus.append([(_dot(h2, wg_ref[:, sl]), _dot(h2, wu_ref[:, sl])) for sl in chunks])
    for rows, x1, gu in zip(groups, x1s, gus):
        acc = None
        for sl, (g, u) in zip(chunks, gu):
            d = _dot((g * _sigmoid(g) * u).astype(BF16), wd_ref[sl, :])
            acc = d if acc is None else acc + d
        o_ref[0, rows] = x1 + gt2 * acc


def _out_ffn(x, om, on, gm, mod, g2, wom, won, wout, wg, wu, wd):
    B, S, D = x.shape
    tm = TM_OUT
    tok = lambda w: pl.BlockSpec((1, tm, w), lambda b, i: (b, i, 0))
    wspec = lambda w: pl.BlockSpec(w.shape, lambda b, i: (0, 0), pipeline_mode=pl.Buffered(1))
    return pl.pallas_call(
        _out_ffn_kernel,
        grid=(B, S // tm),
        in_specs=[tok(D), tok(om.shape[2]), tok(on.shape[2]), tok(2 * D),
                  pl.BlockSpec((1, N_MOD, D), lambda b, i: (b, 0, 0)),
                  _const_spec(g2.shape)] + [wspec(w) for w in (wom, won, wout, wg, wu, wd)],
        out_specs=tok(D),
        out_shape=jax.ShapeDtypeStruct((B, S, D), F32),
        compiler_params=pltpu.CompilerParams(dimension_semantics=("arbitrary", "arbitrary"),
                                             vmem_limit_bytes=VMEM_LIMIT),
        name="out_ffn",
    )(x, om, on, gm, mod, g2, wom, won, wout, wg, wu, wd)


def _rope_expansion():
    texp = np.zeros((LANES, 4 * LANES), np.float32)
    trow = np.zeros((1, 4 * LANES), np.float32)
    hm, hn = MLA_ROPE // 2, NSA_ROT // 2
    trow[0, 0:LANES] = 1.0
    trow[0, 2 * LANES:3 * LANES] = 1.0
    for i in range(hm):
        for off, sgn in ((MLA_NOPE + i, -1.0), (MLA_NOPE + hm + i, 1.0)):
            texp[i, off] = 1.0
            trow[0, off] = 0.0
            texp[N_FREQ + i, LANES + off] = sgn
    for i in range(hn):
        for off, sgn in ((i, -1.0), (hn + i, 1.0)):
            texp[hm + i, 2 * LANES + off] = 1.0
            trow[0, 2 * LANES + off] = 0.0
            texp[N_FREQ + hm + i, 3 * LANES + off] = sgn
    return jnp.asarray(texp, BF16), jnp.asarray(trow, F32)


def _mask_tables(S):
    tq = TQ_ATT
    n_chunk = S // CMP_STRIDE
    n_sel = S // SEL_LEN
    starts = np.arange(n_chunk) * CMP_STRIDE
    sel_start = np.arange(LANES) * SEL_LEN
    ovt = ((starts[None, :] < sel_start[:, None] + SEL_LEN) &
           (starts[None, :] + CMP_LEN > sel_start[:, None]) &
           (np.arange(n_chunk)[None, :] < n_chunk - 1) &
           (np.arange(LANES)[:, None] < n_sel))
    gcol = np.arange(LANES)[:, None]
    head = np.arange(NSA_REP * NSA_HEAD)[None, :] // NSA_HEAD
    gexp = np.stack([np.stack([gcol == (g * NSA_REP + head) * N_NSA_BRANCH + br
                               for br in range(N_NSA_BRANCH)]) for g in range(NSA_KV_GROUPS)])
    qi = np.arange(tq)[:, None]
    dbias = np.where(np.arange(tq)[None, :] <= qi, 0.0, NEG)
    kk = np.arange(WINDOW + tq)[None, :]
    band = lambda d: np.where((d >= 0) & (d < WINDOW), 0.0, NEG)
    wbias = np.stack([band(qi - kk), band(qi + WINDOW - kk)])
    return (jnp.asarray(ovt, BF16), jnp.asarray(gexp, BF16),
            jnp.asarray(dbias, F32), jnp.asarray(wbias, F32))


def _pad_heads(w, n_heads, width):
    k = w.shape[0]
    w = w.reshape(k, n_heads, width)
    return jnp.pad(w, ((0, 0), (0, 0), (0, HEAD_PAD - width))).reshape(k, n_heads * HEAD_PAD)


def _pad_row(g, lo=0):
    return jnp.pad(g, (lo, HEAD_PAD - lo - g.shape[0])).reshape(1, HEAD_PAD)


def _layer(x, mod, cs, p):
    B, S, D = x.shape
    w_in = p["w_in"]
    o = 0
    cols = {}
    for name, wdt in (("cq", MLA_Q_LORA), ("ckv", MLA_KV_LORA), ("kpe", MLA_ROPE),
                      ("qn", NSA_HEADS * NSA_HEAD), ("kc", KV_W), ("vc", KV_W), ("ks", KV_W),
                      ("vs", KV_W), ("kw", KV_W), ("vw", KV_W),
                      ("gn", NSA_HEADS * N_NSA_BRANCH), ("gm", 2 * D)):
        cols[name] = w_in[:, o:o + wdt]
        o += wdt
    G = NSA_KV_GROUPS
    n_gate = NSA_HEADS * N_NSA_BRANCH
    zc = lambda n: jnp.zeros((D, n), F32)
    wsm = jnp.concatenate([cols["gn"], zc(MLA_NOPE - n_gate), cols["kpe"], zc(LANES - MLA_QK)], axis=1)
    wkv6 = jnp.concatenate([cols["kc"], cols["vc"]] +
                           [_pad_heads(cols[k], G, NSA_HEAD) for k in ("ks", "vs", "kw", "vw")], axis=1)
    wkvb = p["mla_w_kv_b"].reshape(MLA_KV_LORA, MLA_HEADS, MLA_NOPE + MLA_V)
    wkvb = jnp.concatenate([_pad_heads(wkvb[:, :, :MLA_NOPE].reshape(MLA_KV_LORA, -1), MLA_HEADS, MLA_NOPE),
                            _pad_heads(wkvb[:, :, MLA_NOPE:].reshape(MLA_KV_LORA, -1), MLA_HEADS, MLA_V)], axis=1)
    bf = lambda w: w.astype(BF16)
    row = lambda g: g.reshape(1, -1)
    weights = tuple(bf(w) for w in (cols["cq"], cols["ckv"], wsm, _pad_heads(cols["qn"], NSA_HEADS, NSA_HEAD),
                                    wkv6, cols["gm"],
                                    _pad_heads(p["mla_w_q_b"], MLA_HEADS, MLA_QK), wkvb))
    vone = jnp.tile(jnp.zeros((1, HEAD_PAD), F32).at[0, MLA_V].set(1.0), (1, MLA_HEADS))
    rows = (row(p["mla_q_a_gain"]), row(p["mla_kv_a_gain"]),
            _pad_row(p["mla_q_gain"]), _pad_row(p["mla_k_gain"][:MLA_NOPE]),
            _pad_row(p["mla_k_gain"][MLA_NOPE:], MLA_NOPE),
            _pad_row(p["nsa_q_gain"]), _pad_row(p["nsa_ks_gain"]), _pad_row(p["nsa_kw_gain"]), vone)
    texp, trow = _rope_expansion()
    (qm, km, vm, qn, ks, kw, vs, vw, kcin, vcin, gn, gm) = _inproj(
        x, mod, cs, (row(p["norm1_gain"]), texp, trow), weights, rows)

    n_chunk = S // CMP_STRIDE
    cs_end = cs[:, CMP_LEN - 1::CMP_STRIDE]
    cs_end = jnp.pad(cs_end, ((0, 0), (0, n_chunk - cs_end.shape[1]), (0, 0)))
    w2k = jnp.pad(p["cmp_w2_k"], ((0, 0), (0, HEAD_PAD - NSA_HEAD)))
    kc, vc = _compress(kcin, vcin, p["cmp_pos_k"], p["cmp_pos_v"],
                       bf(p["cmp_w1_k"]), bf(w2k), bf(p["cmp_w1_v"]), bf(p["cmp_w2_v"]),
                       _pad_row(p["nsa_kc_gain"]), cs_end, texp, trow)

    ovt, gexp, dbias, wbias = _mask_tables(S)
    o_nsa = _nsa_attention(qn, kc, vc, ks, vs, kw, vw, gn, ovt, gexp, dbias, wbias)
    o_mla = _mla_attention(qm, km, vm, dbias)

    return _out_ffn(x, o_mla, o_nsa, gm, mod, row(p["norm2_gain"]),
                    bf(p["w_o_mla"]), bf(p["w_o_nsa"]), bf(p["w_out"]),
                    bf(p["ffn_w_gate"]), bf(p["ffn_w_up"]), bf(p["ffn_w_down"]))


def kernel(x, c, positions, ada_w, ada_b, norm1_gain, w_in, mla_q_a_gain, mla_w_q_b, mla_kv_a_gain, mla_w_kv_b, mla_q_gain, mla_k_gain, nsa_q_gain, nsa_kc_gain, nsa_ks_gain, nsa_kw_gain, cmp_pos_k, cmp_w1_k, cmp_w2_k, cmp_pos_v, cmp_w1_v, cmp_w2_v, w_o_mla, w_o_nsa, w_out, norm2_gain, ffn_w_gate, ffn_w_up, ffn_w_down):
    params = dict(norm1_gain=norm1_gain, w_in=w_in, mla_q_a_gain=mla_q_a_gain, mla_w_q_b=mla_w_q_b,
                  mla_kv_a_gain=mla_kv_a_gain, mla_w_kv_b=mla_w_kv_b, mla_q_gain=mla_q_gain,
                  mla_k_gain=mla_k_gain, nsa_q_gain=nsa_q_gain, nsa_kc_gain=nsa_kc_gain,
                  nsa_ks_gain=nsa_ks_gain, nsa_kw_gain=nsa_kw_gain, cmp_pos_k=cmp_pos_k,
                  cmp_w1_k=cmp_w1_k, cmp_w2_k=cmp_w2_k, cmp_pos_v=cmp_pos_v, cmp_w1_v=cmp_w1_v,
                  cmp_w2_v=cmp_w2_v, w_o_mla=w_o_mla, w_o_nsa=w_o_nsa, w_out=w_out,
                  norm2_gain=norm2_gain, ffn_w_gate=ffn_w_gate, ffn_w_up=ffn_w_up, ffn_w_down=ffn_w_down)
    B = x.shape[0]
    inv_m = ROPE_THETA ** (-jnp.arange(0, MLA_ROPE, 2, dtype=F32) / MLA_ROPE)
    inv_n = ROPE_THETA ** (-jnp.arange(0, NSA_ROT, 2, dtype=F32) / NSA_ROT)
    n_unused = N_FREQ - inv_m.shape[0] - inv_n.shape[0]
    cs = _rope_tables(positions, jnp.concatenate([inv_m, inv_n, jnp.zeros((n_unused,), F32)]))
    depth = ada_w.shape[0]
    for l in range(depth):
        mod = _ada(c, ada_w, ada_b, l).reshape(B, N_MOD, D_MODEL)
        x = _layer(x, mod, cs, {k: v[l] for k, v in params.items()})
    return x
```

```python
import numpy as np
import jax
import jax.numpy as jnp
from jax import lax
from jax.experimental import pallas as pl
from jax.experimental.pallas import tpu as pltpu

F32 = jnp.float32
BF16 = jnp.bfloat16

D_MODEL = 1024
ROPE_THETA = 500000.0
EPS = 1e-6
NEG = -1e30
LOG2E = 1.4426950408889634

MLA_HEADS = 8
MLA_NOPE = 64
MLA_ROPE = 32
MLA_QK = MLA_NOPE + MLA_ROPE
MLA_V = 64
MLA_Q_LORA = 768
MLA_KV_LORA = 256

NSA_HEADS = 8
NSA_KV_GROUPS = 2
NSA_REP = NSA_HEADS // NSA_KV_GROUPS
NSA_HEAD = 64
NSA_ROT = NSA_HEAD // 4
CMP_LEN = 32
CMP_STRIDE = 16
CMP_HIDDEN = 256
SEL_LEN = 64
SEL_TOP = 8
WINDOW = 256
N_NSA_BRANCH = 3
FORCE_BONUS = 1e4
KV_W = NSA_KV_GROUPS * NSA_HEAD

D_FF = -(-8 * D_MODEL // (3 * 256)) * 256
N_MOD = 6
LANES = 128
HEAD_PAD = LANES
N_FREQ = MLA_ROPE // 2 + NSA_ROT // 2

TM_IN = 512
IN_ROWS = 256
TQ_ATT = 256
TM_OUT = 512
OUT_ROWS = 256
MXU_TILE = 256
CMP_PACK = MXU_TILE // NSA_HEAD
FF_SPLITS = (0, 6 * MXU_TILE, D_FF)
VMEM_LIMIT = 56 * 1024 * 1024


def _dot(a, b):
    return jnp.dot(a, b, preferred_element_type=F32)


def _dot_nt(a, b):
    return lax.dot_general(a, b, (((1,), (1,)), ((), ())), preferred_element_type=F32)


def _split_hilo(a):
    hi = a.astype(BF16)
    return hi, (a - hi.astype(F32)).astype(BF16)


def _dot_hilo(a, m):
    hi, lo = _split_hilo(a)
    return _dot(hi, m) + _dot(lo, m)


def _sigmoid(v):
    return 1.0 / (1.0 + jnp.exp(-v))


def _rms(v, n):
    return v * lax.rsqrt(jnp.sum(v * v, axis=-1, keepdims=True) * (1.0 / n) + EPS)


def _rope(v, cos_v, sin_v, lo, half):
    lane = lax.broadcasted_iota(jnp.int32, v.shape, 1)
    is_x1 = (lane >= lo) & (lane < lo + half)
    rot = jnp.where(is_x1, pltpu.roll(v, LANES - half, 1), pltpu.roll(v, half, 1))
    return v * cos_v + rot * sin_v


def _rope_multipliers(cs, texp_ref, trow_ref):
    tabs = _dot_hilo(cs, texp_ref[...]) + trow_ref[...]
    return tuple(tabs[:, LANES * i:LANES * (i + 1)] for i in range(4))


def _const_spec(shape, single_buffer=False):
    nd = len(shape)
    mode = {"pipeline_mode": pl.Buffered(1)} if single_buffer else {}
    return pl.BlockSpec(shape, lambda *_: (0,) * nd, **mode)


def _rowmax(s):
    return jnp.max(s, axis=-1, keepdims=True)


def _attention_scores(q, k_ref, kmax, dbias):
    k0 = kmax - dbias.shape[1]
    s_d = _dot_nt(q, k_ref(k0, kmax)) + dbias
    m = _rowmax(s_d)
    s_m = None
    if k0 > 0:
        s_m = _dot_nt(q, k_ref(0, k0))
        m = jnp.maximum(m, _rowmax(s_m))
    return s_m, s_d, m


def _attention_probs(scores):
    s_m, s_d, m = scores
    p_m = None if s_m is None else jnp.exp2(s_m - m).astype(BF16)
    return p_m, jnp.exp2(s_d - m).astype(BF16)


def _attention_out(probs, v_ref, kmax):
    p_m, p_d = probs
    k0 = kmax - p_d.shape[1]
    acc = _dot(p_d, v_ref(k0, kmax))
    if p_m is not None:
        acc = acc + _dot(p_m, v_ref(0, k0))
    return acc[:, :NSA_HEAD] / acc[:, NSA_HEAD:NSA_HEAD + 1]


def _software_pipeline(units, stages):
    last = len(stages) - 1
    carry = {}
    for step in range(-last, len(units)):
        for s, stage in enumerate(stages):
            n = step + last - s
            if 0 <= n < len(units):
                carry[n] = stage(units[n], carry.get(n))


def _rope_kernel(pos_ref, inv_ref, cs_ref):
    ang = pos_ref[0].astype(F32) * inv_ref[...]
    nf, S = ang.shape
    rows = jnp.concatenate([jnp.cos(ang), jnp.sin(ang), jnp.zeros((LANES - 2 * nf, S), F32)], axis=0)
    cs_ref[0] = rows.T


def _rope_tables(positions, inv):
    B, S = positions.shape
    nf = inv.shape[0]
    return pl.pallas_call(
        _rope_kernel,
        grid=(B,),
        in_specs=[pl.BlockSpec((1, 1, S), lambda b: (b, 0, 0)),
                  _const_spec((nf, 1))],
        out_specs=pl.BlockSpec((1, S, LANES), lambda b: (b, 0, 0)),
        out_shape=jax.ShapeDtypeStruct((B, S, LANES), F32),
        name="rope_tables",
    )(positions.reshape(B, 1, S), inv.reshape(nf, 1))


def _ada_kernel(c_ref, w_ref, b_ref, o_ref):
    c = c_ref[...]
    sc = c * _sigmoid(c)
    o_ref[...] = _dot(sc.astype(BF16), w_ref[0].astype(BF16)) + b_ref[0]


def _ada(c, w, b, layer):
    B, D = c.shape
    N = w.shape[2]
    tn = D_MODEL
    return pl.pallas_call(
        _ada_kernel,
        grid=(N // tn,),
        in_specs=[_const_spec((B, D)),
                  pl.BlockSpec((1, D, tn), lambda j: (layer, 0, j)),
                  pl.BlockSpec((1, 1, tn), lambda j: (layer, 0, j))],
        out_specs=pl.BlockSpec((B, tn), lambda j: (0, j)),
        out_shape=jax.ShapeDtypeStruct((B, N), F32),
        name="ada_mod",
    )(c, w, b.reshape(b.shape[0], 1, N))


def _inproj_kernel(x_ref, mod_ref, cs_ref, g1_ref, texp_ref, trow_ref,
                   wcq_ref, wckv_ref, wsm_ref, wqn_ref, wkv6_ref, wgm_ref,
                   qag_ref, wqb_ref, kvag_ref, wkvb_ref,
                   mqg_ref, mkn_ref, mkr_ref, nqg_ref, nksg_ref, nkwg_ref, vone_ref,
                   qm_ref, km_ref, vm_ref, qn_ref, ks_ref, kw_ref, vs_ref, vw_ref,
                   kcin_ref, vcin_ref, gn_ref, gm_ref):
    tm = x_ref.shape[1]
    mod = mod_ref[0]
    sh1, sc1 = mod[0:1], mod[1:2]
    lane = lax.broadcasted_iota(jnp.int32, (IN_ROWS, LANES), 1)
    blk = lambda a, i: a[:, HEAD_PAD * i:HEAD_PAD * (i + 1)]
    hm = MLA_ROPE // 2
    hn = NSA_ROT // 2

    def front(rows):
        h = _rms(x_ref[0, rows], D_MODEL) * g1_ref[...] * (1.0 + sc1) + sh1
        hb = h.astype(BF16)
        return dict(
            hb=hb,
            tabs=_rope_multipliers(cs_ref[0, rows], texp_ref, trow_ref),
            cq=_dot(hb, wcq_ref[...]),
            ckv=_dot(hb, wckv_ref[...]),
            zs=_dot(hb, wsm_ref[...]),
            qn=_dot(hb, wqn_ref[...]),
            kv6=_dot(hb, wkv6_ref[...]))

    def middle(st):
        cqn = (_rms(st["cq"], MLA_Q_LORA) * qag_ref[...]).astype(BF16)
        st["q"] = _dot(cqn, wqb_ref[...])
        st["gates"] = _dot(st["hb"], wgm_ref[...])
        ckvn = (_rms(st["ckv"], MLA_KV_LORA) * kvag_ref[...]).astype(BF16)
        st["kv"] = _dot(ckvn, wkvb_ref[...])

    def back(rows, st):
        cos_m, sin_m, cos_n, sin_n = st["tabs"]
        zs, qn, kv6, q, kv = st["zs"], st["qn"], st["kv6"], st["q"], st["kv"]

        nqg = nqg_ref[...]
        n_scale = NSA_HEAD ** -0.5 * LOG2E
        for hd in range(NSA_HEADS):
            qh = _rope(_rms(blk(qn, hd), NSA_HEAD) * nqg, cos_n, sin_n, 0, hn) * n_scale
            qn_ref[0, rows, HEAD_PAD * hd:HEAD_PAD * (hd + 1)] = qh.astype(BF16)

        tok = pl.program_id(1) * tm + rows.start + lax.broadcasted_iota(jnp.int32, (IN_ROWS, 1), 0)
        sblk = lax.shift_right_logical(tok, SEL_LEN.bit_length() - 1)
        ind = jnp.where(lane - NSA_HEAD == sblk, NEG, 0.0)
        vone = vone_ref[:, 0:HEAD_PAD]
        nksg, nkwg = nksg_ref[...], nkwg_ref[...]
        kcin_ref[0, rows] = blk(kv6, 0)
        vcin_ref[0, rows] = blk(kv6, 1)
        for g in range(NSA_KV_GROUPS):
            ks = _rope(_rms(blk(kv6, 2 + g), NSA_HEAD) * nksg, cos_n, sin_n, 0, hn)
            ks_ref[0, g, rows] = (ks + ind).astype(BF16)
            vs_ref[0, g, rows] = (blk(kv6, 4 + g) + vone).astype(BF16)
            kw = _rope(_rms(blk(kv6, 6 + g), NSA_HEAD) * nkwg, cos_n, sin_n, 0, hn)
            kw_ref[0, g, rows] = kw.astype(BF16)
            vw_ref[0, g, rows] = (blk(kv6, 8 + g) + vone).astype(BF16)

        mqg = mqg_ref[...]
        m_scale = MLA_QK ** -0.5 * LOG2E
        for hd in range(MLA_HEADS):
            qh = _rope(_rms(blk(q, hd), MLA_QK) * mqg, cos_m, sin_m, MLA_NOPE, hm) * m_scale
            qm_ref[0, rows, HEAD_PAD * hd:HEAD_PAD * (hd + 1)] = qh.astype(BF16)

        gn_ref[0, rows] = _sigmoid(zs)
        gm_ref[0, rows] = _sigmoid(st["gates"]).astype(BF16)

        kpe = jnp.where((lane >= MLA_NOPE) & (lane < MLA_QK), zs, 0.0)
        kpe_ss = jnp.sum(kpe * kpe, axis=-1, keepdims=True)
        kr = _rope(kpe * mkr_ref[...], cos_m, sin_m, MLA_NOPE, hm)
        mkn = mkn_ref[...]
        for hd in range(MLA_HEADS):
            kn = blk(kv, hd)
            inv = lax.rsqrt((jnp.sum(kn * kn, axis=-1, keepdims=True) + kpe_ss) * (1.0 / MLA_QK) + EPS)
            km_ref[0, rows, HEAD_PAD * hd:HEAD_PAD * (hd + 1)] = ((kn * mkn + kr) * inv).astype(BF16)
        vm_ref[0, rows] = (kv[:, MLA_HEADS * HEAD_PAD:] + vone_ref[...]).astype(BF16)

    groups = [slice(s, s + IN_ROWS) for s in range(0, tm, IN_ROWS)]
    states = [front(rows) for rows in groups]
    for st in states:
        middle(st)
    for rows, st in zip(groups, states):
        back(rows, st)


def _inproj(x, mod, cs, consts, weights, rows):
    B, S, D = x.shape
    tm = TM_IN
    tok = lambda w: pl.BlockSpec((1, tm, w), lambda b, i: (b, i, 0))
    head = lambda n, w: pl.BlockSpec((1, n, tm, w), lambda b, i: (b, 0, i, 0))
    operands = list(consts) + list(weights[:6]) + [rows[0], weights[6], rows[1], weights[7]] + list(rows[2:])
    in_specs = [tok(D), pl.BlockSpec((1, N_MOD, D), lambda b, i: (b, 0, 0)), tok(LANES)]
    in_specs += [_const_spec(a.shape, single_buffer=True) for a in operands]
    G = NSA_KV_GROUPS
    sds = jax.ShapeDtypeStruct
    wide = MLA_HEADS * HEAD_PAD
    outs = [
        (tok(wide), sds((B, S, wide), BF16)),
        (tok(wide), sds((B, S, wide), BF16)),
        (tok(wide), sds((B, S, wide), BF16)),
        (tok(wide), sds((B, S, wide), BF16)),
        (head(G, HEAD_PAD), sds((B, G, S, HEAD_PAD), BF16)),
        (head(G, HEAD_PAD), sds((B, G, S, HEAD_PAD), BF16)),
        (head(G, HEAD_PAD), sds((B, G, S, HEAD_PAD), BF16)),
        (head(G, HEAD_PAD), sds((B, G, S, HEAD_PAD), BF16)),
        (tok(KV_W), sds((B, S, KV_W), F32)),
        (tok(KV_W), sds((B, S, KV_W), F32)),
        (tok(LANES), sds((B, S, LANES), F32)),
        (tok(2 * D), sds((B, S, 2 * D), BF16)),
    ]
    return pl.pallas_call(
        _inproj_kernel,
        grid=(B, S // tm),
        in_specs=in_specs,
        out_specs=[o[0] for o in outs],
        out_shape=[o[1] for o in outs],
        compiler_params=pltpu.CompilerParams(dimension_semantics=("arbitrary", "arbitrary"),
                                             vmem_limit_bytes=VMEM_LIMIT),
        name="inproj_prep",
    )(x, mod, cs, *operands)


def _compress_kernel(kcin_ref, vcin_ref, pk_ref, pv_ref, w1k_ref, w2k_ref, w1v_ref, w2v_ref,
                     kcg_ref, cs_ref, texp_ref, trow_ref, kc_ref, vc_ref):
    n = kcin_ref.shape[1] // CMP_STRIDE
    G = NSA_KV_GROUPS

    def hidden(cin_ref, pos_ref, w1_ref):
        a, b = [None] * G, [None] * G
        for l in range(0, CMP_STRIDE, CMP_PACK):
            toks = [cin_ref[0, pl.ds(l + j, n, stride=CMP_STRIDE), :] for j in range(CMP_PACK)]
            for g in range(G):
                lanes = slice(NSA_HEAD * g, NSA_HEAD * (g + 1))
                pack = lambda o: jnp.concatenate(
                    [toks[j][:, lanes] + pos_ref[o + l + j:o + l + j + 1] for j in range(CMP_PACK)],
                    axis=-1).astype(BF16)
                rows = lambda o: slice(NSA_HEAD * (o + l), NSA_HEAD * (o + l + CMP_PACK))
                da = _dot(pack(0), w1_ref[rows(0), :])
                db = _dot(pack(CMP_STRIDE), w1_ref[rows(CMP_STRIDE), :])
                a[g], b[g] = (da, db) if a[g] is None else (a[g] + da, b[g] + db)
        hid = [a[g] + pltpu.roll(b[g], n - 1, 0) for g in range(G)]
        return [(h * _sigmoid(h)).astype(BF16) for h in hid]

    _, _, cos_n, sin_n = _rope_multipliers(cs_ref[0], texp_ref, trow_ref)
    hk = hidden(kcin_ref, pk_ref, w1k_ref)
    hv = hidden(vcin_ref, pv_ref, w1v_ref)
    for g in range(G):
        kg = _rms(_dot(hk[g], w2k_ref[...]), NSA_HEAD) * kcg_ref[...]
        kc_ref[0, g] = _rope(kg, cos_n, sin_n, 0, NSA_ROT // 2).astype(BF16)
        vc_ref[0, g] = _dot(hv[g], w2v_ref[...]).astype(BF16)


def _compress(kcin, vcin, pk, pv, w1k, w2k, w1v, w2v, kcg, cs_end, texp, trow):
    B, S, w = kcin.shape
    G = NSA_KV_GROUPS
    n = S // CMP_STRIDE
    oblk = lambda wd: pl.BlockSpec((1, G, n, wd), lambda b: (b, 0, 0, 0))
    consts = (pk, pv, w1k, w2k, w1v, w2v, kcg)
    return pl.pallas_call(
        _compress_kernel,
        grid=(B,),
        in_specs=[pl.BlockSpec((1, S, w), lambda b: (b, 0, 0))] * 2 + [_const_spec(a.shape) for a in consts] +
                 [pl.BlockSpec((1, n, LANES), lambda b: (b, 0, 0)),
                  _const_spec(texp.shape), _const_spec(trow.shape)],
        out_specs=[oblk(HEAD_PAD), oblk(NSA_HEAD)],
        out_shape=[jax.ShapeDtypeStruct((B, G, n, HEAD_PAD), BF16),
                   jax.ShapeDtypeStruct((B, G, n, NSA_HEAD), BF16)],
        name="nsa_compress",
    )(kcin, vcin, *consts, cs_end, texp, trow)


def _nsa_kernel(q_ref, kc_ref, vc_ref, ks_ref, vs_ref, kw_ref, vw_ref, gn_ref,
                ovt_ref, gexp_ref, dbias_ref, wbias_ref, o_ref, imp_ref):
    tq = dbias_ref.shape[0]
    S = q_ref.shape[1]
    R = NSA_REP
    M = R * tq
    n_sel = imp_ref.shape[0]
    ncp = kc_ref.shape[2]
    span = WINDOW + tq
    grp = pl.program_id(1)
    dbias = dbias_ref[...]
    kf = lambda a, b: ks_ref[0, 0, a:b, :]
    vf = lambda a, b: vs_ref[0, 0, a:b, :]
    row = lax.broadcasted_iota(jnp.int32, (M, 1), 0)
    n_idx = lax.broadcasted_iota(jnp.int32, (M, ncp), 1)
    j = lax.broadcasted_iota(jnp.int32, (n_sel, tq), 0)
    head_q = lambda i, r: q_ref[0, i * tq:(i + 1) * tq, HEAD_PAD * r:HEAD_PAD * (r + 1)]
    tile_q = lambda i: jnp.concatenate([head_q(i, r) for r in range(R)], axis=0)
    tiles = {}

    def compressed_and_select(i, s):
        q0 = i * tq
        t = q0 + jnp.bitwise_and(row, tq - 1)
        valid = (n_idx * CMP_STRIDE + (CMP_LEN - 1)) <= t
        sm = jnp.where(valid, s, NEG)
        e = jnp.where(valid, jnp.exp2(sm - _rowmax(sm)), 0.0)
        den = jnp.sum(e, axis=-1, keepdims=True)
        p_c = e / jnp.where(den > 0.0, den, 1.0)
        o_c = _dot(p_c.astype(BF16), vc_ref[0, 0])
        psum = p_c[0:tq]
        for r in range(1, R):
            psum = psum + p_c[r * tq:(r + 1) * tq]
        hi, lo = _split_hilo(psum.T)
        imp = (_dot(ovt_ref[...], hi) + _dot(ovt_ref[...], lo))[0:n_sel]
        cur = lax.shift_right_logical(q0 + lax.broadcasted_iota(jnp.int32, (1, tq), 1),
                                      SEL_LEN.bit_length() - 1)
        forced = (j == 0) | (j == cur) | (j == cur - 1)
        imp = jnp.where(forced, imp + FORCE_BONUS, imp)
        imp = jnp.where(j <= cur, imp, NEG)
        imp_ref[...] = imp
        cnt = jnp.zeros((n_sel, tq), F32)
        for jj in range(n_sel):
            other = imp_ref[jj:jj + 1, :]
            beats = (other > imp) | ((other == imp) & (j > jj))
            cnt = cnt + jnp.where(beats, 1.0, 0.0)
        nsel = jnp.where((cnt < float(SEL_TOP)) & (j <= cur), 0.0, 1.0)
        nsel = jnp.concatenate([jnp.zeros((NSA_HEAD, tq), F32), nsel,
                                jnp.zeros((LANES - NSA_HEAD - n_sel, tq), F32)], axis=0).T.astype(BF16)
        tiles[i] = dict(o_c=o_c, nsel=nsel, o_s=[], o_w=[])

    def scores(u, _):
        kind, i, r = u
        if kind == "cmp":
            return _dot_nt(tile_q(i), kc_ref[0, 0])
        if kind == "sel":
            return _attention_scores(head_q(i, r) + tiles[i]["nsel"], kf, (i + 1) * tq, dbias)
        w0 = max(i * tq - WINDOW, 0)
        wb = wbias_ref[min(i, 1)]
        sw = _dot_nt(head_q(i, r), kw_ref[0, 0, w0:w0 + span, :]) + wb
        return sw, _rowmax(sw)

    def probs(u, sc):
        kind, i, r = u
        if kind == "cmp":
            return sc
        if kind == "sel":
            return _attention_probs(sc)
        sw, mw = sc
        return jnp.exp2(sw - mw).astype(BF16)

    def finish(u, ps):
        kind, i, r = u
        if kind == "cmp":
            compressed_and_select(i, ps)
        elif kind == "sel":
            tiles[i]["o_s"].append(_attention_out(ps, vf, (i + 1) * tq))
        elif kind == "win":
            w0 = max(i * tq - WINDOW, 0)
            acc_w = _dot(ps, vw_ref[0, 0, w0:w0 + span, :])
            tiles[i]["o_w"].append(acc_w[:, :NSA_HEAD] / acc_w[:, NSA_HEAD:NSA_HEAD + 1])
        if kind == "win" and r == R - 1:
            tile = tiles.pop(i)
            g_hi, g_lo = _split_hilo(gn_ref[0, i * tq:(i + 1) * tq, :])
            o_c = jnp.concatenate([tile["o_c"][r * tq:(r + 1) * tq] for r in range(R)], axis=-1)
            branches = (o_c, jnp.concatenate(tile["o_s"], axis=-1), jnp.concatenate(tile["o_w"], axis=-1))
            out = None
            for br, o_b in enumerate(branches):
                gate = _dot(g_hi, gexp_ref[grp, br]) + _dot(g_lo, gexp_ref[grp, br])
                out = gate * o_b if out is None else out + gate * o_b
            o_ref[0, i * tq:(i + 1) * tq, :] = out.astype(BF16)

    nq = S // tq
    first = ("cmp", 0, 0)
    finish(first, scores(first, None))
    units = []
    for i in range(nq):
        units += [("cmp", i + 1, 0)] if i + 1 < nq else []
        for r in range(R):
            units += [("sel", i, r), ("win", i, r)]
    _software_pipeline(units, (scores, probs, finish))


def _nsa_attention(qn, kc, vc, ks, vs, kw, vw, gn, ovt, gexp, dbias, wbias):
    B, S, _ = qn.shape
    G, R, Dh = NSA_KV_GROUPS, NSA_REP, NSA_HEAD
    ncp = kc.shape[2]
    full = pl.BlockSpec((1, 1, S, HEAD_PAD), lambda b, g: (b, g, 0, 0))
    cmp_spec = lambda w: pl.BlockSpec((1, 1, ncp, w), lambda b, g: (b, g, 0, 0))
    return pl.pallas_call(
        _nsa_kernel,
        grid=(B, G),
        in_specs=[pl.BlockSpec((1, S, R * HEAD_PAD), lambda b, g: (b, 0, g)),
                  cmp_spec(HEAD_PAD), cmp_spec(Dh), full, full, full, full,
                  pl.BlockSpec((1, S, LANES), lambda b, g: (b, 0, 0)),
                  _const_spec(ovt.shape), _const_spec(gexp.shape),
                  _const_spec(dbias.shape), _const_spec(wbias.shape)],
        out_specs=pl.BlockSpec((1, S, R * Dh), lambda b, g: (b, 0, g)),
        out_shape=jax.ShapeDtypeStruct((B, S, G * R * Dh), BF16),
        scratch_shapes=[pltpu.VMEM((S // SEL_LEN, TQ_ATT), F32)],
        compiler_params=pltpu.CompilerParams(dimension_semantics=("arbitrary",) * 2,
                                             vmem_limit_bytes=VMEM_LIMIT),
        name="nsa_attention",
    )(qn, kc, vc, ks, vs, kw, vw, gn, ovt, gexp, dbias, wbias)


def _mla_kernel(q_ref, k_ref, v_ref, dbias_ref, o_ref):
    tq = dbias_ref.shape[0]
    S = q_ref.shape[1]
    dbias = dbias_ref[...]
    units = [(i, hh) for i in range(S // tq) for hh in range(2)]

    def scores(u, _):
        i, hh = u
        cols = slice(HEAD_PAD * hh, HEAD_PAD * (hh + 1))
        q = q_ref[0, i * tq:(i + 1) * tq, cols]
        return _attention_scores(q, lambda a, b: k_ref[0, a:b, cols], (i + 1) * tq, dbias)

    def probs(u, sc):
        return _attention_probs(sc)

    def finish(u, ps):
        i, hh = u
        cols = slice(HEAD_PAD * hh, HEAD_PAD * (hh + 1))
        o = _attention_out(ps, lambda a, b: v_ref[0, a:b, cols], (i + 1) * tq)
        o_ref[0, i * tq:(i + 1) * tq, MLA_V * hh:MLA_V * (hh + 1)] = o.astype(BF16)

    _software_pipeline(units, (scores, probs, finish))


def _mla_attention(qm, km, vm, dbias):
    B, S, _ = qm.shape
    pair = pl.BlockSpec((1, S, 2 * HEAD_PAD), lambda b, h: (b, 0, h))
    return pl.pallas_call(
        _mla_kernel,
        grid=(B, MLA_HEADS // 2),
        in_specs=[pair, pair, pair, _const_spec(dbias.shape)],
        out_specs=pl.BlockSpec((1, S, 2 * MLA_V), lambda b, h: (b, 0, h)),
        out_shape=jax.ShapeDtypeStruct((B, S, MLA_HEADS * MLA_V), BF16),
        compiler_params=pltpu.CompilerParams(dimension_semantics=("arbitrary",) * 2,
                                             vmem_limit_bytes=VMEM_LIMIT),
        name="mla_attention",
    )(qm, km, vm, dbias)


def _out_ffn_kernel(x_ref, om_ref, on_ref, gm_ref, mod_ref, g2_ref,
                    wom_ref, won_ref, wout_ref, wg_ref, wu_ref, wd_ref, o_ref):
    mod = mod_ref[0]
    gt1, sh2, sc2, gt2 = mod[2:3], mod[3:4], mod[4:5], mod[5:6]
    groups = [slice(s, s + OUT_ROWS) for s in range(0, x_ref.shape[1], OUT_ROWS)]
    chunks = [slice(lo, hi) for lo, hi in zip(FF_SPLITS[:-1], FF_SPLITS[1:])]
    heads = [(_dot(om_ref[0, rows], wom_ref[...]), _dot(on_ref[0, rows], won_ref[...])) for rows in groups]
    x1s = []
    for rows, (ym, yn) in zip(groups, heads):
        merged = gm_ref[0, rows, :D_MODEL] * ym + gm_ref[0, rows, D_MODEL:] * yn
        x1s.append(x_ref[0, rows] + gt1 * _dot(merged.astype(BF16), wout_ref[...]))
    gus = []
    for x1 in x1s:
        h2 = (_rms(x1, D_MODEL) * g2_ref[...] * (1.0 + sc2) + sh2).astype(BF16)
        gus.append([(_dot(h2, wg_ref[:, sl]), _dot(h2, wu_ref[:, sl])) for sl in chunks])
    for rows, x1, gu in zip(groups, x1s, gus):
        acc = None
        for sl, (g, u) in zip(chunks, gu):
            d = _dot((g * _sigmoid(g) * u).astype(BF16), wd_ref[sl, :])
            acc = d if acc is None else acc + d
        o_ref[0, rows] = x1 + gt2 * acc


def _out_ffn(x, om, on, gm, mod, g2, wom, won, wout, wg, wu, wd):
    B, S, D = x.shape
    tm = TM_OUT
    tok = lambda w: pl.BlockSpec((1, tm, w), lambda b, i: (b, i, 0))
    wspec = lambda w: pl.BlockSpec(w.shape, lambda b, i: (0, 0), pipeline_mode=pl.Buffered(1))
    return pl.pallas_call(
        _out_ffn_kernel,
        grid=(B, S // tm),
        in_specs=[tok(D), tok(om.shape[2]), tok(on.shape[2]), tok(2 * D),
                  pl.BlockSpec((1, N_MOD, D), lambda b, i: (b, 0, 0)),
                  _const_spec(g2.shape)] + [wspec(w) for w in (wom, won, wout, wg, wu, wd)],
        out_specs=tok(D),
        out_shape=jax.ShapeDtypeStruct((B, S, D), F32),
        compiler_params=pltpu.CompilerParams(dimension_semantics=("arbitrary", "arbitrary"),
                                             vmem_limit_bytes=VMEM_LIMIT),
        name="out_ffn",
    )(x, om, on, gm, mod, g2, wom, won, wout, wg, wu, wd)


def _rope_expansion():
    texp = np.zeros((LANES, 4 * LANES), np.float32)
    trow = np.zeros((1, 4 * LANES), np.float32)
    hm, hn = MLA_ROPE // 2, NSA_ROT // 2
    trow[0, 0:LANES] = 1.0
    trow[0, 2 * LANES:3 * LANES] = 1.0
    for i in range(hm):
        for off, sgn in ((MLA_NOPE + i, -1.0), (MLA_NOPE + hm + i, 1.0)):
            texp[i, off] = 1.0
            trow[0, off] = 0.0
            texp[N_FREQ + i, LANES + off] = sgn
    for i in range(hn):
        for off, sgn in ((i, -1.0), (hn + i, 1.0)):
            texp[hm + i, 2 * LANES + off] = 1.0
            trow[0, 2 * LANES + off] = 0.0
            texp[N_FREQ + hm + i, 3 * LANES + off] = sgn
    return jnp.asarray(texp, BF16), jnp.asarray(trow, F32)


def _mask_tables(S):
    tq = TQ_ATT
    n_chunk = S // CMP_STRIDE
    n_sel = S // SEL_LEN
    starts = np.arange(n_chunk) * CMP_STRIDE
    sel_start = np.arange(LANES) * SEL_LEN
    ovt = ((starts[None, :] < sel_start[:, None] + SEL_LEN) &
           (starts[None, :] + CMP_LEN > sel_start[:, None]) &
           (np.arange(n_chunk)[None, :] < n_chunk - 1) &
           (np.arange(LANES)[:, None] < n_sel))
    gcol = np.arange(LANES)[:, None]
    head = np.arange(NSA_REP * NSA_HEAD)[None, :] // NSA_HEAD
    gexp = np.stack([np.stack([gcol == (g * NSA_REP + head) * N_NSA_BRANCH + br
                               for br in range(N_NSA_BRANCH)]) for g in range(NSA_KV_GROUPS)])
    qi = np.arange(tq)[:, None]
    dbias = np.where(np.arange(tq)[None, :] <= qi, 0.0, NEG)
    kk = np.arange(WINDOW + tq)[None, :]
    band = lambda d: np.where((d >= 0) & (d < WINDOW), 0.0, NEG)
    wbias = np.stack([band(qi - kk), band(qi + WINDOW - kk)])
    return (jnp.asarray(ovt, BF16), jnp.asarray(gexp, BF16),
            jnp.asarray(dbias, F32), jnp.asarray(wbias, F32))


def _pad_heads(w, n_heads, width):
    k = w.shape[0]
    w = w.reshape(k, n_heads, width)
    return jnp.pad(w, ((0, 0), (0, 0), (0, HEAD_PAD - width))).reshape(k, n_heads * HEAD_PAD)


def _pad_row(g, lo=0):
    return jnp.pad(g, (lo, HEAD_PAD - lo - g.shape[0])).reshape(1, HEAD_PAD)


def _layer(x, mod, cs, p):
    B, S, D = x.shape
    w_in = p["w_in"]
    o = 0
    cols = {}
    for name, wdt in (("cq", MLA_Q_LORA), ("ckv", MLA_KV_LORA), ("kpe", MLA_ROPE),
                      ("qn", NSA_HEADS * NSA_HEAD), ("kc", KV_W), ("vc", KV_W), ("ks", KV_W),
                      ("vs", KV_W), ("kw", KV_W), ("vw", KV_W),
                      ("gn", NSA_HEADS * N_NSA_BRANCH), ("gm", 2 * D)):
        cols[name] = w_in[:, o:o + wdt]
        o += wdt
    G = NSA_KV_GROUPS
    n_gate = NSA_HEADS * N_NSA_BRANCH
    zc = lambda n: jnp.zeros((D, n), F32)
    wsm = jnp.concatenate([cols["gn"], zc(MLA_NOPE - n_gate), cols["kpe"], zc(LANES - MLA_QK)], axis=1)
    wkv6 = jnp.concatenate([cols["kc"], cols["vc"]] +
                           [_pad_heads(cols[k], G, NSA_HEAD) for k in ("ks", "vs", "kw", "vw")], axis=1)
    wkvb = p["mla_w_kv_b"].reshape(MLA_KV_LORA, MLA_HEADS, MLA_NOPE + MLA_V)
    wkvb = jnp.concatenate([_pad_heads(wkvb[:, :, :MLA_NOPE].reshape(MLA_KV_LORA, -1), MLA_HEADS, MLA_NOPE),
                            _pad_heads(wkvb[:, :, MLA_NOPE:].reshape(MLA_KV_LORA, -1), MLA_HEADS, MLA_V)], axis=1)
    bf = lambda w: w.astype(BF16)
    row = lambda g: g.reshape(1, -1)
    weights = tuple(bf(w) for w in (cols["cq"], cols["ckv"], wsm, _pad_heads(cols["qn"], NSA_HEADS, NSA_HEAD),
                                    wkv6, cols["gm"],
                                    _pad_heads(p["mla_w_q_b"], MLA_HEADS, MLA_QK), wkvb))
    vone = jnp.tile(jnp.zeros((1, HEAD_PAD), F32).at[0, MLA_V].set(1.0), (1, MLA_HEADS))
    rows = (row(p["mla_q_a_gain"]), row(p["mla_kv_a_gain"]),
            _pad_row(p["mla_q_gain"]), _pad_row(p["mla_k_gain"][:MLA_NOPE]),
            _pad_row(p["mla_k_gain"][MLA_NOPE:], MLA_NOPE),
            _pad_row(p["nsa_q_gain"]), _pad_row(p["nsa_ks_gain"]), _pad_row(p["nsa_kw_gain"]), vone)
    texp, trow = _rope_expansion()
    (qm, km, vm, qn, ks, kw, vs, vw, kcin, vcin, gn, gm) = _inproj(
        x, mod, cs, (row(p["norm1_gain"]), texp, trow), weights, rows)

    n_chunk = S // CMP_STRIDE
    cs_end = cs[:, CMP_LEN - 1::CMP_STRIDE]
    cs_end = jnp.pad(cs_end, ((0, 0), (0, n_chunk - cs_end.shape[1]), (0, 0)))
    w2k = jnp.pad(p["cmp_w2_k"], ((0, 0), (0, HEAD_PAD - NSA_HEAD)))
    kc, vc = _compress(kcin, vcin, p["cmp_pos_k"], p["cmp_pos_v"],
                       bf(p["cmp_w1_k"]), bf(w2k), bf(p["cmp_w1_v"]), bf(p["cmp_w2_v"]),
                       _pad_row(p["nsa_kc_gain"]), cs_end, texp, trow)

    ovt, gexp, dbias, wbias = _mask_tables(S)
    o_nsa = _nsa_attention(qn, kc, vc, ks, vs, kw, vw, gn, ovt, gexp, dbias, wbias)
    o_mla = _mla_attention(qm, km, vm, dbias)

    return _out_ffn(x, o_mla, o_nsa, gm, mod, row(p["norm2_gain"]),
                    bf(p["w_o_mla"]), bf(p["w_o_nsa"]), bf(p["w_out"]),
                    bf(p["ffn_w_gate"]), bf(p["ffn_w_up"]), bf(p["ffn_w_down"]))


def kernel(x, c, positions, ada_w, ada_b, norm1_gain, w_in, mla_q_a_gain, mla_w_q_b, mla_kv_a_gain, mla_w_kv_b, mla_q_gain, mla_k_gain, nsa_q_gain, nsa_kc_gain, nsa_ks_gain, nsa_kw_gain, cmp_pos_k, cmp_w1_k, cmp_w2_k, cmp_pos_v, cmp_w1_v, cmp_w2_v, w_o_mla, w_o_nsa, w_out, norm2_gain, ffn_w_gate, ffn_w_up, ffn_w_down):
    params = dict(norm1_gain=norm1_gain, w_in=w_in, mla_q_a_gain=mla_q_a_gain, mla_w_q_b=mla_w_q_b,
                  mla_kv_a_gain=mla_kv_a_gain, mla_w_kv_b=mla_w_kv_b, mla_q_gain=mla_q_gain,
                  mla_k_gain=mla_k_gain, nsa_q_gain=nsa_q_gain, nsa_kc_gain=nsa_kc_gain,
                  nsa_ks_gain=nsa_ks_gain, nsa_kw_gain=nsa_kw_gain, cmp_pos_k=cmp_pos_k,
                  cmp_w1_k=cmp_w1_k, cmp_w2_k=cmp_w2_k, cmp_pos_v=cmp_pos_v, cmp_w1_v=cmp_w1_v,
                  cmp_w2_v=cmp_w2_v, w_o_mla=w_o_mla, w_o_nsa=w_o_nsa, w_out=w_out,
                  norm2_gain=norm2_gain, ffn_w_gate=ffn_w_gate, ffn_w_up=ffn_w_up, ffn_w_down=ffn_w_down)
    B = x.shape[0]
    inv_m = ROPE_THETA ** (-jnp.arange(0, MLA_ROPE, 2, dtype=F32) / MLA_ROPE)
    inv_n = ROPE_THETA ** (-jnp.arange(0, NSA_ROT, 2, dtype=F32) / NSA_ROT)
    cs = _rope_tables(positions, jnp.concatenate([inv_m, inv_n]))
    depth = ada_w.shape[0]
    for l in range(depth):
        mod = _ada(c, ada_w, ada_b, l).reshape(B, N_MOD, D_MODEL)
        x = _layer(x, mod, cs, {k: v[l] for k, v in params.items()})
    return x
```

```python
import numpy as np
import jax
import jax.numpy as jnp
from jax import lax
from jax.experimental import pallas as pl
from jax.experimental.pallas import tpu as pltpu

F32 = jnp.float32
BF16 = jnp.bfloat16

D_MODEL = 1024
ROPE_THETA = 500000.0
EPS = 1e-6
NEG = -1e30
LOG2E = 1.4426950408889634

MLA_HEADS = 8
MLA_NOPE = 64
MLA_ROPE = 32
MLA_QK = MLA_NOPE + MLA_ROPE
MLA_V = 64
MLA_Q_LORA = 768
MLA_KV_LORA = 256

NSA_HEADS = 8
NSA_KV_GROUPS = 2
NSA_REP = NSA_HEADS // NSA_KV_GROUPS
NSA_HEAD = 64
NSA_ROT = NSA_HEAD // 4
CMP_LEN = 32
CMP_STRIDE = 16
CMP_HIDDEN = 256
SEL_LEN = 64
SEL_TOP = 8
WINDOW = 256
N_NSA_BRANCH = 3
FORCE_BONUS = 1e4
KV_W = NSA_KV_GROUPS * NSA_HEAD

D_FF = -(-8 * D_MODEL // (3 * 256)) * 256
N_MOD = 6
LANES = 128
HEAD_PAD = LANES
N_FREQ = MLA_ROPE // 2 + NSA_ROT // 2

TM_IN = 512
IN_ROWS = 256
TQ_ATT = 256
TM_OUT = 512
OUT_ROWS = 256
MXU_TILE = 256
CMP_PACK = MXU_TILE // NSA_HEAD
FF_SPLITS = (0, 6 * MXU_TILE, D_FF)
V7X_VMEM_BYTES = 64 * 1024 * 1024
VMEM_LIMIT = V7X_VMEM_BYTES * 7 // 8
V_DIM = MLA_V
assert NSA_HEAD == V_DIM


def _dot(a, b):
    return jnp.dot(a, b, preferred_element_type=F32)


def _dot_nt(a, b):
    return lax.dot_general(a, b, (((1,), (1,)), ((), ())), preferred_element_type=F32)


def _split_hilo(a):
    hi = a.astype(BF16)
    return hi, (a - hi.astype(F32)).astype(BF16)


def _dot_hilo(a, m):
    hi, lo = _split_hilo(a)
    return _dot(hi, m) + _dot(lo, m)


def _sigmoid(v):
    return 1.0 / (1.0 + jnp.exp(-v))


def _rms(v, n):
    return v * lax.rsqrt(jnp.sum(v * v, axis=-1, keepdims=True) * (1.0 / n) + EPS)


def _rope(v, cos_v, sin_v, lo, half):
    lane = lax.broadcasted_iota(jnp.int32, v.shape, 1)
    is_x1 = (lane >= lo) & (lane < lo + half)
    rot = jnp.where(is_x1, pltpu.roll(v, LANES - half, 1), pltpu.roll(v, half, 1))
    return v * cos_v + rot * sin_v


def _rope_multipliers(cs, texp_ref, trow_ref):
    tabs = _dot_hilo(cs, texp_ref[...]) + trow_ref[...]
    return tuple(tabs[:, LANES * i:LANES * (i + 1)] for i in range(4))


def _const_spec(shape, single_buffer=False):
    nd = len(shape)
    mode = {"pipeline_mode": pl.Buffered(1)} if single_buffer else {}
    return pl.BlockSpec(shape, lambda *_: (0,) * nd, **mode)


def _rowmax(s):
    return jnp.max(s, axis=-1, keepdims=True)


def _attention_scores(q, k_ref, kmax, dbias):
    k0 = kmax - dbias.shape[1]
    s_d = _dot_nt(q, k_ref(k0, kmax)) + dbias
    m = _rowmax(s_d)
    s_m = None
    if k0 > 0:
        s_m = _dot_nt(q, k_ref(0, k0))
        m = jnp.maximum(m, _rowmax(s_m))
    return s_m, s_d, m


def _attention_probs(scores):
    s_m, s_d, m = scores
    p_m = None if s_m is None else jnp.exp2(s_m - m).astype(BF16)
    return p_m, jnp.exp2(s_d - m).astype(BF16)


def _attention_out(probs, v_ref, kmax):
    p_m, p_d = probs
    k0 = kmax - p_d.shape[1]
    acc = _dot(p_d, v_ref(k0, kmax))
    if p_m is not None:
        acc = acc + _dot(p_m, v_ref(0, k0))
    return acc[:, :V_DIM] / acc[:, V_DIM:V_DIM + 1]


def _software_pipeline(units, stages):
    last = len(stages) - 1
    carry = {}
    for step in range(-last, len(units)):
        for s, stage in enumerate(stages):
            n = step + last - s
            if 0 <= n < len(units):
                carry[n] = stage(units[n], carry.get(n))


def _rope_kernel(pos_ref, inv_ref, cs_ref):
    ang = pos_ref[0].astype(F32) * inv_ref[...]
    nf, S = ang.shape
    rows = jnp.concatenate([jnp.cos(ang), jnp.sin(ang), jnp.zeros((LANES - 2 * nf, S), F32)], axis=0)
    cs_ref[0] = rows.T


def _rope_tables(positions, inv):
    B, S = positions.shape
    nf = inv.shape[0]
    return pl.pallas_call(
        _rope_kernel,
        grid=(B,),
        in_specs=[pl.BlockSpec((1, 1, S), lambda b: (b, 0, 0)),
                  _const_spec((nf, 1))],
        out_specs=pl.BlockSpec((1, S, LANES), lambda b: (b, 0, 0)),
        out_shape=jax.ShapeDtypeStruct((B, S, LANES), F32),
        name="rope_tables",
    )(positions.reshape(B, 1, S), inv.reshape(nf, 1))


def _ada_kernel(c_ref, w_ref, b_ref, o_ref):
    c = c_ref[...]
    sc = c * _sigmoid(c)
    o_ref[...] = jnp.dot(sc, w_ref[0], preferred_element_type=F32,
                         precision=lax.Precision.HIGHEST) + b_ref[0]


def _ada(c, w, b, layer):
    B, D = c.shape
    N = w.shape[2]
    tn = D_MODEL
    return pl.pallas_call(
        _ada_kernel,
        grid=(N // tn,),
        in_specs=[_const_spec((B, D)),
                  pl.BlockSpec((1, D, tn), lambda j: (layer, 0, j)),
                  pl.BlockSpec((1, 1, tn), lambda j: (layer, 0, j))],
        out_specs=pl.BlockSpec((B, tn), lambda j: (0, j)),
        out_shape=jax.ShapeDtypeStruct((B, N), F32),
        name="ada_mod",
    )(c, w, b.reshape(b.shape[0], 1, N))


def _inproj_kernel(x_ref, mod_ref, cs_ref, g1_ref, texp_ref, trow_ref,
                   wcq_ref, wckv_ref, wsm_ref, wqn_ref, wkv6_ref, wgm_ref,
                   qag_ref, wqb_ref, kvag_ref, wkvb_ref,
                   mqg_ref, mkn_ref, mkr_ref, nqg_ref, nksg_ref, nkwg_ref, vone_ref,
                   qm_ref, km_ref, vm_ref, qn_ref, ks_ref, kw_ref, vs_ref, vw_ref,
                   kcin_ref, vcin_ref, gn_ref, gm_ref):
    tm = x_ref.shape[1]
    mod = mod_ref[0]
    sh1, sc1 = mod[0:1], mod[1:2]
    lane = lax.broadcasted_iota(jnp.int32, (IN_ROWS, LANES), 1)
    blk = lambda a, i: a[:, HEAD_PAD * i:HEAD_PAD * (i + 1)]
    hm = MLA_ROPE // 2
    hn = NSA_ROT // 2

    def front(rows):
        h = _rms(x_ref[0, rows], D_MODEL) * g1_ref[...] * (1.0 + sc1) + sh1
        hb = h.astype(BF16)
        return dict(
            hb=hb,
            tabs=_rope_multipliers(cs_ref[0, rows], texp_ref, trow_ref),
            cq=_dot(hb, wcq_ref[...]),
            ckv=_dot(hb, wckv_ref[...]),
            zs=_dot(hb, wsm_ref[...]),
            qn=_dot(hb, wqn_ref[...]),
            kv6=_dot(hb, wkv6_ref[...]))

    def middle(st):
        cqn = (_rms(st["cq"], MLA_Q_LORA) * qag_ref[...]).astype(BF16)
        st["q"] = _dot(cqn, wqb_ref[...])
        st["gates"] = _dot(st["hb"], wgm_ref[...])
        ckvn = (_rms(st["ckv"], MLA_KV_LORA) * kvag_ref[...]).astype(BF16)
        st["kv"] = _dot(ckvn, wkvb_ref[...])

    def back(rows, st):
        cos_m, sin_m, cos_n, sin_n = st["tabs"]
        zs, qn, kv6, q, kv = st["zs"], st["qn"], st["kv6"], st["q"], st["kv"]

        nqg = nqg_ref[...]
        n_scale = NSA_HEAD ** -0.5 * LOG2E
        for hd in range(NSA_HEADS):
            qh = _rope(_rms(blk(qn, hd), NSA_HEAD) * nqg, cos_n, sin_n, 0, hn) * n_scale
            qn_ref[0, rows, HEAD_PAD * hd:HEAD_PAD * (hd + 1)] = qh.astype(BF16)

        tok = pl.program_id(1) * tm + rows.start + lax.broadcasted_iota(jnp.int32, (IN_ROWS, 1), 0)
        sblk = lax.shift_right_logical(tok, SEL_LEN.bit_length() - 1)
        ind = jnp.where(lane - NSA_HEAD == sblk, NEG, 0.0)
        vone = vone_ref[:, 0:HEAD_PAD]
        nksg, nkwg = nksg_ref[...], nkwg_ref[...]
        kcin_ref[0, rows] = blk(kv6, 0)
        vcin_ref[0, rows] = blk(kv6, 1)
        for g in range(NSA_KV_GROUPS):
            ks = _rope(_rms(blk(kv6, 2 + g), NSA_HEAD) * nksg, cos_n, sin_n, 0, hn)
            ks_ref[0, g, rows] = (ks + ind).astype(BF16)
            vs_ref[0, g, rows] = (blk(kv6, 4 + g) + vone).astype(BF16)
            kw = _rope(_rms(blk(kv6, 6 + g), NSA_HEAD) * nkwg, cos_n, sin_n, 0, hn)
            kw_ref[0, g, rows] = kw.astype(BF16)
            vw_ref[0, g, rows] = (blk(kv6, 8 + g) + vone).astype(BF16)

        mqg = mqg_ref[...]
        m_scale = MLA_QK ** -0.5 * LOG2E
        for hd in range(MLA_HEADS):
            qh = _rope(_rms(blk(q, hd), MLA_QK) * mqg, cos_m, sin_m, MLA_NOPE, hm) * m_scale
            qm_ref[0, rows, HEAD_PAD * hd:HEAD_PAD * (hd + 1)] = qh.astype(BF16)

        gn_ref[0, rows] = _sigmoid(zs)
        gm_ref[0, rows] = _sigmoid(st["gates"]).astype(BF16)

        kpe = jnp.where((lane >= MLA_NOPE) & (lane < MLA_QK), zs, 0.0)
        kpe_ss = jnp.sum(kpe * kpe, axis=-1, keepdims=True)
        kr = _rope(kpe * mkr_ref[...], cos_m, sin_m, MLA_NOPE, hm)
        mkn = mkn_ref[...]
        for hd in range(MLA_HEADS):
            kn = blk(kv, hd)
            inv = lax.rsqrt((jnp.sum(kn * kn, axis=-1, keepdims=True) + kpe_ss) * (1.0 / MLA_QK) + EPS)
            km_ref[0, rows, HEAD_PAD * hd:HEAD_PAD * (hd + 1)] = ((kn * mkn + kr) * inv).astype(BF16)
        vm_ref[0, rows] = (kv[:, MLA_HEADS * HEAD_PAD:] + vone_ref[...]).astype(BF16)

    groups = [slice(s, s + IN_ROWS) for s in range(0, tm, IN_ROWS)]
    states = [front(rows) for rows in groups]
    for st in states:
        middle(st)
    for rows, st in zip(groups, states):
        back(rows, st)


def _inproj(x, mod, cs, consts, weights, rows):
    B, S, D = x.shape
    tm = TM_IN
    tok = lambda w: pl.BlockSpec((1, tm, w), lambda b, i: (b, i, 0))
    head = lambda n, w: pl.BlockSpec((1, n, tm, w), lambda b, i: (b, 0, i, 0))
    operands = list(consts) + list(weights[:6]) + [rows[0], weights[6], rows[1], weights[7]] + list(rows[2:])
    in_specs = [tok(D), pl.BlockSpec((1, N_MOD, D), lambda b, i: (b, 0, 0)), tok(LANES)]
    in_specs += [_const_spec(a.shape, single_buffer=True) for a in operands]
    G = NSA_KV_GROUPS
    sds = jax.ShapeDtypeStruct
    wide = MLA_HEADS * HEAD_PAD
    outs = [
        (tok(wide), sds((B, S, wide), BF16)),
        (tok(wide), sds((B, S, wide), BF16)),
        (tok(wide), sds((B, S, wide), BF16)),
        (tok(wide), sds((B, S, wide), BF16)),
        (head(G, HEAD_PAD), sds((B, G, S, HEAD_PAD), BF16)),
        (head(G, HEAD_PAD), sds((B, G, S, HEAD_PAD), BF16)),
        (head(G, HEAD_PAD), sds((B, G, S, HEAD_PAD), BF16)),
        (head(G, HEAD_PAD), sds((B, G, S, HEAD_PAD), BF16)),
        (tok(KV_W), sds((B, S, KV_W), F32)),
        (tok(KV_W), sds((B, S, KV_W), F32)),
        (tok(LANES), sds((B, S, LANES), F32)),
        (tok(2 * D), sds((B, S, 2 * D), BF16)),
    ]
    return pl.pallas_call(
        _inproj_kernel,
        grid=(B, S // tm),
        in_specs=in_specs,
        out_specs=[o[0] for o in outs],
        out_shape=[o[1] for o in outs],
        compiler_params=pltpu.CompilerParams(dimension_semantics=("arbitrary", "arbitrary"),
                                             vmem_limit_bytes=VMEM_LIMIT),
        name="inproj_prep",
    )(x, mod, cs, *operands)


def _compress_kernel(kcin_ref, vcin_ref, pk_ref, pv_ref, w1k_ref, w2k_ref, w1v_ref, w2v_ref,
                     kcg_ref, cs_ref, texp_ref, trow_ref, kc_ref, vc_ref):
    n = kcin_ref.shape[1] // CMP_STRIDE
    G = NSA_KV_GROUPS

    def hidden(cin_ref, pos_ref, w1_ref):
        a, b = [None] * G, [None] * G
        for l in range(0, CMP_STRIDE, CMP_PACK):
            toks = [cin_ref[0, pl.ds(l + j, n, stride=CMP_STRIDE), :] for j in range(CMP_PACK)]
            for g in range(G):
                lanes = slice(NSA_HEAD * g, NSA_HEAD * (g + 1))
                pack = lambda o: jnp.concatenate(
                    [toks[j][:, lanes] + pos_ref[o + l + j:o + l + j + 1] for j in range(CMP_PACK)],
                    axis=-1).astype(BF16)
                rows = lambda o: slice(NSA_HEAD * (o + l), NSA_HEAD * (o + l + CMP_PACK))
                da = _dot(pack(0), w1_ref[rows(0), :])
                db = _dot(pack(CMP_STRIDE), w1_ref[rows(CMP_STRIDE), :])
                a[g], b[g] = (da, db) if a[g] is None else (a[g] + da, b[g] + db)
        hid = [a[g] + pltpu.roll(b[g], n - 1, 0) for g in range(G)]
        return [(h * _sigmoid(h)).astype(BF16) for h in hid]

    _, _, cos_n, sin_n = _rope_multipliers(cs_ref[0], texp_ref, trow_ref)
    hk = hidden(kcin_ref, pk_ref, w1k_ref)
    hv = hidden(vcin_ref, pv_ref, w1v_ref)
    for g in range(G):
        kg = _rms(_dot(hk[g], w2k_ref[...]), NSA_HEAD) * kcg_ref[...]
        kc_ref[0, g] = _rope(kg, cos_n, sin_n, 0, NSA_ROT // 2).astype(BF16)
        vc_ref[0, g] = _dot(hv[g], w2v_ref[...]).astype(BF16)


def _compress(kcin, vcin, pk, pv, w1k, w2k, w1v, w2v, kcg, cs_end, texp, trow):
    B, S, w = kcin.shape
    G = NSA_KV_GROUPS
    n = S // CMP_STRIDE
    oblk = lambda wd: pl.BlockSpec((1, G, n, wd), lambda b: (b, 0, 0, 0))
    consts = (pk, pv, w1k, w2k, w1v, w2v, kcg)
    return pl.pallas_call(
        _compress_kernel,
        grid=(B,),
        in_specs=[pl.BlockSpec((1, S, w), lambda b: (b, 0, 0))] * 2 + [_const_spec(a.shape) for a in consts] +
                 [pl.BlockSpec((1, n, LANES), lambda b: (b, 0, 0)),
                  _const_spec(texp.shape), _const_spec(trow.shape)],
        out_specs=[oblk(HEAD_PAD), oblk(NSA_HEAD)],
        out_shape=[jax.ShapeDtypeStruct((B, G, n, HEAD_PAD), BF16),
                   jax.ShapeDtypeStruct((B, G, n, NSA_HEAD), BF16)],
        name="nsa_compress",
    )(kcin, vcin, *consts, cs_end, texp, trow)


def _nsa_kernel(q_ref, kc_ref, vc_ref, ks_ref, vs_ref, kw_ref, vw_ref, gn_ref,
                ovt_ref, gexp_ref, dbias_ref, wbias_ref, o_ref, imp_ref):
    tq = dbias_ref.shape[0]
    S = q_ref.shape[1]
    R = NSA_REP
    M = R * tq
    n_sel = imp_ref.shape[0]
    ncp = kc_ref.shape[2]
    span = WINDOW + tq
    grp = pl.program_id(1)
    dbias = dbias_ref[...]
    kf = lambda a, b: ks_ref[0, 0, a:b, :]
    vf = lambda a, b: vs_ref[0, 0, a:b, :]
    row = lax.broadcasted_iota(jnp.int32, (M, 1), 0)
    n_idx = lax.broadcasted_iota(jnp.int32, (M, ncp), 1)
    j = lax.broadcasted_iota(jnp.int32, (n_sel, tq), 0)
    head_q = lambda i, r: q_ref[0, i * tq:(i + 1) * tq, HEAD_PAD * r:HEAD_PAD * (r + 1)]
    tile_q = lambda i: jnp.concatenate([head_q(i, r) for r in range(R)], axis=0)
    tiles = {}

    def compressed_and_select(i, s):
        q0 = i * tq
        t = q0 + jnp.bitwise_and(row, tq - 1)
        valid = (n_idx * CMP_STRIDE + (CMP_LEN - 1)) <= t
        sm = jnp.where(valid, s, NEG)
        e = jnp.where(valid, jnp.exp2(sm - _rowmax(sm)), 0.0)
        den = jnp.sum(e, axis=-1, keepdims=True)
        p_c = e / jnp.where(den > 0.0, den, 1.0)
        o_c = _dot(p_c.astype(BF16), vc_ref[0, 0])
        psum = p_c[0:tq]
        for r in range(1, R):
            psum = psum + p_c[r * tq:(r + 1) * tq]
        hi, lo = _split_hilo(psum.T)
        imp = (_dot(ovt_ref[...], hi) + _dot(ovt_ref[...], lo))[0:n_sel]
        cur = lax.shift_right_logical(q0 + lax.broadcasted_iota(jnp.int32, (1, tq), 1),
                                      SEL_LEN.bit_length() - 1)
        forced = (j == 0) | (j == cur) | (j == cur - 1)
        imp = jnp.where(forced, imp + FORCE_BONUS, imp)
        imp = jnp.where(j <= cur, imp, NEG)
        imp_ref[...] = imp
        cnt = jnp.zeros((n_sel, tq), F32)
        for jj in range(n_sel):
            other = imp_ref[jj:jj + 1, :]
            beats = (other > imp) | ((other == imp) & (j > jj))
            cnt = cnt + jnp.where(beats, 1.0, 0.0)
        nsel = jnp.where((cnt < float(SEL_TOP)) & (j <= cur), 0.0, 1.0)
        nsel = jnp.concatenate([jnp.zeros((NSA_HEAD, tq), F32), nsel,
                                jnp.zeros((LANES - NSA_HEAD - n_sel, tq), F32)], axis=0).T.astype(BF16)
        tiles[i] = dict(o_c=o_c, nsel=nsel, o_s=[], o_w=[])

    def scores(u, _):
        kind, i, r = u
        if kind == "cmp":
            return _dot_nt(tile_q(i), kc_ref[0, 0])
        if kind == "sel":
            return _attention_scores(head_q(i, r) + tiles[i]["nsel"], kf, (i + 1) * tq, dbias)
        w0 = max(i * tq - WINDOW, 0)
        wb = wbias_ref[min(i, 1)]
        sw = _dot_nt(head_q(i, r), kw_ref[0, 0, w0:w0 + span, :]) + wb
        return sw, _rowmax(sw)

    def probs(u, sc):
        kind, i, r = u
        if kind == "cmp":
            return sc
        if kind == "sel":
            return _attention_probs(sc)
        sw, mw = sc
        return jnp.exp2(sw - mw).astype(BF16)

    def finish(u, ps):
        kind, i, r = u
        if kind == "cmp":
            compressed_and_select(i, ps)
        elif kind == "sel":
            tiles[i]["o_s"].append(_attention_out(ps, vf, (i + 1) * tq))
        elif kind == "win":
            w0 = max(i * tq - WINDOW, 0)
            acc_w = _dot(ps, vw_ref[0, 0, w0:w0 + span, :])
            tiles[i]["o_w"].append(acc_w[:, :V_DIM] / acc_w[:, V_DIM:V_DIM + 1])
        if kind == "win" and r == R - 1:
            tile = tiles.pop(i)
            g_hi, g_lo = _split_hilo(gn_ref[0, i * tq:(i + 1) * tq, :])
            o_c = jnp.concatenate([tile["o_c"][r * tq:(r + 1) * tq] for r in range(R)], axis=-1)
            branches = (o_c, jnp.concatenate(tile["o_s"], axis=-1), jnp.concatenate(tile["o_w"], axis=-1))
            out = None
            for br, o_b in enumerate(branches):
                gate = _dot(g_hi, gexp_ref[grp, br]) + _dot(g_lo, gexp_ref[grp, br])
                out = gate * o_b if out is None else out + gate * o_b
            o_ref[0, i * tq:(i + 1) * tq, :] = out.astype(BF16)

    nq = S // tq
    first = ("cmp", 0, 0)
    finish(first, scores(first, None))
    units = []
    for i in range(nq):
        units += [("cmp", i + 1, 0)] if i + 1 < nq else []
        for r in range(R):
            units += [("sel", i, r), ("win", i, r)]
    _software_pipeline(units, (scores, probs, finish))


def _nsa_attention(qn, kc, vc, ks, vs, kw, vw, gn, ovt, gexp, dbias, wbias):
    B, S, _ = qn.shape
    G, R, Dh = NSA_KV_GROUPS, NSA_REP, NSA_HEAD
    ncp = kc.shape[2]
    full = pl.BlockSpec((1, 1, S, HEAD_PAD), lambda b, g: (b, g, 0, 0))
    cmp_spec = lambda w: pl.BlockSpec((1, 1, ncp, w), lambda b, g: (b, g, 0, 0))
    return pl.pallas_call(
        _nsa_kernel,
        grid=(B, G),
        in_specs=[pl.BlockSpec((1, S, R * HEAD_PAD), lambda b, g: (b, 0, g)),
                  cmp_spec(HEAD_PAD), cmp_spec(Dh), full, full, full, full,
                  pl.BlockSpec((1, S, LANES), lambda b, g: (b, 0, 0)),
                  _const_spec(ovt.shape), _const_spec(gexp.shape),
                  _const_spec(dbias.shape), _const_spec(wbias.shape)],
        out_specs=pl.BlockSpec((1, S, R * Dh), lambda b, g: (b, 0, g)),
        out_shape=jax.ShapeDtypeStruct((B, S, G * R * Dh), BF16),
        scratch_shapes=[pltpu.VMEM((S // SEL_LEN, TQ_ATT), F32)],
        compiler_params=pltpu.CompilerParams(dimension_semantics=("arbitrary",) * 2,
                                             vmem_limit_bytes=VMEM_LIMIT),
        name="nsa_attention",
    )(qn, kc, vc, ks, vs, kw, vw, gn, ovt, gexp, dbias, wbias)


def _mla_kernel(q_ref, k_ref, v_ref, dbias_ref, o_ref):
    tq = dbias_ref.shape[0]
    S = q_ref.shape[1]
    dbias = dbias_ref[...]
    units = [(i, hh) for i in range(S // tq) for hh in range(2)]

    def scores(u, _):
        i, hh = u
        cols = slice(HEAD_PAD * hh, HEAD_PAD * (hh + 1))
        q = q_ref[0, i * tq:(i + 1) * tq, cols]
        return _attention_scores(q, lambda a, b: k_ref[0, a:b, cols], (i + 1) * tq, dbias)

    def probs(u, sc):
        return _attention_probs(sc)

    def finish(u, ps):
        i, hh = u
        cols = slice(HEAD_PAD * hh, HEAD_PAD * (hh + 1))
        o = _attention_out(ps, lambda a, b: v_ref[0, a:b, cols], (i + 1) * tq)
        o_ref[0, i * tq:(i + 1) * tq, MLA_V * hh:MLA_V * (hh + 1)] = o.astype(BF16)

    _software_pipeline(units, (scores, probs, finish))


def _mla_attention(qm, km, vm, dbias):
    B, S, _ = qm.shape
    pair = pl.BlockSpec((1, S, 2 * HEAD_PAD), lambda b, h: (b, 0, h))
    return pl.pallas_call(
        _mla_kernel,
        grid=(B, MLA_HEADS // 2),
        in_specs=[pair, pair, pair, _const_spec(dbias.shape)],
        out_specs=pl.BlockSpec((1, S, 2 * MLA_V), lambda b, h: (b, 0, h)),
        out_shape=jax.ShapeDtypeStruct((B, S, MLA_HEADS * MLA_V), BF16),
        compiler_params=pltpu.CompilerParams(dimension_semantics=("arbitrary",) * 2,
                                             vmem_limit_bytes=VMEM_LIMIT),
        name="mla_attention",
    )(qm, km, vm, dbias)


def _out_ffn_kernel(x_ref, om_ref, on_ref, gm_ref, mod_ref, g2_ref,
                    wom_ref, won_ref, wout_ref, wg_ref, wu_ref, wd_ref, o_ref):
    mod = mod_ref[0]
    gt1, sh2, sc2, gt2 = mod[2:3], mod[3:4], mod[4:5], mod[5:6]
    groups = [slice(s, s + OUT_ROWS) for s in range(0, x_ref.shape[1], OUT_ROWS)]
    chunks = [slice(lo, hi) for lo, hi in zip(FF_SPLITS[:-1], FF_SPLITS[1:])]
    heads = [(_dot(om_ref[0, rows], wom_ref[...]), _dot(on_ref[0, rows], won_ref[...])) for rows in groups]
    x1s = []
    for rows, (ym, yn) in zip(groups, heads):
        merged = gm_ref[0, rows, :D_MODEL] * ym + gm_ref[0, rows, D_MODEL:] * yn
        x1s.append(x_ref[0, rows] + gt1 * _dot(merged.astype(BF16), wout_ref[...]))
    gus = []
    for x1 in x1s:
        h2 = (_rms(x1, D_MODEL) * g2_ref[...] * (1.0 + sc2) + sh2).astype(BF16)
        gus.append([(_dot(h2, wg_ref[:, sl]), _dot(h2, wu_ref[:, sl])) for sl in chunks])
    for rows, x1, gu in zip(groups, x1s, gus):
        acc = None
        for sl, (g, u) in zip(chunks, gu):
            d = _dot((g * _sigmoid(g) * u).astype(BF16), wd_ref[sl, :])
            acc = d if acc is None else acc + d
        o_ref[0, rows] = x1 + gt2 * acc


def _out_ffn(x, om, on, gm, mod, g2, wom, won, wout, wg, wu, wd):
    B, S, D = x.shape
    tm = TM_OUT
    tok = lambda w: pl.BlockSpec((1, tm, w), lambda b, i: (b, i, 0))
    wspec = lambda w: pl.BlockSpec(w.shape, lambda b, i: (0, 0), pipeline_mode=pl.Buffered(1))
    return pl.pallas_call(
        _out_ffn_kernel,
        grid=(B, S // tm),
        in_specs=[tok(D), tok(om.shape[2]), tok(on.shape[2]), tok(2 * D),
                  pl.BlockSpec((1, N_MOD, D), lambda b, i: (b, 0, 0)),
                  _const_spec(g2.shape)] + [wspec(w) for w in (wom, won, wout, wg, wu, wd)],
        out_specs=tok(D),
        out_shape=jax.ShapeDtypeStruct((B, S, D), F32),
        compiler_params=pltpu.CompilerParams(dimension_semantics=("arbitrary", "arbitrary"),
                                             vmem_limit_bytes=VMEM_LIMIT),
        name="out_ffn",
    )(x, om, on, gm, mod, g2, wom, won, wout, wg, wu, wd)


def _rope_expansion():
    texp = np.zeros((LANES, 4 * LANES), np.float32)
    trow = np.zeros((1, 4 * LANES), np.float32)
    hm, hn = MLA_ROPE // 2, NSA_ROT // 2
    trow[0, 0:LANES] = 1.0
    trow[0, 2 * LANES:3 * LANES] = 1.0
    for i in range(hm):
        for off, sgn in ((MLA_NOPE + i, -1.0), (MLA_NOPE + hm + i, 1.0)):
            texp[i, off] = 1.0
            trow[0, off] = 0.0
            texp[N_FREQ + i, LANES + off] = sgn
    for i in range(hn):
        for off, sgn in ((i, -1.0), (hn + i, 1.0)):
            texp[hm + i, 2 * LANES + off] = 1.0
            trow[0, 2 * LANES + off] = 0.0
            texp[N_FREQ + hm + i, 3 * LANES + off] = sgn
    return jnp.asarray(texp, BF16), jnp.asarray(trow, F32)


def _mask_tables(S):
    tq = TQ_ATT
    n_chunk = S // CMP_STRIDE
    n_sel = S // SEL_LEN
    starts = np.arange(n_chunk) * CMP_STRIDE
    sel_start = np.arange(LANES) * SEL_LEN
    ovt = ((starts[None, :] < sel_start[:, None] + SEL_LEN) &
           (starts[None, :] + CMP_LEN > sel_start[:, None]) &
           (np.arange(n_chunk)[None, :] < n_chunk - 1) &
           (np.arange(LANES)[:, None] < n_sel))
    gcol = np.arange(LANES)[:, None]
    head = np.arange(NSA_REP * NSA_HEAD)[None, :] // NSA_HEAD
    gexp = np.stack([np.stack([gcol == (g * NSA_REP + head) * N_NSA_BRANCH + br
                               for br in range(N_NSA_BRANCH)]) for g in range(NSA_KV_GROUPS)])
    qi = np.arange(tq)[:, None]
    dbias = np.where(np.arange(tq)[None, :] <= qi, 0.0, NEG)
    kk = np.arange(WINDOW + tq)[None, :]
    band = lambda d: np.where((d >= 0) & (d < WINDOW), 0.0, NEG)
    wbias = np.stack([band(qi - kk), band(qi + WINDOW - kk)])
    return (jnp.asarray(ovt, BF16), jnp.asarray(gexp, BF16),
            jnp.asarray(dbias, F32), jnp.asarray(wbias, F32))


def _pad_heads(w, n_heads, width):
    k = w.shape[0]
    w = w.reshape(k, n_heads, width)
    return jnp.pad(w, ((0, 0), (0, 0), (0, HEAD_PAD - width))).reshape(k, n_heads * HEAD_PAD)


def _pad_row(g, lo=0):
    return jnp.pad(g, (lo, HEAD_PAD - lo - g.shape[0])).reshape(1, HEAD_PAD)


def _layer(x, mod, cs, p):
    B, S, D = x.shape
    w_in = p["w_in"]
    o = 0
    cols = {}
    for name, wdt in (("cq", MLA_Q_LORA), ("ckv", MLA_KV_LORA), ("kpe", MLA_ROPE),
                      ("qn", NSA_HEADS * NSA_HEAD), ("kc", KV_W), ("vc", KV_W), ("ks", KV_W),
                      ("vs", KV_W), ("kw", KV_W), ("vw", KV_W),
                      ("gn", NSA_HEADS * N_NSA_BRANCH), ("gm", 2 * D)):
        cols[name] = w_in[:, o:o + wdt]
        o += wdt
    G = NSA_KV_GROUPS
    n_gate = NSA_HEADS * N_NSA_BRANCH
    zc = lambda n: jnp.zeros((D, n), F32)
    wsm = jnp.concatenate([cols["gn"], zc(MLA_NOPE - n_gate), cols["kpe"], zc(LANES - MLA_QK)], axis=1)
    wkv6 = jnp.concatenate([cols["kc"], cols["vc"]] +
                           [_pad_heads(cols[k], G, NSA_HEAD) for k in ("ks", "vs", "kw", "vw")], axis=1)
    wkvb = p["mla_w_kv_b"].reshape(MLA_KV_LORA, MLA_HEADS, MLA_NOPE + MLA_V)
    wkvb = jnp.concatenate([_pad_heads(wkvb[:, :, :MLA_NOPE].reshape(MLA_KV_LORA, -1), MLA_HEADS, MLA_NOPE),
                            _pad_heads(wkvb[:, :, MLA_NOPE:].reshape(MLA_KV_LORA, -1), MLA_HEADS, MLA_V)], axis=1)
    bf = lambda w: w.astype(BF16)
    row = lambda g: g.reshape(1, -1)
    weights = tuple(bf(w) for w in (cols["cq"], cols["ckv"], wsm, _pad_heads(cols["qn"], NSA_HEADS, NSA_HEAD),
                                    wkv6, cols["gm"],
                                    _pad_heads(p["mla_w_q_b"], MLA_HEADS, MLA_QK), wkvb))
    vone = jnp.tile(jnp.zeros((1, HEAD_PAD), F32).at[0, MLA_V].set(1.0), (1, MLA_HEADS))
    rows = (row(p["mla_q_a_gain"]), row(p["mla_kv_a_gain"]),
            _pad_row(p["mla_q_gain"]), _pad_row(p["mla_k_gain"][:MLA_NOPE]),
            _pad_row(p["mla_k_gain"][MLA_NOPE:], MLA_NOPE),
            _pad_row(p["nsa_q_gain"]), _pad_row(p["nsa_ks_gain"]), _pad_row(p["nsa_kw_gain"]), vone)
    texp, trow = _rope_expansion()
    (qm, km, vm, qn, ks, kw, vs, vw, kcin, vcin, gn, gm) = _inproj(
        x, mod, cs, (row(p["norm1_gain"]), texp, trow), weights, rows)

    n_chunk = S // CMP_STRIDE
    cs_end = cs[:, CMP_LEN - 1::CMP_STRIDE]
    cs_end = jnp.pad(cs_end, ((0, 0), (0, n_chunk - cs_end.shape[1]), (0, 0)))
    w2k = jnp.pad(p["cmp_w2_k"], ((0, 0), (0, HEAD_PAD - NSA_HEAD)))
    kc, vc = _compress(kcin, vcin, p["cmp_pos_k"], p["cmp_pos_v"],
                       bf(p["cmp_w1_k"]), bf(w2k), bf(p["cmp_w1_v"]), bf(p["cmp_w2_v"]),
                       _pad_row(p["nsa_kc_gain"]), cs_end, texp, trow)

    ovt, gexp, dbias, wbias = _mask_tables(S)
    o_nsa = _nsa_attention(qn, kc, vc, ks, vs, kw, vw, gn, ovt, gexp, dbias, wbias)
    o_mla = _mla_attention(qm, km, vm, dbias)

    return _out_ffn(x, o_mla, o_nsa, gm, mod, row(p["norm2_gain"]),
                    bf(p["w_o_mla"]), bf(p["w_o_nsa"]), bf(p["w_out"]),
                    bf(p["ffn_w_gate"]), bf(p["ffn_w_up"]), bf(p["ffn_w_down"]))


def kernel(x, c, positions, ada_w, ada_b, norm1_gain, w_in, mla_q_a_gain, mla_w_q_b, mla_kv_a_gain, mla_w_kv_b, mla_q_gain, mla_k_gain, nsa_q_gain, nsa_kc_gain, nsa_ks_gain, nsa_kw_gain, cmp_pos_k, cmp_w1_k, cmp_w2_k, cmp_pos_v, cmp_w1_v, cmp_w2_v, w_o_mla, w_o_nsa, w_out, norm2_gain, ffn_w_gate, ffn_w_up, ffn_w_down):
    params = dict(norm1_gain=norm1_gain, w_in=w_in, mla_q_a_gain=mla_q_a_gain, mla_w_q_b=mla_w_q_b,
                  mla_kv_a_gain=mla_kv_a_gain, mla_w_kv_b=mla_w_kv_b, mla_q_gain=mla_q_gain,
                  mla_k_gain=mla_k_gain, nsa_q_gain=nsa_q_gain, nsa_kc_gain=nsa_kc_gain,
                  nsa_ks_gain=nsa_ks_gain, nsa_kw_gain=nsa_kw_gain, cmp_pos_k=cmp_pos_k,
                  cmp_w1_k=cmp_w1_k, cmp_w2_k=cmp_w2_k, cmp_pos_v=cmp_pos_v, cmp_w1_v=cmp_w1_v,
                  cmp_w2_v=cmp_w2_v, w_o_mla=w_o_mla, w_o_nsa=w_o_nsa, w_out=w_out,
                  norm2_gain=norm2_gain, ffn_w_gate=ffn_w_gate, ffn_w_up=ffn_w_up, ffn_w_down=ffn_w_down)
    B = x.shape[0]
    inv_m = ROPE_THETA ** (-jnp.arange(0, MLA_ROPE, 2, dtype=F32) / MLA_ROPE)
    inv_n = ROPE_THETA ** (-jnp.arange(0, NSA_ROT, 2, dtype=F32) / NSA_ROT)
    cs = _rope_tables(positions, jnp.concatenate([inv_m, inv_n]))
    depth = ada_w.shape[0]
    for l in range(depth):
        mod = _ada(c, ada_w, ada_b, l).reshape(B, N_MOD, D_MODEL)
        x = _layer(x, mod, cs, {k: v[l] for k, v in params.items()})
    return x
```

```python
import numpy as np
import jax
import jax.numpy as jnp
from jax import lax
from jax.experimental import pallas as pl
from jax.experimental.pallas import tpu as pltpu

F32 = jnp.float32
BF16 = jnp.bfloat16

D_MODEL = 1024
ROPE_THETA = 500000.0
EPS = 1e-6
NEG = -1e30
LOG2E = 1.4426950408889634

MLA_HEADS = 8
MLA_NOPE = 64
MLA_ROPE = 32
MLA_QK = MLA_NOPE + MLA_ROPE
MLA_V = 64
MLA_Q_LORA = 768
MLA_KV_LORA = 256

NSA_HEADS = 8
NSA_KV_GROUPS = 2
NSA_REP = NSA_HEADS // NSA_KV_GROUPS
NSA_HEAD = 64
NSA_ROT = NSA_HEAD // 4
CMP_LEN = 32
CMP_STRIDE = 16
CMP_HIDDEN = 256
SEL_LEN = 64
SEL_TOP = 8
WINDOW = 256
N_NSA_BRANCH = 3
FORCE_BONUS = 1e4
KV_W = NSA_KV_GROUPS * NSA_HEAD

D_FF = -(-8 * D_MODEL // (3 * 256)) * 256
N_MOD = 6
LANES = 128
HEAD_PAD = LANES
N_FREQ = MLA_ROPE // 2 + NSA_ROT // 2

TM_IN = 512
IN_ROWS = 256
TQ_ATT = 256
MLA_STEP_HEADS = 4
TM_OUT = 512
OUT_ROWS = 256
MXU_TILE = 256
CMP_PACK = MXU_TILE // NSA_HEAD
FF_SPLITS = (0, 6 * MXU_TILE, D_FF)
V7X_VMEM_BYTES = 64 * 1024 * 1024
VMEM_LIMIT = V7X_VMEM_BYTES * 7 // 8
V_DIM = MLA_V
assert NSA_HEAD == V_DIM


def _dot(a, b):
    return jnp.dot(a, b, preferred_element_type=F32)


def _dot_nt(a, b):
    return lax.dot_general(a, b, (((1,), (1,)), ((), ())), preferred_element_type=F32)


def _split_hilo(a):
    hi = a.astype(BF16)
    return hi, (a - hi.astype(F32)).astype(BF16)


def _dot_hilo(a, m):
    hi, lo = _split_hilo(a)
    return _dot(hi, m) + _dot(lo, m)


def _sigmoid(v):
    return 1.0 / (1.0 + jnp.exp(-v))


def _rms(v, n):
    return v * lax.rsqrt(jnp.sum(v * v, axis=-1, keepdims=True) * (1.0 / n) + EPS)


def _rope(v, cos_v, sin_v, lo, half):
    lane = lax.broadcasted_iota(jnp.int32, v.shape, 1)
    is_x1 = (lane >= lo) & (lane < lo + half)
    rot = jnp.where(is_x1, pltpu.roll(v, LANES - half, 1), pltpu.roll(v, half, 1))
    return v * cos_v + rot * sin_v


def _rope_multipliers(cs, texp_ref, trow_ref):
    tabs = _dot_hilo(cs, texp_ref[...]) + trow_ref[...]
    return tuple(tabs[:, LANES * i:LANES * (i + 1)] for i in range(4))


def _const_spec(shape, single_buffer=False):
    nd = len(shape)
    mode = {"pipeline_mode": pl.Buffered(1)} if single_buffer else {}
    return pl.BlockSpec(shape, lambda *_: (0,) * nd, **mode)


def _rowmax(s):
    return jnp.max(s, axis=-1, keepdims=True)


def _attention_scores(q, k_ref, kmax, dbias):
    k0 = kmax - dbias.shape[1]
    s_d = _dot_nt(q, k_ref(k0, kmax)) + dbias
    m = _rowmax(s_d)
    s_m = None
    if k0 > 0:
        s_m = _dot_nt(q, k_ref(0, k0))
        m = jnp.maximum(m, _rowmax(s_m))
    return s_m, s_d, m


def _attention_probs(scores):
    s_m, s_d, m = scores
    p_m = None if s_m is None else jnp.exp2(s_m - m).astype(BF16)
    return p_m, jnp.exp2(s_d - m).astype(BF16)


def _attention_out(probs, v_ref, kmax):
    p_m, p_d = probs
    k0 = kmax - p_d.shape[1]
    acc = _dot(p_d, v_ref(k0, kmax))
    if p_m is not None:
        acc = acc + _dot(p_m, v_ref(0, k0))
    return acc[:, :V_DIM] / acc[:, V_DIM:V_DIM + 1]


def _software_pipeline(units, stages):
    last = len(stages) - 1
    carry = {}
    for step in range(-last, len(units)):
        for s, stage in enumerate(stages):
            n = step + last - s
            if 0 <= n < len(units):
                carry[n] = stage(units[n], carry.get(n))


def _rope_kernel(pos_ref, inv_ref, cs_ref):
    ang = pos_ref[0].astype(F32) * inv_ref[...]
    nf, S = ang.shape
    rows = jnp.concatenate([jnp.cos(ang), jnp.sin(ang), jnp.zeros((LANES - 2 * nf, S), F32)], axis=0)
    cs_ref[0] = rows.T


def _rope_tables(positions, inv):
    B, S = positions.shape
    nf = inv.shape[0]
    return pl.pallas_call(
        _rope_kernel,
        grid=(B,),
        in_specs=[pl.BlockSpec((1, 1, S), lambda b: (b, 0, 0)),
                  _const_spec((nf, 1))],
        out_specs=pl.BlockSpec((1, S, LANES), lambda b: (b, 0, 0)),
        out_shape=jax.ShapeDtypeStruct((B, S, LANES), F32),
        name="rope_tables",
    )(positions.reshape(B, 1, S), inv.reshape(nf, 1))


def _ada_kernel(c_ref, w_ref, b_ref, o_ref):
    c = c_ref[...]
    sc = c * _sigmoid(c)
    o_ref[...] = jnp.dot(sc, w_ref[0], preferred_element_type=F32,
                         precision=lax.Precision.HIGHEST) + b_ref[0]


def _ada(c, w, b, layer):
    B, D = c.shape
    N = w.shape[2]
    tn = D_MODEL
    return pl.pallas_call(
        _ada_kernel,
        grid=(N // tn,),
        in_specs=[_const_spec((B, D)),
                  pl.BlockSpec((1, D, tn), lambda j: (layer, 0, j)),
                  pl.BlockSpec((1, 1, tn), lambda j: (layer, 0, j))],
        out_specs=pl.BlockSpec((B, tn), lambda j: (0, j)),
        out_shape=jax.ShapeDtypeStruct((B, N), F32),
        name="ada_mod",
    )(c, w, b.reshape(b.shape[0], 1, N))


def _inproj_kernel(x_ref, mod_ref, cs_ref, g1_ref, texp_ref, trow_ref,
                   wcq_ref, wckv_ref, wsm_ref, wqn_ref, wkv6_ref, wgm_ref,
                   qag_ref, wqb_ref, kvag_ref, wkvb_ref,
                   mqg_ref, mkn_ref, mkr_ref, nqg_ref, nksg_ref, nkwg_ref, vone_ref,
                   qm_ref, km_ref, vm_ref, qn_ref, ks_ref, kw_ref, vs_ref, vw_ref,
                   kcin_ref, vcin_ref, gn_ref, gm_ref):
    tm = x_ref.shape[1]
    mod = mod_ref[0]
    sh1, sc1 = mod[0:1], mod[1:2]
    lane = lax.broadcasted_iota(jnp.int32, (IN_ROWS, LANES), 1)
    blk = lambda a, i: a[:, HEAD_PAD * i:HEAD_PAD * (i + 1)]
    hm = MLA_ROPE // 2
    hn = NSA_ROT // 2

    def front(rows):
        h = _rms(x_ref[0, rows], D_MODEL) * g1_ref[...] * (1.0 + sc1) + sh1
        hb = h.astype(BF16)
        return dict(
            hb=hb,
            tabs=_rope_multipliers(cs_ref[0, rows], texp_ref, trow_ref),
            cq=_dot(hb, wcq_ref[...]),
            ckv=_dot(hb, wckv_ref[...]),
            zs=_dot(hb, wsm_ref[...]),
            qn=_dot(hb, wqn_ref[...]),
            kv6=_dot(hb, wkv6_ref[...]))

    def middle(st):
        cqn = (_rms(st["cq"], MLA_Q_LORA) * qag_ref[...]).astype(BF16)
        st["q"] = _dot(cqn, wqb_ref[...])
        st["gates"] = _dot(st["hb"], wgm_ref[...])
        ckvn = (_rms(st["ckv"], MLA_KV_LORA) * kvag_ref[...]).astype(BF16)
        st["kv"] = _dot(ckvn, wkvb_ref[...])

    def back(rows, st):
        cos_m, sin_m, cos_n, sin_n = st["tabs"]
        zs, qn, kv6, q, kv = st["zs"], st["qn"], st["kv6"], st["q"], st["kv"]

        nqg = nqg_ref[...]
        n_scale = NSA_HEAD ** -0.5 * LOG2E
        for hd in range(NSA_HEADS):
            qh = _rope(_rms(blk(qn, hd), NSA_HEAD) * nqg, cos_n, sin_n, 0, hn) * n_scale
            qn_ref[0, rows, HEAD_PAD * hd:HEAD_PAD * (hd + 1)] = qh.astype(BF16)

        tok = pl.program_id(1) * tm + rows.start + lax.broadcasted_iota(jnp.int32, (IN_ROWS, 1), 0)
        sblk = lax.shift_right_logical(tok, SEL_LEN.bit_length() - 1)
        ind = jnp.where(lane - NSA_HEAD == sblk, NEG, 0.0)
        vone = vone_ref[:, 0:HEAD_PAD]
        nksg, nkwg = nksg_ref[...], nkwg_ref[...]
        kcin_ref[0, rows] = blk(kv6, 0)
        vcin_ref[0, rows] = blk(kv6, 1)
        for g in range(NSA_KV_GROUPS):
            ks = _rope(_rms(blk(kv6, 2 + g), NSA_HEAD) * nksg, cos_n, sin_n, 0, hn)
            ks_ref[0, g, rows] = (ks + ind).astype(BF16)
            vs_ref[0, g, rows] = (blk(kv6, 4 + g) + vone).astype(BF16)
            kw = _rope(_rms(blk(kv6, 6 + g), NSA_HEAD) * nkwg, cos_n, sin_n, 0, hn)
            kw_ref[0, g, rows] = kw.astype(BF16)
            vw_ref[0, g, rows] = (blk(kv6, 8 + g) + vone).astype(BF16)

        mqg = mqg_ref[...]
        m_scale = MLA_QK ** -0.5 * LOG2E
        for hd in range(MLA_HEADS):
            qh = _rope(_rms(blk(q, hd), MLA_QK) * mqg, cos_m, sin_m, MLA_NOPE, hm) * m_scale
            qm_ref[0, rows, HEAD_PAD * hd:HEAD_PAD * (hd + 1)] = qh.astype(BF16)

        gn_ref[0, rows] = _sigmoid(zs)
        gm_ref[0, rows] = _sigmoid(st["gates"]).astype(BF16)

        kpe = jnp.where((lane >= MLA_NOPE) & (lane < MLA_QK), zs, 0.0)
        kpe_ss = jnp.sum(kpe * kpe, axis=-1, keepdims=True)
        kr = _rope(kpe * mkr_ref[...], cos_m, sin_m, MLA_NOPE, hm)
        mkn = mkn_ref[...]
        for hd in range(MLA_HEADS):
            kn = blk(kv, hd)
            inv = lax.rsqrt((jnp.sum(kn * kn, axis=-1, keepdims=True) + kpe_ss) * (1.0 / MLA_QK) + EPS)
            km_ref[0, rows, HEAD_PAD * hd:HEAD_PAD * (hd + 1)] = ((kn * mkn + kr) * inv).astype(BF16)
        vm_ref[0, rows] = (kv[:, MLA_HEADS * HEAD_PAD:] + vone_ref[...]).astype(BF16)

    groups = [slice(s, s + IN_ROWS) for s in range(0, tm, IN_ROWS)]
    states = [front(rows) for rows in groups]
    for st in states:
        middle(st)
    for rows, st in zip(groups, states):
        back(rows, st)


def _inproj(x, mod, cs, consts, weights, rows):
    B, S, D = x.shape
    tm = TM_IN
    tok = lambda w: pl.BlockSpec((1, tm, w), lambda b, i: (b, i, 0))
    head = lambda n, w: pl.BlockSpec((1, n, tm, w), lambda b, i: (b, 0, i, 0))
    operands = list(consts) + list(weights[:6]) + [rows[0], weights[6], rows[1], weights[7]] + list(rows[2:])
    in_specs = [tok(D), pl.BlockSpec((1, N_MOD, D), lambda b, i: (b, 0, 0)), tok(LANES)]
    in_specs += [_const_spec(a.shape, single_buffer=True) for a in operands]
    G = NSA_KV_GROUPS
    sds = jax.ShapeDtypeStruct
    wide = MLA_HEADS * HEAD_PAD
    outs = [
        (tok(wide), sds((B, S, wide), BF16)),
        (tok(wide), sds((B, S, wide), BF16)),
        (tok(wide), sds((B, S, wide), BF16)),
        (tok(wide), sds((B, S, wide), BF16)),
        (head(G, HEAD_PAD), sds((B, G, S, HEAD_PAD), BF16)),
        (head(G, HEAD_PAD), sds((B, G, S, HEAD_PAD), BF16)),
        (head(G, HEAD_PAD), sds((B, G, S, HEAD_PAD), BF16)),
        (head(G, HEAD_PAD), sds((B, G, S, HEAD_PAD), BF16)),
        (tok(KV_W), sds((B, S, KV_W), F32)),
        (tok(KV_W), sds((B, S, KV_W), F32)),
        (tok(LANES), sds((B, S, LANES), F32)),
        (tok(2 * D), sds((B, S, 2 * D), BF16)),
    ]
    return pl.pallas_call(
        _inproj_kernel,
        grid=(B, S // tm),
        in_specs=in_specs,
        out_specs=[o[0] for o in outs],
        out_shape=[o[1] for o in outs],
        compiler_params=pltpu.CompilerParams(dimension_semantics=("arbitrary", "arbitrary"),
                                             vmem_limit_bytes=VMEM_LIMIT),
        name="inproj_prep",
    )(x, mod, cs, *operands)


def _compress_kernel(kcin_ref, vcin_ref, pk_ref, pv_ref, w1k_ref, w2k_ref, w1v_ref, w2v_ref,
                     kcg_ref, cs_ref, texp_ref, trow_ref, kc_ref, vc_ref):
    n = kcin_ref.shape[1] // CMP_STRIDE
    G = NSA_KV_GROUPS

    def hidden(cin_ref, pos_ref, w1_ref):
        a, b = [None] * G, [None] * G
        for l in range(0, CMP_STRIDE, CMP_PACK):
            toks = [cin_ref[0, pl.ds(l + j, n, stride=CMP_STRIDE), :] for j in range(CMP_PACK)]
            for g in range(G):
                lanes = slice(NSA_HEAD * g, NSA_HEAD * (g + 1))
                pack = lambda o: jnp.concatenate(
                    [toks[j][:, lanes] + pos_ref[o + l + j:o + l + j + 1] for j in range(CMP_PACK)],
                    axis=-1).astype(BF16)
                rows = lambda o: slice(NSA_HEAD * (o + l), NSA_HEAD * (o + l + CMP_PACK))
                da = _dot(pack(0), w1_ref[rows(0), :])
                db = _dot(pack(CMP_STRIDE), w1_ref[rows(CMP_STRIDE), :])
                a[g], b[g] = (da, db) if a[g] is None else (a[g] + da, b[g] + db)
        hid = [a[g] + pltpu.roll(b[g], n - 1, 0) for g in range(G)]
        return [(h * _sigmoid(h)).astype(BF16) for h in hid]

    _, _, cos_n, sin_n = _rope_multipliers(cs_ref[0], texp_ref, trow_ref)
    hk = hidden(kcin_ref, pk_ref, w1k_ref)
    hv = hidden(vcin_ref, pv_ref, w1v_ref)
    for g in range(G):
        kg = _rms(_dot(hk[g], w2k_ref[...]), NSA_HEAD) * kcg_ref[...]
        kc_ref[0, g] = _rope(kg, cos_n, sin_n, 0, NSA_ROT // 2).astype(BF16)
        vc_ref[0, g] = _dot(hv[g], w2v_ref[...]).astype(BF16)


def _compress(kcin, vcin, pk, pv, w1k, w2k, w1v, w2v, kcg, cs_end, texp, trow):
    B, S, w = kcin.shape
    G = NSA_KV_GROUPS
    n = S // CMP_STRIDE
    oblk = lambda wd: pl.BlockSpec((1, G, n, wd), lambda b: (b, 0, 0, 0))
    consts = (pk, pv, w1k, w2k, w1v, w2v, kcg)
    return pl.pallas_call(
        _compress_kernel,
        grid=(B,),
        in_specs=[pl.BlockSpec((1, S, w), lambda b: (b, 0, 0))] * 2 + [_const_spec(a.shape) for a in consts] +
                 [pl.BlockSpec((1, n, LANES), lambda b: (b, 0, 0)),
                  _const_spec(texp.shape), _const_spec(trow.shape)],
        out_specs=[oblk(HEAD_PAD), oblk(NSA_HEAD)],
        out_shape=[jax.ShapeDtypeStruct((B, G, n, HEAD_PAD), BF16),
                   jax.ShapeDtypeStruct((B, G, n, NSA_HEAD), BF16)],
        name="nsa_compress",
    )(kcin, vcin, *consts, cs_end, texp, trow)


def _nsa_kernel(q_ref, kc_ref, vc_ref, ks_ref, vs_ref, kw_ref, vw_ref, gn_ref,
                ovt_ref, gexp_ref, dbias_ref, wbias_ref, o_ref, imp_ref):
    tq = dbias_ref.shape[0]
    S = q_ref.shape[1]
    R = NSA_REP
    M = R * tq
    n_sel = imp_ref.shape[0]
    ncp = kc_ref.shape[2]
    span = WINDOW + tq
    grp = pl.program_id(1)
    dbias = dbias_ref[...]
    kf = lambda a, b: ks_ref[0, 0, a:b, :]
    vf = lambda a, b: vs_ref[0, 0, a:b, :]
    row = lax.broadcasted_iota(jnp.int32, (M, 1), 0)
    n_idx = lax.broadcasted_iota(jnp.int32, (M, ncp), 1)
    j = lax.broadcasted_iota(jnp.int32, (n_sel, tq), 0)
    head_q = lambda i, r: q_ref[0, i * tq:(i + 1) * tq, HEAD_PAD * r:HEAD_PAD * (r + 1)]
    tile_q = lambda i: jnp.concatenate([head_q(i, r) for r in range(R)], axis=0)
    tiles = {}

    def compressed_and_select(i, s):
        q0 = i * tq
        t = q0 + jnp.bitwise_and(row, tq - 1)
        valid = (n_idx * CMP_STRIDE + (CMP_LEN - 1)) <= t
        sm = jnp.where(valid, s, NEG)
        e = jnp.where(valid, jnp.exp2(sm - _rowmax(sm)), 0.0)
        den = jnp.sum(e, axis=-1, keepdims=True)
        p_c = e / jnp.where(den > 0.0, den, 1.0)
        o_c = _dot(p_c.astype(BF16), vc_ref[0, 0])
        psum = p_c[0:tq]
        for r in range(1, R):
            psum = psum + p_c[r * tq:(r + 1) * tq]
        hi, lo = _split_hilo(psum.T)
        imp = (_dot(ovt_ref[...], hi) + _dot(ovt_ref[...], lo))[0:n_sel]
        cur = lax.shift_right_logical(q0 + lax.broadcasted_iota(jnp.int32, (1, tq), 1),
                                      SEL_LEN.bit_length() - 1)
        forced = (j == 0) | (j == cur) | (j == cur - 1)
        imp = jnp.where(forced, imp + FORCE_BONUS, imp)
        imp = jnp.where(j <= cur, imp, NEG)
        imp_ref[...] = imp
        cnt = jnp.zeros((n_sel, tq), F32)
        for jj in range(n_sel):
            other = imp_ref[jj:jj + 1, :]
            beats = (other > imp) | ((other == imp) & (j > jj))
            cnt = cnt + jnp.where(beats, 1.0, 0.0)
        nsel = jnp.where((cnt < float(SEL_TOP)) & (j <= cur), 0.0, 1.0)
        nsel = jnp.concatenate([jnp.zeros((NSA_HEAD, tq), F32), nsel,
                                jnp.zeros((LANES - NSA_HEAD - n_sel, tq), F32)], axis=0).T.astype(BF16)
        tiles[i] = dict(o_c=o_c, nsel=nsel, o_s=[], o_w=[])

    def scores(u, _):
        kind, i, r = u
        if kind == "cmp":
            return _dot_nt(tile_q(i), kc_ref[0, 0])
        if kind == "sel":
            return _attention_scores(head_q(i, r) + tiles[i]["nsel"], kf, (i + 1) * tq, dbias)
        w0 = max(i * tq - WINDOW, 0)
        wb = wbias_ref[min(i, 1)]
        sw = _dot_nt(head_q(i, r), kw_ref[0, 0, w0:w0 + span, :]) + wb
        return sw, _rowmax(sw)

    def probs(u, sc):
        kind, i, r = u
        if kind == "cmp":
            return sc
        if kind == "sel":
            return _attention_probs(sc)
        sw, mw = sc
        return jnp.exp2(sw - mw).astype(BF16)

    def finish(u, ps):
        kind, i, r = u
        if kind == "cmp":
            compressed_and_select(i, ps)
        elif kind == "sel":
            tiles[i]["o_s"].append(_attention_out(ps, vf, (i + 1) * tq))
        elif kind == "win":
            w0 = max(i * tq - WINDOW, 0)
            acc_w = _dot(ps, vw_ref[0, 0, w0:w0 + span, :])
            tiles[i]["o_w"].append(acc_w[:, :V_DIM] / acc_w[:, V_DIM:V_DIM + 1])
        if kind == "win" and r == R - 1:
            tile = tiles.pop(i)
            g_hi, g_lo = _split_hilo(gn_ref[0, i * tq:(i + 1) * tq, :])
            o_c = jnp.concatenate([tile["o_c"][r * tq:(r + 1) * tq] for r in range(R)], axis=-1)
            branches = (o_c, jnp.concatenate(tile["o_s"], axis=-1), jnp.concatenate(tile["o_w"], axis=-1))
            out = None
            for br, o_b in enumerate(branches):
                gate = _dot(g_hi, gexp_ref[grp, br]) + _dot(g_lo, gexp_ref[grp, br])
                out = gate * o_b if out is None else out + gate * o_b
            o_ref[0, i * tq:(i + 1) * tq, :] = out.astype(BF16)

    nq = S // tq
    first = ("cmp", 0, 0)
    finish(first, scores(first, None))
    units = []
    for i in range(nq):
        units += [("cmp", i + 1, 0)] if i + 1 < nq else []
        for r in range(R):
            units += [("sel", i, r), ("win", i, r)]
    _software_pipeline(units, (scores, probs, finish))


def _nsa_attention(qn, kc, vc, ks, vs, kw, vw, gn, ovt, gexp, dbias, wbias):
    B, S, _ = qn.shape
    G, R, Dh = NSA_KV_GROUPS, NSA_REP, NSA_HEAD
    ncp = kc.shape[2]
    full = pl.BlockSpec((1, 1, S, HEAD_PAD), lambda b, g: (b, g, 0, 0))
    cmp_spec = lambda w: pl.BlockSpec((1, 1, ncp, w), lambda b, g: (b, g, 0, 0))
    return pl.pallas_call(
        _nsa_kernel,
        grid=(B, G),
        in_specs=[pl.BlockSpec((1, S, R * HEAD_PAD), lambda b, g: (b, 0, g)),
                  cmp_spec(HEAD_PAD), cmp_spec(Dh), full, full, full, full,
                  pl.BlockSpec((1, S, LANES), lambda b, g: (b, 0, 0)),
                  _const_spec(ovt.shape), _const_spec(gexp.shape),
                  _const_spec(dbias.shape), _const_spec(wbias.shape)],
        out_specs=pl.BlockSpec((1, S, R * Dh), lambda b, g: (b, 0, g)),
        out_shape=jax.ShapeDtypeStruct((B, S, G * R * Dh), BF16),
        scratch_shapes=[pltpu.VMEM((S // SEL_LEN, TQ_ATT), F32)],
        compiler_params=pltpu.CompilerParams(dimension_semantics=("arbitrary",) * 2,
                                             vmem_limit_bytes=VMEM_LIMIT),
        name="nsa_attention",
    )(qn, kc, vc, ks, vs, kw, vw, gn, ovt, gexp, dbias, wbias)


def _mla_kernel(q_ref, k_ref, v_ref, dbias_ref, o_ref):
    tq = dbias_ref.shape[0]
    S = q_ref.shape[1]
    dbias = dbias_ref[...]
    units = [(i, hh) for i in range(S // tq) for hh in range(MLA_STEP_HEADS)]

    def scores(u, _):
        i, hh = u
        cols = slice(HEAD_PAD * hh, HEAD_PAD * (hh + 1))
        q = q_ref[0, i * tq:(i + 1) * tq, cols]
        return _attention_scores(q, lambda a, b: k_ref[0, a:b, cols], (i + 1) * tq, dbias)

    def probs(u, sc):
        return _attention_probs(sc)

    def finish(u, ps):
        i, hh = u
        cols = slice(HEAD_PAD * hh, HEAD_PAD * (hh + 1))
        o = _attention_out(ps, lambda a, b: v_ref[0, a:b, cols], (i + 1) * tq)
        o_ref[0, i * tq:(i + 1) * tq, MLA_V * hh:MLA_V * (hh + 1)] = o.astype(BF16)

    _software_pipeline(units, (scores, probs, finish))


def _mla_attention(qm, km, vm, dbias):
    B, S, _ = qm.shape
    nh = MLA_STEP_HEADS
    heads = pl.BlockSpec((1, S, nh * HEAD_PAD), lambda b, h: (b, 0, h))
    return pl.pallas_call(
        _mla_kernel,
        grid=(B, MLA_HEADS // nh),
        in_specs=[heads, heads, heads, _const_spec(dbias.shape)],
        out_specs=pl.BlockSpec((1, S, nh * MLA_V), lambda b, h: (b, 0, h)),
        out_shape=jax.ShapeDtypeStruct((B, S, MLA_HEADS * MLA_V), BF16),
        compiler_params=pltpu.CompilerParams(dimension_semantics=("arbitrary",) * 2,
                                             vmem_limit_bytes=VMEM_LIMIT),
        name="mla_attention",
    )(qm, km, vm, dbias)


def _out_ffn_kernel(x_ref, om_ref, on_ref, gm_ref, mod_ref, g2_ref,
                    wom_ref, won_ref, wout_ref, wg_ref, wu_ref, wd_ref, o_ref):
    mod = mod_ref[0]
    gt1, sh2, sc2, gt2 = mod[2:3], mod[3:4], mod[4:5], mod[5:6]
    groups = [slice(s, s + OUT_ROWS) for s in range(0, x_ref.shape[1], OUT_ROWS)]
    chunks = [slice(lo, hi) for lo, hi in zip(FF_SPLITS[:-1], FF_SPLITS[1:])]
    heads = [(_dot(om_ref[0, rows], wom_ref[...]), _dot(on_ref[0, rows], won_ref[...])) for rows in groups]
    x1s = []
    for rows, (ym, yn) in zip(groups, heads):
        merged = gm_ref[0, rows, :D_MODEL] * ym + gm_ref[0, rows, D_MODEL:] * yn
        x1s.append(x_ref[0, rows] + gt1 * _dot(merged.astype(BF16), wout_ref[...]))
    gus = []
    for x1 in x1s:
        h2 = (_rms(x1, D_MODEL) * g2_ref[...] * (1.0 + sc2) + sh2).astype(BF16)
        gus.append([(_dot(h2, wg_ref[:, sl]), _dot(h2, wu_ref[:, sl])) for sl in chunks])
    for rows, x1, gu in zip(groups, x1s, gus):
        acc = None
        for sl, (g, u) in zip(chunks, gu):
            d = _dot((g * _sigmoid(g) * u).astype(BF16), wd_ref[sl, :])
            acc = d if acc is None else acc + d
        o_ref[0, rows] = x1 + gt2 * acc


def _out_ffn(x, om, on, gm, mod, g2, wom, won, wout, wg, wu, wd):
    B, S, D = x.shape
    tm = TM_OUT
    tok = lambda w: pl.BlockSpec((1, tm, w), lambda b, i: (b, i, 0))
    wspec = lambda w: pl.BlockSpec(w.shape, lambda b, i: (0, 0), pipeline_mode=pl.Buffered(1))
    return pl.pallas_call(
        _out_ffn_kernel,
        grid=(B, S // tm),
        in_specs=[tok(D), tok(om.shape[2]), tok(on.shape[2]), tok(2 * D),
                  pl.BlockSpec((1, N_MOD, D), lambda b, i: (b, 0, 0)),
                  _const_spec(g2.shape)] + [wspec(w) for w in (wom, won, wout, wg, wu, wd)],
        out_specs=tok(D),
        out_shape=jax.ShapeDtypeStruct((B, S, D), F32),
        compiler_params=pltpu.CompilerParams(dimension_semantics=("arbitrary", "arbitrary"),
                                             vmem_limit_bytes=VMEM_LIMIT),
        name="out_ffn",
    )(x, om, on, gm, mod, g2, wom, won, wout, wg, wu, wd)


def _rope_expansion():
    texp = np.zeros((LANES, 4 * LANES), np.float32)
    trow = np.zeros((1, 4 * LANES), np.float32)
    hm, hn = MLA_ROPE // 2, NSA_ROT // 2
    trow[0, 0:LANES] = 1.0
    trow[0, 2 * LANES:3 * LANES] = 1.0
    for i in range(hm):
        for off, sgn in ((MLA_NOPE + i, -1.0), (MLA_NOPE + hm + i, 1.0)):
            texp[i, off] = 1.0
            trow[0, off] = 0.0
            texp[N_FREQ + i, LANES + off] = sgn
    for i in range(hn):
        for off, sgn in ((i, -1.0), (hn + i, 1.0)):
            texp[hm + i, 2 * LANES + off] = 1.0
            trow[0, 2 * LANES + off] = 0.0
            texp[N_FREQ + hm + i, 3 * LANES + off] = sgn
    return jnp.asarray(texp, BF16), jnp.asarray(trow, F32)


def _mask_tables(S):
    tq = TQ_ATT
    n_chunk = S // CMP_STRIDE
    n_sel = S // SEL_LEN
    starts = np.arange(n_chunk) * CMP_STRIDE
    sel_start = np.arange(LANES) * SEL_LEN
    ovt = ((starts[None, :] < sel_start[:, None] + SEL_LEN) &
           (starts[None, :] + CMP_LEN > sel_start[:, None]) &
           (np.arange(n_chunk)[None, :] < n_chunk - 1) &
           (np.arange(LANES)[:, None] < n_sel))
    gcol = np.arange(LANES)[:, None]
    head = np.arange(NSA_REP * NSA_HEAD)[None, :] // NSA_HEAD
    gexp = np.stack([np.stack([gcol == (g * NSA_REP + head) * N_NSA_BRANCH + br
                               for br in range(N_NSA_BRANCH)]) for g in range(NSA_KV_GROUPS)])
    qi = np.arange(tq)[:, None]
    dbias = np.where(np.arange(tq)[None, :] <= qi, 0.0, NEG)
    kk = np.arange(WINDOW + tq)[None, :]
    band = lambda d: np.where((d >= 0) & (d < WINDOW), 0.0, NEG)
    wbias = np.stack([band(qi - kk), band(qi + WINDOW - kk)])
    return (jnp.asarray(ovt, BF16), jnp.asarray(gexp, BF16),
            jnp.asarray(dbias, F32), jnp.asarray(wbias, F32))


def _pad_heads(w, n_heads, width):
    k = w.shape[0]
    w = w.reshape(k, n_heads, width)
    return jnp.pad(w, ((0, 0), (0, 0), (0, HEAD_PAD - width))).reshape(k, n_heads * HEAD_PAD)


def _pad_row(g, lo=0):
    return jnp.pad(g, (lo, HEAD_PAD - lo - g.shape[0])).reshape(1, HEAD_PAD)


def _layer(x, mod, cs, p):
    B, S, D = x.shape
    w_in = p["w_in"]
    o = 0
    cols = {}
    for name, wdt in (("cq", MLA_Q_LORA), ("ckv", MLA_KV_LORA), ("kpe", MLA_ROPE),
                      ("qn", NSA_HEADS * NSA_HEAD), ("kc", KV_W), ("vc", KV_W), ("ks", KV_W),
                      ("vs", KV_W), ("kw", KV_W), ("vw", KV_W),
                      ("gn", NSA_HEADS * N_NSA_BRANCH), ("gm", 2 * D)):
        cols[name] = w_in[:, o:o + wdt]
        o += wdt
    G = NSA_KV_GROUPS
    n_gate = NSA_HEADS * N_NSA_BRANCH
    zc = lambda n: jnp.zeros((D, n), F32)
    wsm = jnp.concatenate([cols["gn"], zc(MLA_NOPE - n_gate), cols["kpe"], zc(LANES - MLA_QK)], axis=1)
    wkv6 = jnp.concatenate([cols["kc"], cols["vc"]] +
                           [_pad_heads(cols[k], G, NSA_HEAD) for k in ("ks", "vs", "kw", "vw")], axis=1)
    wkvb = p["mla_w_kv_b"].reshape(MLA_KV_LORA, MLA_HEADS, MLA_NOPE + MLA_V)
    wkvb = jnp.concatenate([_pad_heads(wkvb[:, :, :MLA_NOPE].reshape(MLA_KV_LORA, -1), MLA_HEADS, MLA_NOPE),
                            _pad_heads(wkvb[:, :, MLA_NOPE:].reshape(MLA_KV_LORA, -1), MLA_HEADS, MLA_V)], axis=1)
    bf = lambda w: w.astype(BF16)
    row = lambda g: g.reshape(1, -1)
    weights = tuple(bf(w) for w in (cols["cq"], cols["ckv"], wsm, _pad_heads(cols["qn"], NSA_HEADS, NSA_HEAD),
                                    wkv6, cols["gm"],
                                    _pad_heads(p["mla_w_q_b"], MLA_HEADS, MLA_QK), wkvb))
    vone = jnp.tile(jnp.zeros((1, HEAD_PAD), F32).at[0, MLA_V].set(1.0), (1, MLA_HEADS))
    rows = (row(p["mla_q_a_gain"]), row(p["mla_kv_a_gain"]),
            _pad_row(p["mla_q_gain"]), _pad_row(p["mla_k_gain"][:MLA_NOPE]),
            _pad_row(p["mla_k_gain"][MLA_NOPE:], MLA_NOPE),
            _pad_row(p["nsa_q_gain"]), _pad_row(p["nsa_ks_gain"]), _pad_row(p["nsa_kw_gain"]), vone)
    texp, trow = _rope_expansion()
    (qm, km, vm, qn, ks, kw, vs, vw, kcin, vcin, gn, gm) = _inproj(
        x, mod, cs, (row(p["norm1_gain"]), texp, trow), weights, rows)

    n_chunk = S // CMP_STRIDE
    cs_end = cs[:, CMP_LEN - 1::CMP_STRIDE]
    cs_end = jnp.pad(cs_end, ((0, 0), (0, n_chunk - cs_end.shape[1]), (0, 0)))
    w2k = jnp.pad(p["cmp_w2_k"], ((0, 0), (0, HEAD_PAD - NSA_HEAD)))
    kc, vc = _compress(kcin, vcin, p["cmp_pos_k"], p["cmp_pos_v"],
                       bf(p["cmp_w1_k"]), bf(w2k), bf(p["cmp_w1_v"]), bf(p["cmp_w2_v"]),
                       _pad_row(p["nsa_kc_gain"]), cs_end, texp, trow)

    ovt, gexp, dbias, wbias = _mask_tables(S)
    o_nsa = _nsa_attention(qn, kc, vc, ks, vs, kw, vw, gn, ovt, gexp, dbias, wbias)
    o_mla = _mla_attention(qm, km, vm, dbias)

    return _out_ffn(x, o_mla, o_nsa, gm, mod, row(p["norm2_gain"]),
                    bf(p["w_o_mla"]), bf(p["w_o_nsa"]), bf(p["w_out"]),
                    bf(p["ffn_w_gate"]), bf(p["ffn_w_up"]), bf(p["ffn_w_down"]))


def kernel(x, c, positions, ada_w, ada_b, norm1_gain, w_in, mla_q_a_gain, mla_w_q_b, mla_kv_a_gain, mla_w_kv_b, mla_q_gain, mla_k_gain, nsa_q_gain, nsa_kc_gain, nsa_ks_gain, nsa_kw_gain, cmp_pos_k, cmp_w1_k, cmp_w2_k, cmp_pos_v, cmp_w1_v, cmp_w2_v, w_o_mla, w_o_nsa, w_out, norm2_gain, ffn_w_gate, ffn_w_up, ffn_w_down):
    params = dict(norm1_gain=norm1_gain, w_in=w_in, mla_q_a_gain=mla_q_a_gain, mla_w_q_b=mla_w_q_b,
                  mla_kv_a_gain=mla_kv_a_gain, mla_w_kv_b=mla_w_kv_b, mla_q_gain=mla_q_gain,
                  mla_k_gain=mla_k_gain, nsa_q_gain=nsa_q_gain, nsa_kc_gain=nsa_kc_gain,
                  nsa_ks_gain=nsa_ks_gain, nsa_kw_gain=nsa_kw_gain, cmp_pos_k=cmp_pos_k,
                  cmp_w1_k=cmp_w1_k, cmp_w2_k=cmp_w2_k, cmp_pos_v=cmp_pos_v, cmp_w1_v=cmp_w1_v,
                  cmp_w2_v=cmp_w2_v, w_o_mla=w_o_mla, w_o_nsa=w_o_nsa, w_out=w_out,
                  norm2_gain=norm2_gain, ffn_w_gate=ffn_w_gate, ffn_w_up=ffn_w_up, ffn_w_down=ffn_w_down)
    B = x.shape[0]
    inv_m = ROPE_THETA ** (-jnp.arange(0, MLA_ROPE, 2, dtype=F32) / MLA_ROPE)
    inv_n = ROPE_THETA ** (-jnp.arange(0, NSA_ROT, 2, dtype=F32) / NSA_ROT)
    cs = _rope_tables(positions, jnp.concatenate([inv_m, inv_n]))
    depth = ada_w.shape[0]
    for l in range(depth):
        mod = _ada(c, ada_w, ada_b, l).reshape(B, N_MOD, D_MODEL)
        x = _layer(x, mod, cs, {k: v[l] for k, v in params.items()})
    return x
```

```python
import numpy as np
import jax
import jax.numpy as jnp
from jax import lax
from jax.experimental import pallas as pl
from jax.experimental.pallas import tpu as pltpu

F32 = jnp.float32
BF16 = jnp.bfloat16

D_MODEL = 1024
ROPE_THETA = 500000.0
EPS = 1e-6
NEG = -1e30
LOG2E = 1.4426950408889634

MLA_HEADS = 8
MLA_NOPE = 64
MLA_ROPE = 32
MLA_QK = MLA_NOPE + MLA_ROPE
MLA_V = 64
MLA_Q_LORA = 768
MLA_KV_LORA = 256

NSA_HEADS = 8
NSA_KV_GROUPS = 2
NSA_REP = NSA_HEADS // NSA_KV_GROUPS
NSA_HEAD = 64
NSA_ROT = NSA_HEAD // 4
CMP_LEN = 32
CMP_STRIDE = 16
CMP_HIDDEN = 256
SEL_LEN = 64
SEL_TOP = 8
WINDOW = 256
N_NSA_BRANCH = 3
FORCE_BONUS = 1e4
KV_W = NSA_KV_GROUPS * NSA_HEAD

D_FF = -(-8 * D_MODEL // (3 * 256)) * 256
N_MOD = 6
LANES = 128
HEAD_PAD = LANES
N_FREQ = MLA_ROPE // 2 + NSA_ROT // 2

TM_IN = 512
IN_ROWS = 256
TQ_ATT = 256
MLA_STEP_HEADS = 4
TM_OUT = 512
OUT_ROWS = 256
MXU_TILE = 256
CMP_PACK = MXU_TILE // NSA_HEAD
FF_SPLITS = (0, 6 * MXU_TILE, D_FF)
V7X_VMEM_BYTES = 64 * 1024 * 1024
VMEM_LIMIT = V7X_VMEM_BYTES * 7 // 8
V_DIM = MLA_V
assert NSA_HEAD == V_DIM


def _dot(a, b):
    return jnp.dot(a, b, preferred_element_type=F32)


def _dot_nt(a, b):
    return lax.dot_general(a, b, (((1,), (1,)), ((), ())), preferred_element_type=F32)


def _split_hilo(a):
    hi = a.astype(BF16)
    return hi, (a - hi.astype(F32)).astype(BF16)


def _dot_hilo(a, m):
    hi, lo = _split_hilo(a)
    return _dot(hi, m) + _dot(lo, m)


def _sigmoid(v):
    return 1.0 / (1.0 + jnp.exp(-v))


def _rms(v, n):
    return v * lax.rsqrt(jnp.sum(v * v, axis=-1, keepdims=True) * (1.0 / n) + EPS)


def _rope(v, cos_v, sin_v, lo, half):
    lane = lax.broadcasted_iota(jnp.int32, v.shape, 1)
    is_x1 = (lane >= lo) & (lane < lo + half)
    rot = jnp.where(is_x1, pltpu.roll(v, LANES - half, 1), pltpu.roll(v, half, 1))
    return v * cos_v + rot * sin_v


def _rope_multipliers(cs, texp_ref, trow_ref):
    tabs = _dot_hilo(cs, texp_ref[...]) + trow_ref[...]
    return tuple(tabs[:, LANES * i:LANES * (i + 1)] for i in range(4))


def _const_spec(shape, single_buffer=False):
    nd = len(shape)
    mode = {"pipeline_mode": pl.Buffered(1)} if single_buffer else {}
    return pl.BlockSpec(shape, lambda *_: (0,) * nd, **mode)


def _rowmax(s):
    return jnp.max(s, axis=-1, keepdims=True)


def _attention_scores(q, k_ref, kmax, dbias, keys_on_lanes=False):
    qk = _dot if keys_on_lanes else _dot_nt
    k0 = kmax - dbias.shape[1]
    s_d = qk(q, k_ref(k0, kmax)) + dbias
    m = _rowmax(s_d)
    s_m = None
    if k0 > 0:
        s_m = qk(q, k_ref(0, k0))
        m = jnp.maximum(m, _rowmax(s_m))
    return s_m, s_d, m


def _attention_probs(scores):
    s_m, s_d, m = scores
    p_m = None if s_m is None else jnp.exp2(s_m - m).astype(BF16)
    return p_m, jnp.exp2(s_d - m).astype(BF16)


def _attention_out(probs, v_ref, kmax):
    p_m, p_d = probs
    k0 = kmax - p_d.shape[1]
    acc = _dot(p_d, v_ref(k0, kmax))
    if p_m is not None:
        acc = acc + _dot(p_m, v_ref(0, k0))
    return acc[:, :V_DIM] / acc[:, V_DIM:V_DIM + 1]


def _software_pipeline(units, stages):
    last = len(stages) - 1
    carry = {}
    for step in range(-last, len(units)):
        for s, stage in enumerate(stages):
            n = step + last - s
            if 0 <= n < len(units):
                carry[n] = stage(units[n], carry.get(n))


def _rope_kernel(pos_ref, inv_ref, cs_ref):
    ang = pos_ref[0].astype(F32) * inv_ref[...]
    nf, S = ang.shape
    rows = jnp.concatenate([jnp.cos(ang), jnp.sin(ang), jnp.zeros((LANES - 2 * nf, S), F32)], axis=0)
    cs_ref[0] = rows.T


def _rope_tables(positions, inv):
    B, S = positions.shape
    nf = inv.shape[0]
    return pl.pallas_call(
        _rope_kernel,
        grid=(B,),
        in_specs=[pl.BlockSpec((1, 1, S), lambda b: (b, 0, 0)),
                  _const_spec((nf, 1))],
        out_specs=pl.BlockSpec((1, S, LANES), lambda b: (b, 0, 0)),
        out_shape=jax.ShapeDtypeStruct((B, S, LANES), F32),
        name="rope_tables",
    )(positions.reshape(B, 1, S), inv.reshape(nf, 1))


def _ada_kernel(c_ref, w_ref, b_ref, o_ref):
    c = c_ref[...]
    sc = c * _sigmoid(c)
    o_ref[...] = jnp.dot(sc, w_ref[0], preferred_element_type=F32,
                         precision=lax.Precision.HIGHEST) + b_ref[0]


def _ada(c, w, b, layer):
    B, D = c.shape
    N = w.shape[2]
    tn = D_MODEL
    return pl.pallas_call(
        _ada_kernel,
        grid=(N // tn,),
        in_specs=[_const_spec((B, D)),
                  pl.BlockSpec((1, D, tn), lambda j: (layer, 0, j)),
                  pl.BlockSpec((1, 1, tn), lambda j: (layer, 0, j))],
        out_specs=pl.BlockSpec((B, tn), lambda j: (0, j)),
        out_shape=jax.ShapeDtypeStruct((B, N), F32),
        name="ada_mod",
    )(c, w, b.reshape(b.shape[0], 1, N))


def _inproj_kernel(x_ref, mod_ref, cs_ref, g1_ref, texp_ref, trow_ref,
                   wcq_ref, wckv_ref, wsm_ref, wqn_ref, wkv6_ref, wgm_ref,
                   qag_ref, wqb_ref, kvag_ref, wkvb_ref,
                   mqg_ref, mkn_ref, mkr_ref, nqg_ref, nksg_ref, nkwg_ref, vone_ref,
                   qm_ref, km_ref, vm_ref, qn_ref, ks_ref, kw_ref, vs_ref, vw_ref,
                   kcin_ref, vcin_ref, gn_ref, gm_ref):
    tm = x_ref.shape[1]
    mod = mod_ref[0]
    sh1, sc1 = mod[0:1], mod[1:2]
    lane = lax.broadcasted_iota(jnp.int32, (IN_ROWS, LANES), 1)
    blk = lambda a, i: a[:, HEAD_PAD * i:HEAD_PAD * (i + 1)]
    hm = MLA_ROPE // 2
    hn = NSA_ROT // 2

    def front(rows):
        h = _rms(x_ref[0, rows], D_MODEL) * g1_ref[...] * (1.0 + sc1) + sh1
        hb = h.astype(BF16)
        return dict(
            hb=hb,
            tabs=_rope_multipliers(cs_ref[0, rows], texp_ref, trow_ref),
            cq=_dot(hb, wcq_ref[...]),
            ckv=_dot(hb, wckv_ref[...]),
            zs=_dot(hb, wsm_ref[...]),
            qn=_dot(hb, wqn_ref[...]),
            kv6=_dot(hb, wkv6_ref[...]))

    def middle(st):
        cqn = (_rms(st["cq"], MLA_Q_LORA) * qag_ref[...]).astype(BF16)
        st["q"] = _dot(cqn, wqb_ref[...])
        st["gates"] = _dot(st["hb"], wgm_ref[...])
        ckvn = (_rms(st["ckv"], MLA_KV_LORA) * kvag_ref[...]).astype(BF16)
        st["kv"] = _dot(ckvn, wkvb_ref[...])

    def back(rows, st):
        cos_m, sin_m, cos_n, sin_n = st["tabs"]
        zs, qn, kv6, q, kv = st["zs"], st["qn"], st["kv6"], st["q"], st["kv"]

        nqg = nqg_ref[...]
        n_scale = NSA_HEAD ** -0.5 * LOG2E
        for hd in range(NSA_HEADS):
            qh = _rope(_rms(blk(qn, hd), NSA_HEAD) * nqg, cos_n, sin_n, 0, hn) * n_scale
            qn_ref[0, rows, HEAD_PAD * hd:HEAD_PAD * (hd + 1)] = qh.astype(BF16)

        tok = pl.program_id(1) * tm + rows.start + lax.broadcasted_iota(jnp.int32, (IN_ROWS, 1), 0)
        sblk = lax.shift_right_logical(tok, SEL_LEN.bit_length() - 1)
        ind = jnp.where(lane - NSA_HEAD == sblk, NEG, 0.0)
        vone = vone_ref[:, 0:HEAD_PAD]
        nksg, nkwg = nksg_ref[...], nkwg_ref[...]
        kcin_ref[0, rows] = blk(kv6, 0)
        vcin_ref[0, rows] = blk(kv6, 1)
        for g in range(NSA_KV_GROUPS):
            ks = _rope(_rms(blk(kv6, 2 + g), NSA_HEAD) * nksg, cos_n, sin_n, 0, hn)
            ks_ref[0, g, rows] = (ks + ind).astype(BF16)
            vs_ref[0, g, rows] = (blk(kv6, 4 + g) + vone).astype(BF16)
            kw = _rope(_rms(blk(kv6, 6 + g), NSA_HEAD) * nkwg, cos_n, sin_n, 0, hn)
            kw_ref[0, g, rows] = kw.astype(BF16)
            vw_ref[0, g, rows] = (blk(kv6, 8 + g) + vone).astype(BF16)

        mqg = mqg_ref[...]
        m_scale = MLA_QK ** -0.5 * LOG2E
        for hd in range(MLA_HEADS):
            qh = _rope(_rms(blk(q, hd), MLA_QK) * mqg, cos_m, sin_m, MLA_NOPE, hm) * m_scale
            qm_ref[0, rows, HEAD_PAD * hd:HEAD_PAD * (hd + 1)] = qh.astype(BF16)

        gn_ref[0, rows] = _sigmoid(zs)
        gm_ref[0, rows] = _sigmoid(st["gates"]).astype(BF16)

        kpe = jnp.where((lane >= MLA_NOPE) & (lane < MLA_QK), zs, 0.0)
        kpe_ss = jnp.sum(kpe * kpe, axis=-1, keepdims=True)
        kr = _rope(kpe * mkr_ref[...], cos_m, sin_m, MLA_NOPE, hm)
        mkn = mkn_ref[...]
        for hd in range(MLA_HEADS):
            kn = blk(kv, hd)
            inv = lax.rsqrt((jnp.sum(kn * kn, axis=-1, keepdims=True) + kpe_ss) * (1.0 / MLA_QK) + EPS)
            km_ref[0, HEAD_PAD * hd:HEAD_PAD * (hd + 1), rows] = ((kn * mkn + kr) * inv).T.astype(BF16)
        vm_ref[0, rows] = (kv[:, MLA_HEADS * HEAD_PAD:] + vone_ref[...]).astype(BF16)

    groups = [slice(s, s + IN_ROWS) for s in range(0, tm, IN_ROWS)]
    states = [front(rows) for rows in groups]
    for st in states:
        middle(st)
    for rows, st in zip(groups, states):
        back(rows, st)


def _inproj(x, mod, cs, consts, weights, rows):
    B, S, D = x.shape
    tm = TM_IN
    tok = lambda w: pl.BlockSpec((1, tm, w), lambda b, i: (b, i, 0))
    head = lambda n, w: pl.BlockSpec((1, n, tm, w), lambda b, i: (b, 0, i, 0))
    operands = list(consts) + list(weights[:6]) + [rows[0], weights[6], rows[1], weights[7]] + list(rows[2:])
    in_specs = [tok(D), pl.BlockSpec((1, N_MOD, D), lambda b, i: (b, 0, 0)), tok(LANES)]
    in_specs += [_const_spec(a.shape, single_buffer=True) for a in operands]
    G = NSA_KV_GROUPS
    sds = jax.ShapeDtypeStruct
    wide = MLA_HEADS * HEAD_PAD
    outs = [
        (tok(wide), sds((B, S, wide), BF16)),
        (pl.BlockSpec((1, wide, tm), lambda b, i: (b, 0, i)), sds((B, wide, S), BF16)),
        (tok(wide), sds((B, S, wide), BF16)),
        (tok(wide), sds((B, S, wide), BF16)),
        (head(G, HEAD_PAD), sds((B, G, S, HEAD_PAD), BF16)),
        (head(G, HEAD_PAD), sds((B, G, S, HEAD_PAD), BF16)),
        (head(G, HEAD_PAD), sds((B, G, S, HEAD_PAD), BF16)),
        (head(G, HEAD_PAD), sds((B, G, S, HEAD_PAD), BF16)),
        (tok(KV_W), sds((B, S, KV_W), F32)),
        (tok(KV_W), sds((B, S, KV_W), F32)),
        (tok(LANES), sds((B, S, LANES), F32)),
        (tok(2 * D), sds((B, S, 2 * D), BF16)),
    ]
    return pl.pallas_call(
        _inproj_kernel,
        grid=(B, S // tm),
        in_specs=in_specs,
        out_specs=[o[0] for o in outs],
        out_shape=[o[1] for o in outs],
        compiler_params=pltpu.CompilerParams(dimension_semantics=("arbitrary", "arbitrary"),
                                             vmem_limit_bytes=VMEM_LIMIT),
        name="inproj_prep",
    )(x, mod, cs, *operands)


def _compress_kernel(kcin_ref, vcin_ref, pk_ref, pv_ref, w1k_ref, w2k_ref, w1v_ref, w2v_ref,
                     kcg_ref, cs_ref, texp_ref, trow_ref, kc_ref, vc_ref):
    n = kcin_ref.shape[1] // CMP_STRIDE
    G = NSA_KV_GROUPS

    def hidden(cin_ref, pos_ref, w1_ref):
        a, b = [None] * G, [None] * G
        for l in range(0, CMP_STRIDE, CMP_PACK):
            toks = [cin_ref[0, pl.ds(l + j, n, stride=CMP_STRIDE), :] for j in range(CMP_PACK)]
            for g in range(G):
                lanes = slice(NSA_HEAD * g, NSA_HEAD * (g + 1))
                pack = lambda o: jnp.concatenate(
                    [toks[j][:, lanes] + pos_ref[o + l + j:o + l + j + 1] for j in range(CMP_PACK)],
                    axis=-1).astype(BF16)
                rows = lambda o: slice(NSA_HEAD * (o + l), NSA_HEAD * (o + l + CMP_PACK))
                da = _dot(pack(0), w1_ref[rows(0), :])
                db = _dot(pack(CMP_STRIDE), w1_ref[rows(CMP_STRIDE), :])
                a[g], b[g] = (da, db) if a[g] is None else (a[g] + da, b[g] + db)
        hid = [a[g] + pltpu.roll(b[g], n - 1, 0) for g in range(G)]
        return [(h * _sigmoid(h)).astype(BF16) for h in hid]

    _, _, cos_n, sin_n = _rope_multipliers(cs_ref[0], texp_ref, trow_ref)
    hk = hidden(kcin_ref, pk_ref, w1k_ref)
    hv = hidden(vcin_ref, pv_ref, w1v_ref)
    for g in range(G):
        kg = _rms(_dot(hk[g], w2k_ref[...]), NSA_HEAD) * kcg_ref[...]
        kc_ref[0, g] = _rope(kg, cos_n, sin_n, 0, NSA_ROT // 2).astype(BF16)
        vc_ref[0, g] = _dot(hv[g], w2v_ref[...]).astype(BF16)


def _compress(kcin, vcin, pk, pv, w1k, w2k, w1v, w2v, kcg, cs_end, texp, trow):
    B, S, w = kcin.shape
    G = NSA_KV_GROUPS
    n = S // CMP_STRIDE
    oblk = lambda wd: pl.BlockSpec((1, G, n, wd), lambda b: (b, 0, 0, 0))
    consts = (pk, pv, w1k, w2k, w1v, w2v, kcg)
    return pl.pallas_call(
        _compress_kernel,
        grid=(B,),
        in_specs=[pl.BlockSpec((1, S, w), lambda b: (b, 0, 0))] * 2 + [_const_spec(a.shape) for a in consts] +
                 [pl.BlockSpec((1, n, LANES), lambda b: (b, 0, 0)),
                  _const_spec(texp.shape), _const_spec(trow.shape)],
        out_specs=[oblk(HEAD_PAD), oblk(NSA_HEAD)],
        out_shape=[jax.ShapeDtypeStruct((B, G, n, HEAD_PAD), BF16),
                   jax.ShapeDtypeStruct((B, G, n, NSA_HEAD), BF16)],
        name="nsa_compress",
    )(kcin, vcin, *consts, cs_end, texp, trow)


def _nsa_kernel(q_ref, kc_ref, vc_ref, ks_ref, vs_ref, kw_ref, vw_ref, gn_ref,
                ovt_ref, gexp_ref, dbias_ref, wbias_ref, o_ref, imp_ref):
    tq = dbias_ref.shape[0]
    S = q_ref.shape[1]
    R = NSA_REP
    M = R * tq
    n_sel = imp_ref.shape[0]
    ncp = kc_ref.shape[2]
    span = WINDOW + tq
    grp = pl.program_id(1)
    dbias = dbias_ref[...]
    kf = lambda a, b: ks_ref[0, 0, a:b, :]
    vf = lambda a, b: vs_ref[0, 0, a:b, :]
    row = lax.broadcasted_iota(jnp.int32, (M, 1), 0)
    n_idx = lax.broadcasted_iota(jnp.int32, (M, ncp), 1)
    j = lax.broadcasted_iota(jnp.int32, (n_sel, tq), 0)
    head_q = lambda i, r: q_ref[0, i * tq:(i + 1) * tq, HEAD_PAD * r:HEAD_PAD * (r + 1)]
    tile_q = lambda i: jnp.concatenate([head_q(i, r) for r in range(R)], axis=0)
    tiles = {}

    def compressed_and_select(i, s):
        q0 = i * tq
        t = q0 + jnp.bitwise_and(row, tq - 1)
        valid = (n_idx * CMP_STRIDE + (CMP_LEN - 1)) <= t
        sm = jnp.where(valid, s, NEG)
        e = jnp.where(valid, jnp.exp2(sm - _rowmax(sm)), 0.0)
        den = jnp.sum(e, axis=-1, keepdims=True)
        p_c = e / jnp.where(den > 0.0, den, 1.0)
        o_c = _dot(p_c.astype(BF16), vc_ref[0, 0])
        psum = p_c[0:tq]
        for r in range(1, R):
            psum = psum + p_c[r * tq:(r + 1) * tq]
        hi, lo = _split_hilo(psum.T)
        imp = (_dot(ovt_ref[...], hi) + _dot(ovt_ref[...], lo))[0:n_sel]
        cur = lax.shift_right_logical(q0 + lax.broadcasted_iota(jnp.int32, (1, tq), 1),
                                      SEL_LEN.bit_length() - 1)
        forced = (j == 0) | (j == cur) | (j == cur - 1)
        imp = jnp.where(forced, imp + FORCE_BONUS, imp)
        imp = jnp.where(j <= cur, imp, NEG)
        imp_ref[...] = imp
        cnt = jnp.zeros((n_sel, tq), F32)
        for jj in range(n_sel):
            other = imp_ref[jj:jj + 1, :]
            beats = (other > imp) | ((other == imp) & (j > jj))
            cnt = cnt + jnp.where(beats, 1.0, 0.0)
        nsel = jnp.where((cnt < float(SEL_TOP)) & (j <= cur), 0.0, 1.0)
        nsel = jnp.concatenate([jnp.zeros((NSA_HEAD, tq), F32), nsel,
                                jnp.zeros((LANES - NSA_HEAD - n_sel, tq), F32)], axis=0).T.astype(BF16)
        tiles[i] = dict(o_c=o_c, nsel=nsel, o_s=[], o_w=[])

    def scores(u, _):
        kind, i, r = u
        if kind == "cmp":
            return _dot_nt(tile_q(i), kc_ref[0, 0])
        if kind == "sel":
            return _attention_scores(head_q(i, r) + tiles[i]["nsel"], kf, (i + 1) * tq, dbias)
        w0 = max(i * tq - WINDOW, 0)
        wb = wbias_ref[min(i, 1)]
        sw = _dot_nt(head_q(i, r), kw_ref[0, 0, w0:w0 + span, :]) + wb
        return sw, _rowmax(sw)

    def probs(u, sc):
        kind, i, r = u
        if kind == "cmp":
            return sc
        if kind == "sel":
            return _attention_probs(sc)
        sw, mw = sc
        return jnp.exp2(sw - mw).astype(BF16)

    def finish(u, ps):
        kind, i, r = u
        if kind == "cmp":
            compressed_and_select(i, ps)
        elif kind == "sel":
            tiles[i]["o_s"].append(_attention_out(ps, vf, (i + 1) * tq))
        elif kind == "win":
            w0 = max(i * tq - WINDOW, 0)
            acc_w = _dot(ps, vw_ref[0, 0, w0:w0 + span, :])
            tiles[i]["o_w"].append(acc_w[:, :V_DIM] / acc_w[:, V_DIM:V_DIM + 1])
        if kind == "win" and r == R - 1:
            tile = tiles.pop(i)
            g_hi, g_lo = _split_hilo(gn_ref[0, i * tq:(i + 1) * tq, :])
            o_c = jnp.concatenate([tile["o_c"][r * tq:(r + 1) * tq] for r in range(R)], axis=-1)
            branches = (o_c, jnp.concatenate(tile["o_s"], axis=-1), jnp.concatenate(tile["o_w"], axis=-1))
            out = None
            for br, o_b in enumerate(branches):
                gate = _dot(g_hi, gexp_ref[grp, br]) + _dot(g_lo, gexp_ref[grp, br])
                out = gate * o_b if out is None else out + gate * o_b
            o_ref[0, i * tq:(i + 1) * tq, :] = out.astype(BF16)

    nq = S // tq
    first = ("cmp", 0, 0)
    finish(first, scores(first, None))
    units = []
    for i in range(nq):
        units += [("cmp", i + 1, 0)] if i + 1 < nq else []
        for r in range(R):
            units += [("sel", i, r), ("win", i, r)]
    _software_pipeline(units, (scores, probs, finish))


def _nsa_attention(qn, kc, vc, ks, vs, kw, vw, gn, ovt, gexp, dbias, wbias):
    B, S, _ = qn.shape
    G, R, Dh = NSA_KV_GROUPS, NSA_REP, NSA_HEAD
    ncp = kc.shape[2]
    full = pl.BlockSpec((1, 1, S, HEAD_PAD), lambda b, g: (b, g, 0, 0))
    cmp_spec = lambda w: pl.BlockSpec((1, 1, ncp, w), lambda b, g: (b, g, 0, 0))
    return pl.pallas_call(
        _nsa_kernel,
        grid=(B, G),
        in_specs=[pl.BlockSpec((1, S, R * HEAD_PAD), lambda b, g: (b, 0, g)),
                  cmp_spec(HEAD_PAD), cmp_spec(Dh), full, full, full, full,
                  pl.BlockSpec((1, S, LANES), lambda b, g: (b, 0, 0)),
                  _const_spec(ovt.shape), _const_spec(gexp.shape),
                  _const_spec(dbias.shape), _const_spec(wbias.shape)],
        out_specs=pl.BlockSpec((1, S, R * Dh), lambda b, g: (b, 0, g)),
        out_shape=jax.ShapeDtypeStruct((B, S, G * R * Dh), BF16),
        scratch_shapes=[pltpu.VMEM((S // SEL_LEN, TQ_ATT), F32)],
        compiler_params=pltpu.CompilerParams(dimension_semantics=("arbitrary",) * 2,
                                             vmem_limit_bytes=VMEM_LIMIT),
        name="nsa_attention",
    )(qn, kc, vc, ks, vs, kw, vw, gn, ovt, gexp, dbias, wbias)


def _mla_kernel(q_ref, k_ref, v_ref, dbias_ref, o_ref):
    tq = dbias_ref.shape[0]
    S = q_ref.shape[1]
    dbias = dbias_ref[...]
    units = [(i, hh) for i in range(S // tq) for hh in range(MLA_STEP_HEADS)]

    def scores(u, _):
        i, hh = u
        cols = slice(HEAD_PAD * hh, HEAD_PAD * (hh + 1))
        q = q_ref[0, i * tq:(i + 1) * tq, cols]
        return _attention_scores(q, lambda a, b: k_ref[0, cols, a:b], (i + 1) * tq, dbias, keys_on_lanes=True)

    def probs(u, sc):
        return _attention_probs(sc)

    def finish(u, ps):
        i, hh = u
        cols = slice(HEAD_PAD * hh, HEAD_PAD * (hh + 1))
        o = _attention_out(ps, lambda a, b: v_ref[0, a:b, cols], (i + 1) * tq)
        o_ref[0, i * tq:(i + 1) * tq, MLA_V * hh:MLA_V * (hh + 1)] = o.astype(BF16)

    _software_pipeline(units, (scores, probs, finish))


def _mla_attention(qm, km, vm, dbias):
    B, S, _ = qm.shape
    nh = MLA_STEP_HEADS
    heads = pl.BlockSpec((1, S, nh * HEAD_PAD), lambda b, h: (b, 0, h))
    return pl.pallas_call(
        _mla_kernel,
        grid=(B, MLA_HEADS // nh),
        in_specs=[heads, pl.BlockSpec((1, nh * HEAD_PAD, S), lambda b, h: (b, h, 0)), heads,
                  _const_spec(dbias.shape)],
        out_specs=pl.BlockSpec((1, S, nh * MLA_V), lambda b, h: (b, 0, h)),
        out_shape=jax.ShapeDtypeStruct((B, S, MLA_HEADS * MLA_V), BF16),
        compiler_params=pltpu.CompilerParams(dimension_semantics=("arbitrary",) * 2,
                                             vmem_limit_bytes=VMEM_LIMIT),
        name="mla_attention",
    )(qm, km, vm, dbias)


def _out_ffn_kernel(x_ref, om_ref, on_ref, gm_ref, mod_ref, g2_ref,
                    wom_ref, won_ref, wout_ref, wg_ref, wu_ref, wd_ref, o_ref):
    mod = mod_ref[0]
    gt1, sh2, sc2, gt2 = mod[2:3], mod[3:4], mod[4:5], mod[5:6]
    groups = [slice(s, s + OUT_ROWS) for s in range(0, x_ref.shape[1], OUT_ROWS)]
    chunks = [slice(lo, hi) for lo, hi in zip(FF_SPLITS[:-1], FF_SPLITS[1:])]
    heads = [(_dot(om_ref[0, rows], wom_ref[...]), _dot(on_ref[0, rows], won_ref[...])) for rows in groups]
    x1s = []
    for rows, (ym, yn) in zip(groups, heads):
        merged = gm_ref[0, rows, :D_MODEL] * ym + gm_ref[0, rows, D_MODEL:] * yn
        x1s.append(x_ref[0, rows] + gt1 * _dot(merged.astype(BF16), wout_ref[...]))
    gus = []
    for x1 in x1s:
        h2 = (_rms(x1, D_MODEL) * g2_ref[...] * (1.0 + sc2) + sh2).astype(BF16)
        gus.append([(_dot(h2, wg_ref[:, sl]), _dot(h2, wu_ref[:, sl])) for sl in chunks])
    for rows, x1, gu in zip(groups, x1s, gus):
        acc = None
        for sl, (g, u) in zip(chunks, gu):
            d = _dot((g * _sigmoid(g) * u).astype(BF16), wd_ref[sl, :])
            acc = d if acc is None else acc + d
        o_ref[0, rows] = x1 + gt2 * acc


def _out_ffn(x, om, on, gm, mod, g2, wom, won, wout, wg, wu, wd):
    B, S, D = x.shape
    tm = TM_OUT
    tok = lambda w: pl.BlockSpec((1, tm, w), lambda b, i: (b, i, 0))
    wspec = lambda w: pl.BlockSpec(w.shape, lambda b, i: (0, 0), pipeline_mode=pl.Buffered(1))
    return pl.pallas_call(
        _out_ffn_kernel,
        grid=(B, S // tm),
        in_specs=[tok(D), tok(om.shape[2]), tok(on.shape[2]), tok(2 * D),
                  pl.BlockSpec((1, N_MOD, D), lambda b, i: (b, 0, 0)),
                  _const_spec(g2.shape)] + [wspec(w) for w in (wom, won, wout, wg, wu, wd)],
        out_specs=tok(D),
        out_shape=jax.ShapeDtypeStruct((B, S, D), F32),
        compiler_params=pltpu.CompilerParams(dimension_semantics=("arbitrary", "arbitrary"),
                                             vmem_limit_bytes=VMEM_LIMIT),
        name="out_ffn",
    )(x, om, on, gm, mod, g2, wom, won, wout, wg, wu, wd)


def _rope_expansion():
    texp = np.zeros((LANES, 4 * LANES), np.float32)
    trow = np.zeros((1, 4 * LANES), np.float32)
    hm, hn = MLA_ROPE // 2, NSA_ROT // 2
    trow[0, 0:LANES] = 1.0
    trow[0, 2 * LANES:3 * LANES] = 1.0
    for i in range(hm):
        for off, sgn in ((MLA_NOPE + i, -1.0), (MLA_NOPE + hm + i, 1.0)):
            texp[i, off] = 1.0
            trow[0, off] = 0.0
            texp[N_FREQ + i, LANES + off] = sgn
    for i in range(hn):
        for off, sgn in ((i, -1.0), (hn + i, 1.0)):
            texp[hm + i, 2 * LANES + off] = 1.0
            trow[0, 2 * LANES + off] = 0.0
            texp[N_FREQ + hm + i, 3 * LANES + off] = sgn
    return jnp.asarray(texp, BF16), jnp.asarray(trow, F32)


def _mask_tables(S):
    tq = TQ_ATT
    n_chunk = S // CMP_STRIDE
    n_sel = S // SEL_LEN
    starts = np.arange(n_chunk) * CMP_STRIDE
    sel_start = np.arange(LANES) * SEL_LEN
    ovt = ((starts[None, :] < sel_start[:, None] + SEL_LEN) &
           (starts[None, :] + CMP_LEN > sel_start[:, None]) &
           (np.arange(n_chunk)[None, :] < n_chunk - 1) &
           (np.arange(LANES)[:, None] < n_sel))
    gcol = np.arange(LANES)[:, None]
    head = np.arange(NSA_REP * NSA_HEAD)[None, :] // NSA_HEAD
    gexp = np.stack([np.stack([gcol == (g * NSA_REP + head) * N_NSA_BRANCH + br
                               for br in range(N_NSA_BRANCH)]) for g in range(NSA_KV_GROUPS)])
    qi = np.arange(tq)[:, None]
    dbias = np.where(np.arange(tq)[None, :] <= qi, 0.0, NEG)
    kk = np.arange(WINDOW + tq)[None, :]
    band = lambda d: np.where((d >= 0) & (d < WINDOW), 0.0, NEG)
    wbias = np.stack([band(qi - kk), band(qi + WINDOW - kk)])
    return (jnp.asarray(ovt, BF16), jnp.asarray(gexp, BF16),
            jnp.asarray(dbias, F32), jnp.asarray(wbias, F32))


def _pad_heads(w, n_heads, width):
    k = w.shape[0]
    w = w.reshape(k, n_heads, width)
    return jnp.pad(w, ((0, 0), (0, 0), (0, HEAD_PAD - width))).reshape(k, n_heads * HEAD_PAD)


def _pad_row(g, lo=0):
    return jnp.pad(g, (lo, HEAD_PAD - lo - g.shape[0])).reshape(1, HEAD_PAD)


def _layer(x, mod, cs, p):
    B, S, D = x.shape
    w_in = p["w_in"]
    o = 0
    cols = {}
    for name, wdt in (("cq", MLA_Q_LORA), ("ckv", MLA_KV_LORA), ("kpe", MLA_ROPE),
                      ("qn", NSA_HEADS * NSA_HEAD), ("kc", KV_W), ("vc", KV_W), ("ks", KV_W),
                      ("vs", KV_W), ("kw", KV_W), ("vw", KV_W),
                      ("gn", NSA_HEADS * N_NSA_BRANCH), ("gm", 2 * D)):
        cols[name] = w_in[:, o:o + wdt]
        o += wdt
    G = NSA_KV_GROUPS
    n_gate = NSA_HEADS * N_NSA_BRANCH
    zc = lambda n: jnp.zeros((D, n), F32)
    wsm = jnp.concatenate([cols["gn"], zc(MLA_NOPE - n_gate), cols["kpe"], zc(LANES - MLA_QK)], axis=1)
    wkv6 = jnp.concatenate([cols["kc"], cols["vc"]] +
                           [_pad_heads(cols[k], G, NSA_HEAD) for k in ("ks", "vs", "kw", "vw")], axis=1)
    wkvb = p["mla_w_kv_b"].reshape(MLA_KV_LORA, MLA_HEADS, MLA_NOPE + MLA_V)
    wkvb = jnp.concatenate([_pad_heads(wkvb[:, :, :MLA_NOPE].reshape(MLA_KV_LORA, -1), MLA_HEADS, MLA_NOPE),
                            _pad_heads(wkvb[:, :, MLA_NOPE:].reshape(MLA_KV_LORA, -1), MLA_HEADS, MLA_V)], axis=1)
    bf = lambda w: w.astype(BF16)
    row = lambda g: g.reshape(1, -1)
    weights = tuple(bf(w) for w in (cols["cq"], cols["ckv"], wsm, _pad_heads(cols["qn"], NSA_HEADS, NSA_HEAD),
                                    wkv6, cols["gm"],
                                    _pad_heads(p["mla_w_q_b"], MLA_HEADS, MLA_QK), wkvb))
    vone = jnp.tile(jnp.zeros((1, HEAD_PAD), F32).at[0, MLA_V].set(1.0), (1, MLA_HEADS))
    rows = (row(p["mla_q_a_gain"]), row(p["mla_kv_a_gain"]),
            _pad_row(p["mla_q_gain"]), _pad_row(p["mla_k_gain"][:MLA_NOPE]),
            _pad_row(p["mla_k_gain"][MLA_NOPE:], MLA_NOPE),
            _pad_row(p["nsa_q_gain"]), _pad_row(p["nsa_ks_gain"]), _pad_row(p["nsa_kw_gain"]), vone)
    texp, trow = _rope_expansion()
    (qm, km, vm, qn, ks, kw, vs, vw, kcin, vcin, gn, gm) = _inproj(
        x, mod, cs, (row(p["norm1_gain"]), texp, trow), weights, rows)

    n_chunk = S // CMP_STRIDE
    cs_end = cs[:, CMP_LEN - 1::CMP_STRIDE]
    cs_end = jnp.pad(cs_end, ((0, 0), (0, n_chunk - cs_end.shape[1]), (0, 0)))
    w2k = jnp.pad(p["cmp_w2_k"], ((0, 0), (0, HEAD_PAD - NSA_HEAD)))
    kc, vc = _compress(kcin, vcin, p["cmp_pos_k"], p["cmp_pos_v"],
                       bf(p["cmp_w1_k"]), bf(w2k), bf(p["cmp_w1_v"]), bf(p["cmp_w2_v"]),
                       _pad_row(p["nsa_kc_gain"]), cs_end, texp, trow)

    ovt, gexp, dbias, wbias = _mask_tables(S)
    o_nsa = _nsa_attention(qn, kc, vc, ks, vs, kw, vw, gn, ovt, gexp, dbias, wbias)
    o_mla = _mla_attention(qm, km, vm, dbias)

    return _out_ffn(x, o_mla, o_nsa, gm, mod, row(p["norm2_gain"]),
                    bf(p["w_o_mla"]), bf(p["w_o_nsa"]), bf(p["w_out"]),
                    bf(p["ffn_w_gate"]), bf(p["ffn_w_up"]), bf(p["ffn_w_down"]))


def kernel(x, c, positions, ada_w, ada_b, norm1_gain, w_in, mla_q_a_gain, mla_w_q_b, mla_kv_a_gain, mla_w_kv_b, mla_q_gain, mla_k_gain, nsa_q_gain, nsa_kc_gain, nsa_ks_gain, nsa_kw_gain, cmp_pos_k, cmp_w1_k, cmp_w2_k, cmp_pos_v, cmp_w1_v, cmp_w2_v, w_o_mla, w_o_nsa, w_out, norm2_gain, ffn_w_gate, ffn_w_up, ffn_w_down):
    params = dict(norm1_gain=norm1_gain, w_in=w_in, mla_q_a_gain=mla_q_a_gain, mla_w_q_b=mla_w_q_b,
                  mla_kv_a_gain=mla_kv_a_gain, mla_w_kv_b=mla_w_kv_b, mla_q_gain=mla_q_gain,
                  mla_k_gain=mla_k_gain, nsa_q_gain=nsa_q_gain, nsa_kc_gain=nsa_kc_gain,
                  nsa_ks_gain=nsa_ks_gain, nsa_kw_gain=nsa_kw_gain, cmp_pos_k=cmp_pos_k,
                  cmp_w1_k=cmp_w1_k, cmp_w2_k=cmp_w2_k, cmp_pos_v=cmp_pos_v, cmp_w1_v=cmp_w1_v,
                  cmp_w2_v=cmp_w2_v, w_o_mla=w_o_mla, w_o_nsa=w_o_nsa, w_out=w_out,
                  norm2_gain=norm2_gain, ffn_w_gate=ffn_w_gate, ffn_w_up=ffn_w_up, ffn_w_down=ffn_w_down)
    B = x.shape[0]
    inv_m = ROPE_THETA ** (-jnp.arange(0, MLA_ROPE, 2, dtype=F32) / MLA_ROPE)
    inv_n = ROPE_THETA ** (-jnp.arange(0, NSA_ROT, 2, dtype=F32) / NSA_ROT)
    cs = _rope_tables(positions, jnp.concatenate([inv_m, inv_n]))
    depth = ada_w.shape[0]
    for l in range(depth):
        mod = _ada(c, ada_w, ada_b, l).reshape(B, N_MOD, D_MODEL)
        x = _layer(x, mod, cs, {k: v[l] for k, v in params.items()})
    return x
```
